```python
import math
import jax, jax.numpy as jnp
from jax import lax
import numpy as np

D_MODEL = 1024
BATCH = 2
SEQ = 8192
DEPTH = 4

GRID_W = 64
CTX_LEN = 256
EPS = 1e-6

NA_HEADS = 4
NA_HEAD_DIM = 64
NA_WIN_H = 8
NA_WIN_W = 16
MLA_HEADS = 4
MLA_Q_LORA = 256
MLA_KV_LORA = 128
MLA_NOPE = 64
MLA_ROPE = 32
MLA_V = 64
ROPE_BASE = 10000.0
Q_BLOCK = 128
GDN_HEADS = 4
GDN_DK = 64
GDN_DV = 64
GDN_CONV = 4
GDN_CHUNK = 64
S5_GROUPS = 16
S5_GROUP_CH = 16
S5_STATE = 64
D_FF = 4 * D_MODEL

NA_W = NA_HEADS * NA_HEAD_DIM
MLA_W = MLA_HEADS * MLA_V
GDN_W = GDN_HEADS * GDN_DV
S5_W = S5_GROUPS * S5_GROUP_CH
BRANCH_W = 256
N_BRANCH = 4
IN_SPLITS = (3 * NA_W,
             MLA_Q_LORA,
             MLA_KV_LORA + MLA_ROPE,
             2 * GDN_HEADS * GDN_DK + GDN_W,
             GDN_W,
             2 * GDN_HEADS,
             2 * GDN_HEADS,
             S5_W,
             N_BRANCH * D_MODEL)
D_IN = 768 + 256 + 160 + 768 + 256 + 8 + 8 + 256 + 4096

kernel_name = 'hybrid_gated_branch_diffusion_block'


def rms_norm(x, gain):
    xf = x.astype(jnp.float32)
    y = xf * lax.rsqrt(jnp.mean(xf * xf, axis=-1, keepdims=True) + EPS)
    return (y * gain.astype(jnp.float32)).astype(x.dtype)


def l2_normalize(x):
    xf = x.astype(jnp.float32)
    return xf * lax.rsqrt(jnp.sum(xf * xf, axis=-1, keepdims=True) + EPS)


def split_cols(z, sizes):
    out, off = [], 0
    for n in sizes:
        out.append(z[..., off:off + n])
        off += n
    return out


def to_heads(a, n_heads):
    return a.reshape(a.shape[:2] + (n_heads, a.shape[-1] // n_heads))


def axial_rope(x, rows, cols):
    half = x.shape[-1] // 2
    quarter = half // 2
    inv_freq = ROPE_BASE ** (-jnp.arange(quarter, dtype=jnp.float32) / quarter)

    def rotate(xa, pos):
        ang = pos.astype(jnp.float32)[:, None] * inv_freq[None, :]
        cos = jnp.cos(ang)[None, :, None, :].astype(x.dtype)
        sin = jnp.sin(ang)[None, :, None, :].astype(x.dtype)
        x1, x2 = xa[..., :quarter], xa[..., quarter:]
        return jnp.concatenate([x1 * cos - x2 * sin, x2 * cos + x1 * sin], axis=-1)

    return jnp.concatenate([rotate(x[..., :half], rows), rotate(x[..., half:], cols)], axis=-1)


def dense_attention(q, k, v):
    B_, T, H, dq = q.shape
    scale = dq ** -0.5
    nb = T // Q_BLOCK
    qb = jnp.moveaxis(q.reshape(B_, nb, Q_BLOCK, H, dq), 1, 0)

    def attend(qblk):
        s = jnp.einsum('bqhd,bnhd->bhqn', qblk, k).astype(jnp.float32) * scale
        p = jax.nn.softmax(s, axis=-1).astype(v.dtype)
        return jnp.einsum('bhqn,bnhd->bqhd', p, v)

    o = lax.map(attend, qb)
    return jnp.moveaxis(o, 0, 1).reshape(B_, T, H, v.shape[-1])


def neighborhood_attention(q, k, v, k_ctx, v_ctx, rpb):
    B_, S, H, d = q.shape
    rows = S // GRID_W
    kh = min(NA_WIN_H, rows)
    qg = q.reshape(B_, rows, GRID_W, H, d)
    kg = k.reshape(B_, rows, GRID_W, H, d)
    vg = v.reshape(B_, rows, GRID_W, H, d)
    r = jnp.arange(rows)
    row_start = jnp.clip(r - kh // 2, 0, rows - kh)
    row_idx = row_start[:, None] + jnp.arange(kh)[None, :]
    k_band = kg[:, row_idx]
    v_band = vg[:, row_idx]
    col = jnp.arange(GRID_W)
    col_start = jnp.clip(col - NA_WIN_W // 2, 0, GRID_W - NA_WIN_W)
    col_in = (col[None, :] >= col_start[:, None]) & (col[None, :] < col_start[:, None] + NA_WIN_W)
    scale = d ** -0.5
    s_band = jnp.einsum('brqhd,brikhd->bhrqik', qg, k_band).astype(jnp.float32) * scale
    di = row_idx - r[:, None] + NA_WIN_H - 1
    dj = jnp.clip(col[None, :] - col[:, None] + NA_WIN_W - 1, 0, 2 * NA_WIN_W - 2)
    bias = rpb.astype(jnp.float32)[:, di[:, None, :, None], dj[None, :, None, :]]
    s_band = jnp.where(col_in[:, None, :], s_band + bias[None], -jnp.inf)
    s_ctx = jnp.einsum('brqhd,blhd->bhrql', qg, k_ctx).astype(jnp.float32) * scale
    n_band = kh * GRID_W
    scores = jnp.concatenate([s_band.reshape(B_, H, rows, GRID_W, n_band), s_ctx], axis=-1)
    p = jax.nn.softmax(scores, axis=-1).astype(v.dtype)
    p_band = p[..., :n_band].reshape(B_, H, rows, GRID_W, kh, GRID_W)
    o = (jnp.einsum('bhrqik,brikhd->brqhd', p_band, v_band)
         + jnp.einsum('bhrql,blhd->brqhd', p[..., n_band:], v_ctx))
    return o.reshape(B_, S, H, d)


def mla_queries(cq, q_norm, w_uq, rows, cols):
    B_, T, _ = cq.shape
    q = (rms_norm(cq, q_norm) @ w_uq).reshape(B_, T, MLA_HEADS, MLA_NOPE + MLA_ROPE)
    if rows is None:
        return q
    return jnp.concatenate([q[..., :MLA_NOPE], axial_rope(q[..., MLA_NOPE:], rows, cols)], axis=-1)


def mla_keys_values(ckv, kv_norm, w_ukv, rows, cols):
    B_, T, _ = ckv.shape
    c_kv, k_rope = ckv[..., :MLA_KV_LORA], ckv[..., MLA_KV_LORA:]
    kv = (rms_norm(c_kv, kv_norm) @ w_ukv).reshape(B_, T, MLA_HEADS, MLA_NOPE + MLA_V)
    k_rope = k_rope[:, :, None, :]
    if rows is not None:
        k_rope = axial_rope(k_rope, rows, cols)
    k = jnp.concatenate([kv[..., :MLA_NOPE], jnp.broadcast_to(k_rope, (B_, T, MLA_HEADS, MLA_ROPE))], axis=-1)
    return k, kv[..., MLA_NOPE:]


def short_conv(x, w):
    return lax.conv_general_dilated(
        x, w[:, None, :].astype(x.dtype), window_strides=(1,),
        padding=[(GDN_CONV // 2, GDN_CONV - 1 - GDN_CONV // 2)],
        dimension_numbers=('NWC', 'WIO', 'NWC'), feature_group_count=x.shape[-1])


def gdn_prepare(qkv, a, b, conv_w, a_log, dt_bias):
    B_, T, _ = qkv.shape
    qkv = jax.nn.silu(short_conv(qkv, conv_w))
    q, k, v = split_cols(qkv, (GDN_HEADS * GDN_DK, GDN_HEADS * GDN_DK, GDN_W))
    q = l2_normalize(to_heads(q, GDN_HEADS)) * (GDN_DK ** -0.5)
    k = l2_normalize(to_heads(k, GDN_HEADS))
    v = to_heads(v, GDN_HEADS).astype(jnp.float32)
    a = a.reshape(B_, T, 2, GDN_HEADS).astype(jnp.float32)
    g = -jnp.exp(a_log.astype(jnp.float32)) * jax.nn.softplus(a + dt_bias.astype(jnp.float32))
    beta = jax.nn.sigmoid(b.reshape(B_, T, 2, GDN_HEADS).astype(jnp.float32))
    return q, k, v, g, beta


def to_chunks(a):
    B_, T = a.shape[:2]
    a = jnp.moveaxis(a, 2, 1)
    return a.reshape((B_, a.shape[1], T // GDN_CHUNK, GDN_CHUNK) + a.shape[3:])


def gated_delta_chunked(q, k, v, g, beta, s0, with_output):
    B_, T, H, _ = q.shape
    q, k, v, g, beta = [to_chunks(t.astype(jnp.float32)) for t in (q, k, v, g, beta)]
    dv = v.shape[-1]
    gc = jnp.cumsum(g, axis=-1)
    idx = jnp.arange(GDN_CHUNK)
    incl = idx[:, None] >= idx[None, :]
    strict = idx[:, None] > idx[None, :]
    dec_incl = jnp.exp(jnp.where(incl, gc[..., :, None] - gc[..., None, :], -jnp.inf))
    dec_strict = jnp.where(strict, dec_incl, 0.0)
    kk = jnp.einsum('bhnid,bhnjd->bhnij', k, k)
    a_mat = jnp.eye(GDN_CHUNK, dtype=jnp.float32) + beta[..., :, None] * kk * dec_strict
    rhs = jnp.concatenate([v * beta[..., None], k * (beta * jnp.exp(gc))[..., None]], axis=-1)
    sol = lax.linalg.triangular_solve(a_mat, rhs, left_side=True, lower=True)
    u_val, k_cum = sol[..., :dv], sol[..., dv:]
    k_tail = k * jnp.exp(gc[..., -1:] - gc)[..., None]
    g_last = jnp.exp(gc[..., -1])
    xs = (u_val, k_cum, k_tail, g_last)
    if with_output:
        qk = jnp.einsum('bhnid,bhnjd->bhnij', q, k) * dec_incl
        xs = xs + (q * jnp.exp(gc)[..., None], qk)
    xs = tuple(jnp.moveaxis(t, 2, 0) for t in xs)

    def step(S, inp):
        u, kc, kt, gl = inp[:4]
        v_new = u - jnp.einsum('bhcd,bhde->bhce', kc, S)
        S_next = S * gl[..., None, None] + jnp.einsum('bhcd,bhce->bhde', kt, v_new)
        if not with_output:
            return S_next, None
        qd, qk_c = inp[4], inp[5]
        o = jnp.einsum('bhcd,bhde->bhce', qd, S) + jnp.einsum('bhij,bhje->bhie', qk_c, v_new)
        return S_next, o

    s_fin, o = lax.scan(step, s0, xs)
    if not with_output:
        return None, s_fin
    return o.transpose(1, 0, 3, 2, 4).reshape(B_, T, H, dv), s_fin


def gdn_bidirectional(q, k, v, g, beta, s0_fwd, s0_bwd, with_output):
    fl = lambda t: jnp.flip(t, axis=1)
    o_f, s_f = gated_delta_chunked(q, k, v, g[:, :, 0], beta[:, :, 0], s0_fwd, with_output)
    o_b, s_b = gated_delta_chunked(fl(q), fl(k), fl(v), fl(g[:, :, 1]), fl(beta[:, :, 1]), s0_bwd, with_output)
    o = o_f + fl(o_b) if with_output else None
    return o, s_f, s_b


def gdn_output(o, z, norm_w):
    B_, T = z.shape[:2]
    y = rms_norm(o, norm_w) * jax.nn.silu(to_heads(z, GDN_HEADS).astype(jnp.float32))
    return y.reshape(B_, T, GDN_W).astype(z.dtype)


def s5_discretize(a_re, a_im, log_dt, b_re, b_im):
    f32 = jnp.float32
    lam_re = jnp.minimum(a_re.astype(f32), -1e-4)
    lam_im = a_im.astype(f32)
    dt = jnp.exp(log_dt.astype(f32))[:, None]
    mag = jnp.exp(lam_re * dt)
    lb_re = mag * jnp.cos(lam_im * dt)
    lb_im = mag * jnp.sin(lam_im * dt)
    den = lam_re * lam_re + lam_im * lam_im
    f_re = ((lb_re - 1.0) * lam_re + lb_im * lam_im) / den
    f_im = (lb_im * lam_re - (lb_re - 1.0) * lam_im) / den
    b_re = b_re.astype(f32)
    b_im = b_im.astype(f32)
    bb_re = f_re[..., None] * b_re - f_im[..., None] * b_im
    bb_im = f_re[..., None] * b_im + f_im[..., None] * b_re
    return lb_re, lb_im, bb_re, bb_im


def s5_scan(bu_re, bu_im, lb_re, lb_im, h0_re, h0_im):
    bu_re = bu_re.at[:, 0].add(lb_re * h0_re - lb_im * h0_im)
    bu_im = bu_im.at[:, 0].add(lb_re * h0_im + lb_im * h0_re)
    a_re = jnp.broadcast_to(lb_re, bu_re.shape)
    a_im = jnp.broadcast_to(lb_im, bu_im.shape)

    def combine(e1, e2):
        a1r, a1i, b1r, b1i = e1
        a2r, a2i, b2r, b2i = e2
        return (a1r * a2r - a1i * a2i, a1r * a2i + a1i * a2r,
                a2r * b1r - a2i * b1i + b2r, a2r * b1i + a2i * b1r + b2i)

    _, _, x_re, x_im = lax.associative_scan(combine, (a_re, a_im, bu_re, bu_im), axis=1)
    return x_re, x_im


def s5_direction(u, disc, c_re, c_im, h0, reverse, need_y):
    lb_re, lb_im, bb_re, bb_im = disc
    B_, T, _ = u.shape
    ug = u.reshape(B_, T, S5_GROUPS, S5_GROUP_CH).astype(jnp.float32)
    if reverse:
        ug = jnp.flip(ug, axis=1)
    bu_re = jnp.einsum('btgc,gpc->btgp', ug, bb_re)
    bu_im = jnp.einsum('btgc,gpc->btgp', ug, bb_im)
    x_re, x_im = s5_scan(bu_re, bu_im, lb_re, lb_im, h0[0], h0[1])
    final = (x_re[:, -1], x_im[:, -1])
    if not need_y:
        return None, final
    y = (jnp.einsum('btgp,gcp->btgc', x_re, c_re.astype(jnp.float32))
         - jnp.einsum('btgp,gcp->btgc', x_im, c_im.astype(jnp.float32)))
    if reverse:
        y = jnp.flip(y, axis=1)
    return y.reshape(B_, T, S5_W), final


def s5_mixer(u, u_c, need_ctx, a_re, a_im, log_dt, b_re, b_im, c_re, c_im, d_skip, glu_w, glu_b):
    disc = [s5_discretize(a_re[i], a_im[i], log_dt[i], b_re, b_im) for i in range(2)]
    zero = jnp.zeros((u.shape[0], S5_GROUPS, S5_STATE), jnp.float32)
    yc_f, h_f = s5_direction(u_c, disc[0], c_re[0], c_im[0], (zero, zero), False, need_ctx)
    yc_b, h_b = s5_direction(u_c, disc[1], c_re[1], c_im[1], (zero, zero), True, need_ctx)
    y_f, _ = s5_direction(u, disc[0], c_re[0], c_im[0], h_f, False, True)
    y_b, _ = s5_direction(u, disc[1], c_re[1], c_im[1], h_b, True, True)

    def finish(inp, yf, yb):
        y = yf + yb + d_skip.astype(jnp.float32) * inp.astype(jnp.float32)
        y = jax.nn.gelu(y)
        y = y * jax.nn.sigmoid(y @ glu_w.astype(jnp.float32) + glu_b.astype(jnp.float32))
        return y.astype(inp.dtype)

    return finish(u, y_f, y_b), (finish(u_c, yc_f, yc_b) if need_ctx else None)


def merge_branches(ys, gate_logits, w_branch, w_out):
    B_, T = gate_logits.shape[:2]
    gates = jax.nn.sigmoid(gate_logits.astype(jnp.float32)).astype(gate_logits.dtype)
    gates = gates.reshape(B_, T, N_BRANCH, D_MODEL)
    proj = jnp.einsum('btiw,iwd->btid', jnp.stack(ys, axis=2), w_branch)
    return jnp.sum(gates * proj, axis=2) @ w_out


def token_mixer(h, hc, need_ctx, w_in, na_rpb, mla_q_norm, mla_kv_norm, mla_w_uq, mla_w_ukv,
                gdn_conv, gdn_a_log, gdn_dt_bias, gdn_norm, s5_a_re, s5_a_im, s5_log_dt,
                s5_b_re, s5_b_im, s5_c_re, s5_c_im, s5_d, s5_glu_w, s5_glu_b, w_branch, w_out):
    S = h.shape[1]
    t = jnp.arange(S)
    rows, cols = t // GRID_W, t % GRID_W
    (na_qkv, mla_cq, mla_ckv, gdn_qkv, gdn_z, gdn_a, gdn_b, s5_u, gate_logits) = split_cols(h @ w_in, IN_SPLITS)
    (na_qkv_c, mla_cq_c, mla_ckv_c, gdn_qkv_c, gdn_z_c, gdn_a_c, gdn_b_c, s5_u_c, gate_logits_c) = split_cols(hc @ w_in, IN_SPLITS)

    q, k, v = [to_heads(a, NA_HEADS) for a in jnp.split(na_qkv, 3, axis=-1)]
    q_c, k_c, v_c = [to_heads(a, NA_HEADS) for a in jnp.split(na_qkv_c, 3, axis=-1)]
    y_na = neighborhood_attention(q, k, v, k_c, v_c, na_rpb).reshape(h.shape[:2] + (NA_W,))

    k_lat, v_lat = mla_keys_values(mla_ckv, mla_kv_norm, mla_w_ukv, rows, cols)
    k_ctx, v_ctx = mla_keys_values(mla_ckv_c, mla_kv_norm, mla_w_ukv, None, None)
    q_lat = mla_queries(mla_cq, mla_q_norm, mla_w_uq, rows, cols)
    y_mla = dense_attention(q_lat, jnp.concatenate([k_lat, k_ctx], axis=1),
                            jnp.concatenate([v_lat, v_ctx], axis=1)).reshape(h.shape[:2] + (MLA_W,))

    gq, gk, gv, gg, gbeta = gdn_prepare(gdn_qkv, gdn_a, gdn_b, gdn_conv, gdn_a_log, gdn_dt_bias)
    cq, ck, cv, cg, cbeta = gdn_prepare(gdn_qkv_c, gdn_a_c, gdn_b_c, gdn_conv, gdn_a_log, gdn_dt_bias)
    s_zero = jnp.zeros((h.shape[0], GDN_HEADS, GDN_DK, GDN_DV), jnp.float32)
    o_c, s_f, s_b = gdn_bidirectional(cq, ck, cv, cg, cbeta, s_zero, s_zero, need_ctx)
    o_l, _, _ = gdn_bidirectional(gq, gk, gv, gg, gbeta, s_f, s_b, True)
    y_gdn = gdn_output(o_l, gdn_z, gdn_norm)

    y_s5, y_s5_c = s5_mixer(s5_u, s5_u_c, need_ctx, s5_a_re, s5_a_im, s5_log_dt, s5_b_re, s5_b_im,
                            s5_c_re, s5_c_im, s5_d, s5_glu_w, s5_glu_b)

    y = merge_branches([y_na, y_mla, y_gdn, y_s5], gate_logits, w_branch, w_out)
    if not need_ctx:
        return y, None
    L = hc.shape[1]
    y_na_c = dense_attention(q_c, k_c, v_c).reshape(hc.shape[:2] + (NA_W,))
    y_mla_c = dense_attention(mla_queries(mla_cq_c, mla_q_norm, mla_w_uq, None, None), k_ctx, v_ctx).reshape(hc.shape[:2] + (MLA_W,))
    y_gdn_c = gdn_output(o_c, gdn_z_c, gdn_norm)
    y_c = merge_branches([y_na_c, y_mla_c, y_gdn_c, y_s5_c], gate_logits_c, w_branch, w_out)
    return y, y_c


def sq_relu_mlp(h, w1, w2):
    return jnp.square(jax.nn.relu(h @ w1)) @ w2


def setup_inputs(seed: int = 0) -> dict:
    key = jax.random.key(seed)
    ks = jax.random.split(key, 30)
    f32 = jnp.float32

    def nrm(i, shape, scale):
        return scale * jax.random.normal(ks[i], shape, f32)

    def unif(i, shape, lo, hi):
        return jax.random.uniform(ks[i], shape, f32, lo, hi)

    dt = jnp.exp(unif(15, (DEPTH, 2, GDN_HEADS), math.log(1e-3), math.log(1e-1)))
    return {
        'x': nrm(0, (BATCH, SEQ, D_MODEL), 1.0),
        'c': nrm(1, (BATCH, D_MODEL), 1.0),
        'ctx': nrm(2, (BATCH, CTX_LEN, D_MODEL), 1.0),
        'c_ctx': nrm(3, (D_MODEL,), 1.0),
        'ada_w': nrm(4, (DEPTH, D_MODEL, 6 * D_MODEL), 0.5 * D_MODEL ** -0.5),
        'ada_b': nrm(5, (DEPTH, 6 * D_MODEL), 0.02),
        'norm_gains': 1.0 + nrm(6, (DEPTH, 4, D_MODEL), 0.05),
        'w_in': nrm(7, (DEPTH, D_MODEL, D_IN), D_MODEL ** -0.5),
        'na_rpb': nrm(8, (DEPTH, NA_HEADS, 2 * NA_WIN_H - 1, 2 * NA_WIN_W - 1), 0.1),
        'mla_q_norm': 1.0 + nrm(9, (DEPTH, MLA_Q_LORA), 0.05),
        'mla_kv_norm': 1.0 + nrm(10, (DEPTH, MLA_KV_LORA), 0.05),
        'mla_w_uq': nrm(11, (DEPTH, MLA_Q_LORA, MLA_HEADS * (MLA_NOPE + MLA_ROPE)), MLA_Q_LORA ** -0.5),
        'mla_w_ukv': nrm(12, (DEPTH, MLA_KV_LORA, MLA_HEADS * (MLA_NOPE + MLA_V)), MLA_KV_LORA ** -0.5),
        'gdn_conv': nrm(13, (DEPTH, GDN_CONV, 2 * GDN_HEADS * GDN_DK + GDN_W), GDN_CONV ** -0.5),
        'gdn_a_log': jnp.log(unif(14, (DEPTH, 2, GDN_HEADS), 1.0, 16.0)),
        'gdn_dt_bias': dt + jnp.log(-jnp.expm1(-dt)),
        'gdn_norm': 1.0 + nrm(16, (DEPTH, GDN_DV), 0.05),
        's5_a_re': -0.5 + nrm(17, (DEPTH, 2, S5_GROUPS, S5_STATE), 0.01),
        's5_a_im': jnp.broadcast_to(math.pi * jnp.arange(S5_STATE, dtype=f32), (DEPTH, 2, S5_GROUPS, S5_STATE)),
        's5_log_dt': unif(18, (DEPTH, 2, S5_GROUPS), math.log(1e-3), math.log(1e-1)),
        's5_b_re': nrm(19, (DEPTH, S5_GROUPS, S5_STATE, S5_GROUP_CH), (2 * S5_GROUP_CH) ** -0.5),
        's5_b_im': nrm(20, (DEPTH, S5_GROUPS, S5_STATE, S5_GROUP_CH), (2 * S5_GROUP_CH) ** -0.5),
        's5_c_re': nrm(21, (DEPTH, 2, S5_GROUPS, S5_GROUP_CH, S5_STATE), S5_STATE ** -0.5),
        's5_c_im': nrm(22, (DEPTH, 2, S5_GROUPS, S5_GROUP_CH, S5_STATE), S5_STATE ** -0.5),
        's5_d': nrm(23, (DEPTH, S5_W), 1.0),
        's5_glu_w': nrm(24, (DEPTH, S5_W, S5_W), S5_W ** -0.5),
        's5_glu_b': nrm(25, (DEPTH, S5_W), 0.02),
        'w_branch': nrm(26, (DEPTH, N_BRANCH, BRANCH_W, D_MODEL), BRANCH_W ** -0.5),
        'w_out': nrm(27, (DEPTH, D_MODEL, D_MODEL), D_MODEL ** -0.5),
        'mlp_w1': nrm(28, (DEPTH, D_MODEL, D_FF), D_MODEL ** -0.5),
        'mlp_w2': nrm(29, (DEPTH, D_FF, D_MODEL), D_FF ** -0.5),
    }


def reference(x, c, ctx, c_ctx, ada_w, ada_b, norm_gains, w_in, na_rpb, mla_q_norm, mla_kv_norm,
              mla_w_uq, mla_w_ukv, gdn_conv, gdn_a_log, gdn_dt_bias, gdn_norm, s5_a_re, s5_a_im,
              s5_log_dt, s5_b_re, s5_b_im, s5_c_re, s5_c_im, s5_d, s5_glu_w, s5_glu_b, w_branch,
              w_out, mlp_w1, mlp_w2):
    xc = ctx
    for l in range(DEPTH):
        need_ctx = l < DEPTH - 1
        mod = jax.nn.silu(c) @ ada_w[l] + ada_b[l]
        mod_c = jax.nn.silu(c_ctx) @ ada_w[l] + ada_b[l]
        sh1, sc1, g1, sh2, sc2, g2 = jnp.split(mod[:, None, :], 6, axis=-1)
        sh1c, sc1c, g1c, sh2c, sc2c, g2c = jnp.split(mod_c[None, None, :], 6, axis=-1)

        h = rms_norm(x, norm_gains[l, 0]) * (1.0 + sc1) + sh1
        hc = rms_norm(xc, norm_gains[l, 0]) * (1.0 + sc1c) + sh1c
        y, y_c = token_mixer(h, hc, need_ctx, w_in[l], na_rpb[l], mla_q_norm[l], mla_kv_norm[l],
                             mla_w_uq[l], mla_w_ukv[l], gdn_conv[l], gdn_a_log[l], gdn_dt_bias[l],
                             gdn_norm[l], s5_a_re[l], s5_a_im[l], s5_log_dt[l], s5_b_re[l], s5_b_im[l],
                             s5_c_re[l], s5_c_im[l], s5_d[l], s5_glu_w[l], s5_glu_b[l], w_branch[l], w_out[l])
        x = x + g1 * rms_norm(y, norm_gains[l, 1])
        h = rms_norm(x, norm_gains[l, 2]) * (1.0 + sc2) + sh2
        x = x + g2 * rms_norm(sq_relu_mlp(h, mlp_w1[l], mlp_w2[l]), norm_gains[l, 3])
        if need_ctx:
            xc = xc + g1c * rms_norm(y_c, norm_gains[l, 1])
            hc = rms_norm(xc, norm_gains[l, 2]) * (1.0 + sc2c) + sh2c
            xc = xc + g2c * rms_norm(sq_relu_mlp(hc, mlp_w1[l], mlp_w2[l]), norm_gains[l, 3])
    return x
```

```python
import functools
import math

import numpy as np
import jax
import jax.numpy as jnp
from jax import lax
from jax.experimental import pallas as pl
from jax.experimental.pallas import tpu as pltpu

F32 = jnp.float32
BF16 = jnp.bfloat16
HI = lax.Precision.HIGHEST
EPS = 1e-6

D_MODEL = 1024
GRID_W = 64
NA_WIN_H = 8
NA_WIN_W = 16
N_HEADS = 4
HEAD_W = 64
BRANCH_W = 256
MLA_NOPE = 64
MLA_ROPE = 32
MLA_KV_LORA = 128
ROPE_BASE = 10000.0
GDN_CONV = 4
CHUNK = 64
S5_GROUPS = 16
S5_GROUP_CH = 16
S5_STATE = 64
D_FF = 4 * D_MODEL

TM = 512
TN = 512
TS = 256
NEG = -1e30

C_GDN = 0
C_NA = 768
C_CQ = 1536
C_CKV = 1792
C_Z = 2048
C_U = 2304
ZW = 2560
L_KR = 128
L_A = 160
L_B = 168
L_KRS = 176
ROPE_SWAP = np.concatenate([np.arange(8, 16), np.arange(0, 8), np.arange(24, 32), np.arange(16, 24)])


def _cparams(sem, vmem_mb=None):
    kw = dict(dimension_semantics=sem)
    if vmem_mb is not None:
        kw["vmem_limit_bytes"] = vmem_mb * 1024 * 1024
    return pltpu.CompilerParams(**kw)


def _dot(a, b, **kw):
    return jnp.dot(a, b, preferred_element_type=F32, **kw)


def _dot_nt(a, b):
    return lax.dot_general(a, b, (((1,), (1,)), ((), ())), preferred_element_type=F32)


def _sigmoid(x):
    return 1.0 / (1.0 + jnp.exp(-x))


def _silu(x):
    return x * _sigmoid(x)


def _mod_kernel(c_ref, w_ref, b_ref, o_ref):
    c = c_ref[...]
    o_ref[...] = _dot(_silu(c), w_ref[...], precision=HI) + b_ref[...]


def _modulation(cvec, ada_w, ada_b):
    depth, d, n = ada_w.shape
    tn = 1536
    return pl.pallas_call(
        _mod_kernel,
        grid=(depth, n // tn),
        in_specs=[pl.BlockSpec((8, d), lambda l, j: (0, 0)),
                  pl.BlockSpec((None, d, tn), lambda l, j: (l, 0, j)),
                  pl.BlockSpec((None, 1, tn), lambda l, j: (l, 0, j))],
        out_specs=pl.BlockSpec((None, 8, tn), lambda l, j: (l, 0, j)),
        out_shape=jax.ShapeDtypeStruct((depth, 8, n), F32),
        compiler_params=_cparams(("parallel", "parallel"), 40),
        name="modulation",
    )(cvec, ada_w, ada_b.reshape(depth, 1, n))


def _norm_mod(x, gain, sc, sh):
    r = lax.rsqrt(jnp.mean(x * x, axis=-1, keepdims=True) + EPS)
    return (x * r * gain) * (1.0 + sc) + sh


def _inproj_kernel(x_ref, sc_ref, sh_ref, gain_ref, w_ref, o_ref, h_ref, *, gate):
    @pl.when(pl.program_id(1) == 0)
    def _():
        h_ref[...] = _norm_mod(x_ref[...], gain_ref[...], sc_ref[...], sh_ref[...]).astype(BF16)

    acc = _dot(h_ref[...], w_ref[...])
    if gate:
        acc = _sigmoid(acc)
    o_ref[...] = acc.astype(BF16)


def _inproj(x, sc, sh, gain, w, layer, tps, nb, gate):
    nt = x.shape[0] // TM
    width = w.shape[-1]
    row = lambda i, j: (jnp.minimum(i // tps, nb), 0, 0)
    return pl.pallas_call(
        functools.partial(_inproj_kernel, gate=gate),
        grid=(nt, width // TN),
        in_specs=[pl.BlockSpec((TM, D_MODEL), lambda i, j: (i, 0)),
                  pl.BlockSpec((None, 1, D_MODEL), row),
                  pl.BlockSpec((None, 1, D_MODEL), row),
                  pl.BlockSpec((1, D_MODEL), lambda i, j: (0, 0)),
                  pl.BlockSpec((None, D_MODEL, TN), lambda i, j: (layer, 0, j))],
        out_specs=pl.BlockSpec((TM, TN), lambda i, j: (i, j)),
        out_shape=jax.ShapeDtypeStruct((x.shape[0], width), BF16),
        scratch_shapes=[pltpu.VMEM((TM, D_MODEL), BF16)],
        compiler_params=_cparams(("parallel", "arbitrary")),
        name="inproj_gates" if gate else "inproj",
    )(x, sc, sh, gain, w)


def _head_lane_mask(width, head_w, h):
    lane = lax.broadcasted_iota(jnp.int32, (1, width), 1)
    return (lane >= h * head_w) & (lane < (h + 1) * head_w)


def _na_kernel(q_ref, k_ref, v_ref, kc_ref, vc_ref, bias_ref, o_ref, *, rows_total):
    i = pl.program_id(1)
    kb = jnp.clip(2 * i - 1, 0, rows_total // 4 - 4)
    start = pl.multiple_of(kb * (4 * GRID_W), 4 * GRID_W)
    nk = 2 * NA_WIN_H * GRID_W
    q = q_ref[...]
    kw = k_ref[pl.ds(start, nk), :]
    vw = v_ref[pl.ds(start, nk), :]
    kc = kc_ref[...]
    vc = vc_ref[...]
    scale = HEAD_W ** -0.5
    out = jnp.zeros(q.shape, F32)
    for h in range(N_HEADS):
        hm = _head_lane_mask(BRANCH_W, HEAD_W, h)
        qh = jnp.where(hm, q, jnp.zeros_like(q))
        sb = _dot_nt(qh, kw) * scale + bias_ref[h].astype(F32)
        sc = _dot_nt(qh, kc) * scale
        m = jnp.maximum(jnp.max(sb, axis=-1, keepdims=True), jnp.max(sc, axis=-1, keepdims=True))
        pb = jnp.exp(sb - m)
        pc = jnp.exp(sc - m)
        den = jnp.sum(pb, axis=-1, keepdims=True) + jnp.sum(pc, axis=-1, keepdims=True)
        o = _dot(pb.astype(BF16), vw) + _dot(pc.astype(BF16), vc)
        out = jnp.where(hm, o / den, out)
    o_ref[...] = out.astype(BF16)


def _na_bias_table(rpb, rows_total):
    w = GRID_W
    out = []
    for r0, kb0 in ((0, 0), (8, 4), (rows_total - 8, rows_total - 16)):
        qr = r0 + np.arange(8)[:, None, None, None]
        qc = np.arange(w)[None, :, None, None]
        kr = kb0 + np.arange(16)[None, None, :, None]
        kc = np.arange(w)[None, None, None, :]
        rs = np.clip(qr - NA_WIN_H // 2, 0, rows_total - NA_WIN_H)
        cs = np.clip(qc - NA_WIN_W // 2, 0, w - NA_WIN_W)
        valid = (kr >= rs) & (kr < rs + NA_WIN_H) & (kc >= cs) & (kc < cs + NA_WIN_W)
        di = np.clip(kr - qr + NA_WIN_H - 1, 0, 2 * NA_WIN_H - 2)
        dj = np.clip(kc - qc + NA_WIN_W - 1, 0, 2 * NA_WIN_W - 2)
        di, dj, valid = [np.broadcast_to(a, (8, w, 16, w)).reshape(8 * w, 16 * w) for a in (di, dj, valid)]
        bias = rpb[:, di, dj]
        out.append(jnp.where(valid[None], bias, NEG))
    return jnp.stack(out).astype(BF16)


def _na_latent(z, bias, nb, s):
    rows_total = s // GRID_W
    qb = 8 * GRID_W
    nq = s // qb
    nl = nb * s

    def case(b, i):
        return (jnp.where(i == 0, 0, jnp.where(i == nq - 1, 2, 1)), 0, 0, 0)

    return pl.pallas_call(
        functools.partial(_na_kernel, rows_total=rows_total),
        grid=(nb, nq),
        in_specs=[pl.BlockSpec((qb, BRANCH_W), lambda b, i: (b * nq + i, C_NA // 256)),
                  pl.BlockSpec((s, BRANCH_W), lambda b, i: (b, C_NA // 256 + 1)),
                  pl.BlockSpec((s, BRANCH_W), lambda b, i: (b, C_NA // 256 + 2)),
                  pl.BlockSpec((TS, BRANCH_W), lambda b, i: (nl // TS + b, C_NA // 256 + 1)),
                  pl.BlockSpec((TS, BRANCH_W), lambda b, i: (nl // TS + b, C_NA // 256 + 2)),
                  pl.BlockSpec((None, N_HEADS, qb, 2 * qb), case)],
        out_specs=pl.BlockSpec((qb, BRANCH_W), lambda b, i: (b * nq + i, 0)),
        out_shape=jax.ShapeDtypeStruct((nl, BRANCH_W), BF16),
        compiler_params=_cparams(("parallel", "arbitrary"), 56),
        name="na_latent",
    )(z, z, z, z, z, bias)


def _ctx_attn_kernel(q_ref, k_ref, v_ref, o_ref, *, scale):
    q = q_ref[...]
    k = k_ref[...]
    v = v_ref[...]
    qw = q.shape[-1]
    out = jnp.zeros((q.shape[0], BRANCH_W), F32)
    for h in range(N_HEADS):
        qh = jnp.where(_head_lane_mask(qw, qw // N_HEADS, h), q, jnp.zeros_like(q))
        s = _dot_nt(qh, k) * scale
        m = jnp.max(s, axis=-1, keepdims=True)
        p = jnp.exp(s - m)
        den = jnp.sum(p, axis=-1, keepdims=True)
        o = _dot(p.astype(BF16), v)
        out = jnp.where(_head_lane_mask(BRANCH_W, HEAD_W, h), o / den, out)
    o_ref[...] = out.astype(BF16)


def _ctx_attention(q, k, v, qcol, kcol, vcol, qw, row0, nb, scale, name):
    return pl.pallas_call(
        functools.partial(_ctx_attn_kernel, scale=scale),
        grid=(nb,),
        in_specs=[pl.BlockSpec((TS, qw), lambda b: (row0 + b, qcol)),
                  pl.BlockSpec((TS, qw), lambda b: (row0 + b, kcol)),
                  pl.BlockSpec((TS, BRANCH_W), lambda b: (row0 + b, vcol))],
        out_specs=pl.BlockSpec((TS, BRANCH_W), lambda b: (b, 0)),
        out_shape=jax.ShapeDtypeStruct((nb * TS, BRANCH_W), BF16),
        compiler_params=_cparams(("parallel",)),
        name=name,
    )(q, k, v)


def _rms(x, gain):
    return x * lax.rsqrt(jnp.mean(x * x, axis=-1, keepdims=True) + EPS) * gain


def _mla_prep_kernel(cq_ref, ckv_ref, cos_ref, sin_ref, qn_ref, kvn_ref, wq_ref, wqs_ref, wk_ref, wv_ref,
                     p1_ref, p2_ref, q_ref, k_ref, v_ref):
    cos = jnp.concatenate([cos_ref[...]] * N_HEADS, axis=-1)
    sin = jnp.concatenate([sin_ref[...]] * N_HEADS, axis=-1)
    cqn = _rms(cq_ref[...].astype(F32), qn_ref[...]).astype(BF16)
    scale = (MLA_NOPE + MLA_ROPE) ** -0.5
    q = _dot(cqn, wq_ref[...]) * cos + _dot(cqn, wqs_ref[...]) * sin
    q_ref[...] = (q * scale).astype(BF16)
    ckv = ckv_ref[...]
    kvn = _rms(ckv[:, :MLA_KV_LORA].astype(F32), kvn_ref[...]).astype(BF16)
    k = (_dot(kvn, wk_ref[...]) + _dot(ckv, p1_ref[...])) * cos + _dot(ckv, p2_ref[...]) * sin
    k_ref[...] = k.astype(BF16)
    v_ref[...] = _dot(kvn, wv_ref[...]).astype(BF16)


def _mla_prep(z, cos_t, sin_t, qn, kvn, wq, wqs, wk, wv, p1, p2, tps, n_lat_tiles):
    nt_rows = z.shape[0]
    nt = nt_rows // TM
    full = lambda a: pl.BlockSpec(a.shape, lambda i: (0,) * a.ndim)
    tab = lambda i: (jnp.where(i < n_lat_tiles, i % tps, tps), 0)
    hw = N_HEADS * 128
    return pl.pallas_call(
        _mla_prep_kernel,
        grid=(nt,),
        in_specs=[pl.BlockSpec((TM, 256), lambda i: (i, C_CQ // 256)),
                  pl.BlockSpec((TM, 256), lambda i: (i, C_CKV // 256)),
                  pl.BlockSpec((TM, 128), tab), pl.BlockSpec((TM, 128), tab),
                  full(qn), full(kvn), full(wq), full(wqs), full(wk), full(wv), full(p1), full(p2)],
        out_specs=[pl.BlockSpec((TM, hw), lambda i: (i, 0)),
                   pl.BlockSpec((TM, hw), lambda i: (i, 0)),
                   pl.BlockSpec((TM, BRANCH_W), lambda i: (i, 0))],
        out_shape=[jax.ShapeDtypeStruct((nt_rows, hw), BF16),
                   jax.ShapeDtypeStruct((nt_rows, hw), BF16),
                   jax.ShapeDtypeStruct((nt_rows, BRANCH_W), BF16)],
        compiler_params=_cparams(("parallel",)),
        name="mla_prep",
    )(z, z, cos_t, sin_t, qn, kvn, wq, wqs, wk, wv, p1, p2)


def _flash_kernel(q_ref, kl_ref, vl_ref, kc_ref, vc_ref, o_ref, *, tk, n_lat):
    tq = q_ref.shape[0]
    qs = [q_ref[:, 128 * hh:128 * (hh + 1)] for hh in range(2)]

    def update(carry, ks, v):
        new = []
        for hh in range(2):
            m, l, acc = carry[hh]
            s = _dot_nt(qs[hh], ks[hh])
            m_new = jnp.maximum(m, jnp.max(s, axis=-1, keepdims=True))
            a = jnp.exp(m - m_new)
            p = jnp.exp(s - m_new)
            l = a * l + jnp.sum(p, axis=-1, keepdims=True)
            acc = a * acc + _dot(p.astype(BF16), v)
            new.append((m_new, l, acc))
        return tuple(new)

    def body(t, carry):
        r0 = pl.multiple_of(t * tk, tk)
        ks = [kl_ref[pl.ds(r0, tk), 128 * hh:128 * (hh + 1)] for hh in range(2)]
        return update(carry, ks, vl_ref[pl.ds(r0, tk), :])

    init = tuple((jnp.full((tq, 1), NEG, F32), jnp.zeros((tq, 1), F32), jnp.zeros((tq, 128), F32))
                 for _ in range(2))
    carry = lax.fori_loop(0, n_lat, body, init)
    carry = update(carry, [kc_ref[:, 128 * hh:128 * (hh + 1)] for hh in range(2)], vc_ref[...])
    lane = lax.broadcasted_iota(jnp.int32, (1, 128), 1)
    o = jnp.where(lane < HEAD_W, carry[0][2] / carry[0][1], carry[1][2] / carry[1][1])
    o_ref[...] = o.astype(BF16)


def _mla_latent(qm, km, vm, nb, s, tq=512, tk=512):
    nq = s // tq
    nl = nb * s
    return pl.pallas_call(
        functools.partial(_flash_kernel, tk=tk, n_lat=s // tk),
        grid=(nb, 2, nq),
        in_specs=[pl.BlockSpec((tq, 256), lambda b, hp, i: (b * nq + i, hp)),
                  pl.BlockSpec((s, 256), lambda b, hp, i: (b, hp)),
                  pl.BlockSpec((s, 128), lambda b, hp, i: (b, hp)),
                  pl.BlockSpec((TS, 256), lambda b, hp, i: (nl // TS + b, hp)),
                  pl.BlockSpec((TS, 128), lambda b, hp, i: (nl // TS + b, hp))],
        out_specs=pl.BlockSpec((tq, 128), lambda b, hp, i: (b * nq + i, hp)),
        out_shape=jax.ShapeDtypeStruct((nl, BRANCH_W), BF16),
        compiler_params=_cparams(("parallel", "parallel", "arbitrary"), 48),
        name="mla_flash",
    )(qm, km, vm, km, vm)


def _gdn_prep_kernel(prev_ref, cur_ref, next_ref, ckv_ref, conv_ref, alog_ref, dtb_ref, bseg_ref, ef_ref,
                     q_ref, k_ref, v_ref, gcx_ref, btx_ref, gcc_ref, *, tps, n_lat_tiles):
    i = pl.program_id(0)
    is_ctx = i >= n_lat_tiles
    first = is_ctx | (i % tps == 0)
    last = is_ctx | (i % tps == tps - 1)
    prev = jnp.where(first, 0.0, prev_ref[...].astype(F32))
    nxt = jnp.where(last, 0.0, next_ref[...].astype(F32))
    ext = jnp.concatenate([prev, cur_ref[...].astype(F32), nxt], axis=0)
    n_ext = TS + 16
    acc = jnp.zeros((TS, 3 * BRANCH_W), F32)
    for j in range(GDN_CONV):
        shifted = pltpu.roll(ext, n_ext - (8 - GDN_CONV // 2 + j), axis=0)[:TS]
        acc = acc + shifted * conv_ref[j:j + 1, :]
    x = _silu(acc)
    bseg = bseg_ref[...]

    def l2n(a):
        return a * lax.rsqrt(_dot(a * a, bseg, precision=HI) + EPS)

    q_ref[...] = l2n(x[:, :BRANCH_W]) * (HEAD_W ** -0.5)
    k_ref[...] = l2n(x[:, BRANCH_W:2 * BRANCH_W])
    v_ref[...] = x[:, 2 * BRANCH_W:]

    ab = ckv_ref[...].astype(F32)
    sp_in = ab + dtb_ref[...]
    softplus = jnp.maximum(sp_in, 0.0) + jnp.log1p(jnp.exp(-jnp.abs(sp_in)))
    lane = lax.broadcasted_iota(jnp.int32, (1, 256), 1)
    g = jnp.where((lane >= L_A) & (lane < L_B), -jnp.exp(alog_ref[...]) * softplus, 0.0)
    beta = jnp.where((lane >= L_B) & (lane < L_KRS), _sigmoid(ab), 0.0)
    r = lax.broadcasted_iota(jnp.int32, (TS, TS), 0)
    c = lax.broadcasted_iota(jnp.int32, (TS, TS), 1)
    same = (r // CHUNK) == (c // CHUNK)
    lower = jnp.where(same & (r >= c), 1.0, 0.0)
    upper = jnp.where(same & (r <= c), 1.0, 0.0)
    fwd_lane = (lane >= L_A) & (lane < L_A + N_HEADS)
    gc = jnp.where(fwd_lane, _dot(lower, g, precision=HI), _dot(upper, g, precision=HI))
    gcc_ref[...] = gc
    gcx_ref[...] = _dot(gc, ef_ref[0], precision=HI)
    btx_ref[...] = _dot(beta, ef_ref[1], precision=HI)


def _gdn_prep(z, conv_w, alog_row, dtb_row, bseg, ef, tps_s, n_lat_tiles):
    nt_rows = z.shape[0]
    nt = nt_rows // TS
    hb = TS // 8
    full = lambda a: pl.BlockSpec(a.shape, lambda i: (0,) * a.ndim)
    xw = 2 * N_HEADS * 128
    return pl.pallas_call(
        functools.partial(_gdn_prep_kernel, tps=tps_s, n_lat_tiles=n_lat_tiles),
        grid=(nt,),
        in_specs=[pl.BlockSpec((8, 768), lambda i: (jnp.maximum(i * hb - 1, 0), 0)),
                  pl.BlockSpec((TS, 768), lambda i: (i, 0)),
                  pl.BlockSpec((8, 768), lambda i: (jnp.minimum((i + 1) * hb, nt * hb - 1), 0)),
                  pl.BlockSpec((TS, 256), lambda i: (i, C_CKV // 256)),
                  full(conv_w), full(alog_row), full(dtb_row), full(bseg), full(ef)],
        out_specs=[pl.BlockSpec((TS, 256), lambda i: (i, 0))] * 3
                  + [pl.BlockSpec((TS, xw), lambda i: (i, 0))] * 2
                  + [pl.BlockSpec((TS, 256), lambda i: (i, 0))],
        out_shape=[jax.ShapeDtypeStruct((nt_rows, 256), F32)] * 3
                  + [jax.ShapeDtypeStruct((nt_rows, xw), F32)] * 2
                  + [jax.ShapeDtypeStruct((nt_rows, 256), F32)],
        compiler_params=_cparams(("parallel",)),
        name="gdn_prep",
    )(z, z, z, z, conv_w, alog_row, dtb_row, bseg, ef)


def _gdn_scan_kernel(q_ref, k_ref, kt_ref, vk_ref, gc_ref, bt_ref, gr_ref, o_ref, s_ref, x_ref, n_ref, *, g_chunks):
    @pl.when(pl.program_id(1) == 0)
    def _():
        s_ref[...] = jnp.zeros(s_ref.shape, F32)

    c = CHUNK
    r = lax.broadcasted_iota(jnp.int32, (c, c), 0)
    cc = lax.broadcasted_iota(jnp.int32, (c, c), 1)
    incl = (r >= cc)[None]
    strict = (r > cc)[None]
    lane = lax.broadcasted_iota(jnp.int32, (1, 1, 2 * HEAD_W), 2)
    bmm = lambda a, b: jnp.einsum("hij,hjk->hik", a, b, preferred_element_type=F32)

    per_chunk = []
    for ci in range(g_chunks):
        sl = slice(ci * c, (ci + 1) * c)
        q = q_ref[:, sl, :]
        k = k_ref[:, sl, :]
        kt = kt_ref[:, ci]
        gc = gc_ref[:, sl, :]
        bt = bt_ref[:, sl, :]
        grow = gr_ref[:, ci][:, 0:1, :]
        gcol = gc[:, :, :c]
        dec = jnp.where(incl, jnp.exp(jnp.minimum(gcol - grow, 0.0)), 0.0)
        ktb = kt.astype(BF16)
        kk = bmm(k.astype(BF16), ktb)
        n_ref[ci * N_HEADS:(ci + 1) * N_HEADS] = jnp.where(strict, bt[:, :, :c] * kk * dec, 0.0)
        egc = jnp.exp(gc)
        x_ref[ci * N_HEADS:(ci + 1) * N_HEADS] = vk_ref[:, sl, :] * jnp.where(lane < HEAD_W, bt, bt * egc)
        qk = bmm(q.astype(BF16), ktb) * dec
        qd = q * egc[:, :, :c]
        glast = gc[:, c - 1:c, :c]
        ktail = kt * jnp.exp(glast - grow)
        per_chunk.append((qk, qd, ktail, jnp.exp(glast)))

    for j in range(c - 1):
        r0 = (j + 1) // 8 * 8
        xj = x_ref[:, j:j + 1, :]
        ncol = n_ref[:, r0:, j:j + 1]
        x_ref[:, r0:, :] = x_ref[:, r0:, :] - ncol * xj

    for ci in range(g_chunks):
        qk, qd, ktail, eg = per_chunk[ci]
        x = x_ref[ci * N_HEADS:(ci + 1) * N_HEADS]
        s_ext = s_ref[...]
        v_new = x[:, :, :HEAD_W] - bmm(x.astype(BF16), s_ext.astype(BF16))
        s_old = s_ext[:, HEAD_W:, :]
        vb = v_new.astype(BF16)
        o_ref[:, ci * c:(ci + 1) * c, :] = bmm(qd.astype(BF16), s_old.astype(BF16)) + bmm(qk.astype(BF16), vb)
        s_ref[:, HEAD_W:, :] = s_old * eg + bmm(ktail.astype(BF16), vb)


def _gdn_scan(q, k, kt, vk, gc, bt, gr, g_chunks=4):
    db, h, t, _ = q.shape
    ng = t // (g_chunks * CHUNK)
    blk = g_chunks * CHUNK
    tok = lambda w: pl.BlockSpec((None, h, blk, w), lambda d, n: (d, 0, n, 0))
    return pl.pallas_call(
        functools.partial(_gdn_scan_kernel, g_chunks=g_chunks),
        grid=(db, ng),
        in_specs=[tok(HEAD_W), tok(HEAD_W),
                  pl.BlockSpec((None, h, g_chunks, HEAD_W, CHUNK), lambda d, n: (d, 0, n, 0, 0)),
                  tok(2 * HEAD_W), tok(2 * HEAD_W), tok(2 * HEAD_W),
                  pl.BlockSpec((None, h, g_chunks, 8, CHUNK), lambda d, n: (d, 0, n, 0, 0))],
        out_specs=tok(HEAD_W),
        out_shape=jax.ShapeDtypeStruct((db, h, t, HEAD_W), F32),
        scratch_shapes=[pltpu.VMEM((h, 2 * HEAD_W, HEAD_W), F32),
                        pltpu.VMEM((g_chunks * h, CHUNK, 2 * HEAD_W), F32),
                        pltpu.VMEM((g_chunks * h, CHUNK, CHUNK), F32)],
        compiler_params=_cparams(("parallel", "arbitrary")),
        name="gdn_scan",
    )(q, k, kt, vk, gc, bt, gr)


def _gdn_finish_kernel(of_ref, ob_ref, z_ref, nw_ref, bseg_ref, y_ref):
    o = of_ref[...] + ob_ref[...]
    ms = _dot(o * o, bseg_ref[...], precision=HI) * (1.0 / HEAD_W)
    y = o * lax.rsqrt(ms + EPS) * nw_ref[...] * _silu(z_ref[...].astype(F32))
    y_ref[...] = y.astype(BF16)


def _gdn_finish(of, ob, z, nw_row, bseg):
    nt = of.shape[0] // TM
    full = lambda a: pl.BlockSpec(a.shape, lambda i: (0,) * a.ndim)
    return pl.pallas_call(
        _gdn_finish_kernel,
        grid=(nt,),
        in_specs=[pl.BlockSpec((TM, 256), lambda i: (i, 0)),
                  pl.BlockSpec((TM, 256), lambda i: (i, 0)),
                  pl.BlockSpec((TM, 256), lambda i: (i, C_Z // 256)),
                  full(nw_row), full(bseg)],
        out_specs=pl.BlockSpec((TM, 256), lambda i: (i, 0)),
        out_shape=jax.ShapeDtypeStruct((of.shape[0], 256), BF16),
        compiler_params=_cparams(("parallel",)),
        name="gdn_finish",
    )(of, ob, z, nw_row, bseg)


def _s5_param_kernel(are_r, aim_r, ldt_r, are_c, aim_c, ldt_c, are2_r, aim2_r, ldt2_r,
                     bt_re, bt_im, ct_re, ct_im, erow_ref, ecol_ref, m_ref, win_ref, wout_ref, lbl_ref):
    ell = CHUNK
    gch = S5_GROUP_CH
    p = S5_STATE

    def disc(a_re, a_im, log_dt):
        lam_re = jnp.minimum(a_re, -1e-4)
        dt = jnp.exp(log_dt)
        return lam_re, a_im, dt

    def power(lam_re, lam_im, dt, tau):
        mag = jnp.exp(lam_re * dt * tau)
        ang = lam_im * dt * tau
        return mag * jnp.cos(ang), mag * jnp.sin(ang)

    def zoh(lam_re, lam_im, dt):
        lb_re, lb_im = power(lam_re, lam_im, dt, 1.0)
        den = lam_re * lam_re + lam_im * lam_im
        f_re = ((lb_re - 1.0) * lam_re + lb_im * lam_im) / den
        f_im = (lb_im * lam_re - (lb_re - 1.0) * lam_im) / den
        return f_re, f_im

    lr, li, dtr = disc(are_r[...], aim_r[...], ldt_r[...])
    f_re, f_im = zoh(lr, li, dtr)
    bbt_re = f_re * bt_re[...] - f_im * bt_im[...]
    bbt_im = f_re * bt_im[...] + f_im * bt_re[...]
    tau_in = (ell - 1 - lax.broadcasted_iota(jnp.int32, (ell, 1), 0)).astype(F32)
    pin_re, pin_im = power(lr, li, dtr, tau_in)
    pin_re = _dot(erow_ref[...], pin_re, precision=HI)
    pin_im = _dot(erow_ref[...], pin_im, precision=HI)
    win_re = pin_re * bbt_re - pin_im * bbt_im
    win_im = pin_re * bbt_im + pin_im * bbt_re
    win_ref[...] = jnp.concatenate([win_re, win_im], axis=-1).astype(BF16)
    bbt_re = bbt_re[:gch]
    bbt_im = bbt_im[:gch]

    lc, lic, dtc = disc(are_c[...], aim_c[...], ldt_c[...])
    tau_l = lax.broadcasted_iota(jnp.int32, (1, ell), 1).astype(F32)
    pw_re, pw_im = power(lc, lic, dtc, tau_l)
    pw_re = _dot(pw_re, ecol_ref[...], precision=HI)
    pw_im = _dot(pw_im, ecol_ref[...], precision=HI)
    g_re = ct_re[...] * pw_re - ct_im[...] * pw_im
    g_im = ct_re[...] * pw_im + ct_im[...] * pw_re
    kc = _dot(bbt_re, g_re, precision=HI) - _dot(bbt_im, g_im, precision=HI)
    lane = lax.broadcasted_iota(jnp.int32, (gch, ell * gch), 1)
    for i in range(ell):
        rolled = kc if i == 0 else pltpu.roll(kc, i * gch, axis=1)
        m_ref[i * gch:(i + 1) * gch, :] = jnp.where(lane >= i * gch, rolled, 0.0).astype(BF16)
    lb_re, lb_im = power(lc, lic, dtc, 1.0)
    wo_re = g_re * lb_re - g_im * lb_im
    wo_im = g_re * lb_im + g_im * lb_re
    wout_ref[...] = jnp.concatenate([wo_re, -wo_im], axis=0).astype(BF16)

    l2, li2, dt2 = disc(are2_r[...], aim2_r[...], ldt2_r[...])
    pl_re, pl_im = power(l2, li2, dt2, float(ell))
    lane2 = lax.broadcasted_iota(jnp.int32, (1, 2 * p), 1)
    lbl_ref[0:1, :] = pl_re
    lbl_ref[1:2, :] = jnp.where(lane2 < p, -pl_im, pl_im)


def _s5_params(a_re, a_im, log_dt, b_re, b_im, c_re, c_im):
    g, p, gch, ell = S5_GROUPS, S5_STATE, S5_GROUP_CH, CHUNK
    ldt = jnp.broadcast_to(log_dt[:, :, None], (2, g, p))
    row = lambda a: a.reshape(2, g, 1, p)
    col = lambda a: a.reshape(2, g, p, 1)
    row2 = lambda a: jnp.concatenate([a, a], axis=-1).reshape(2, g, 1, 2 * p)
    bt_re = jnp.tile(jnp.swapaxes(b_re, 1, 2).astype(F32), (1, ell, 1))
    bt_im = jnp.tile(jnp.swapaxes(b_im, 1, 2).astype(F32), (1, ell, 1))
    ct_re = jnp.tile(jnp.swapaxes(c_re, 2, 3).astype(F32), (1, 1, 1, ell))
    ct_im = jnp.tile(jnp.swapaxes(c_im, 2, 3).astype(F32), (1, 1, 1, ell))
    dg = lambda *shape: pl.BlockSpec((None, None) + shape, lambda d, gi: (d, gi) + (0,) * len(shape))
    gonly = lambda *shape: pl.BlockSpec((None,) + shape, lambda d, gi: (gi,) + (0,) * len(shape))
    full = lambda a: pl.BlockSpec(a.shape, lambda d, gi: (0,) * a.ndim)
    n = ell * gch
    erow = jnp.asarray((np.arange(n)[:, None] // gch == np.arange(ell)[None, :]).astype(np.float32))
    ecol = erow.T
    a_re, a_im = a_re.astype(F32), a_im.astype(F32)
    return pl.pallas_call(
        _s5_param_kernel,
        grid=(2, g),
        in_specs=[dg(1, p)] * 3 + [dg(p, 1)] * 3 + [dg(1, 2 * p)] * 3
                 + [gonly(n, p)] * 2 + [dg(p, n)] * 2 + [full(erow), full(ecol)],
        out_specs=[dg(n, n), dg(n, 2 * p), dg(2 * p, n), dg(2, 2 * p)],
        out_shape=[jax.ShapeDtypeStruct((2, g, n, n), BF16),
                   jax.ShapeDtypeStruct((2, g, n, 2 * p), BF16),
                   jax.ShapeDtypeStruct((2, g, 2 * p, n), BF16),
                   jax.ShapeDtypeStruct((2, g, 2, 2 * p), F32)],
        compiler_params=_cparams(("parallel", "parallel")),
        name="s5_params",
    )(row(a_re), row(a_im), row(ldt), col(a_re), col(a_im), col(ldt), row2(a_re), row2(a_im), row2(ldt),
      bt_re, bt_im, ct_re, ct_im, erow, ecol)


def _s5_scan_kernel(u_ref, m_ref, win_ref, wout_ref, lbl_ref, y_ref, xin_ref, xprev_ref, *, nb, nc):
    u = u_ref[...]
    xin_ref[...] = _dot(u, win_ref[...])
    a1 = lbl_ref[0:1, :]
    a2 = lbl_ref[1:2, :]

    def step(n, xs):
        new = []
        for b in range(nb):
            x = xs[b]
            xprev_ref[pl.ds(b * nc + n, 1), :] = x[0:1]
            new.append(a1 * x + a2 * pltpu.roll(x, S5_STATE, axis=1) + xin_ref[pl.ds(b * nc + n, 1), :])
        return tuple(new)

    lax.fori_loop(0, nc, step, tuple(jnp.zeros((8, 2 * S5_STATE), F32) for _ in range(nb)))
    y_ref[...] = _dot(u, m_ref[...]) + _dot(xprev_ref[...].astype(BF16), wout_ref[...])


def _s5_scan(uc, m, win, wout, lbl, nb):
    _, g, rows, n = uc.shape
    p2 = 2 * S5_STATE
    dg = lambda *shape: pl.BlockSpec((None, None) + shape, lambda d, gi: (d, gi) + (0,) * len(shape))
    return pl.pallas_call(
        functools.partial(_s5_scan_kernel, nb=nb, nc=rows // nb),
        grid=(2, g),
        in_specs=[dg(rows, n), dg(n, n), dg(n, p2), dg(p2, n), dg(2, p2)],
        out_specs=dg(rows, n),
        out_shape=jax.ShapeDtypeStruct((2, g, rows, n), F32),
        scratch_shapes=[pltpu.VMEM((rows, p2), F32), pltpu.VMEM((rows, p2), F32)],
        compiler_params=_cparams(("parallel", "parallel")),
        name="s5_scan",
    )(uc, m, win, wout, lbl)


def _s5_finish_kernel(yf_ref, yb_ref, u_ref, d_ref, w_ref, b_ref, o_ref):
    y = yf_ref[...] + yb_ref[...] + d_ref[...] * u_ref[...].astype(F32)
    y = jax.nn.gelu(y)
    gate = _sigmoid(_dot(y.astype(BF16), w_ref[...]) + b_ref[...])
    o_ref[...] = (y * gate).astype(BF16)


def _s5_finish(yf, yb, z, d_row, glu_w, glu_b_row):
    nt = yf.shape[0] // TM
    full = lambda a: pl.BlockSpec(a.shape, lambda i: (0,) * a.ndim)
    return pl.pallas_call(
        _s5_finish_kernel,
        grid=(nt,),
        in_specs=[pl.BlockSpec((TM, 256), lambda i: (i, 0)),
                  pl.BlockSpec((TM, 256), lambda i: (i, 0)),
                  pl.BlockSpec((TM, 256), lambda i: (i, C_U // 256)),
                  full(d_row), full(glu_w), full(glu_b_row)],
        out_specs=pl.BlockSpec((TM, 256), lambda i: (i, 0)),
        out_shape=jax.ShapeDtypeStruct((yf.shape[0], 256), BF16),
        compiler_params=_cparams(("parallel",)),
        name="s5_finish",
    )(yf, yb, z, d_row, glu_w, glu_b_row)


def _merge_kernel(x_ref, y0_ref, y1_ref, y2_ref, y3_ref, gates_ref, wb_ref, wo_ref, g1_ref, gain_ref, o_ref):
    acc = jnp.zeros((TM, D_MODEL), F32)
    for bi, y_ref in enumerate((y0_ref, y1_ref, y2_ref, y3_ref)):
        proj = _dot(y_ref[...], wb_ref[bi])
        acc = acc + gates_ref[:, bi * D_MODEL:(bi + 1) * D_MODEL].astype(F32) * proj
    y = _dot(acc.astype(BF16), wo_ref[...])
    o_ref[...] = x_ref[...] + g1_ref[...] * _rms(y, gain_ref[...])


def _merge(x, ys, gates, wb, wo, g1, gain, layer, tps, nb):
    nt = x.shape[0] // TM
    row = lambda i: (jnp.minimum(i // tps, nb), 0, 0)
    tile = lambda w: pl.BlockSpec((TM, w), lambda i: (i, 0))
    return pl.pallas_call(
        _merge_kernel,
        grid=(nt,),
        in_specs=[tile(D_MODEL)] + [tile(BRANCH_W)] * 4 + [tile(4 * D_MODEL),
                  pl.BlockSpec((None, 4, BRANCH_W, D_MODEL), lambda i: (layer, 0, 0, 0)),
                  pl.BlockSpec((None, D_MODEL, D_MODEL), lambda i: (layer, 0, 0)),
                  pl.BlockSpec((None, 1, D_MODEL), row),
                  pl.BlockSpec((1, D_MODEL), lambda i: (0, 0))],
        out_specs=tile(D_MODEL),
        out_shape=jax.ShapeDtypeStruct(x.shape, F32),
        compiler_params=_cparams(("parallel",), 48),
        name="merge",
    )(x, *ys, gates, wb, wo, g1, gain)


def _mlp_kernel(x_ref, sc_ref, sh_ref, g2_ref, gin_ref, gout_ref, w1_ref, w2_ref, o_ref, h_ref, acc_ref):
    j = pl.program_id(1)

    @pl.when(j == 0)
    def _():
        h_ref[...] = _norm_mod(x_ref[...], gin_ref[...], sc_ref[...], sh_ref[...]).astype(BF16)
        acc_ref[...] = jnp.zeros(acc_ref.shape, F32)

    t = jnp.maximum(_dot(h_ref[...], w1_ref[...]), 0.0)
    acc_ref[...] += _dot((t * t).astype(BF16), w2_ref[...])

    @pl.when(j == pl.num_programs(1) - 1)
    def _():
        o_ref[...] = x_ref[...] + g2_ref[...] * _rms(acc_ref[...], gout_ref[...])


def _mlp(x, sc, sh, g2, gin, gout, w1, w2, layer, tps, nb, tf=1024):
    nt = x.shape[0] // TM
    row = lambda i, j: (jnp.minimum(i // tps, nb), 0, 0)
    vec = pl.BlockSpec((1, D_MODEL), lambda i, j: (0, 0))
    return pl.pallas_call(
        _mlp_kernel,
        grid=(nt, D_FF // tf),
        in_specs=[pl.BlockSpec((TM, D_MODEL), lambda i, j: (i, 0)),
                  pl.BlockSpec((None, 1, D_MODEL), row), pl.BlockSpec((None, 1, D_MODEL), row),
                  pl.BlockSpec((None, 1, D_MODEL), row), vec, vec,
                  pl.BlockSpec((None, D_MODEL, tf), lambda i, j: (layer, 0, j)),
                  pl.BlockSpec((None, tf, D_MODEL), lambda i, j: (layer, j, 0))],
        out_specs=pl.BlockSpec((TM, D_MODEL), lambda i, j: (i, 0)),
        out_shape=jax.ShapeDtypeStruct(x.shape, F32),
        scratch_shapes=[pltpu.VMEM((TM, D_MODEL), BF16), pltpu.VMEM((TM, D_MODEL), F32)],
        compiler_params=_cparams(("parallel", "arbitrary"), 48),
        name="mlp",
    )(x, sc, sh, g2, gin, gout, w1, w2)


def _pack_w_in(w_in):
    o_na, o_cq, o_ckv, o_gdn, o_z, o_a, o_b, o_u, o_gate = 0, 768, 1024, 1184, 1952, 2208, 2216, 2224, 2480
    idx = np.zeros(ZW, np.int32)
    keep = np.zeros(ZW, np.float32)

    def put(dst, src):
        idx[dst:dst + len(src)] = src
        keep[dst:dst + len(src)] = 1.0

    put(C_GDN, np.arange(o_gdn, o_gdn + 768))
    put(C_NA, np.arange(o_na, o_na + 768))
    put(C_CQ, np.arange(o_cq, o_cq + 256))
    put(C_CKV, np.arange(o_ckv, o_ckv + 160))
    put(C_CKV + L_A, np.arange(o_a, o_a + 8))
    put(C_CKV + L_B, np.arange(o_b, o_b + 8))
    put(C_CKV + L_KRS, o_ckv + MLA_KV_LORA + ROPE_SWAP)
    put(C_Z, np.arange(o_z, o_z + 256))
    put(C_U, np.arange(o_u, o_u + 256))
    small = (jnp.take(w_in, jnp.asarray(idx), axis=2) * jnp.asarray(keep)).astype(BF16)
    gates = w_in[:, :, o_gate:].astype(BF16)
    return small, gates


def _mla_weights(w_uq, w_ukv):
    depth = w_uq.shape[0]
    hq = MLA_NOPE + MLA_ROPE
    wq = w_uq.reshape(depth, -1, N_HEADS, hq)
    pad = lambda a, lo, hi: jnp.pad(a, ((0, 0), (0, 0), (0, 0), (lo, hi)))
    wq_ext = pad(wq, 0, 128 - hq).reshape(depth, -1, N_HEADS * 128)
    wq_sw = pad(wq[..., MLA_NOPE:][..., ROPE_SWAP], MLA_NOPE, 128 - hq).reshape(depth, -1, N_HEADS * 128)
    wkv = w_ukv.reshape(depth, -1, N_HEADS, MLA_NOPE + HEAD_W)
    wk = pad(wkv[..., :MLA_NOPE], 0, 128 - MLA_NOPE).reshape(depth, -1, N_HEADS * 128)
    wv = wkv[..., MLA_NOPE:].reshape(depth, -1, N_HEADS * HEAD_W)
    return [a.astype(BF16) for a in (wq_ext, wq_sw, wk, wv)]


def _rope_place_mats():
    p1 = np.zeros((256, N_HEADS * 128), np.float32)
    p2 = np.zeros((256, N_HEADS * 128), np.float32)
    for h in range(N_HEADS):
        for r in range(MLA_ROPE):
            p1[L_KR + r, h * 128 + MLA_NOPE + r] = 1.0
            p2[L_KRS + r, h * 128 + MLA_NOPE + r] = 1.0
    return jnp.asarray(p1, BF16), jnp.asarray(p2, BF16)


def _rope_tables(s):
    quarter = MLA_ROPE // 4
    inv_freq = ROPE_BASE ** (-jnp.arange(quarter, dtype=F32) / quarter)
    t = jnp.arange(s)
    ang_r = (t // GRID_W).astype(F32)[:, None] * inv_freq[None, :]
    ang_c = (t % GRID_W).astype(F32)[:, None] * inv_freq[None, :]
    cr, sr, cc, sn = jnp.cos(ang_r), jnp.sin(ang_r), jnp.cos(ang_c), jnp.sin(ang_c)
    cos = jnp.concatenate([jnp.ones((s, MLA_NOPE), F32), cr, cr, cc, cc, jnp.ones((s, 32), F32)], axis=1)
    sin = jnp.concatenate([jnp.zeros((s, MLA_NOPE), F32), -sr, sr, -sn, sn, jnp.zeros((s, 32), F32)], axis=1)
    cos = jnp.concatenate([cos, jnp.ones((TM, 128), F32)], axis=0)
    sin = jnp.concatenate([sin, jnp.zeros((TM, 128), F32)], axis=0)
    return cos, sin


def _head_block_ones():
    r = np.arange(BRANCH_W)
    return jnp.asarray((r[:, None] // HEAD_W == r[None, :] // HEAD_W).astype(np.float32))


def _gdn_expand_mats():
    e = np.zeros((2, 256, 2 * N_HEADS * 128), np.float32)
    for dh in range(2 * N_HEADS):
        e[0, L_A + dh, dh * 128:(dh + 1) * 128] = 1.0
        e[1, L_B + dh, dh * 128:(dh + 1) * 128] = 1.0
    return jnp.asarray(e)


def _lane_row(vals, offset, width=256):
    return jnp.zeros((1, width), F32).at[0, offset:offset + vals.shape[0]].set(vals.astype(F32))


def _scan_order(a, nb, s, ctx_len):
    lat = a[:nb * s].reshape((nb, s) + a.shape[1:])
    ctx = a[nb * s:].reshape((nb, ctx_len) + a.shape[1:])
    fwd = jnp.concatenate([ctx, lat], axis=1)
    bwd = jnp.concatenate([ctx[:, ::-1], lat[:, ::-1]], axis=1)
    return jnp.stack([fwd, bwd])


def _token_order(a, nb, s, ctx_len, flipped):
    ctx, lat = a[:, :ctx_len], a[:, ctx_len:]
    if flipped:
        ctx, lat = ctx[:, ::-1], lat[:, ::-1]
    return jnp.concatenate([lat.reshape((nb * s,) + a.shape[2:]), ctx.reshape((nb * ctx_len,) + a.shape[2:])], axis=0)


def _gdn_mixer(z, conv_w, a_log, dt_bias, norm_w, consts, nb, s, ctx_len):
    bseg, ef = consts["bseg"], consts["ef"]
    alog_row = _lane_row(a_log.reshape(-1), L_A)
    dtb_row = _lane_row(dt_bias.reshape(-1), L_A)
    q, k, v, gcx, btx, gcc = _gdn_prep(z, conv_w, alog_row, dtb_row, bseg, ef, s // TS, nb * s // TS)
    t = s + ctx_len
    nc = t // CHUNK
    h = N_HEADS

    def heads(a, w):
        return jnp.moveaxis(_scan_order(a.reshape(-1, h, w), nb, s, ctx_len), 3, 2)

    def pick_dir(a, w):
        a5 = jnp.moveaxis(_scan_order(a.reshape(-1, 2, h, w), nb, s, ctx_len), 4, 2)
        return jnp.stack([a5[0, :, :, :, 0], a5[1, :, :, :, 1]]).reshape(2 * nb, h, t, w)

    qs = heads(q, HEAD_W).reshape(2 * nb, h, t, HEAD_W)
    ks = heads(k, HEAD_W).reshape(2 * nb, h, t, HEAD_W)
    vs = heads(v, HEAD_W).reshape(2 * nb, h, t, HEAD_W)
    kt = jnp.swapaxes(ks.reshape(2 * nb, h, nc, CHUNK, HEAD_W), 3, 4)
    vk = jnp.concatenate([vs, ks], axis=-1)
    gcs = pick_dir(gcx, 128)
    bts = pick_dir(btx, 128)
    gr = jnp.broadcast_to(pick_dir(gcc[:, L_A:L_B], 1).reshape(2 * nb, h, nc, 1, CHUNK), (2 * nb, h, nc, 8, CHUNK))
    o = _gdn_scan(qs, ks, kt, vk, gcs, bts, gr)
    o = jnp.moveaxis(o.reshape(2, nb, h, t, HEAD_W), 2, 3).reshape(2, nb, t, h * HEAD_W)
    of = _token_order(o[0], nb, s, ctx_len, False)
    ob = _token_order(o[1], nb, s, ctx_len, True)
    return _gdn_finish(of, ob, z, jnp.tile(norm_w.astype(F32), h)[None, :], bseg)


def _s5_mixer(z, a_re, a_im, log_dt, b_re, b_im, c_re, c_im, d_skip, glu_w, glu_b, nb, s, ctx_len):
    m, win, wout, lbl = _s5_params(a_re, a_im, log_dt, b_re, b_im, c_re, c_im)
    g, gch = S5_GROUPS, S5_GROUP_CH
    t = s + ctx_len
    nc = t // CHUNK
    u = z[:, C_U:C_U + 256]
    us = _scan_order(u.reshape(-1, g, gch), nb, s, ctx_len)
    uc = jnp.transpose(us.reshape(2, nb, nc, CHUNK, g, gch), (0, 4, 1, 2, 3, 5)).reshape(2, g, nb * nc, CHUNK * gch)
    y = _s5_scan(uc, m, win, wout, lbl, nb)
    y = jnp.transpose(y.reshape(2, g, nb, nc, CHUNK, gch), (0, 2, 3, 4, 1, 5)).reshape(2, nb, t, g * gch)
    yf = _token_order(y[0], nb, s, ctx_len, False)
    yb = _token_order(y[1], nb, s, ctx_len, True)
    return _s5_finish(yf, yb, z, d_skip.astype(F32)[None, :], glu_w.astype(BF16), glu_b.astype(F32)[None, :])


def kernel(x, c, ctx, c_ctx, ada_w, ada_b, norm_gains, w_in, na_rpb, mla_q_norm, mla_kv_norm, mla_w_uq, mla_w_ukv, gdn_conv, gdn_a_log, gdn_dt_bias, gdn_norm, s5_a_re, s5_a_im, s5_log_dt, s5_b_re, s5_b_im, s5_c_re, s5_c_im, s5_d, s5_glu_w, s5_glu_b, w_branch, w_out, mlp_w1, mlp_w2):
    nb, s, d = x.shape
    ctx_len = ctx.shape[1]
    depth = ada_w.shape[0]
    assert d == D_MODEL and ctx_len == TS and nb * ctx_len == TM and s % TM == 0 and s // GRID_W >= 16
    nl = nb * s
    tps = s // TM

    xs = jnp.concatenate([x.reshape(nl, d), ctx.reshape(nb * ctx_len, d)], axis=0)
    cvec = jnp.zeros((8, d), F32).at[:nb].set(c).at[nb].set(c_ctx)
    mod = _modulation(cvec, ada_w, ada_b)
    mod = mod[:, :nb + 1].reshape(depth, nb + 1, 6, 1, d)

    w_small, w_gates = _pack_w_in(w_in)
    wq_ext, wq_sw, wk_ext, wv = _mla_weights(mla_w_uq, mla_w_ukv)
    p1, p2 = _rope_place_mats()
    cos_t, sin_t = _rope_tables(s)
    consts = {"bseg": _head_block_ones(), "ef": _gdn_expand_mats()}
    wb = w_branch.astype(BF16)
    wo = w_out.astype(BF16)
    w1 = mlp_w1.astype(BF16)
    w2 = mlp_w2.astype(BF16)
    gains = norm_gains.astype(F32)

    for l in range(depth):
        sh1, sc1, g1, sh2, sc2, g2 = [mod[l, :, i] for i in range(6)]
        z = _inproj(xs, sc1, sh1, gains[l, 0][None], w_small, l, tps, nb, False)
        gates = _inproj(xs, sc1, sh1, gains[l, 0][None], w_gates, l, tps, nb, True)

        bias = _na_bias_table(na_rpb[l].astype(F32), s // GRID_W)
        y_na = jnp.concatenate([
            _na_latent(z, bias, nb, s),
            _ctx_attention(z, z, z, C_NA // 256, C_NA // 256 + 1, C_NA // 256 + 2, 256, nl // TS, nb,
                           HEAD_W ** -0.5, "na_ctx")], axis=0)

        qm, km, vm = _mla_prep(z, cos_t, sin_t, mla_q_norm[l].astype(F32)[None], mla_kv_norm[l].astype(F32)[None],
                               wq_ext[l], wq_sw[l], wk_ext[l], wv[l], p1, p2, tps, nl // TM)
        y_mla = jnp.concatenate([
            _mla_latent(qm, km, vm, nb, s),
            _ctx_attention(qm, km, vm, 0, 0, 0, N_HEADS * 128, nl // TS, nb, 1.0, "mla_ctx")], axis=0)

        y_gdn = _gdn_mixer(z, gdn_conv[l].astype(F32), gdn_a_log[l], gdn_dt_bias[l], gdn_norm[l], consts,
                           nb, s, ctx_len)
        y_s5 = _s5_mixer(z, s5_a_re[l], s5_a_im[l], s5_log_dt[l], s5_b_re[l], s5_b_im[l], s5_c_re[l], s5_c_im[l],
                         s5_d[l], s5_glu_w[l], s5_glu_b[l], nb, s, ctx_len)

        xs = _merge(xs, (y_na, y_mla, y_gdn, y_s5), gates, wb, wo, g1, gains[l, 1][None], l, tps, nb)
        xs = _mlp(xs, sc2, sh2, g2, gains[l, 2][None], gains[l, 3][None], w1, w2, l, tps, nb)
    return xs[:nl].reshape(nb, s, d)
```

```python
import functools
import math

import numpy as np
import jax
import jax.numpy as jnp
from jax import lax
from jax.experimental import pallas as pl
from jax.experimental.pallas import tpu as pltpu

F32 = jnp.float32
BF16 = jnp.bfloat16
HI = lax.Precision.HIGHEST
EPS = 1e-6

D_MODEL = 1024
GRID_W = 64
NA_WIN_H = 8
NA_WIN_W = 16
N_HEADS = 4
HEAD_W = 64
BRANCH_W = 256
MLA_NOPE = 64
MLA_ROPE = 32
MLA_KV_LORA = 128
ROPE_BASE = 10000.0
GDN_CONV = 4
CHUNK = 64
S5_GROUPS = 16
S5_GROUP_CH = 16
S5_STATE = 64
D_FF = 4 * D_MODEL

TM = 512
TN = 512
TS = 256
NEG = -1e30

C_GDN = 0
C_NA = 768
C_CQ = 1536
C_CKV = 1792
C_Z = 2048
C_U = 2304
ZW = 2560
L_KR = 128
L_A = 160
L_B = 168
L_KRS = 176
ROPE_SWAP = np.concatenate([np.arange(8, 16), np.arange(0, 8), np.arange(24, 32), np.arange(16, 24)])


def _cparams(sem, vmem_mb=None):
    kw = dict(dimension_semantics=sem)
    if vmem_mb is not None:
        kw["vmem_limit_bytes"] = vmem_mb * 1024 * 1024
    return pltpu.CompilerParams(**kw)


def _dot(a, b, **kw):
    return jnp.dot(a, b, preferred_element_type=F32, **kw)


def _dot_nt(a, b):
    return lax.dot_general(a, b, (((1,), (1,)), ((), ())), preferred_element_type=F32)


def _sigmoid(x):
    return 1.0 / (1.0 + jnp.exp(-x))


def _silu(x):
    return x * _sigmoid(x)


def _mod_kernel(c_ref, w_ref, b_ref, o_ref):
    c = c_ref[...]
    o_ref[...] = _dot(_silu(c), w_ref[...], precision=HI) + b_ref[...]


def _modulation(cvec, ada_w, ada_b):
    depth, d, n = ada_w.shape
    tn = 1536
    return pl.pallas_call(
        _mod_kernel,
        grid=(depth, n // tn),
        in_specs=[pl.BlockSpec((8, d), lambda l, j: (0, 0)),
                  pl.BlockSpec((None, d, tn), lambda l, j: (l, 0, j)),
                  pl.BlockSpec((None, 1, tn), lambda l, j: (l, 0, j))],
        out_specs=pl.BlockSpec((None, 8, tn), lambda l, j: (l, 0, j)),
        out_shape=jax.ShapeDtypeStruct((depth, 8, n), F32),
        compiler_params=_cparams(("parallel", "parallel"), 40),
        name="modulation",
    )(cvec, ada_w, ada_b.reshape(depth, 1, n))


def _norm_mod(x, gain, sc, sh):
    r = lax.rsqrt(jnp.mean(x * x, axis=-1, keepdims=True) + EPS)
    return (x * r * gain) * (1.0 + sc) + sh


def _inproj_kernel(x_ref, sc_ref, sh_ref, gain_ref, w_ref, o_ref, h_ref, *, gate):
    @pl.when(pl.program_id(1) == 0)
    def _():
        h_ref[...] = _norm_mod(x_ref[...], gain_ref[...], sc_ref[...], sh_ref[...]).astype(BF16)

    acc = _dot(h_ref[...], w_ref[...])
    if gate:
        acc = _sigmoid(acc)
    o_ref[...] = acc.astype(BF16)


def _inproj(x, sc, sh, gain, w, layer, tps, nb, gate):
    nt = x.shape[0] // TM
    width = w.shape[-1]
    row = lambda i, j: (jnp.minimum(i // tps, nb), 0, 0)
    return pl.pallas_call(
        functools.partial(_inproj_kernel, gate=gate),
        grid=(nt, width // TN),
        in_specs=[pl.BlockSpec((TM, D_MODEL), lambda i, j: (i, 0)),
                  pl.BlockSpec((None, 1, D_MODEL), row),
                  pl.BlockSpec((None, 1, D_MODEL), row),
                  pl.BlockSpec((1, D_MODEL), lambda i, j: (0, 0)),
                  pl.BlockSpec((None, D_MODEL, TN), lambda i, j: (layer, 0, j))],
        out_specs=pl.BlockSpec((TM, TN), lambda i, j: (i, j)),
        out_shape=jax.ShapeDtypeStruct((x.shape[0], width), BF16),
        scratch_shapes=[pltpu.VMEM((TM, D_MODEL), BF16)],
        compiler_params=_cparams(("parallel", "arbitrary")),
        name="inproj_gates" if gate else "inproj",
    )(x, sc, sh, gain, w)


def _head_lane_mask(width, head_w, h):
    lane = lax.broadcasted_iota(jnp.int32, (1, width), 1)
    return (lane >= h * head_w) & (lane < (h + 1) * head_w)


def _na_build_bias(rpb_ref, bias_ref, r0, kb0, rows_total):
    w = GRID_W
    qc = lax.broadcasted_iota(jnp.int32, (w, 2 * w), 0)
    lane = lax.broadcasted_iota(jnp.int32, (w, 2 * w), 1)
    kc = lane % w
    cs = jnp.clip(qc - NA_WIN_W // 2, 0, w - NA_WIN_W)
    col_ok = (kc >= cs) & (kc < cs + NA_WIN_W)
    left = lane < w
    neg = jnp.full((w, 2 * w), NEG, F32)
    for h in range(N_HEADS):
        t = rpb_ref[h]
        toep = []
        for a in range(2 * NA_WIN_H - 1):
            row = jnp.broadcast_to(t[a:a + 1, :], (w, 2 * w))
            ra = pltpu.roll(row, 2 * w - (NA_WIN_W - 1), axis=1, stride=1, stride_axis=0)
            rb = pltpu.roll(ra, w, axis=1)
            toep.append((jnp.where(col_ok, ra, NEG), jnp.where(col_ok, rb, NEG)))
        for qr in range(8):
            rs = min(max(r0 + qr - NA_WIN_H // 2, 0), rows_total - NA_WIN_H)
            for kp in range(8):
                halves = []
                for side in range(2):
                    kr = kb0 + 2 * kp + side
                    halves.append(toep[kr - (r0 + qr) + NA_WIN_H - 1][side] if rs <= kr < rs + NA_WIN_H else neg)
                bias_ref[h, qr * w:(qr + 1) * w, kp * 2 * w:(kp + 1) * 2 * w] = jnp.where(left, halves[0], halves[1])


def _na_kernel(q_ref, k_ref, v_ref, kc_ref, vc_ref, rpb_ref, o_ref, bias_ref, *, rows_total):
    i = pl.program_id(1)
    last = pl.num_programs(1) - 1

    @pl.when(i == 0)
    def _():
        _na_build_bias(rpb_ref, bias_ref, 0, 0, rows_total)

    @pl.when(i == 1)
    def _():
        _na_build_bias(rpb_ref, bias_ref, 8, 4, rows_total)

    @pl.when(i == last)
    def _():
        _na_build_bias(rpb_ref, bias_ref, rows_total - 8, rows_total - 16, rows_total)

    kb = jnp.clip(2 * i - 1, 0, rows_total // 4 - 4)
    start = pl.multiple_of(kb * (4 * GRID_W), 4 * GRID_W)
    nk = 2 * NA_WIN_H * GRID_W
    q = q_ref[...]
    kw = k_ref[pl.ds(start, nk), :]
    vw = v_ref[pl.ds(start, nk), :]
    kc = kc_ref[...]
    vc = vc_ref[...]
    scale = HEAD_W ** -0.5
    out = jnp.zeros(q.shape, F32)
    for h in range(N_HEADS):
        hm = _head_lane_mask(BRANCH_W, HEAD_W, h)
        qh = jnp.where(hm, q, jnp.zeros_like(q))
        sb = _dot_nt(qh, kw) * scale + bias_ref[h]
        sc = _dot_nt(qh, kc) * scale
        m = jnp.maximum(jnp.max(sb, axis=-1, keepdims=True), jnp.max(sc, axis=-1, keepdims=True))
        pb = jnp.exp(sb - m)
        pc = jnp.exp(sc - m)
        den = jnp.sum(pb, axis=-1, keepdims=True) + jnp.sum(pc, axis=-1, keepdims=True)
        o = _dot(pb.astype(BF16), vw) + _dot(pc.astype(BF16), vc)
        out = jnp.where(hm, o / den, out)
    o_ref[...] = out.astype(BF16)


def _na_latent(z, rpb, nb, s):
    rows_total = s // GRID_W
    qb = 8 * GRID_W
    nq = s // qb
    nl = nb * s
    rpb = jnp.pad(rpb.astype(F32), ((0, 0), (0, 1), (0, 2 * GRID_W - (2 * NA_WIN_W - 1))))
    return pl.pallas_call(
        functools.partial(_na_kernel, rows_total=rows_total),
        grid=(nb, nq),
        in_specs=[pl.BlockSpec((qb, BRANCH_W), lambda b, i: (b * nq + i, C_NA // 256)),
                  pl.BlockSpec((s, BRANCH_W), lambda b, i: (b, C_NA // 256 + 1)),
                  pl.BlockSpec((s, BRANCH_W), lambda b, i: (b, C_NA // 256 + 2)),
                  pl.BlockSpec((TS, BRANCH_W), lambda b, i: (nl // TS + b, C_NA // 256 + 1)),
                  pl.BlockSpec((TS, BRANCH_W), lambda b, i: (nl // TS + b, C_NA // 256 + 2)),
                  pl.BlockSpec(rpb.shape, lambda b, i: (0, 0, 0))],
        out_specs=pl.BlockSpec((qb, BRANCH_W), lambda b, i: (b * nq + i, 0)),
        out_shape=jax.ShapeDtypeStruct((nl, BRANCH_W), BF16),
        scratch_shapes=[pltpu.VMEM((N_HEADS, qb, 2 * qb), F32)],
        compiler_params=_cparams(("parallel", "arbitrary"), 56),
        name="na_latent",
    )(z, z, z, z, z, rpb)


def _ctx_attn_kernel(q_ref, k_ref, v_ref, o_ref, *, scale):
    q = q_ref[...]
    k = k_ref[...]
    v = v_ref[...]
    qw = q.shape[-1]
    out = jnp.zeros((q.shape[0], BRANCH_W), F32)
    for h in range(N_HEADS):
        qh = jnp.where(_head_lane_mask(qw, qw // N_HEADS, h), q, jnp.zeros_like(q))
        s = _dot_nt(qh, k) * scale
        m = jnp.max(s, axis=-1, keepdims=True)
        p = jnp.exp(s - m)
        den = jnp.sum(p, axis=-1, keepdims=True)
        o = _dot(p.astype(BF16), v)
        out = jnp.where(_head_lane_mask(BRANCH_W, HEAD_W, h), o / den, out)
    o_ref[...] = out.astype(BF16)


def _ctx_attention(q, k, v, qcol, kcol, vcol, qw, row0, nb, scale, name):
    return pl.pallas_call(
        functools.partial(_ctx_attn_kernel, scale=scale),
        grid=(nb,),
        in_specs=[pl.BlockSpec((TS, qw), lambda b: (row0 + b, qcol)),
                  pl.BlockSpec((TS, qw), lambda b: (row0 + b, kcol)),
                  pl.BlockSpec((TS, BRANCH_W), lambda b: (row0 + b, vcol))],
        out_specs=pl.BlockSpec((TS, BRANCH_W), lambda b: (b, 0)),
        out_shape=jax.ShapeDtypeStruct((nb * TS, BRANCH_W), BF16),
        compiler_params=_cparams(("parallel",)),
        name=name,
    )(q, k, v)


def _rms(x, gain):
    return x * lax.rsqrt(jnp.mean(x * x, axis=-1, keepdims=True) + EPS) * gain


def _mla_prep_kernel(cq_ref, ckv_ref, cos_ref, sin_ref, qn_ref, kvn_ref, wq_ref, wqs_ref, wk_ref, wv_ref,
                     p1_ref, p2_ref, q_ref, k_ref, v_ref):
    cos = jnp.concatenate([cos_ref[...]] * N_HEADS, axis=-1)
    sin = jnp.concatenate([sin_ref[...]] * N_HEADS, axis=-1)
    cqn = _rms(cq_ref[...].astype(F32), qn_ref[...]).astype(BF16)
    scale = (MLA_NOPE + MLA_ROPE) ** -0.5
    q = _dot(cqn, wq_ref[...]) * cos + _dot(cqn, wqs_ref[...]) * sin
    q_ref[...] = (q * scale).astype(BF16)
    ckv = ckv_ref[...]
    kvn = _rms(ckv[:, :MLA_KV_LORA].astype(F32), kvn_ref[...]).astype(BF16)
    k = (_dot(kvn, wk_ref[...]) + _dot(ckv, p1_ref[...])) * cos + _dot(ckv, p2_ref[...]) * sin
    k_ref[...] = k.astype(BF16)
    v_ref[...] = _dot(kvn, wv_ref[...]).astype(BF16)


def _mla_prep(z, cos_t, sin_t, qn, kvn, wq, wqs, wk, wv, p1, p2, tps, n_lat_tiles):
    nt_rows = z.shape[0]
    nt = nt_rows // TM
    full = lambda a: pl.BlockSpec(a.shape, lambda i: (0,) * a.ndim)
    tab = lambda i: (jnp.where(i < n_lat_tiles, i % tps, tps), 0)
    hw = N_HEADS * 128
    return pl.pallas_call(
        _mla_prep_kernel,
        grid=(nt,),
        in_specs=[pl.BlockSpec((TM, 256), lambda i: (i, C_CQ // 256)),
                  pl.BlockSpec((TM, 256), lambda i: (i, C_CKV // 256)),
                  pl.BlockSpec((TM, 128), tab), pl.BlockSpec((TM, 128), tab),
                  full(qn), full(kvn), full(wq), full(wqs), full(wk), full(wv), full(p1), full(p2)],
        out_specs=[pl.BlockSpec((TM, hw), lambda i: (i, 0)),
                   pl.BlockSpec((TM, hw), lambda i: (i, 0)),
                   pl.BlockSpec((TM, BRANCH_W), lambda i: (i, 0))],
        out_shape=[jax.ShapeDtypeStruct((nt_rows, hw), BF16),
                   jax.ShapeDtypeStruct((nt_rows, hw), BF16),
                   jax.ShapeDtypeStruct((nt_rows, BRANCH_W), BF16)],
        compiler_params=_cparams(("parallel",)),
        name="mla_prep",
    )(z, z, cos_t, sin_t, qn, kvn, wq, wqs, wk, wv, p1, p2)


def _flash_kernel(q_ref, kl_ref, vl_ref, kc_ref, vc_ref, o_ref, *, tk, n_lat):
    tq = q_ref.shape[0]
    qs = [q_ref[:, 128 * hh:128 * (hh + 1)] for hh in range(2)]

    def update(carry, ks, v):
        new = []
        for hh in range(2):
            m, l, acc = carry[hh]
            s = _dot_nt(qs[hh], ks[hh])
            m_new = jnp.maximum(m, jnp.max(s, axis=-1, keepdims=True))
            a = jnp.exp(m - m_new)
            p = jnp.exp(s - m_new)
            l = a * l + jnp.sum(p, axis=-1, keepdims=True)
            acc = a * acc + _dot(p.astype(BF16), v)
            new.append((m_new, l, acc))
        return tuple(new)

    def body(t, carry):
        r0 = pl.multiple_of(t * tk, tk)
        ks = [kl_ref[pl.ds(r0, tk), 128 * hh:128 * (hh + 1)] for hh in range(2)]
        return update(carry, ks, vl_ref[pl.ds(r0, tk), :])

    init = tuple((jnp.full((tq, 1), NEG, F32), jnp.zeros((tq, 1), F32), jnp.zeros((tq, 128), F32))
                 for _ in range(2))
    carry = lax.fori_loop(0, n_lat, body, init)
    carry = update(carry, [kc_ref[:, 128 * hh:128 * (hh + 1)] for hh in range(2)], vc_ref[...])
    lane = lax.broadcasted_iota(jnp.int32, (1, 128), 1)
    o = jnp.where(lane < HEAD_W, carry[0][2] / carry[0][1], carry[1][2] / carry[1][1])
    o_ref[...] = o.astype(BF16)


def _mla_latent(qm, km, vm, nb, s, tq=512, tk=512):
    nq = s // tq
    nl = nb * s
    return pl.pallas_call(
        functools.partial(_flash_kernel, tk=tk, n_lat=s // tk),
        grid=(nb, 2, nq),
        in_specs=[pl.BlockSpec((tq, 256), lambda b, hp, i: (b * nq + i, hp)),
                  pl.BlockSpec((s, 256), lambda b, hp, i: (b, hp)),
                  pl.BlockSpec((s, 128), lambda b, hp, i: (b, hp)),
                  pl.BlockSpec((TS, 256), lambda b, hp, i: (nl // TS + b, hp)),
                  pl.BlockSpec((TS, 128), lambda b, hp, i: (nl // TS + b, hp))],
        out_specs=pl.BlockSpec((tq, 128), lambda b, hp, i: (b * nq + i, hp)),
        out_shape=jax.ShapeDtypeStruct((nl, BRANCH_W), BF16),
        compiler_params=_cparams(("parallel", "parallel", "arbitrary"), 48),
        name="mla_flash",
    )(qm, km, vm, km, vm)


def _gdn_prep_kernel(prev_ref, cur_ref, next_ref, ckv_ref, conv_ref, alog_ref, dtb_ref, bseg_ref, ef_ref, sel_ref,
                     qk_ref, kk_ref, vk_ref, gcx_ref, btx_ref, gt_ref, *, tps, n_lat_tiles):
    i = pl.program_id(0)
    is_ctx = i >= n_lat_tiles
    first = is_ctx | (i % tps == 0)
    last = is_ctx | (i % tps == tps - 1)
    prev = jnp.where(first, 0.0, prev_ref[...].astype(F32))
    nxt = jnp.where(last, 0.0, next_ref[...].astype(F32))
    ext = jnp.concatenate([prev, cur_ref[...].astype(F32), nxt], axis=0)
    n_ext = TS + 16
    acc = jnp.zeros((TS, 3 * BRANCH_W), F32)
    for j in range(GDN_CONV):
        shifted = pltpu.roll(ext, n_ext - (8 - GDN_CONV // 2 + j), axis=0)[:TS]
        acc = acc + shifted * conv_ref[j:j + 1, :]
    x = _silu(acc)
    bseg = bseg_ref[...]

    def l2n(a):
        return a * lax.rsqrt(_dot(a * a, bseg, precision=HI) + EPS)

    q = l2n(x[:, :BRANCH_W]) * (HEAD_W ** -0.5)
    k = l2n(x[:, BRANCH_W:2 * BRANCH_W])
    v = x[:, 2 * BRANCH_W:]
    for h in range(N_HEADS):
        hs = slice(h * HEAD_W, (h + 1) * HEAD_W)
        qk_ref[h] = jnp.concatenate([q[:, hs], k[:, hs]], axis=-1)
        kk_ref[h] = jnp.concatenate([k[:, hs], k[:, hs]], axis=-1)
        vk_ref[h] = jnp.concatenate([v[:, hs], k[:, hs]], axis=-1)

    ab = ckv_ref[...].astype(F32)
    sp_in = ab + dtb_ref[...]
    softplus = jnp.maximum(sp_in, 0.0) + jnp.log1p(jnp.exp(-jnp.abs(sp_in)))
    lane = lax.broadcasted_iota(jnp.int32, (1, 256), 1)
    g = jnp.where((lane >= L_A) & (lane < L_B), -jnp.exp(alog_ref[...]) * softplus, 0.0)
    beta = jnp.where((lane >= L_B) & (lane < L_KRS), _sigmoid(ab), 0.0)
    r = lax.broadcasted_iota(jnp.int32, (TS, TS), 0)
    c = lax.broadcasted_iota(jnp.int32, (TS, TS), 1)
    same = (r // CHUNK) == (c // CHUNK)
    lower = jnp.where(same & (r >= c), 1.0, 0.0)
    upper = jnp.where(same & (r <= c), 1.0, 0.0)
    fwd_lane = (lane >= L_A) & (lane < L_A + N_HEADS)
    gc = jnp.where(fwd_lane, _dot(lower, g, precision=HI), _dot(upper, g, precision=HI))
    gt_ref[...] = lax.dot_general(sel_ref[...], gc, (((1,), (1,)), ((), ())), precision=HI,
                                  preferred_element_type=F32)
    gcx_ref[...] = _dot(gc, ef_ref[0], precision=HI)
    btx_ref[...] = _dot(beta, ef_ref[1], precision=HI)


def _gdn_prep(z, conv_w, alog_row, dtb_row, bseg, ef, sel, tps_s, n_lat_tiles):
    nt_rows = z.shape[0]
    nt = nt_rows // TS
    hb = TS // 8
    full = lambda a: pl.BlockSpec(a.shape, lambda i: (0,) * a.ndim)
    xw = 2 * N_HEADS * 128
    hsp = pl.BlockSpec((N_HEADS, TS, 128), lambda i: (0, i, 0))
    return pl.pallas_call(
        functools.partial(_gdn_prep_kernel, tps=tps_s, n_lat_tiles=n_lat_tiles),
        grid=(nt,),
        in_specs=[pl.BlockSpec((8, 768), lambda i: (jnp.maximum(i * hb - 1, 0), 0)),
                  pl.BlockSpec((TS, 768), lambda i: (i, 0)),
                  pl.BlockSpec((8, 768), lambda i: (jnp.minimum((i + 1) * hb, nt * hb - 1), 0)),
                  pl.BlockSpec((TS, 256), lambda i: (i, C_CKV // 256)),
                  full(conv_w), full(alog_row), full(dtb_row), full(bseg), full(ef), full(sel)],
        out_specs=[hsp] * 3
                  + [pl.BlockSpec((TS, xw), lambda i: (i, 0))] * 2
                  + [pl.BlockSpec((None, 8, TS), lambda i: (i, 0, 0))],
        out_shape=[jax.ShapeDtypeStruct((N_HEADS, nt_rows, 128), F32)] * 3
                  + [jax.ShapeDtypeStruct((nt_rows, xw), F32)] * 2
                  + [jax.ShapeDtypeStruct((nt, 8, TS), F32)],
        compiler_params=_cparams(("parallel",)),
        name="gdn_prep",
    )(z, z, z, z, conv_w, alog_row, dtb_row, bseg, ef, sel)


def _tri_solve(n, x, reverse):
    nblk = CHUNK // 8
    xb = [x[:, 8 * r:8 * (r + 1), :] for r in range(nblk)]
    nb_ = [n[:, 8 * r:8 * (r + 1), :] for r in range(nblk)]
    steps = range(CHUNK - 1, 0, -1) if reverse else range(CHUNK - 1)
    for j in steps:
        xj = xb[j // 8][:, j % 8:j % 8 + 1, :]
        blocks = range(0, j // 8 + 1) if reverse else range(j // 8, nblk)
        for rb in blocks:
            xb[rb] = xb[rb] - nb_[rb][:, :, j:j + 1] * xj
    return jnp.concatenate(xb, axis=1)


def _gdn_direction(qk_ref, kk_ref, vk_ref, gc_ref, bt_ref, gt_ref, o_ref, s_ref, d, reverse):
    c = CHUNK
    nh = N_HEADS
    r = lax.broadcasted_iota(jnp.int32, (c, c), 0)
    cc = lax.broadcasted_iota(jnp.int32, (c, c), 1)
    incl = ((r <= cc) if reverse else (r >= cc))[None]
    strict = ((r < cc) if reverse else (r > cc))[None]
    lo = lax.broadcasted_iota(jnp.int32, (1, 1, 2 * HEAD_W), 2) < HEAD_W
    bmm = lambda a, b: jnp.einsum("hij,hjk->hik", a, b, preferred_element_type=F32)
    bmm_nt = lambda a, b: jnp.einsum("hid,hjd->hij", a, b, preferred_element_type=F32)
    bmm_tn = lambda a, b: jnp.einsum("hcd,hce->hde", a, b, preferred_element_type=F32)
    n_chunks = TS // c
    for ci in (range(n_chunks - 1, -1, -1) if reverse else range(n_chunks)):
        sl = slice(ci * c, (ci + 1) * c)
        qk = qk_ref[:, sl, :]
        kk = kk_ref[:, sl, :]
        vk = vk_ref[:, sl, :]
        gc = jnp.stack([gc_ref[sl, 128 * h:128 * (h + 1)] for h in range(nh)])
        bt = jnp.stack([bt_ref[sl, 128 * h:128 * (h + 1)] for h in range(nh)])
        grow = jnp.stack([gt_ref[d * nh + h:d * nh + h + 1, sl] for h in range(nh)])
        dec = jnp.where(incl, jnp.exp(jnp.minimum(gc[:, :, :c] - grow, 0.0)), 0.0)
        kkb = kk.astype(BF16)
        k_dot_k = 0.5 * bmm_nt(kkb, kkb)
        q_only = jnp.where(lo, qk, 0.0)
        q_dot_k = bmm_nt(q_only.astype(BF16), kkb) * dec
        n = jnp.where(strict, bt[:, :, :c] * k_dot_k * dec, 0.0)
        egc = jnp.exp(gc)
        x = _tri_solve(n, vk * jnp.where(lo, bt, bt * egc), reverse)
        edge = 0 if reverse else c - 1
        glast = gc[:, edge:edge + 1, :]
        ktail = kk * jnp.exp(glast - gc)
        s2 = s_ref[d]
        s2b = s2.astype(BF16)
        v_new = x[:, :, :HEAD_W] - bmm(jnp.where(lo, 0.0, x).astype(BF16), s2b)
        vb = v_new.astype(BF16)
        o_ref[:, sl, :] = bmm((q_only * egc).astype(BF16), s2b) + bmm(q_dot_k.astype(BF16), vb)
        s_ref[d] = s2 * jnp.exp(glast)[:, :, :HEAD_W] + bmm_tn(ktail.astype(BF16), vb)


def _gdn_scan_kernel(qkf, qkb, kkf, kkb, vkf, vkb, gcf, gcb, btf, btb, gtf, gtb, of_ref, ob_ref, s_ref):
    @pl.when(pl.program_id(1) == 0)
    def _():
        s_ref[...] = jnp.zeros(s_ref.shape, F32)

    _gdn_direction(qkf, kkf, vkf, gcf, btf, gtf, of_ref, s_ref, 0, False)
    _gdn_direction(qkb, kkb, vkb, gcb, btb, gtb, ob_ref, s_ref, 1, True)


def _gdn_scan(qk, kk, vk, gcx, btx, gt, nb, s):
    h, nt_rows, _ = qk.shape
    tps = s // TS
    nlt = nb * tps
    fwd = lambda b, n: jnp.where(n == 0, nlt + b, b * tps + n - 1)
    bwd = lambda b, n: jnp.where(n == 0, nlt + b, b * tps + tps - n)
    hw = h * 128

    def views(shape, imap):
        return [pl.BlockSpec(shape, functools.partial(imap, t)) for t in (fwd, bwd)]

    heads = views((h, TS, 128), lambda t, b, n: (0, t(b, n), 0))
    lanes = [pl.BlockSpec((TS, hw), lambda b, n: (fwd(b, n), 0)), pl.BlockSpec((TS, hw), lambda b, n: (bwd(b, n), 1))]
    rows = views((None, 8, TS), lambda t, b, n: (t(b, n), 0, 0))
    outs = views((h, TS, HEAD_W), lambda t, b, n: (0, t(b, n), 0))
    return pl.pallas_call(
        _gdn_scan_kernel,
        grid=(nb, tps + 1),
        in_specs=heads * 3 + lanes * 2 + rows,
        out_specs=outs,
        out_shape=[jax.ShapeDtypeStruct((h, nt_rows, HEAD_W), F32)] * 2,
        scratch_shapes=[pltpu.VMEM((2, h, 2 * HEAD_W, HEAD_W), F32)],
        compiler_params=_cparams(("parallel", "arbitrary")),
        name="gdn_scan",
    )(qk, qk, kk, kk, vk, vk, gcx, gcx, btx, btx, gt, gt)


def _gdn_finish_kernel(of_ref, ob_ref, z_ref, nw_ref, y_ref):
    o = of_ref[...] + ob_ref[...]
    y = o * lax.rsqrt(jnp.mean(o * o, axis=-1, keepdims=True) + EPS) * nw_ref[...]
    y = jnp.concatenate([y[h] for h in range(N_HEADS)], axis=-1)
    y_ref[...] = (y * _silu(z_ref[...].astype(F32))).astype(BF16)


def _gdn_finish(of, ob, z, nw_row):
    nt = of.shape[1] // TM
    hsp = pl.BlockSpec((N_HEADS, TM, HEAD_W), lambda i: (0, i, 0))
    return pl.pallas_call(
        _gdn_finish_kernel,
        grid=(nt,),
        in_specs=[hsp, hsp, pl.BlockSpec((TM, 256), lambda i: (i, C_Z // 256)),
                  pl.BlockSpec(nw_row.shape, lambda i: (0, 0))],
        out_specs=pl.BlockSpec((TM, 256), lambda i: (i, 0)),
        out_shape=jax.ShapeDtypeStruct((of.shape[1], 256), BF16),
        compiler_params=_cparams(("parallel",)),
        name="gdn_finish",
    )(of, ob, z, nw_row)


def _s5_param_kernel(are_r, aim_r, ldt_r, are_c, aim_c, ldt_c, are2_r, aim2_r, ldt2_r,
                     bt_re, bt_im, ct_re, ct_im, erow_ref, ecol_ref, m_ref, win_ref, wout_ref, lbl_ref):
    ell = CHUNK
    gch = S5_GROUP_CH
    p = S5_STATE

    def disc(a_re, a_im, log_dt):
        lam_re = jnp.minimum(a_re, -1e-4)
        dt = jnp.exp(log_dt)
        return lam_re, a_im, dt

    def power(lam_re, lam_im, dt, tau):
        mag = jnp.exp(lam_re * dt * tau)
        ang = lam_im * dt * tau
        return mag * jnp.cos(ang), mag * jnp.sin(ang)

    def zoh(lam_re, lam_im, dt):
        lb_re, lb_im = power(lam_re, lam_im, dt, 1.0)
        den = lam_re * lam_re + lam_im * lam_im
        f_re = ((lb_re - 1.0) * lam_re + lb_im * lam_im) / den
        f_im = (lb_im * lam_re - (lb_re - 1.0) * lam_im) / den
        return f_re, f_im

    lr, li, dtr = disc(are_r[...], aim_r[...], ldt_r[...])
    f_re, f_im = zoh(lr, li, dtr)
    bbt_re = f_re * bt_re[...] - f_im * bt_im[...]
    bbt_im = f_re * bt_im[...] + f_im * bt_re[...]
    tau_in = (ell - 1 - lax.broadcasted_iota(jnp.int32, (ell, 1), 0)).astype(F32)
    pin_re, pin_im = power(lr, li, dtr, tau_in)
    pin_re = _dot(erow_ref[...], pin_re, precision=HI)
    pin_im = _dot(erow_ref[...], pin_im, precision=HI)
    win_re = pin_re * bbt_re - pin_im * bbt_im
    win_im = pin_re * bbt_im + pin_im * bbt_re
    win_ref[...] = jnp.concatenate([win_re, win_im], axis=-1).astype(BF16)
    bbt_re = bbt_re[:gch]
    bbt_im = bbt_im[:gch]

    lc, lic, dtc = disc(are_c[...], aim_c[...], ldt_c[...])
    tau_l = lax.broadcasted_iota(jnp.int32, (1, ell), 1).astype(F32)
    pw_re, pw_im = power(lc, lic, dtc, tau_l)
    pw_re = _dot(pw_re, ecol_ref[...], precision=HI)
    pw_im = _dot(pw_im, ecol_ref[...], precision=HI)
    g_re = ct_re[...] * pw_re - ct_im[...] * pw_im
    g_im = ct_re[...] * pw_im + ct_im[...] * pw_re
    kc = _dot(bbt_re, g_re, precision=HI) - _dot(bbt_im, g_im, precision=HI)
    lane = lax.broadcasted_iota(jnp.int32, (gch, ell * gch), 1)
    for i in range(ell):
        rolled = kc if i == 0 else pltpu.roll(kc, i * gch, axis=1)
        m_ref[i * gch:(i + 1) * gch, :] = jnp.where(lane >= i * gch, rolled, 0.0).astype(BF16)
    lb_re, lb_im = power(lc, lic, dtc, 1.0)
    wo_re = g_re * lb_re - g_im * lb_im
    wo_im = g_re * lb_im + g_im * lb_re
    wout_ref[...] = jnp.concatenate([wo_re, -wo_im], axis=0).astype(BF16)

    l2, li2, dt2 = disc(are2_r[...], aim2_r[...], ldt2_r[...])
    pl_re, pl_im = power(l2, li2, dt2, float(ell))
    lane2 = lax.broadcasted_iota(jnp.int32, (1, 2 * p), 1)
    lbl_ref[0:1, :] = pl_re
    lbl_ref[1:2, :] = jnp.where(lane2 < p, -pl_im, pl_im)


def _s5_params(a_re, a_im, log_dt, b_re, b_im, c_re, c_im):
    g, p, gch, ell = S5_GROUPS, S5_STATE, S5_GROUP_CH, CHUNK
    ldt = jnp.broadcast_to(log_dt[:, :, None], (2, g, p))
    row = lambda a: a.reshape(2, g, 1, p)
    col = lambda a: a.reshape(2, g, p, 1)
    row2 = lambda a: jnp.concatenate([a, a], axis=-1).reshape(2, g, 1, 2 * p)
    bt_re = jnp.tile(jnp.swapaxes(b_re, 1, 2).astype(F32), (1, ell, 1))
    bt_im = jnp.tile(jnp.swapaxes(b_im, 1, 2).astype(F32), (1, ell, 1))
    ct_re = jnp.tile(jnp.swapaxes(c_re, 2, 3).astype(F32), (1, 1, 1, ell))
    ct_im = jnp.tile(jnp.swapaxes(c_im, 2, 3).astype(F32), (1, 1, 1, ell))
    dg = lambda *shape: pl.BlockSpec((None, None) + shape, lambda d, gi: (d, gi) + (0,) * len(shape))
    gonly = lambda *shape: pl.BlockSpec((None,) + shape, lambda d, gi: (gi,) + (0,) * len(shape))
    full = lambda a: pl.BlockSpec(a.shape, lambda d, gi: (0,) * a.ndim)
    n = ell * gch
    erow = jnp.asarray((np.arange(n)[:, None] // gch == np.arange(ell)[None, :]).astype(np.float32))
    ecol = erow.T
    a_re, a_im = a_re.astype(F32), a_im.astype(F32)
    return pl.pallas_call(
        _s5_param_kernel,
        grid=(2, g),
        in_specs=[dg(1, p)] * 3 + [dg(p, 1)] * 3 + [dg(1, 2 * p)] * 3
                 + [gonly(n, p)] * 2 + [dg(p, n)] * 2 + [full(erow), full(ecol)],
        out_specs=[dg(n, n), dg(n, 2 * p), dg(2 * p, n), dg(2, 2 * p)],
        out_shape=[jax.ShapeDtypeStruct((2, g, n, n), BF16),
                   jax.ShapeDtypeStruct((2, g, n, 2 * p), BF16),
                   jax.ShapeDtypeStruct((2, g, 2 * p, n), BF16),
                   jax.ShapeDtypeStruct((2, g, 2, 2 * p), F32)],
        compiler_params=_cparams(("parallel", "parallel")),
        name="s5_params",
    )(row(a_re), row(a_im), row(ldt), col(a_re), col(a_im), col(ldt), row2(a_re), row2(a_im), row2(ldt),
      bt_re, bt_im, ct_re, ct_im, erow, ecol)


def _s5_scan_kernel(u_ref, m_ref, win_ref, wout_ref, lbl_ref, y_ref, xin_ref, xprev_ref, *, nb, nc):
    u = u_ref[...]
    xin_ref[...] = _dot(u, win_ref[...])
    a1 = lbl_ref[0:1, :]
    a2 = lbl_ref[1:2, :]

    def step(n, xs):
        new = []
        for b in range(nb):
            x = xs[b]
            xprev_ref[pl.ds(b * nc + n, 1), :] = x[0:1]
            new.append(a1 * x + a2 * pltpu.roll(x, S5_STATE, axis=1) + xin_ref[pl.ds(b * nc + n, 1), :])
        return tuple(new)

    lax.fori_loop(0, nc, step, tuple(jnp.zeros((8, 2 * S5_STATE), F32) for _ in range(nb)))
    y_ref[...] = _dot(u, m_ref[...]) + _dot(xprev_ref[...].astype(BF16), wout_ref[...])


def _s5_scan(uc, m, win, wout, lbl, nb):
    _, g, rows, n = uc.shape
    p2 = 2 * S5_STATE
    dg = lambda *shape: pl.BlockSpec((None, None) + shape, lambda d, gi: (d, gi) + (0,) * len(shape))
    return pl.pallas_call(
        functools.partial(_s5_scan_kernel, nb=nb, nc=rows // nb),
        grid=(2, g),
        in_specs=[dg(rows, n), dg(n, n), dg(n, p2), dg(p2, n), dg(2, p2)],
        out_specs=dg(rows, n),
        out_shape=jax.ShapeDtypeStruct((2, g, rows, n), F32),
        scratch_shapes=[pltpu.VMEM((rows, p2), F32), pltpu.VMEM((rows, p2), F32)],
        compiler_params=_cparams(("parallel", "parallel")),
        name="s5_scan",
    )(uc, m, win, wout, lbl)


def _s5_finish_kernel(yf_ref, yb_ref, u_ref, d_ref, w_ref, b_ref, o_ref):
    y = yf_ref[...] + yb_ref[...] + d_ref[...] * u_ref[...].astype(F32)
    y = jax.nn.gelu(y)
    gate = _sigmoid(_dot(y.astype(BF16), w_ref[...]) + b_ref[...])
    o_ref[...] = (y * gate).astype(BF16)


def _s5_finish(yf, yb, z, d_row, glu_w, glu_b_row):
    nt = yf.shape[0] // TM
    full = lambda a: pl.BlockSpec(a.shape, lambda i: (0,) * a.ndim)
    return pl.pallas_call(
        _s5_finish_kernel,
        grid=(nt,),
        in_specs=[pl.BlockSpec((TM, 256), lambda i: (i, 0)),
                  pl.BlockSpec((TM, 256), lambda i: (i, 0)),
                  pl.BlockSpec((TM, 256), lambda i: (i, C_U // 256)),
                  full(d_row), full(glu_w), full(glu_b_row)],
        out_specs=pl.BlockSpec((TM, 256), lambda i: (i, 0)),
        out_shape=jax.ShapeDtypeStruct((yf.shape[0], 256), BF16),
        compiler_params=_cparams(("parallel",)),
        name="s5_finish",
    )(yf, yb, z, d_row, glu_w, glu_b_row)


def _merge_kernel(x_ref, y0_ref, y1_ref, y2_ref, y3_ref, gates_ref, wb_ref, wo_ref, g1_ref, gain_ref, o_ref):
    acc = jnp.zeros((TM, D_MODEL), F32)
    for bi, y_ref in enumerate((y0_ref, y1_ref, y2_ref, y3_ref)):
        proj = _dot(y_ref[...], wb_ref[bi])
        acc = acc + gates_ref[:, bi * D_MODEL:(bi + 1) * D_MODEL].astype(F32) * proj
    y = _dot(acc.astype(BF16), wo_ref[...])
    o_ref[...] = x_ref[...] + g1_ref[...] * _rms(y, gain_ref[...])


def _merge(x, ys, gates, wb, wo, g1, gain, layer, tps, nb):
    nt = x.shape[0] // TM
    row = lambda i: (jnp.minimum(i // tps, nb), 0, 0)
    tile = lambda w: pl.BlockSpec((TM, w), lambda i: (i, 0))
    return pl.pallas_call(
        _merge_kernel,
        grid=(nt,),
        in_specs=[tile(D_MODEL)] + [tile(BRANCH_W)] * 4 + [tile(4 * D_MODEL),
                  pl.BlockSpec((None, 4, BRANCH_W, D_MODEL), lambda i: (layer, 0, 0, 0)),
                  pl.BlockSpec((None, D_MODEL, D_MODEL), lambda i: (layer, 0, 0)),
                  pl.BlockSpec((None, 1, D_MODEL), row),
                  pl.BlockSpec((1, D_MODEL), lambda i: (0, 0))],
        out_specs=tile(D_MODEL),
        out_shape=jax.ShapeDtypeStruct(x.shape, F32),
        compiler_params=_cparams(("parallel",), 48),
        name="merge",
    )(x, *ys, gates, wb, wo, g1, gain)


def _mlp_kernel(x_ref, sc_ref, sh_ref, g2_ref, gin_ref, gout_ref, w1_ref, w2_ref, o_ref, h_ref, acc_ref):
    j = pl.program_id(1)

    @pl.when(j == 0)
    def _():
        h_ref[...] = _norm_mod(x_ref[...], gin_ref[...], sc_ref[...], sh_ref[...]).astype(BF16)
        acc_ref[...] = jnp.zeros(acc_ref.shape, F32)

    t = jnp.maximum(_dot(h_ref[...], w1_ref[...]), 0.0)
    acc_ref[...] += _dot((t * t).astype(BF16), w2_ref[...])

    @pl.when(j == pl.num_programs(1) - 1)
    def _():
        o_ref[...] = x_ref[...] + g2_ref[...] * _rms(acc_ref[...], gout_ref[...])


def _mlp(x, sc, sh, g2, gin, gout, w1, w2, layer, tps, nb, tf=1024):
    nt = x.shape[0] // TM
    row = lambda i, j: (jnp.minimum(i // tps, nb), 0, 0)
    vec = pl.BlockSpec((1, D_MODEL), lambda i, j: (0, 0))
    return pl.pallas_call(
        _mlp_kernel,
        grid=(nt, D_FF // tf),
        in_specs=[pl.BlockSpec((TM, D_MODEL), lambda i, j: (i, 0)),
                  pl.BlockSpec((None, 1, D_MODEL), row), pl.BlockSpec((None, 1, D_MODEL), row),
                  pl.BlockSpec((None, 1, D_MODEL), row), vec, vec,
                  pl.BlockSpec((None, D_MODEL, tf), lambda i, j: (layer, 0, j)),
                  pl.BlockSpec((None, tf, D_MODEL), lambda i, j: (layer, j, 0))],
        out_specs=pl.BlockSpec((TM, D_MODEL), lambda i, j: (i, 0)),
        out_shape=jax.ShapeDtypeStruct(x.shape, F32),
        scratch_shapes=[pltpu.VMEM((TM, D_MODEL), BF16), pltpu.VMEM((TM, D_MODEL), F32)],
        compiler_params=_cparams(("parallel", "arbitrary"), 48),
        name="mlp",
    )(x, sc, sh, g2, gin, gout, w1, w2)


def _pack_w_in(w_in):
    o_na, o_cq, o_ckv, o_gdn, o_z, o_a, o_b, o_u, o_gate = 0, 768, 1024, 1184, 1952, 2208, 2216, 2224, 2480
    idx = np.zeros(ZW, np.int32)
    keep = np.zeros(ZW, np.float32)

    def put(dst, src):
        idx[dst:dst + len(src)] = src
        keep[dst:dst + len(src)] = 1.0

    put(C_GDN, np.arange(o_gdn, o_gdn + 768))
    put(C_NA, np.arange(o_na, o_na + 768))
    put(C_CQ, np.arange(o_cq, o_cq + 256))
    put(C_CKV, np.arange(o_ckv, o_ckv + 160))
    put(C_CKV + L_A, np.arange(o_a, o_a + 8))
    put(C_CKV + L_B, np.arange(o_b, o_b + 8))
    put(C_CKV + L_KRS, o_ckv + MLA_KV_LORA + ROPE_SWAP)
    put(C_Z, np.arange(o_z, o_z + 256))
    put(C_U, np.arange(o_u, o_u + 256))
    small = (jnp.take(w_in, jnp.asarray(idx), axis=2) * jnp.asarray(keep)).astype(BF16)
    gates = w_in[:, :, o_gate:].astype(BF16)
    return small, gates


def _mla_weights(w_uq, w_ukv):
    depth = w_uq.shape[0]
    hq = MLA_NOPE + MLA_ROPE
    wq = w_uq.reshape(depth, -1, N_HEADS, hq)
    pad = lambda a, lo, hi: jnp.pad(a, ((0, 0), (0, 0), (0, 0), (lo, hi)))
    wq_ext = pad(wq, 0, 128 - hq).reshape(depth, -1, N_HEADS * 128)
    wq_sw = pad(wq[..., MLA_NOPE:][..., ROPE_SWAP], MLA_NOPE, 128 - hq).reshape(depth, -1, N_HEADS * 128)
    wkv = w_ukv.reshape(depth, -1, N_HEADS, MLA_NOPE + HEAD_W)
    wk = pad(wkv[..., :MLA_NOPE], 0, 128 - MLA_NOPE).reshape(depth, -1, N_HEADS * 128)
    wv = wkv[..., MLA_NOPE:].reshape(depth, -1, N_HEADS * HEAD_W)
    return [a.astype(BF16) for a in (wq_ext, wq_sw, wk, wv)]


def _rope_place_mats():
    p1 = np.zeros((256, N_HEADS * 128), np.float32)
    p2 = np.zeros((256, N_HEADS * 128), np.float32)
    for h in range(N_HEADS):
        for r in range(MLA_ROPE):
            p1[L_KR + r, h * 128 + MLA_NOPE + r] = 1.0
            p2[L_KRS + r, h * 128 + MLA_NOPE + r] = 1.0
    return jnp.asarray(p1, BF16), jnp.asarray(p2, BF16)


def _rope_tables(s):
    quarter = MLA_ROPE // 4
    inv_freq = ROPE_BASE ** (-jnp.arange(quarter, dtype=F32) / quarter)
    t = jnp.arange(s)
    ang_r = (t // GRID_W).astype(F32)[:, None] * inv_freq[None, :]
    ang_c = (t % GRID_W).astype(F32)[:, None] * inv_freq[None, :]
    cr, sr, cc, sn = jnp.cos(ang_r), jnp.sin(ang_r), jnp.cos(ang_c), jnp.sin(ang_c)
    cos = jnp.concatenate([jnp.ones((s, MLA_NOPE), F32), cr, cr, cc, cc, jnp.ones((s, 32), F32)], axis=1)
    sin = jnp.concatenate([jnp.zeros((s, MLA_NOPE), F32), -sr, sr, -sn, sn, jnp.zeros((s, 32), F32)], axis=1)
    cos = jnp.concatenate([cos, jnp.ones((TM, 128), F32)], axis=0)
    sin = jnp.concatenate([sin, jnp.zeros((TM, 128), F32)], axis=0)
    return cos, sin


def _head_block_ones():
    r = np.arange(BRANCH_W)
    return jnp.asarray((r[:, None] // HEAD_W == r[None, :] // HEAD_W).astype(np.float32))


def _gdn_expand_mats():
    e = np.zeros((2, 256, 2 * N_HEADS * 128), np.float32)
    for dh in range(2 * N_HEADS):
        e[0, L_A + dh, dh * 128:(dh + 1) * 128] = 1.0
        e[1, L_B + dh, dh * 128:(dh + 1) * 128] = 1.0
    return jnp.asarray(e)


def _lane_row(vals, offset, width=256):
    return jnp.zeros((1, width), F32).at[0, offset:offset + vals.shape[0]].set(vals.astype(F32))


def _scan_order(a, nb, s, ctx_len):
    lat = a[:nb * s].reshape((nb, s) + a.shape[1:])
    ctx = a[nb * s:].reshape((nb, ctx_len) + a.shape[1:])
    fwd = jnp.concatenate([ctx, lat], axis=1)
    bwd = jnp.concatenate([ctx[:, ::-1], lat[:, ::-1]], axis=1)
    return jnp.stack([fwd, bwd])


def _token_order(a, nb, s, ctx_len, flipped):
    ctx, lat = a[:, :ctx_len], a[:, ctx_len:]
    if flipped:
        ctx, lat = ctx[:, ::-1], lat[:, ::-1]
    return jnp.concatenate([lat.reshape((nb * s,) + a.shape[2:]), ctx.reshape((nb * ctx_len,) + a.shape[2:])], axis=0)


def _gdn_mixer(z, conv_w, a_log, dt_bias, norm_w, consts, nb, s, ctx_len):
    bseg, ef = consts["bseg"], consts["ef"]
    alog_row = _lane_row(a_log.reshape(-1), L_A)
    dtb_row = _lane_row(dt_bias.reshape(-1), L_A)
    qk, kk, vk, gcx, btx, gt = _gdn_prep(z, conv_w, alog_row, dtb_row, bseg, ef, consts["sel"], s // TS, nb * s // TS)
    of, ob = _gdn_scan(qk, kk, vk, gcx, btx, gt, nb, s)
    return _gdn_finish(of, ob, z, norm_w.astype(F32)[None, :])


def _s5_mixer(z, a_re, a_im, log_dt, b_re, b_im, c_re, c_im, d_skip, glu_w, glu_b, nb, s, ctx_len):
    m, win, wout, lbl = _s5_params(a_re, a_im, log_dt, b_re, b_im, c_re, c_im)
    g, gch = S5_GROUPS, S5_GROUP_CH
    t = s + ctx_len
    nc = t // CHUNK
    u = z[:, C_U:C_U + 256]
    us = _scan_order(u.reshape(-1, g, gch), nb, s, ctx_len)
    uc = jnp.transpose(us.reshape(2, nb, nc, CHUNK, g, gch), (0, 4, 1, 2, 3, 5)).reshape(2, g, nb * nc, CHUNK * gch)
    y = _s5_scan(uc, m, win, wout, lbl, nb)
    y = jnp.transpose(y.reshape(2, g, nb, nc, CHUNK, gch), (0, 2, 3, 4, 1, 5)).reshape(2, nb, t, g * gch)
    yf = _token_order(y[0], nb, s, ctx_len, False)
    yb = _token_order(y[1], nb, s, ctx_len, True)
    return _s5_finish(yf, yb, z, d_skip.astype(F32)[None, :], glu_w.astype(BF16), glu_b.astype(F32)[None, :])


def kernel(x, c, ctx, c_ctx, ada_w, ada_b, norm_gains, w_in, na_rpb, mla_q_norm, mla_kv_norm, mla_w_uq, mla_w_ukv, gdn_conv, gdn_a_log, gdn_dt_bias, gdn_norm, s5_a_re, s5_a_im, s5_log_dt, s5_b_re, s5_b_im, s5_c_re, s5_c_im, s5_d, s5_glu_w, s5_glu_b, w_branch, w_out, mlp_w1, mlp_w2):
    nb, s, d = x.shape
    ctx_len = ctx.shape[1]
    depth = ada_w.shape[0]
    assert d == D_MODEL and ctx_len == TS and nb * ctx_len == TM and s % TM == 0 and s // GRID_W >= 16
    nl = nb * s
    tps = s // TM

    xs = jnp.concatenate([x.reshape(nl, d), ctx.reshape(nb * ctx_len, d)], axis=0)
    cvec = jnp.zeros((8, d), F32).at[:nb].set(c).at[nb].set(c_ctx)
    mod = _modulation(cvec, ada_w, ada_b)
    mod = mod[:, :nb + 1].reshape(depth, nb + 1, 6, 1, d)

    w_small, w_gates = _pack_w_in(w_in)
    wq_ext, wq_sw, wk_ext, wv = _mla_weights(mla_w_uq, mla_w_ukv)
    p1, p2 = _rope_place_mats()
    cos_t, sin_t = _rope_tables(s)
    sel = np.zeros((8, 256), np.float32)
    sel[np.arange(8), L_A + np.arange(8)] = 1.0
    consts = {"bseg": _head_block_ones(), "ef": _gdn_expand_mats(), "sel": jnp.asarray(sel)}
    wb = w_branch.astype(BF16)
    wo = w_out.astype(BF16)
    w1 = mlp_w1.astype(BF16)
    w2 = mlp_w2.astype(BF16)
    gains = norm_gains.astype(F32)

    for l in range(depth):
        sh1, sc1, g1, sh2, sc2, g2 = [mod[l, :, i] for i in range(6)]
        z = _inproj(xs, sc1, sh1, gains[l, 0][None], w_small, l, tps, nb, False)
        gates = _inproj(xs, sc1, sh1, gains[l, 0][None], w_gates, l, tps, nb, True)

        y_na = jnp.concatenate([
            _na_latent(z, na_rpb[l], nb, s),
            _ctx_attention(z, z, z, C_NA // 256, C_NA // 256 + 1, C_NA // 256 + 2, 256, nl // TS, nb,
                           HEAD_W ** -0.5, "na_ctx")], axis=0)

        qm, km, vm = _mla_prep(z, cos_t, sin_t, mla_q_norm[l].astype(F32)[None], mla_kv_norm[l].astype(F32)[None],
                               wq_ext[l], wq_sw[l], wk_ext[l], wv[l], p1, p2, tps, nl // TM)
        y_mla = jnp.concatenate([
            _mla_latent(qm, km, vm, nb, s),
            _ctx_attention(qm, km, vm, 0, 0, 0, N_HEADS * 128, nl // TS, nb, 1.0, "mla_ctx")], axis=0)

        y_gdn = _gdn_mixer(z, gdn_conv[l].astype(F32), gdn_a_log[l], gdn_dt_bias[l], gdn_norm[l], consts,
                           nb, s, ctx_len)
        y_s5 = _s5_mixer(z, s5_a_re[l], s5_a_im[l], s5_log_dt[l], s5_b_re[l], s5_b_im[l], s5_c_re[l], s5_c_im[l],
                         s5_d[l], s5_glu_w[l], s5_glu_b[l], nb, s, ctx_len)

        xs = _merge(xs, (y_na, y_mla, y_gdn, y_s5), gates, wb, wo, g1, gains[l, 1][None], l, tps, nb)
        xs = _mlp(xs, sc2, sh2, g2, gains[l, 2][None], gains[l, 3][None], w1, w2, l, tps, nb)
    return xs[:nl].reshape(nb, s, d)
```

```python
import functools
import math

import numpy as np
import jax
import jax.numpy as jnp
from jax import lax
from jax.experimental import pallas as pl
from jax.experimental.pallas import tpu as pltpu

F32 = jnp.float32
BF16 = jnp.bfloat16
HI = lax.Precision.HIGHEST
EPS = 1e-6

D_MODEL = 1024
GRID_W = 64
NA_WIN_H = 8
NA_WIN_W = 16
N_HEADS = 4
HEAD_W = 64
BRANCH_W = 256
MLA_NOPE = 64
MLA_ROPE = 32
MLA_KV_LORA = 128
ROPE_BASE = 10000.0
GDN_CONV = 4
CHUNK = 64
S5_GROUPS = 16
S5_GROUP_CH = 16
S5_STATE = 64
D_FF = 4 * D_MODEL

TM = 512
TN = 512
TS = 256
NEG = -1e30

C_GDN = 0
C_NA = 768
C_CQ = 1536
C_CKV = 1792
C_Z = 2048
C_U = 2304
ZW = 2560
L_KR = 128
L_A = 160
L_B = 168
L_KRS = 176
ROPE_SWAP = np.concatenate([np.arange(8, 16), np.arange(0, 8), np.arange(24, 32), np.arange(16, 24)])


def _cparams(sem, vmem_mb=None):
    kw = dict(dimension_semantics=sem)
    if vmem_mb is not None:
        kw["vmem_limit_bytes"] = vmem_mb * 1024 * 1024
    return pltpu.CompilerParams(**kw)


def _dot(a, b, **kw):
    return jnp.dot(a, b, preferred_element_type=F32, **kw)


def _dot_nt(a, b):
    return lax.dot_general(a, b, (((1,), (1,)), ((), ())), preferred_element_type=F32)


def _sigmoid(x):
    return 1.0 / (1.0 + jnp.exp(-x))


def _silu(x):
    return x * _sigmoid(x)


def _mod_kernel(c_ref, w_ref, b_ref, o_ref):
    c = c_ref[...]
    o_ref[...] = _dot(_silu(c), w_ref[...], precision=HI) + b_ref[...]


def _modulation(cvec, ada_w, ada_b):
    depth, d, n = ada_w.shape
    tn = 1536
    return pl.pallas_call(
        _mod_kernel,
        grid=(depth, n // tn),
        in_specs=[pl.BlockSpec((8, d), lambda l, j: (0, 0)),
                  pl.BlockSpec((None, d, tn), lambda l, j: (l, 0, j)),
                  pl.BlockSpec((None, 1, tn), lambda l, j: (l, 0, j))],
        out_specs=pl.BlockSpec((None, 8, tn), lambda l, j: (l, 0, j)),
        out_shape=jax.ShapeDtypeStruct((depth, 8, n), F32),
        compiler_params=_cparams(("parallel", "parallel"), 40),
        name="modulation",
    )(cvec, ada_w, ada_b.reshape(depth, 1, n))


def _norm_mod(x, gain, sc, sh):
    r = lax.rsqrt(jnp.mean(x * x, axis=-1, keepdims=True) + EPS)
    return (x * r * gain) * (1.0 + sc) + sh


def _inproj_kernel(x_ref, sc_ref, sh_ref, gain_ref, w_ref, o_ref, h_ref, *, gate):
    @pl.when(pl.program_id(1) == 0)
    def _():
        h_ref[...] = _norm_mod(x_ref[...], gain_ref[...], sc_ref[...], sh_ref[...]).astype(BF16)

    acc = _dot(h_ref[...], w_ref[...])
    if gate:
        acc = _sigmoid(acc)
    o_ref[...] = acc.astype(BF16)


def _inproj(x, sc, sh, gain, w, layer, tps, nb, gate):
    nt = x.shape[0] // TM
    width = w.shape[-1]
    row = lambda i, j: (jnp.minimum(i // tps, nb), 0, 0)
    return pl.pallas_call(
        functools.partial(_inproj_kernel, gate=gate),
        grid=(nt, width // TN),
        in_specs=[pl.BlockSpec((TM, D_MODEL), lambda i, j: (i, 0)),
                  pl.BlockSpec((None, 1, D_MODEL), row),
                  pl.BlockSpec((None, 1, D_MODEL), row),
                  pl.BlockSpec((1, D_MODEL), lambda i, j: (0, 0)),
                  pl.BlockSpec((None, D_MODEL, TN), lambda i, j: (layer, 0, j))],
        out_specs=pl.BlockSpec((TM, TN), lambda i, j: (i, j)),
        out_shape=jax.ShapeDtypeStruct((x.shape[0], width), BF16),
        scratch_shapes=[pltpu.VMEM((TM, D_MODEL), BF16)],
        compiler_params=_cparams(("parallel", "arbitrary")),
        name="inproj_gates" if gate else "inproj",
    )(x, sc, sh, gain, w)


def _head_lane_mask(width, head_w, h):
    lane = lax.broadcasted_iota(jnp.int32, (1, width), 1)
    return (lane >= h * head_w) & (lane < (h + 1) * head_w)


def _na_build_bias(rpb_ref, bias_ref, r0, kb0, rows_total):
    w = GRID_W
    qc = lax.broadcasted_iota(jnp.int32, (w, 2 * w), 0)
    lane = lax.broadcasted_iota(jnp.int32, (w, 2 * w), 1)
    kc = lane % w
    cs = jnp.clip(qc - NA_WIN_W // 2, 0, w - NA_WIN_W)
    col_ok = (kc >= cs) & (kc < cs + NA_WIN_W)
    left = lane < w
    neg = jnp.full((w, 2 * w), NEG, F32)
    for h in range(N_HEADS):
        t = rpb_ref[h]
        toep = []
        for a in range(2 * NA_WIN_H - 1):
            row = jnp.broadcast_to(t[a:a + 1, :], (w, 2 * w))
            ra = pltpu.roll(row, 2 * w - (NA_WIN_W - 1), axis=1, stride=1, stride_axis=0)
            rb = pltpu.roll(ra, w, axis=1)
            toep.append((jnp.where(col_ok, ra, NEG), jnp.where(col_ok, rb, NEG)))
        for qr in range(8):
            rs = min(max(r0 + qr - NA_WIN_H // 2, 0), rows_total - NA_WIN_H)
            for kp in range(8):
                halves = []
                for side in range(2):
                    kr = kb0 + 2 * kp + side
                    halves.append(toep[kr - (r0 + qr) + NA_WIN_H - 1][side] if rs <= kr < rs + NA_WIN_H else neg)
                bias_ref[h, qr * w:(qr + 1) * w, kp * 2 * w:(kp + 1) * 2 * w] = jnp.where(left, halves[0], halves[1])


def _na_kernel(q_ref, k_ref, v_ref, kc_ref, vc_ref, rpb_ref, o_ref, bias_ref, *, rows_total):
    i = pl.program_id(1)
    last = pl.num_programs(1) - 1

    @pl.when(i == 0)
    def _():
        _na_build_bias(rpb_ref, bias_ref, 0, 0, rows_total)

    @pl.when(i == 1)
    def _():
        _na_build_bias(rpb_ref, bias_ref, 8, 4, rows_total)

    @pl.when(i == last)
    def _():
        _na_build_bias(rpb_ref, bias_ref, rows_total - 8, rows_total - 16, rows_total)

    kb = jnp.clip(2 * i - 1, 0, rows_total // 4 - 4)
    start = pl.multiple_of(kb * (4 * GRID_W), 4 * GRID_W)
    nk = 2 * NA_WIN_H * GRID_W
    q = q_ref[...]
    kw = k_ref[pl.ds(start, nk), :]
    vw = v_ref[pl.ds(start, nk), :]
    kc = kc_ref[...]
    vc = vc_ref[...]
    scale = HEAD_W ** -0.5
    out = jnp.zeros(q.shape, F32)
    for h in range(N_HEADS):
        hm = _head_lane_mask(BRANCH_W, HEAD_W, h)
        qh = jnp.where(hm, q, jnp.zeros_like(q))
        sb = _dot_nt(qh, kw) * scale + bias_ref[h]
        sc = _dot_nt(qh, kc) * scale
        m = jnp.maximum(jnp.max(sb, axis=-1, keepdims=True), jnp.max(sc, axis=-1, keepdims=True))
        pb = jnp.exp(sb - m)
        pc = jnp.exp(sc - m)
        den = jnp.sum(pb, axis=-1, keepdims=True) + jnp.sum(pc, axis=-1, keepdims=True)
        o = _dot(pb.astype(BF16), vw) + _dot(pc.astype(BF16), vc)
        out = jnp.where(hm, o / den, out)
    o_ref[...] = out.astype(BF16)


def _na_latent(z, rpb, nb, s):
    rows_total = s // GRID_W
    qb = 8 * GRID_W
    nq = s // qb
    nl = nb * s
    rpb = jnp.pad(rpb.astype(F32), ((0, 0), (0, 1), (0, 2 * GRID_W - (2 * NA_WIN_W - 1))))
    return pl.pallas_call(
        functools.partial(_na_kernel, rows_total=rows_total),
        grid=(nb, nq),
        in_specs=[pl.BlockSpec((qb, BRANCH_W), lambda b, i: (b * nq + i, C_NA // 256)),
                  pl.BlockSpec((s, BRANCH_W), lambda b, i: (b, C_NA // 256 + 1)),
                  pl.BlockSpec((s, BRANCH_W), lambda b, i: (b, C_NA // 256 + 2)),
                  pl.BlockSpec((TS, BRANCH_W), lambda b, i: (nl // TS + b, C_NA // 256 + 1)),
                  pl.BlockSpec((TS, BRANCH_W), lambda b, i: (nl // TS + b, C_NA // 256 + 2)),
                  pl.BlockSpec(rpb.shape, lambda b, i: (0, 0, 0))],
        out_specs=pl.BlockSpec((qb, BRANCH_W), lambda b, i: (b * nq + i, 0)),
        out_shape=jax.ShapeDtypeStruct((nl, BRANCH_W), BF16),
        scratch_shapes=[pltpu.VMEM((N_HEADS, qb, 2 * qb), F32)],
        compiler_params=_cparams(("parallel", "arbitrary"), 56),
        name="na_latent",
    )(z, z, z, z, z, rpb)


def _ctx_attn_kernel(q_ref, k_ref, v_ref, o_ref, *, scale):
    q = q_ref[...]
    k = k_ref[...]
    v = v_ref[...]
    qw = q.shape[-1]
    out = jnp.zeros((q.shape[0], BRANCH_W), F32)
    for h in range(N_HEADS):
        qh = jnp.where(_head_lane_mask(qw, qw // N_HEADS, h), q, jnp.zeros_like(q))
        s = _dot_nt(qh, k) * scale
        m = jnp.max(s, axis=-1, keepdims=True)
        p = jnp.exp(s - m)
        den = jnp.sum(p, axis=-1, keepdims=True)
        o = _dot(p.astype(BF16), v)
        out = jnp.where(_head_lane_mask(BRANCH_W, HEAD_W, h), o / den, out)
    o_ref[...] = out.astype(BF16)


def _ctx_attention(q, k, v, qcol, kcol, vcol, qw, row0, nb, scale, name):
    return pl.pallas_call(
        functools.partial(_ctx_attn_kernel, scale=scale),
        grid=(nb,),
        in_specs=[pl.BlockSpec((TS, qw), lambda b: (row0 + b, qcol)),
                  pl.BlockSpec((TS, qw), lambda b: (row0 + b, kcol)),
                  pl.BlockSpec((TS, BRANCH_W), lambda b: (row0 + b, vcol))],
        out_specs=pl.BlockSpec((TS, BRANCH_W), lambda b: (b, 0)),
        out_shape=jax.ShapeDtypeStruct((nb * TS, BRANCH_W), BF16),
        compiler_params=_cparams(("parallel",)),
        name=name,
    )(q, k, v)


def _rms(x, gain):
    return x * lax.rsqrt(jnp.mean(x * x, axis=-1, keepdims=True) + EPS) * gain


def _mla_prep_kernel(cq_ref, ckv_ref, cos_ref, sin_ref, qn_ref, kvn_ref, wq_ref, wqs_ref, wk_ref, wv_ref,
                     p1_ref, p2_ref, q_ref, k_ref, v_ref):
    cos = jnp.concatenate([cos_ref[...]] * N_HEADS, axis=-1)
    sin = jnp.concatenate([sin_ref[...]] * N_HEADS, axis=-1)
    cqn = _rms(cq_ref[...].astype(F32), qn_ref[...]).astype(BF16)
    scale = (MLA_NOPE + MLA_ROPE) ** -0.5
    q = _dot(cqn, wq_ref[...]) * cos + _dot(cqn, wqs_ref[...]) * sin
    q_ref[...] = (q * scale).astype(BF16)
    ckv = ckv_ref[...]
    kvn = _rms(ckv[:, :MLA_KV_LORA].astype(F32), kvn_ref[...]).astype(BF16)
    k = (_dot(kvn, wk_ref[...]) + _dot(ckv, p1_ref[...])) * cos + _dot(ckv, p2_ref[...]) * sin
    k_ref[...] = k.astype(BF16)
    v_ref[...] = _dot(kvn, wv_ref[...]).astype(BF16)


def _mla_prep(z, cos_t, sin_t, qn, kvn, wq, wqs, wk, wv, p1, p2, tps, n_lat_tiles):
    nt_rows = z.shape[0]
    nt = nt_rows // TM
    full = lambda a: pl.BlockSpec(a.shape, lambda i: (0,) * a.ndim)
    tab = lambda i: (jnp.where(i < n_lat_tiles, i % tps, tps), 0)
    hw = N_HEADS * 128
    return pl.pallas_call(
        _mla_prep_kernel,
        grid=(nt,),
        in_specs=[pl.BlockSpec((TM, 256), lambda i: (i, C_CQ // 256)),
                  pl.BlockSpec((TM, 256), lambda i: (i, C_CKV // 256)),
                  pl.BlockSpec((TM, 128), tab), pl.BlockSpec((TM, 128), tab),
                  full(qn), full(kvn), full(wq), full(wqs), full(wk), full(wv), full(p1), full(p2)],
        out_specs=[pl.BlockSpec((TM, hw), lambda i: (i, 0)),
                   pl.BlockSpec((TM, hw), lambda i: (i, 0)),
                   pl.BlockSpec((TM, BRANCH_W), lambda i: (i, 0))],
        out_shape=[jax.ShapeDtypeStruct((nt_rows, hw), BF16),
                   jax.ShapeDtypeStruct((nt_rows, hw), BF16),
                   jax.ShapeDtypeStruct((nt_rows, BRANCH_W), BF16)],
        compiler_params=_cparams(("parallel",)),
        name="mla_prep",
    )(z, z, cos_t, sin_t, qn, kvn, wq, wqs, wk, wv, p1, p2)


def _flash_kernel(q_ref, kl_ref, vl_ref, kc_ref, vc_ref, o_ref, *, tk, n_lat):
    tq = q_ref.shape[0]
    qs = [q_ref[:, 128 * hh:128 * (hh + 1)] for hh in range(2)]

    def update(carry, ks, v):
        new = []
        for hh in range(2):
            m, l, acc = carry[hh]
            s = _dot_nt(qs[hh], ks[hh])
            m_new = jnp.maximum(m, jnp.max(s, axis=-1, keepdims=True))
            a = jnp.exp(m - m_new)
            p = jnp.exp(s - m_new)
            l = a * l + jnp.sum(p, axis=-1, keepdims=True)
            acc = a * acc + _dot(p.astype(BF16), v)
            new.append((m_new, l, acc))
        return tuple(new)

    def body(t, carry):
        r0 = pl.multiple_of(t * tk, tk)
        ks = [kl_ref[pl.ds(r0, tk), 128 * hh:128 * (hh + 1)] for hh in range(2)]
        return update(carry, ks, vl_ref[pl.ds(r0, tk), :])

    init = tuple((jnp.full((tq, 1), NEG, F32), jnp.zeros((tq, 1), F32), jnp.zeros((tq, 128), F32))
                 for _ in range(2))
    carry = lax.fori_loop(0, n_lat, body, init)
    carry = update(carry, [kc_ref[:, 128 * hh:128 * (hh + 1)] for hh in range(2)], vc_ref[...])
    lane = lax.broadcasted_iota(jnp.int32, (1, 128), 1)
    o = jnp.where(lane < HEAD_W, carry[0][2] / carry[0][1], carry[1][2] / carry[1][1])
    o_ref[...] = o.astype(BF16)


def _mla_latent(qm, km, vm, nb, s, tq=512, tk=512):
    nq = s // tq
    nl = nb * s
    return pl.pallas_call(
        functools.partial(_flash_kernel, tk=tk, n_lat=s // tk),
        grid=(nb, 2, nq),
        in_specs=[pl.BlockSpec((tq, 256), lambda b, hp, i: (b * nq + i, hp)),
                  pl.BlockSpec((s, 256), lambda b, hp, i: (b, hp)),
                  pl.BlockSpec((s, 128), lambda b, hp, i: (b, hp)),
                  pl.BlockSpec((TS, 256), lambda b, hp, i: (nl // TS + b, hp)),
                  pl.BlockSpec((TS, 128), lambda b, hp, i: (nl // TS + b, hp))],
        out_specs=pl.BlockSpec((tq, 128), lambda b, hp, i: (b * nq + i, hp)),
        out_shape=jax.ShapeDtypeStruct((nl, BRANCH_W), BF16),
        compiler_params=_cparams(("parallel", "parallel", "arbitrary"), 48),
        name="mla_flash",
    )(qm, km, vm, km, vm)


def _gdn_prep_kernel(prev_ref, cur_ref, next_ref, ckv_ref, conv_ref, alog_ref, dtb_ref, bseg_ref, ef_ref, sel_ref,
                     qk_ref, kk_ref, vk_ref, gcx_ref, btx_ref, gt_ref, *, tps, n_lat_tiles):
    i = pl.program_id(0)
    is_ctx = i >= n_lat_tiles
    first = is_ctx | (i % tps == 0)
    last = is_ctx | (i % tps == tps - 1)
    prev = jnp.where(first, 0.0, prev_ref[...].astype(F32))
    nxt = jnp.where(last, 0.0, next_ref[...].astype(F32))
    ext = jnp.concatenate([prev, cur_ref[...].astype(F32), nxt], axis=0)
    n_ext = TS + 16
    acc = jnp.zeros((TS, 3 * BRANCH_W), F32)
    for j in range(GDN_CONV):
        shifted = pltpu.roll(ext, n_ext - (8 - GDN_CONV // 2 + j), axis=0)[:TS]
        acc = acc + shifted * conv_ref[j:j + 1, :]
    x = _silu(acc)
    bseg = bseg_ref[...]

    def l2n(a):
        return a * lax.rsqrt(_dot(a * a, bseg, precision=HI) + EPS)

    q = l2n(x[:, :BRANCH_W]) * (HEAD_W ** -0.5)
    k = l2n(x[:, BRANCH_W:2 * BRANCH_W])
    v = x[:, 2 * BRANCH_W:]
    for h in range(N_HEADS):
        hs = slice(h * HEAD_W, (h + 1) * HEAD_W)
        qk_ref[h] = jnp.concatenate([q[:, hs], k[:, hs]], axis=-1)
        kk_ref[h] = jnp.concatenate([k[:, hs], k[:, hs]], axis=-1)
        vk_ref[h] = jnp.concatenate([v[:, hs], k[:, hs]], axis=-1)

    ab = ckv_ref[...].astype(F32)
    sp_in = ab + dtb_ref[...]
    softplus = jnp.maximum(sp_in, 0.0) + jnp.log1p(jnp.exp(-jnp.abs(sp_in)))
    lane = lax.broadcasted_iota(jnp.int32, (1, 256), 1)
    g = jnp.where((lane >= L_A) & (lane < L_B), -jnp.exp(alog_ref[...]) * softplus, 0.0)
    beta = jnp.where((lane >= L_B) & (lane < L_KRS), _sigmoid(ab), 0.0)
    r = lax.broadcasted_iota(jnp.int32, (TS, TS), 0)
    c = lax.broadcasted_iota(jnp.int32, (TS, TS), 1)
    same = (r // CHUNK) == (c // CHUNK)
    lower = jnp.where(same & (r >= c), 1.0, 0.0)
    upper = jnp.where(same & (r <= c), 1.0, 0.0)
    fwd_lane = (lane >= L_A) & (lane < L_A + N_HEADS)
    gc = jnp.where(fwd_lane, _dot(lower, g, precision=HI), _dot(upper, g, precision=HI))
    gt_ref[...] = lax.dot_general(sel_ref[...], gc, (((1,), (1,)), ((), ())), precision=HI,
                                  preferred_element_type=F32)
    gcx_ref[...] = _dot(gc, ef_ref[0], precision=HI)
    btx_ref[...] = _dot(beta, ef_ref[1], precision=HI)


def _gdn_prep(z, conv_w, alog_row, dtb_row, bseg, ef, sel, tps_s, n_lat_tiles):
    nt_rows = z.shape[0]
    nt = nt_rows // TS
    hb = TS // 8
    full = lambda a: pl.BlockSpec(a.shape, lambda i: (0,) * a.ndim)
    xw = 2 * N_HEADS * 128
    hsp = pl.BlockSpec((N_HEADS, TS, 128), lambda i: (0, i, 0))
    return pl.pallas_call(
        functools.partial(_gdn_prep_kernel, tps=tps_s, n_lat_tiles=n_lat_tiles),
        grid=(nt,),
        in_specs=[pl.BlockSpec((8, 768), lambda i: (jnp.maximum(i * hb - 1, 0), 0)),
                  pl.BlockSpec((TS, 768), lambda i: (i, 0)),
                  pl.BlockSpec((8, 768), lambda i: (jnp.minimum((i + 1) * hb, nt * hb - 1), 0)),
                  pl.BlockSpec((TS, 256), lambda i: (i, C_CKV // 256)),
                  full(conv_w), full(alog_row), full(dtb_row), full(bseg), full(ef), full(sel)],
        out_specs=[hsp] * 3
                  + [pl.BlockSpec((TS, xw), lambda i: (i, 0))] * 2
                  + [pl.BlockSpec((None, 8, TS), lambda i: (i, 0, 0))],
        out_shape=[jax.ShapeDtypeStruct((N_HEADS, nt_rows, 128), F32)] * 3
                  + [jax.ShapeDtypeStruct((nt_rows, xw), F32)] * 2
                  + [jax.ShapeDtypeStruct((nt, 8, TS), F32)],
        compiler_params=_cparams(("parallel",)),
        name="gdn_prep",
    )(z, z, z, z, conv_w, alog_row, dtb_row, bseg, ef, sel)


def _tri_solve(n, x, reverse):
    nblk = CHUNK // 8
    xb = [x[:, 8 * r:8 * (r + 1), :] for r in range(nblk)]
    nb_ = [n[:, 8 * r:8 * (r + 1), :] for r in range(nblk)]
    steps = range(CHUNK - 1, 0, -1) if reverse else range(CHUNK - 1)
    for j in steps:
        xj = xb[j // 8][:, j % 8:j % 8 + 1, :]
        blocks = range(0, j // 8 + 1) if reverse else range(j // 8, nblk)
        for rb in blocks:
            xb[rb] = xb[rb] - nb_[rb][:, :, j:j + 1] * xj
    return jnp.concatenate(xb, axis=1)


def _gdn_direction(qk_ref, kk_ref, vk_ref, gc_ref, bt_ref, gt_ref, o_ref, s_ref, d, reverse):
    c = CHUNK
    nh = N_HEADS
    r = lax.broadcasted_iota(jnp.int32, (c, c), 0)
    cc = lax.broadcasted_iota(jnp.int32, (c, c), 1)
    incl = ((r <= cc) if reverse else (r >= cc))[None]
    strict = ((r < cc) if reverse else (r > cc))[None]
    lo = lax.broadcasted_iota(jnp.int32, (1, 1, 2 * HEAD_W), 2) < HEAD_W
    bmm = lambda a, b: jnp.einsum("hij,hjk->hik", a, b, preferred_element_type=F32)
    bmm_nt = lambda a, b: jnp.einsum("hid,hjd->hij", a, b, preferred_element_type=F32)
    bmm_tn = lambda a, b: jnp.einsum("hcd,hce->hde", a, b, preferred_element_type=F32)
    n_chunks = TS // c
    for ci in (range(n_chunks - 1, -1, -1) if reverse else range(n_chunks)):
        sl = slice(ci * c, (ci + 1) * c)
        qk = qk_ref[:, sl, :]
        kk = kk_ref[:, sl, :]
        vk = vk_ref[:, sl, :]
        gc = jnp.stack([gc_ref[sl, 128 * h:128 * (h + 1)] for h in range(nh)])
        bt = jnp.stack([bt_ref[sl, 128 * h:128 * (h + 1)] for h in range(nh)])
        grow = jnp.stack([gt_ref[d * nh + h:d * nh + h + 1, sl] for h in range(nh)])
        dec = jnp.where(incl, jnp.exp(jnp.minimum(gc[:, :, :c] - grow, 0.0)), 0.0)
        kkb = kk.astype(BF16)
        k_dot_k = 0.5 * bmm_nt(kkb, kkb)
        q_only = jnp.where(lo, qk, 0.0)
        q_dot_k = bmm_nt(q_only.astype(BF16), kkb) * dec
        n = jnp.where(strict, bt[:, :, :c] * k_dot_k * dec, 0.0)
        egc = jnp.exp(gc)
        x = _tri_solve(n, vk * jnp.where(lo, bt, bt * egc), reverse)
        edge = 0 if reverse else c - 1
        glast = gc[:, edge:edge + 1, :]
        ktail = kk * jnp.exp(glast - gc)
        s2 = s_ref[d]
        s2b = s2.astype(BF16)
        v_new = x[:, :, :HEAD_W] - bmm(jnp.where(lo, 0.0, x).astype(BF16), s2b)
        vb = v_new.astype(BF16)
        o_ref[:, sl, :] = bmm((q_only * egc).astype(BF16), s2b) + bmm(q_dot_k.astype(BF16), vb)
        s_ref[d] = s2 * jnp.exp(glast)[:, :, :HEAD_W] + bmm_tn(ktail.astype(BF16), vb)


def _gdn_scan_kernel(qkf, qkb, kkf, kkb, vkf, vkb, gcf, gcb, btf, btb, gtf, gtb, of_ref, ob_ref, s_ref):
    @pl.when(pl.program_id(1) == 0)
    def _():
        s_ref[...] = jnp.zeros(s_ref.shape, F32)

    _gdn_direction(qkf, kkf, vkf, gcf, btf, gtf, of_ref, s_ref, 0, False)
    _gdn_direction(qkb, kkb, vkb, gcb, btb, gtb, ob_ref, s_ref, 1, True)


def _gdn_scan(qk, kk, vk, gcx, btx, gt, nb, s):
    h, nt_rows, _ = qk.shape
    tps = s // TS
    nlt = nb * tps
    fwd = lambda b, n: jnp.where(n == 0, nlt + b, b * tps + n - 1)
    bwd = lambda b, n: jnp.where(n == 0, nlt + b, b * tps + tps - n)
    hw = h * 128

    def views(shape, imap):
        return [pl.BlockSpec(shape, functools.partial(imap, t)) for t in (fwd, bwd)]

    heads = views((h, TS, 128), lambda t, b, n: (0, t(b, n), 0))
    lanes = [pl.BlockSpec((TS, hw), lambda b, n: (fwd(b, n), 0)), pl.BlockSpec((TS, hw), lambda b, n: (bwd(b, n), 1))]
    rows = views((None, 8, TS), lambda t, b, n: (t(b, n), 0, 0))
    outs = views((h, TS, HEAD_W), lambda t, b, n: (0, t(b, n), 0))
    return pl.pallas_call(
        _gdn_scan_kernel,
        grid=(nb, tps + 1),
        in_specs=heads * 3 + lanes * 2 + rows,
        out_specs=outs,
        out_shape=[jax.ShapeDtypeStruct((h, nt_rows, HEAD_W), F32)] * 2,
        scratch_shapes=[pltpu.VMEM((2, h, 2 * HEAD_W, HEAD_W), F32)],
        compiler_params=_cparams(("parallel", "arbitrary")),
        name="gdn_scan",
    )(qk, qk, kk, kk, vk, vk, gcx, gcx, btx, btx, gt, gt)


def _gdn_finish_kernel(of_ref, ob_ref, z_ref, nw_ref, y_ref):
    o = of_ref[...] + ob_ref[...]
    y = o * lax.rsqrt(jnp.mean(o * o, axis=-1, keepdims=True) + EPS) * nw_ref[...]
    y = jnp.concatenate([y[h] for h in range(N_HEADS)], axis=-1)
    y_ref[...] = (y * _silu(z_ref[...].astype(F32))).astype(BF16)


def _gdn_finish(of, ob, z, nw_row):
    nt = of.shape[1] // TM
    hsp = pl.BlockSpec((N_HEADS, TM, HEAD_W), lambda i: (0, i, 0))
    return pl.pallas_call(
        _gdn_finish_kernel,
        grid=(nt,),
        in_specs=[hsp, hsp, pl.BlockSpec((TM, 256), lambda i: (i, C_Z // 256)),
                  pl.BlockSpec(nw_row.shape, lambda i: (0, 0))],
        out_specs=pl.BlockSpec((TM, 256), lambda i: (i, 0)),
        out_shape=jax.ShapeDtypeStruct((of.shape[1], 256), BF16),
        compiler_params=_cparams(("parallel",)),
        name="gdn_finish",
    )(of, ob, z, nw_row)


S5_SUB = 8
S5_NS = S5_GROUPS * S5_STATE


def _s5_param_kernel(are_ref, aim_ref, ldt_ref, bre_ref, bim_ref, cre_ref, cim_ref,
                     bb_ref, kst_ref, cbd_ref, tab_ref):
    d = pl.program_id(0)
    lam_re = jnp.minimum(are_ref[...], -1e-4)
    lam_im = aim_ref[...]
    dt = jnp.exp(ldt_ref[...])

    def power(tau):
        mag = jnp.exp(lam_re * dt * tau)
        ang = lam_im * dt * tau
        return mag * jnp.cos(ang), mag * jnp.sin(ang)

    idx = lax.broadcasted_iota(jnp.int32, (S5_SUB, 1), 0)
    p_re, p_im = power(idx.astype(F32))
    lb_re, lb_im = p_re[1:2], p_im[1:2]
    den = lam_re * lam_re + lam_im * lam_im
    f_re = ((lb_re - 1.0) * lam_re + lb_im * lam_im) / den
    f_im = (lb_im * lam_re - (lb_re - 1.0) * lam_im) / den
    bb_re = f_re * bre_ref[...] - f_im * bim_ref[...]
    bb_im = f_re * bim_ref[...] + f_im * bre_ref[...]
    bb_ref[...] = jnp.concatenate([bb_re, bb_im], axis=-1).astype(BF16)
    c_re = cre_ref[...]
    c_im = cim_ref[...]
    cbd_ref[...] = jnp.concatenate([c_re, -c_im], axis=0).astype(BF16)
    for tau in range(S5_SUB):
        g_re = bb_re * p_re[tau:tau + 1] - bb_im * p_im[tau:tau + 1]
        g_im = bb_re * p_im[tau:tau + 1] + bb_im * p_re[tau:tau + 1]
        kst_ref[tau] = (_dot(g_re, c_re, precision=HI) - _dot(g_im, c_im, precision=HI)).astype(BF16)
    fwd = d == 0
    t_in = jnp.where(fwd, S5_SUB - 1 - idx, idx).astype(F32)
    t_out = jnp.where(fwd, idx + 1, S5_SUB - idx).astype(F32)
    for k, tau in enumerate((t_in, t_out, jnp.full((S5_SUB, 1), float(S5_SUB), F32))):
        tab_ref[k] = jnp.concatenate(power(tau), axis=-1)


def _s5_params(a_re, a_im, log_dt, b_re, b_im, c_re, c_im):
    g, p, gch = S5_GROUPS, S5_STATE, S5_GROUP_CH
    ns, c = S5_NS, g * gch
    eye = jnp.eye(g, dtype=F32)
    row = lambda a: a.astype(F32).reshape(2, 1, ns)
    ldt = jnp.broadcast_to(log_dt.astype(F32)[:, :, None], (2, g, p))
    b_bd = lambda b: jnp.einsum("gpc,gh->gchp", b.astype(F32), eye).reshape(c, ns)
    c_bd = lambda cc: jnp.einsum("dgcp,gh->dgphc", cc.astype(F32), eye).reshape(2, ns, c)
    per_dir = lambda *shape: pl.BlockSpec((None,) + shape, lambda d: (d,) + (0,) * len(shape))
    shared = pl.BlockSpec((c, ns), lambda d: (0, 0))
    return pl.pallas_call(
        _s5_param_kernel,
        grid=(2,),
        in_specs=[per_dir(1, ns)] * 3 + [shared] * 2 + [per_dir(ns, c)] * 2,
        out_specs=[per_dir(c, 2 * ns), per_dir(S5_SUB, c, c), per_dir(2 * ns, c), per_dir(3, S5_SUB, 2 * ns)],
        out_shape=[jax.ShapeDtypeStruct((2, c, 2 * ns), BF16),
                   jax.ShapeDtypeStruct((2, S5_SUB, c, c), BF16),
                   jax.ShapeDtypeStruct((2, 2 * ns, c), BF16),
                   jax.ShapeDtypeStruct((2, 3, S5_SUB, 2 * ns), F32)],
        compiler_params=_cparams(("parallel",), 48),
        name="s5_params",
    )(row(a_re), row(a_im), row(ldt), b_bd(b_re), b_bd(b_im), c_bd(c_re), c_bd(c_im))


def _s5_direction(u_ref, bb_ref, kst_ref, cbd_ref, tab_ref, y_ref, x_ref, xin_ref, xpv_ref, reverse):
    ns, sub = S5_NS, S5_SUB
    nsc = TS // sub
    u = u_ref[...]
    z = _dot(u, bb_ref[...])
    z_re = z[:, :ns].reshape(nsc, sub, ns)
    z_im = z[:, ns:].reshape(nsc, sub, ns)
    w = tab_ref[0]
    w_re, w_im = w[:, :ns][None], w[:, ns:][None]
    group_sum = lambda a: jnp.broadcast_to(jnp.sum(a, axis=1, keepdims=True), a.shape)
    xin_ref[0] = group_sum(w_re * z_re - w_im * z_im)
    xin_ref[1] = group_sum(w_re * z_im + w_im * z_re)
    a = tab_ref[2]
    a_re, a_im = a[:, :ns], a[:, ns:]

    def step(k, carry):
        x_re, x_im = carry
        n = nsc - 1 - k if reverse else k
        xpv_ref[0, n] = x_re
        xpv_ref[1, n] = x_im
        return (a_re * x_re - a_im * x_im + xin_ref[0, n], a_re * x_im + a_im * x_re + xin_ref[1, n])

    x_re, x_im = lax.fori_loop(0, nsc, step, (x_ref[0], x_ref[1]))
    x_ref[0] = x_re
    x_ref[1] = x_im
    o = tab_ref[1]
    o_re, o_im = o[:, :ns][None], o[:, ns:][None]
    p_re = (xpv_ref[0] * o_re - xpv_ref[1] * o_im).reshape(TS, ns)
    p_im = (xpv_ref[0] * o_im + xpv_ref[1] * o_re).reshape(TS, ns)
    y = _dot(jnp.concatenate([p_re, p_im], axis=-1).astype(BF16), cbd_ref[...])
    uf = u.astype(F32)
    pos = lax.broadcasted_iota(jnp.int32, (TS, 1), 0) % sub
    for tau in range(sub):
        if tau == 0:
            shifted = u
        else:
            rolled = pltpu.roll(uf, TS - tau if reverse else tau, axis=0)
            inside = (pos + tau < sub) if reverse else (pos >= tau)
            shifted = jnp.where(inside, rolled, 0.0).astype(BF16)
        y = y + _dot(shifted, kst_ref[tau])
    y_ref[...] = y


def _s5_scan_kernel(uf_ref, ub_ref, bbf, bbb, kstf, kstb, cbdf, cbdb, tabf, tabb, yf_ref, yb_ref,
                    x_ref, xin_ref, xpv_ref):
    @pl.when(pl.program_id(1) == 0)
    def _():
        x_ref[...] = jnp.zeros(x_ref.shape, F32)

    _s5_direction(uf_ref, bbf, kstf, cbdf, tabf, yf_ref, x_ref.at[0], xin_ref.at[0], xpv_ref.at[0], False)
    _s5_direction(ub_ref, bbb, kstb, cbdb, tabb, yb_ref, x_ref.at[1], xin_ref.at[1], xpv_ref.at[1], True)


def _s5_scan(z, bb, kst, cbd, tab, nb, s):
    nt_rows = z.shape[0]
    tps = s // TS
    nlt = nb * tps
    fwd = lambda b, n: jnp.where(n == 0, nlt + b, b * tps + n - 1)
    bwd = lambda b, n: jnp.where(n == 0, nlt + b, b * tps + tps - n)
    ns, c, nsc = S5_NS, S5_GROUPS * S5_GROUP_CH, TS // S5_SUB

    def both(a):
        return [pl.BlockSpec((None,) + a.shape[1:], lambda b, n, d=d: (d,) + (0,) * (a.ndim - 1)) for d in range(2)]

    state = lambda *lead: pltpu.VMEM((2, 2) + lead + (S5_SUB, ns), F32)
    return pl.pallas_call(
        _s5_scan_kernel,
        grid=(nb, tps + 1),
        in_specs=[pl.BlockSpec((TS, c), lambda b, n: (fwd(b, n), C_U // 256)),
                  pl.BlockSpec((TS, c), lambda b, n: (bwd(b, n), C_U // 256))]
                 + both(bb) + both(kst) + both(cbd) + both(tab),
        out_specs=[pl.BlockSpec((TS, c), lambda b, n: (fwd(b, n), 0)),
                   pl.BlockSpec((TS, c), lambda b, n: (bwd(b, n), 0))],
        out_shape=[jax.ShapeDtypeStruct((nt_rows, c), F32)] * 2,
        scratch_shapes=[state(), state(nsc), state(nsc)],
        compiler_params=_cparams(("parallel", "arbitrary"), 56),
        name="s5_scan",
    )(z, z, bb, bb, kst, kst, cbd, cbd, tab, tab)


def _s5_finish_kernel(yf_ref, yb_ref, u_ref, d_ref, w_ref, b_ref, o_ref):
    y = yf_ref[...] + yb_ref[...] + d_ref[...] * u_ref[...].astype(F32)
    y = jax.nn.gelu(y)
    gate = _sigmoid(_dot(y.astype(BF16), w_ref[...]) + b_ref[...])
    o_ref[...] = (y * gate).astype(BF16)


def _s5_finish(yf, yb, z, d_row, glu_w, glu_b_row):
    nt = yf.shape[0] // TM
    full = lambda a: pl.BlockSpec(a.shape, lambda i: (0,) * a.ndim)
    return pl.pallas_call(
        _s5_finish_kernel,
        grid=(nt,),
        in_specs=[pl.BlockSpec((TM, 256), lambda i: (i, 0)),
                  pl.BlockSpec((TM, 256), lambda i: (i, 0)),
                  pl.BlockSpec((TM, 256), lambda i: (i, C_U // 256)),
                  full(d_row), full(glu_w), full(glu_b_row)],
        out_specs=pl.BlockSpec((TM, 256), lambda i: (i, 0)),
        out_shape=jax.ShapeDtypeStruct((yf.shape[0], 256), BF16),
        compiler_params=_cparams(("parallel",)),
        name="s5_finish",
    )(yf, yb, z, d_row, glu_w, glu_b_row)


def _merge_kernel(x_ref, y0_ref, y1_ref, y2_ref, y3_ref, gates_ref, wb_ref, wo_ref, g1_ref, gain_ref, o_ref):
    acc = jnp.zeros((TM, D_MODEL), F32)
    for bi, y_ref in enumerate((y0_ref, y1_ref, y2_ref, y3_ref)):
        proj = _dot(y_ref[...], wb_ref[bi])
        acc = acc + gates_ref[:, bi * D_MODEL:(bi + 1) * D_MODEL].astype(F32) * proj
    y = _dot(acc.astype(BF16), wo_ref[...])
    o_ref[...] = x_ref[...] + g1_ref[...] * _rms(y, gain_ref[...])


def _merge(x, ys, gates, wb, wo, g1, gain, layer, tps, nb):
    nt = x.shape[0] // TM
    row = lambda i: (jnp.minimum(i // tps, nb), 0, 0)
    tile = lambda w: pl.BlockSpec((TM, w), lambda i: (i, 0))
    return pl.pallas_call(
        _merge_kernel,
        grid=(nt,),
        in_specs=[tile(D_MODEL)] + [tile(BRANCH_W)] * 4 + [tile(4 * D_MODEL),
                  pl.BlockSpec((None, 4, BRANCH_W, D_MODEL), lambda i: (layer, 0, 0, 0)),
                  pl.BlockSpec((None, D_MODEL, D_MODEL), lambda i: (layer, 0, 0)),
                  pl.BlockSpec((None, 1, D_MODEL), row),
                  pl.BlockSpec((1, D_MODEL), lambda i: (0, 0))],
        out_specs=tile(D_MODEL),
        out_shape=jax.ShapeDtypeStruct(x.shape, F32),
        compiler_params=_cparams(("parallel",), 48),
        name="merge",
    )(x, *ys, gates, wb, wo, g1, gain)


def _mlp_kernel(x_ref, sc_ref, sh_ref, g2_ref, gin_ref, gout_ref, w1_ref, w2_ref, o_ref, h_ref, acc_ref):
    j = pl.program_id(1)

    @pl.when(j == 0)
    def _():
        h_ref[...] = _norm_mod(x_ref[...], gin_ref[...], sc_ref[...], sh_ref[...]).astype(BF16)
        acc_ref[...] = jnp.zeros(acc_ref.shape, F32)

    t = jnp.maximum(_dot(h_ref[...], w1_ref[...]), 0.0)
    acc_ref[...] += _dot((t * t).astype(BF16), w2_ref[...])

    @pl.when(j == pl.num_programs(1) - 1)
    def _():
        o_ref[...] = x_ref[...] + g2_ref[...] * _rms(acc_ref[...], gout_ref[...])


def _mlp(x, sc, sh, g2, gin, gout, w1, w2, layer, tps, nb, tf=1024):
    nt = x.shape[0] // TM
    row = lambda i, j: (jnp.minimum(i // tps, nb), 0, 0)
    vec = pl.BlockSpec((1, D_MODEL), lambda i, j: (0, 0))
    return pl.pallas_call(
        _mlp_kernel,
        grid=(nt, D_FF // tf),
        in_specs=[pl.BlockSpec((TM, D_MODEL), lambda i, j: (i, 0)),
                  pl.BlockSpec((None, 1, D_MODEL), row), pl.BlockSpec((None, 1, D_MODEL), row),
                  pl.BlockSpec((None, 1, D_MODEL), row), vec, vec,
                  pl.BlockSpec((None, D_MODEL, tf), lambda i, j: (layer, 0, j)),
                  pl.BlockSpec((None, tf, D_MODEL), lambda i, j: (layer, j, 0))],
        out_specs=pl.BlockSpec((TM, D_MODEL), lambda i, j: (i, 0)),
        out_shape=jax.ShapeDtypeStruct(x.shape, F32),
        scratch_shapes=[pltpu.VMEM((TM, D_MODEL), BF16), pltpu.VMEM((TM, D_MODEL), F32)],
        compiler_params=_cparams(("parallel", "arbitrary"), 48),
        name="mlp",
    )(x, sc, sh, g2, gin, gout, w1, w2)


def _pack_w_in(w_in):
    o_na, o_cq, o_ckv, o_gdn, o_z, o_a, o_b, o_u, o_gate = 0, 768, 1024, 1184, 1952, 2208, 2216, 2224, 2480
    idx = np.zeros(ZW, np.int32)
    keep = np.zeros(ZW, np.float32)

    def put(dst, src):
        idx[dst:dst + len(src)] = src
        keep[dst:dst + len(src)] = 1.0

    put(C_GDN, np.arange(o_gdn, o_gdn + 768))
    put(C_NA, np.arange(o_na, o_na + 768))
    put(C_CQ, np.arange(o_cq, o_cq + 256))
    put(C_CKV, np.arange(o_ckv, o_ckv + 160))
    put(C_CKV + L_A, np.arange(o_a, o_a + 8))
    put(C_CKV + L_B, np.arange(o_b, o_b + 8))
    put(C_CKV + L_KRS, o_ckv + MLA_KV_LORA + ROPE_SWAP)
    put(C_Z, np.arange(o_z, o_z + 256))
    put(C_U, np.arange(o_u, o_u + 256))
    small = (jnp.take(w_in, jnp.asarray(idx), axis=2) * jnp.asarray(keep)).astype(BF16)
    gates = w_in[:, :, o_gate:].astype(BF16)
    return small, gates


def _mla_weights(w_uq, w_ukv):
    depth = w_uq.shape[0]
    hq = MLA_NOPE + MLA_ROPE
    wq = w_uq.reshape(depth, -1, N_HEADS, hq)
    pad = lambda a, lo, hi: jnp.pad(a, ((0, 0), (0, 0), (0, 0), (lo, hi)))
    wq_ext = pad(wq, 0, 128 - hq).reshape(depth, -1, N_HEADS * 128)
    wq_sw = pad(wq[..., MLA_NOPE:][..., ROPE_SWAP], MLA_NOPE, 128 - hq).reshape(depth, -1, N_HEADS * 128)
    wkv = w_ukv.reshape(depth, -1, N_HEADS, MLA_NOPE + HEAD_W)
    wk = pad(wkv[..., :MLA_NOPE], 0, 128 - MLA_NOPE).reshape(depth, -1, N_HEADS * 128)
    wv = wkv[..., MLA_NOPE:].reshape(depth, -1, N_HEADS * HEAD_W)
    return [a.astype(BF16) for a in (wq_ext, wq_sw, wk, wv)]


def _rope_place_mats():
    p1 = np.zeros((256, N_HEADS * 128), np.float32)
    p2 = np.zeros((256, N_HEADS * 128), np.float32)
    for h in range(N_HEADS):
        for r in range(MLA_ROPE):
            p1[L_KR + r, h * 128 + MLA_NOPE + r] = 1.0
            p2[L_KRS + r, h * 128 + MLA_NOPE + r] = 1.0
    return jnp.asarray(p1, BF16), jnp.asarray(p2, BF16)


def _rope_tables(s):
    quarter = MLA_ROPE // 4
    inv_freq = ROPE_BASE ** (-jnp.arange(quarter, dtype=F32) / quarter)
    t = jnp.arange(s)
    ang_r = (t // GRID_W).astype(F32)[:, None] * inv_freq[None, :]
    ang_c = (t % GRID_W).astype(F32)[:, None] * inv_freq[None, :]
    cr, sr, cc, sn = jnp.cos(ang_r), jnp.sin(ang_r), jnp.cos(ang_c), jnp.sin(ang_c)
    cos = jnp.concatenate([jnp.ones((s, MLA_NOPE), F32), cr, cr, cc, cc, jnp.ones((s, 32), F32)], axis=1)
    sin = jnp.concatenate([jnp.zeros((s, MLA_NOPE), F32), -sr, sr, -sn, sn, jnp.zeros((s, 32), F32)], axis=1)
    cos = jnp.concatenate([cos, jnp.ones((TM, 128), F32)], axis=0)
    sin = jnp.concatenate([sin, jnp.zeros((TM, 128), F32)], axis=0)
    return cos, sin


def _head_block_ones():
    r = np.arange(BRANCH_W)
    return jnp.asarray((r[:, None] // HEAD_W == r[None, :] // HEAD_W).astype(np.float32))


def _gdn_expand_mats():
    e = np.zeros((2, 256, 2 * N_HEADS * 128), np.float32)
    for dh in range(2 * N_HEADS):
        e[0, L_A + dh, dh * 128:(dh + 1) * 128] = 1.0
        e[1, L_B + dh, dh * 128:(dh + 1) * 128] = 1.0
    return jnp.asarray(e)


def _lane_row(vals, offset, width=256):
    return jnp.zeros((1, width), F32).at[0, offset:offset + vals.shape[0]].set(vals.astype(F32))


def _gdn_mixer(z, conv_w, a_log, dt_bias, norm_w, consts, nb, s, ctx_len):
    bseg, ef = consts["bseg"], consts["ef"]
    alog_row = _lane_row(a_log.reshape(-1), L_A)
    dtb_row = _lane_row(dt_bias.reshape(-1), L_A)
    qk, kk, vk, gcx, btx, gt = _gdn_prep(z, conv_w, alog_row, dtb_row, bseg, ef, consts["sel"], s // TS, nb * s // TS)
    of, ob = _gdn_scan(qk, kk, vk, gcx, btx, gt, nb, s)
    return _gdn_finish(of, ob, z, norm_w.astype(F32)[None, :])


def _s5_mixer(z, a_re, a_im, log_dt, b_re, b_im, c_re, c_im, d_skip, glu_w, glu_b, nb, s, ctx_len):
    bb, kst, cbd, tab = _s5_params(a_re, a_im, log_dt, b_re, b_im, c_re, c_im)
    yf, yb = _s5_scan(z, bb, kst, cbd, tab, nb, s)
    return _s5_finish(yf, yb, z, d_skip.astype(F32)[None, :], glu_w.astype(BF16), glu_b.astype(F32)[None, :])


def kernel(x, c, ctx, c_ctx, ada_w, ada_b, norm_gains, w_in, na_rpb, mla_q_norm, mla_kv_norm, mla_w_uq, mla_w_ukv, gdn_conv, gdn_a_log, gdn_dt_bias, gdn_norm, s5_a_re, s5_a_im, s5_log_dt, s5_b_re, s5_b_im, s5_c_re, s5_c_im, s5_d, s5_glu_w, s5_glu_b, w_branch, w_out, mlp_w1, mlp_w2):
    nb, s, d = x.shape
    ctx_len = ctx.shape[1]
    depth = ada_w.shape[0]
    assert d == D_MODEL and ctx_len == TS and nb * ctx_len == TM and s % TM == 0 and s // GRID_W >= 16
    nl = nb * s
    tps = s // TM

    xs = jnp.concatenate([x.reshape(nl, d), ctx.reshape(nb * ctx_len, d)], axis=0)
    cvec = jnp.zeros((8, d), F32).at[:nb].set(c).at[nb].set(c_ctx)
    mod = _modulation(cvec, ada_w, ada_b)
    mod = mod[:, :nb + 1].reshape(depth, nb + 1, 6, 1, d)

    w_small, w_gates = _pack_w_in(w_in)
    wq_ext, wq_sw, wk_ext, wv = _mla_weights(mla_w_uq, mla_w_ukv)
    p1, p2 = _rope_place_mats()
    cos_t, sin_t = _rope_tables(s)
    sel = np.zeros((8, 256), np.float32)
    sel[np.arange(8), L_A + np.arange(8)] = 1.0
    consts = {"bseg": _head_block_ones(), "ef": _gdn_expand_mats(), "sel": jnp.asarray(sel)}
    wb = w_branch.astype(BF16)
    wo = w_out.astype(BF16)
    w1 = mlp_w1.astype(BF16)
    w2 = mlp_w2.astype(BF16)
    gains = norm_gains.astype(F32)

    for l in range(depth):
        sh1, sc1, g1, sh2, sc2, g2 = [mod[l, :, i] for i in range(6)]
        z = _inproj(xs, sc1, sh1, gains[l, 0][None], w_small, l, tps, nb, False)
        gates = _inproj(xs, sc1, sh1, gains[l, 0][None], w_gates, l, tps, nb, True)

        y_na = jnp.concatenate([
            _na_latent(z, na_rpb[l], nb, s),
            _ctx_attention(z, z, z, C_NA // 256, C_NA // 256 + 1, C_NA // 256 + 2, 256, nl // TS, nb,
                           HEAD_W ** -0.5, "na_ctx")], axis=0)

        qm, km, vm = _mla_prep(z, cos_t, sin_t, mla_q_norm[l].astype(F32)[None], mla_kv_norm[l].astype(F32)[None],
                               wq_ext[l], wq_sw[l], wk_ext[l], wv[l], p1, p2, tps, nl // TM)
        y_mla = jnp.concatenate([
            _mla_latent(qm, km, vm, nb, s),
            _ctx_attention(qm, km, vm, 0, 0, 0, N_HEADS * 128, nl // TS, nb, 1.0, "mla_ctx")], axis=0)

        y_gdn = _gdn_mixer(z, gdn_conv[l].astype(F32), gdn_a_log[l], gdn_dt_bias[l], gdn_norm[l], consts,
                           nb, s, ctx_len)
        y_s5 = _s5_mixer(z, s5_a_re[l], s5_a_im[l], s5_log_dt[l], s5_b_re[l], s5_b_im[l], s5_c_re[l], s5_c_im[l],
                         s5_d[l], s5_glu_w[l], s5_glu_b[l], nb, s, ctx_len)

        xs = _merge(xs, (y_na, y_mla, y_gdn, y_s5), gates, wb, wo, g1, gains[l, 1][None], l, tps, nb)
        xs = _mlp(xs, sc2, sh2, g2, gains[l, 2][None], gains[l, 3][None], w1, w2, l, tps, nb)
    return xs[:nl].reshape(nb, s, d)
```

```python
import functools
import math

import numpy as np
import jax
import jax.numpy as jnp
from jax import lax
from jax.experimental import pallas as pl
from jax.experimental.pallas import tpu as pltpu

F32 = jnp.float32
BF16 = jnp.bfloat16
HI = lax.Precision.HIGHEST
EPS = 1e-6

D_MODEL = 1024
GRID_W = 64
NA_WIN_H = 8
NA_WIN_W = 16
N_HEADS = 4
HEAD_W = 64
BRANCH_W = 256
MLA_NOPE = 64
MLA_ROPE = 32
MLA_KV_LORA = 128
ROPE_BASE = 10000.0
GDN_CONV = 4
CHUNK = 64
S5_GROUPS = 16
S5_GROUP_CH = 16
S5_STATE = 64
D_FF = 4 * D_MODEL

TM = 512
TN = 512
TS = 256
NEG = -1e30

C_GDN = 0
C_NA = 768
C_CQ = 1536
C_CKV = 1792
C_Z = 2048
C_U = 2304
ZW = 2560
L_KR = 128
L_A = 160
L_B = 168
L_KRS = 176
ROPE_SWAP = np.concatenate([np.arange(8, 16), np.arange(0, 8), np.arange(24, 32), np.arange(16, 24)])


def _cparams(sem, vmem_mb=None):
    kw = dict(dimension_semantics=sem)
    if vmem_mb is not None:
        kw["vmem_limit_bytes"] = vmem_mb * 1024 * 1024
    return pltpu.CompilerParams(**kw)


def _dot(a, b, **kw):
    return jnp.dot(a, b, preferred_element_type=F32, **kw)


def _dot_nt(a, b):
    return lax.dot_general(a, b, (((1,), (1,)), ((), ())), preferred_element_type=F32)


def _sigmoid(x):
    return 1.0 / (1.0 + jnp.exp(-x))


def _silu(x):
    return x * _sigmoid(x)


def _mod_kernel(c_ref, w_ref, b_ref, o_ref):
    c = c_ref[...]
    o_ref[...] = _dot(_silu(c), w_ref[...], precision=HI) + b_ref[...]


def _modulation(cvec, ada_w, ada_b):
    depth, d, n = ada_w.shape
    tn = 1536
    return pl.pallas_call(
        _mod_kernel,
        grid=(depth, n // tn),
        in_specs=[pl.BlockSpec((8, d), lambda l, j: (0, 0)),
                  pl.BlockSpec((None, d, tn), lambda l, j: (l, 0, j)),
                  pl.BlockSpec((None, 1, tn), lambda l, j: (l, 0, j))],
        out_specs=pl.BlockSpec((None, 8, tn), lambda l, j: (l, 0, j)),
        out_shape=jax.ShapeDtypeStruct((depth, 8, n), F32),
        compiler_params=_cparams(("parallel", "parallel"), 40),
        name="modulation",
    )(cvec, ada_w, ada_b.reshape(depth, 1, n))


def _norm_mod(x, gain, sc, sh):
    r = lax.rsqrt(jnp.mean(x * x, axis=-1, keepdims=True) + EPS)
    return (x * r * gain) * (1.0 + sc) + sh


def _inproj_kernel(x_ref, sc_ref, sh_ref, gain_ref, w_ref, o_ref, h_ref, *, gate):
    @pl.when(pl.program_id(1) == 0)
    def _():
        h_ref[...] = _norm_mod(x_ref[...], gain_ref[...], sc_ref[...], sh_ref[...]).astype(BF16)

    acc = _dot(h_ref[...], w_ref[...])
    if gate:
        acc = _sigmoid(acc)
    o_ref[...] = acc.astype(BF16)


def _inproj(x, sc, sh, gain, w, layer, tps, nb, gate):
    nt = x.shape[0] // TM
    width = w.shape[-1]
    tn = width // 2
    row = lambda i, j: (jnp.minimum(i // tps, nb), 0, 0)
    return pl.pallas_call(
        functools.partial(_inproj_kernel, gate=gate),
        grid=(nt, width // tn),
        in_specs=[pl.BlockSpec((TM, D_MODEL), lambda i, j: (i, 0)),
                  pl.BlockSpec((None, 1, D_MODEL), row),
                  pl.BlockSpec((None, 1, D_MODEL), row),
                  pl.BlockSpec((1, D_MODEL), lambda i, j: (0, 0)),
                  pl.BlockSpec((None, D_MODEL, tn), lambda i, j: (layer, 0, j))],
        out_specs=pl.BlockSpec((TM, tn), lambda i, j: (i, j)),
        out_shape=jax.ShapeDtypeStruct((x.shape[0], width), BF16),
        scratch_shapes=[pltpu.VMEM((TM, D_MODEL), BF16)],
        compiler_params=_cparams(("parallel", "arbitrary"), 40),
        name="inproj_gates" if gate else "inproj",
    )(x, sc, sh, gain, w)


def _head_lane_mask(width, head_w, h):
    lane = lax.broadcasted_iota(jnp.int32, (1, width), 1)
    return (lane >= h * head_w) & (lane < (h + 1) * head_w)


def _na_build_bias(rpb_ref, bias_ref, r0, kb0, rows_total):
    w = GRID_W
    qc = lax.broadcasted_iota(jnp.int32, (w, 2 * w), 0)
    lane = lax.broadcasted_iota(jnp.int32, (w, 2 * w), 1)
    kc = lane % w
    cs = jnp.clip(qc - NA_WIN_W // 2, 0, w - NA_WIN_W)
    col_ok = (kc >= cs) & (kc < cs + NA_WIN_W)
    left = lane < w
    neg = jnp.full((w, 2 * w), NEG, F32)
    for h in range(N_HEADS):
        t = rpb_ref[h]
        toep = []
        for a in range(2 * NA_WIN_H - 1):
            row = jnp.broadcast_to(t[a:a + 1, :], (w, 2 * w))
            ra = pltpu.roll(row, 2 * w - (NA_WIN_W - 1), axis=1, stride=1, stride_axis=0)
            rb = pltpu.roll(ra, w, axis=1)
            toep.append((jnp.where(col_ok, ra, NEG), jnp.where(col_ok, rb, NEG)))
        for qr in range(8):
            rs = min(max(r0 + qr - NA_WIN_H // 2, 0), rows_total - NA_WIN_H)
            for kp in range(8):
                halves = []
                for side in range(2):
                    kr = kb0 + 2 * kp + side
                    halves.append(toep[kr - (r0 + qr) + NA_WIN_H - 1][side] if rs <= kr < rs + NA_WIN_H else neg)
                bias_ref[h, qr * w:(qr + 1) * w, kp * 2 * w:(kp + 1) * 2 * w] = jnp.where(left, halves[0], halves[1])


def _na_kernel(q_ref, k_ref, v_ref, kc_ref, vc_ref, rpb_ref, o_ref, bias_ref, *, rows_total):
    i = pl.program_id(1)
    last = pl.num_programs(1) - 1

    @pl.when(i == 0)
    def _():
        _na_build_bias(rpb_ref, bias_ref, 0, 0, rows_total)

    @pl.when(i == 1)
    def _():
        _na_build_bias(rpb_ref, bias_ref, 8, 4, rows_total)

    @pl.when(i == last)
    def _():
        _na_build_bias(rpb_ref, bias_ref, rows_total - 8, rows_total - 16, rows_total)

    kb = jnp.clip(2 * i - 1, 0, rows_total // 4 - 4)
    start = pl.multiple_of(kb * (4 * GRID_W), 4 * GRID_W)
    nk = 2 * NA_WIN_H * GRID_W
    q = q_ref[...]
    kw = k_ref[pl.ds(start, nk), :]
    vw = v_ref[pl.ds(start, nk), :]
    kc = kc_ref[...]
    vc = vc_ref[...]
    scale = HEAD_W ** -0.5
    out = jnp.zeros(q.shape, F32)
    for h in range(N_HEADS):
        hm = _head_lane_mask(BRANCH_W, HEAD_W, h)
        qh = jnp.where(hm, q, jnp.zeros_like(q))
        sb = _dot_nt(qh, kw) * scale + bias_ref[h]
        sc = _dot_nt(qh, kc) * scale
        m = jnp.maximum(jnp.max(sb, axis=-1, keepdims=True), jnp.max(sc, axis=-1, keepdims=True))
        pb = jnp.exp(sb - m)
        pc = jnp.exp(sc - m)
        den = jnp.sum(pb, axis=-1, keepdims=True) + jnp.sum(pc, axis=-1, keepdims=True)
        o = _dot(pb.astype(BF16), vw) + _dot(pc.astype(BF16), vc)
        out = jnp.where(hm, o / den, out)
    o_ref[...] = out.astype(BF16)


def _na_latent(z, rpb, nb, s):
    rows_total = s // GRID_W
    qb = 8 * GRID_W
    nq = s // qb
    nl = nb * s
    rpb = jnp.pad(rpb.astype(F32), ((0, 0), (0, 1), (0, 2 * GRID_W - (2 * NA_WIN_W - 1))))
    return pl.pallas_call(
        functools.partial(_na_kernel, rows_total=rows_total),
        grid=(nb, nq),
        in_specs=[pl.BlockSpec((qb, BRANCH_W), lambda b, i: (b * nq + i, C_NA // 256)),
                  pl.BlockSpec((s, BRANCH_W), lambda b, i: (b, C_NA // 256 + 1)),
                  pl.BlockSpec((s, BRANCH_W), lambda b, i: (b, C_NA // 256 + 2)),
                  pl.BlockSpec((TS, BRANCH_W), lambda b, i: (nl // TS + b, C_NA // 256 + 1)),
                  pl.BlockSpec((TS, BRANCH_W), lambda b, i: (nl // TS + b, C_NA // 256 + 2)),
                  pl.BlockSpec(rpb.shape, lambda b, i: (0, 0, 0))],
        out_specs=pl.BlockSpec((qb, BRANCH_W), lambda b, i: (b * nq + i, 0)),
        out_shape=jax.ShapeDtypeStruct((nl, BRANCH_W), BF16),
        scratch_shapes=[pltpu.VMEM((N_HEADS, qb, 2 * qb), F32)],
        compiler_params=_cparams(("parallel", "arbitrary"), 56),
        name="na_latent",
    )(z, z, z, z, z, rpb)


def _ctx_attn_kernel(q_ref, k_ref, v_ref, o_ref, *, scale, base2):
    q = q_ref[...]
    k = k_ref[...]
    v = v_ref[...]
    qw = q.shape[-1]
    vw = v.shape[-1] // N_HEADS
    outs = []
    for h in range(N_HEADS):
        qh = jnp.where(_head_lane_mask(qw, qw // N_HEADS, h), q, jnp.zeros_like(q))
        s = _dot_nt(qh, k) * scale
        m = jnp.max(s, axis=-1, keepdims=True)
        p = jnp.exp2(s - m) if base2 else jnp.exp(s - m)
        den = jnp.sum(p, axis=-1, keepdims=True)
        o = _dot(p.astype(BF16), v)
        outs.append(o[:, h * vw:h * vw + HEAD_W] / den)
    o_ref[...] = jnp.concatenate(outs, axis=-1).astype(BF16)


def _ctx_attention(q, k, v, qcol, kcol, vcol, qw, vw, row0, nb, scale, base2, name):
    return pl.pallas_call(
        functools.partial(_ctx_attn_kernel, scale=scale, base2=base2),
        grid=(nb,),
        in_specs=[pl.BlockSpec((TS, qw), lambda b: (row0 + b, qcol)),
                  pl.BlockSpec((TS, qw), lambda b: (row0 + b, kcol)),
                  pl.BlockSpec((TS, vw), lambda b: (row0 + b, vcol))],
        out_specs=pl.BlockSpec((TS, BRANCH_W), lambda b: (b, 0)),
        out_shape=jax.ShapeDtypeStruct((nb * TS, BRANCH_W), BF16),
        compiler_params=_cparams(("parallel",)),
        name=name,
    )(q, k, v)


def _rms(x, gain):
    return x * lax.rsqrt(jnp.mean(x * x, axis=-1, keepdims=True) + EPS) * gain


def _mla_prep_kernel(cq_ref, ckv_ref, cos_ref, sin_ref, qn_ref, kvn_ref, wq_ref, wqs_ref, wk_ref, wv_ref, wvt_ref,
                     p1_ref, p2_ref, q_ref, k_ref, v_ref, vt_ref):
    cos = jnp.concatenate([cos_ref[...]] * N_HEADS, axis=-1)
    sin = jnp.concatenate([sin_ref[...]] * N_HEADS, axis=-1)
    cqn = _rms(cq_ref[...].astype(F32), qn_ref[...]).astype(BF16)
    scale = (MLA_NOPE + MLA_ROPE) ** -0.5 * math.log2(math.e)
    q = _dot(cqn, wq_ref[...]) * cos + _dot(cqn, wqs_ref[...]) * sin
    q_ref[...] = (q * scale).astype(BF16)
    ckv = ckv_ref[...]
    kvn = _rms(ckv[:, :MLA_KV_LORA].astype(F32), kvn_ref[...]).astype(BF16)
    k = (_dot(kvn, wk_ref[...]) + _dot(ckv, p1_ref[...])) * cos + _dot(ckv, p2_ref[...]) * sin
    k_ref[...] = k.astype(BF16)
    lane = lax.broadcasted_iota(jnp.int32, (1, N_HEADS * 128), 1)
    v_ref[...] = jnp.where(lane % 128 == HEAD_W, 1.0, _dot(kvn, wv_ref[...])).astype(BF16)
    row = lax.broadcasted_iota(jnp.int32, (N_HEADS * 128, 1), 0)
    vt_ref[...] = jnp.where(row % 128 == HEAD_W, 1.0, _dot_nt(wvt_ref[...], kvn)).astype(BF16)


def _mla_prep(z, cos_t, sin_t, qn, kvn, wq, wqs, wk, wv, wvt, p1, p2, tps, n_lat_tiles):
    nt_rows = z.shape[0]
    nt = nt_rows // TM
    full = lambda a: pl.BlockSpec(a.shape, lambda i: (0,) * a.ndim)
    tab = lambda i: (jnp.where(i < n_lat_tiles, i % tps, tps), 0)
    hw = N_HEADS * 128
    return pl.pallas_call(
        _mla_prep_kernel,
        grid=(nt,),
        in_specs=[pl.BlockSpec((TM, 256), lambda i: (i, C_CQ // 256)),
                  pl.BlockSpec((TM, 256), lambda i: (i, C_CKV // 256)),
                  pl.BlockSpec((TM, 128), tab), pl.BlockSpec((TM, 128), tab),
                  full(qn), full(kvn), full(wq), full(wqs), full(wk), full(wv), full(wvt), full(p1), full(p2)],
        out_specs=[pl.BlockSpec((TM, hw), lambda i: (i, 0)),
                   pl.BlockSpec((TM, hw), lambda i: (i, 0)),
                   pl.BlockSpec((TM, hw), lambda i: (i, 0)),
                   pl.BlockSpec((hw, TM), lambda i: (0, i))],
        out_shape=[jax.ShapeDtypeStruct((nt_rows, hw), BF16)] * 3 + [jax.ShapeDtypeStruct((hw, nt_rows), BF16)],
        compiler_params=_cparams(("parallel",)),
        name="mla_prep",
    )(z, z, cos_t, sin_t, qn, kvn, wq, wqs, wk, wv, wvt, p1, p2)


def _flash_kernel(q_ref, kl_ref, vl_ref, kc_ref, vc_ref, o_ref, *, tk, n_lat):
    tq = q_ref.shape[0]
    heads = (slice(0, 128), slice(128, 256))
    qs = [q_ref[:, hs] for hs in heads]

    def update(carry, ks, vts):
        new = []
        for hh in range(2):
            m, acc = carry[hh]
            st = _dot_nt(ks[hh], qs[hh])
            m_new = jnp.maximum(m, jnp.max(st, axis=0, keepdims=True))
            p = jnp.exp2(st - m_new)
            new.append((m_new, jnp.exp2(m - m_new) * acc + _dot(vts[hh], p.astype(BF16))))
        return tuple(new)

    def body(t, carry):
        r0 = pl.multiple_of(t * tk, tk)
        return update(carry, [kl_ref[pl.ds(r0, tk), hs] for hs in heads], [vl_ref[hs, pl.ds(r0, tk)] for hs in heads])

    init = tuple((jnp.full((1, tq), NEG, F32), jnp.zeros((128, tq), F32)) for _ in range(2))
    carry = lax.fori_loop(0, n_lat, body, init)
    carry = update(carry, [kc_ref[:, hs] for hs in heads], [vc_ref[hs, :] for hs in heads])
    outs = []
    for _, acc in carry:
        acc = acc.T
        outs.append(acc[:, :HEAD_W] / acc[:, HEAD_W:HEAD_W + 1])
    o_ref[...] = jnp.concatenate(outs, axis=-1).astype(BF16)


def _mla_latent(qm, km, vt, nb, s, tq=512, tk=512):
    nq = s // tq
    nl = nb * s
    return pl.pallas_call(
        functools.partial(_flash_kernel, tk=tk, n_lat=s // tk),
        grid=(nb, 2, nq),
        in_specs=[pl.BlockSpec((tq, 256), lambda b, hp, i: (b * nq + i, hp)),
                  pl.BlockSpec((s, 256), lambda b, hp, i: (b, hp)),
                  pl.BlockSpec((256, s), lambda b, hp, i: (hp, b)),
                  pl.BlockSpec((TS, 256), lambda b, hp, i: (nl // TS + b, hp)),
                  pl.BlockSpec((256, TS), lambda b, hp, i: (hp, nl // TS + b))],
        out_specs=pl.BlockSpec((tq, 128), lambda b, hp, i: (b * nq + i, hp)),
        out_shape=jax.ShapeDtypeStruct((nl, BRANCH_W), BF16),
        compiler_params=_cparams(("parallel", "parallel", "arbitrary"), 48),
        name="mla_flash",
    )(qm, km, vt, km, vt)


def _gdn_prep_kernel(prev_ref, cur_ref, next_ref, ckv_ref, conv_ref, alog_ref, dtb_ref, bseg_ref, ef_ref, sel_ref,
                     qk_ref, kk_ref, vk_ref, gcx_ref, btx_ref, gt_ref, *, tps, n_lat_tiles):
    i = pl.program_id(0)
    is_ctx = i >= n_lat_tiles
    first = is_ctx | (i % tps == 0)
    last = is_ctx | (i % tps == tps - 1)
    prev = jnp.where(first, 0.0, prev_ref[...].astype(F32))
    nxt = jnp.where(last, 0.0, next_ref[...].astype(F32))
    ext = jnp.concatenate([prev, cur_ref[...].astype(F32), nxt], axis=0)
    n_ext = TS + 16
    acc = jnp.zeros((TS, 3 * BRANCH_W), F32)
    for j in range(GDN_CONV):
        shifted = pltpu.roll(ext, n_ext - (8 - GDN_CONV // 2 + j), axis=0)[:TS]
        acc = acc + shifted * conv_ref[j:j + 1, :]
    x = _silu(acc)
    bseg = bseg_ref[...]

    def l2n(a):
        return a * lax.rsqrt(_dot(a * a, bseg, precision=HI) + EPS)

    q = l2n(x[:, :BRANCH_W]) * (HEAD_W ** -0.5)
    k = l2n(x[:, BRANCH_W:2 * BRANCH_W])
    v = x[:, 2 * BRANCH_W:]
    for h in range(N_HEADS):
        hs = slice(h * HEAD_W, (h + 1) * HEAD_W)
        qk_ref[h] = jnp.concatenate([q[:, hs], k[:, hs]], axis=-1)
        kk_ref[h] = jnp.concatenate([k[:, hs], k[:, hs]], axis=-1)
        vk_ref[h] = jnp.concatenate([v[:, hs], k[:, hs]], axis=-1)

    ab = ckv_ref[...].astype(F32)
    sp_in = ab + dtb_ref[...]
    softplus = jnp.maximum(sp_in, 0.0) + jnp.log1p(jnp.exp(-jnp.abs(sp_in)))
    lane = lax.broadcasted_iota(jnp.int32, (1, 256), 1)
    g = jnp.where((lane >= L_A) & (lane < L_B), -jnp.exp(alog_ref[...]) * softplus, 0.0)
    beta = jnp.where((lane >= L_B) & (lane < L_KRS), _sigmoid(ab), 0.0)
    r = lax.broadcasted_iota(jnp.int32, (TS, TS), 0)
    c = lax.broadcasted_iota(jnp.int32, (TS, TS), 1)
    same = (r // CHUNK) == (c // CHUNK)
    lower = jnp.where(same & (r >= c), 1.0, 0.0)
    upper = jnp.where(same & (r <= c), 1.0, 0.0)
    fwd_lane = (lane >= L_A) & (lane < L_A + N_HEADS)
    gc = jnp.where(fwd_lane, _dot(lower, g, precision=HI), _dot(upper, g, precision=HI))
    gt_ref[...] = lax.dot_general(sel_ref[...], gc, (((1,), (1,)), ((), ())), precision=HI,
                                  preferred_element_type=F32)
    gcx_ref[...] = _dot(gc, ef_ref[0], precision=HI)
    btx_ref[...] = _dot(beta, ef_ref[1], precision=HI)


def _gdn_prep(z, conv_w, alog_row, dtb_row, bseg, ef, sel, tps_s, n_lat_tiles):
    nt_rows = z.shape[0]
    nt = nt_rows // TS
    hb = TS // 8
    full = lambda a: pl.BlockSpec(a.shape, lambda i: (0,) * a.ndim)
    xw = 2 * N_HEADS * 128
    hsp = pl.BlockSpec((N_HEADS, TS, 128), lambda i: (0, i, 0))
    return pl.pallas_call(
        functools.partial(_gdn_prep_kernel, tps=tps_s, n_lat_tiles=n_lat_tiles),
        grid=(nt,),
        in_specs=[pl.BlockSpec((8, 768), lambda i: (jnp.maximum(i * hb - 1, 0), 0)),
                  pl.BlockSpec((TS, 768), lambda i: (i, 0)),
                  pl.BlockSpec((8, 768), lambda i: (jnp.minimum((i + 1) * hb, nt * hb - 1), 0)),
                  pl.BlockSpec((TS, 256), lambda i: (i, C_CKV // 256)),
                  full(conv_w), full(alog_row), full(dtb_row), full(bseg), full(ef), full(sel)],
        out_specs=[hsp] * 3
                  + [pl.BlockSpec((TS, xw), lambda i: (i, 0))] * 2
                  + [pl.BlockSpec((None, 8, TS), lambda i: (i, 0, 0))],
        out_shape=[jax.ShapeDtypeStruct((N_HEADS, nt_rows, 128), F32)] * 3
                  + [jax.ShapeDtypeStruct((nt_rows, xw), F32)] * 2
                  + [jax.ShapeDtypeStruct((nt, 8, TS), F32)],
        compiler_params=_cparams(("parallel",)),
        name="gdn_prep",
    )(z, z, z, z, conv_w, alog_row, dtb_row, bseg, ef, sel)


def _tri_solve(n, x, reverse):
    h = n.shape[0]
    bs = 16
    nblk = CHUNK // bs
    r = lax.broadcasted_iota(jnp.int32, (CHUNK, CHUNK), 0)
    c = lax.broadcasted_iota(jnp.int32, (CHUNK, CHUNK), 1)
    off = jnp.where((r // bs != c // bs)[None], n, 0.0)
    diag = jnp.stack([n[:, bs * b:bs * (b + 1), bs * b:bs * (b + 1)] for b in range(nblk)], axis=1)

    split = lambda a: (a.reshape(h, nblk, bs, a.shape[-1])[:, :, :8], a.reshape(h, nblk, bs, a.shape[-1])[:, :, 8:])
    halves = [list(split(x)), list(split(off))]
    d_top, d_bot = diag[:, :, :8], diag[:, :, 8:]
    for j in (range(bs - 1, 0, -1) if reverse else range(bs - 1)):
        upd_top = reverse or j < 8
        upd_bot = (not reverse) or j >= 8
        c_top = jnp.broadcast_to(d_top[..., j:j + 1], (h, nblk, 8, 2 * HEAD_W)) if upd_top else None
        c_bot = jnp.broadcast_to(d_bot[..., j:j + 1], (h, nblk, 8, 2 * HEAD_W)) if upd_bot else None
        for w in halves:
            wj = w[j // 8][:, :, j % 8:j % 8 + 1, :]
            lanes = wj.shape[-1]
            if upd_top:
                w[0] = w[0] - c_top[..., :lanes] * wj
            if upd_bot:
                w[1] = w[1] - c_bot[..., :lanes] * wj
    z, m = [jnp.concatenate(w, axis=2).reshape(h, CHUNK, -1) for w in halves]

    bmm = lambda a, b: jnp.einsum("hij,hjk->hik", a.astype(BF16), b.astype(BF16), preferred_element_type=F32)
    w1 = z - bmm(m, z)
    return w1 + bmm(bmm(m, m), w1)


def _gdn_direction(qk_ref, kk_ref, vk_ref, gc_ref, bt_ref, gt_ref, o_ref, s_ref, d, reverse):
    c = CHUNK
    nh = N_HEADS
    r = lax.broadcasted_iota(jnp.int32, (c, c), 0)
    cc = lax.broadcasted_iota(jnp.int32, (c, c), 1)
    incl = ((r <= cc) if reverse else (r >= cc))[None]
    strict = ((r < cc) if reverse else (r > cc))[None]
    lo = lax.broadcasted_iota(jnp.int32, (1, 1, 2 * HEAD_W), 2) < HEAD_W
    bmm = lambda a, b: jnp.einsum("hij,hjk->hik", a, b, preferred_element_type=F32)
    bmm_nt = lambda a, b: jnp.einsum("hid,hjd->hij", a, b, preferred_element_type=F32)
    bmm_tn = lambda a, b: jnp.einsum("hcd,hce->hde", a, b, preferred_element_type=F32)
    n_chunks = TS // c
    for ci in (range(n_chunks - 1, -1, -1) if reverse else range(n_chunks)):
        sl = slice(ci * c, (ci + 1) * c)
        qk = qk_ref[:, sl, :]
        kk = kk_ref[:, sl, :]
        vk = vk_ref[:, sl, :]
        gc = jnp.stack([gc_ref[sl, 128 * h:128 * (h + 1)] for h in range(nh)])
        bt = jnp.stack([bt_ref[sl, 128 * h:128 * (h + 1)] for h in range(nh)])
        grow = jnp.stack([gt_ref[d * nh + h:d * nh + h + 1, sl] for h in range(nh)])
        dec = jnp.where(incl, jnp.exp(jnp.minimum(gc[:, :, :c] - grow, 0.0)), 0.0)
        kkb = kk.astype(BF16)
        k_dot_k = 0.5 * bmm_nt(kkb, kkb)
        q_only = jnp.where(lo, qk, 0.0)
        q_dot_k = bmm_nt(q_only.astype(BF16), kkb) * dec
        n = jnp.where(strict, bt[:, :, :c] * k_dot_k * dec, 0.0)
        egc = jnp.exp(gc)
        x = _tri_solve(n, vk * jnp.where(lo, bt, bt * egc), reverse)
        edge = 0 if reverse else c - 1
        glast = gc[:, edge:edge + 1, :]
        ktail = kk * jnp.exp(glast - gc)
        s2 = s_ref[d]
        s2b = s2.astype(BF16)
        v_new = x[:, :, :HEAD_W] - bmm(jnp.where(lo, 0.0, x).astype(BF16), s2b)
        vb = v_new.astype(BF16)
        o_ref[:, sl, :] = bmm((q_only * egc).astype(BF16), s2b) + bmm(q_dot_k.astype(BF16), vb)
        s_ref[d] = s2 * jnp.exp(glast)[:, :, :HEAD_W] + bmm_tn(ktail.astype(BF16), vb)


def _gdn_scan_kernel(qkf, qkb, kkf, kkb, vkf, vkb, gcf, gcb, btf, btb, gtf, gtb, of_ref, ob_ref, s_ref):
    @pl.when(pl.program_id(1) == 0)
    def _():
        s_ref[...] = jnp.zeros(s_ref.shape, F32)

    _gdn_direction(qkf, kkf, vkf, gcf, btf, gtf, of_ref, s_ref, 0, False)
    _gdn_direction(qkb, kkb, vkb, gcb, btb, gtb, ob_ref, s_ref, 1, True)


def _gdn_scan(qk, kk, vk, gcx, btx, gt, nb, s):
    h, nt_rows, _ = qk.shape
    tps = s // TS
    nlt = nb * tps
    fwd = lambda b, n: jnp.where(n == 0, nlt + b, b * tps + n - 1)
    bwd = lambda b, n: jnp.where(n == 0, nlt + b, b * tps + tps - n)
    hw = h * 128

    def views(shape, imap):
        return [pl.BlockSpec(shape, functools.partial(imap, t)) for t in (fwd, bwd)]

    heads = views((h, TS, 128), lambda t, b, n: (0, t(b, n), 0))
    lanes = [pl.BlockSpec((TS, hw), lambda b, n: (fwd(b, n), 0)), pl.BlockSpec((TS, hw), lambda b, n: (bwd(b, n), 1))]
    rows = views((None, 8, TS), lambda t, b, n: (t(b, n), 0, 0))
    outs = views((h, TS, HEAD_W), lambda t, b, n: (0, t(b, n), 0))
    return pl.pallas_call(
        _gdn_scan_kernel,
        grid=(nb, tps + 1),
        in_specs=heads * 3 + lanes * 2 + rows,
        out_specs=outs,
        out_shape=[jax.ShapeDtypeStruct((h, nt_rows, HEAD_W), F32)] * 2,
        scratch_shapes=[pltpu.VMEM((2, h, 2 * HEAD_W, HEAD_W), F32)],
        compiler_params=_cparams(("parallel", "arbitrary")),
        name="gdn_scan",
    )(qk, qk, kk, kk, vk, vk, gcx, gcx, btx, btx, gt, gt)


def _gdn_finish_kernel(of_ref, ob_ref, z_ref, nw_ref, y_ref):
    o = of_ref[...] + ob_ref[...]
    y = o * lax.rsqrt(jnp.mean(o * o, axis=-1, keepdims=True) + EPS) * nw_ref[...]
    y = jnp.concatenate([y[h] for h in range(N_HEADS)], axis=-1)
    y_ref[...] = (y * _silu(z_ref[...].astype(F32))).astype(BF16)


def _gdn_finish(of, ob, z, nw_row):
    nt = of.shape[1] // TM
    hsp = pl.BlockSpec((N_HEADS, TM, HEAD_W), lambda i: (0, i, 0))
    return pl.pallas_call(
        _gdn_finish_kernel,
        grid=(nt,),
        in_specs=[hsp, hsp, pl.BlockSpec((TM, 256), lambda i: (i, C_Z // 256)),
                  pl.BlockSpec(nw_row.shape, lambda i: (0, 0))],
        out_specs=pl.BlockSpec((TM, 256), lambda i: (i, 0)),
        out_shape=jax.ShapeDtypeStruct((of.shape[1], 256), BF16),
        compiler_params=_cparams(("parallel",)),
        name="gdn_finish",
    )(of, ob, z, nw_row)


S5_SUB = 8
S5_NS = S5_GROUPS * S5_STATE


def _s5_param_kernel(are_ref, aim_ref, ldt_ref, bre_ref, bim_ref, cre_ref, cim_ref,
                     bb_ref, kst_ref, cbd_ref, tab_ref):
    d = pl.program_id(0)
    lam_re = jnp.minimum(are_ref[...], -1e-4)
    lam_im = aim_ref[...]
    dt = jnp.exp(ldt_ref[...])

    def power(tau):
        mag = jnp.exp(lam_re * dt * tau)
        ang = lam_im * dt * tau
        return mag * jnp.cos(ang), mag * jnp.sin(ang)

    idx = lax.broadcasted_iota(jnp.int32, (S5_SUB, 1), 0)
    p_re, p_im = power(idx.astype(F32))
    lb_re, lb_im = p_re[1:2], p_im[1:2]
    den = lam_re * lam_re + lam_im * lam_im
    f_re = ((lb_re - 1.0) * lam_re + lb_im * lam_im) / den
    f_im = (lb_im * lam_re - (lb_re - 1.0) * lam_im) / den
    bb_re = f_re * bre_ref[...] - f_im * bim_ref[...]
    bb_im = f_re * bim_ref[...] + f_im * bre_ref[...]
    bb_ref[...] = jnp.concatenate([bb_re, bb_im], axis=-1).astype(BF16)
    c_re = cre_ref[...]
    c_im = cim_ref[...]
    cbd_ref[...] = jnp.concatenate([c_re, -c_im], axis=0).astype(BF16)
    for tau in range(S5_SUB):
        g_re = bb_re * p_re[tau:tau + 1] - bb_im * p_im[tau:tau + 1]
        g_im = bb_re * p_im[tau:tau + 1] + bb_im * p_re[tau:tau + 1]
        kst_ref[tau] = (_dot(g_re, c_re, precision=HI) - _dot(g_im, c_im, precision=HI)).astype(BF16)
    fwd = d == 0
    t_in = jnp.where(fwd, S5_SUB - 1 - idx, idx).astype(F32)
    t_out = jnp.where(fwd, idx + 1, S5_SUB - idx).astype(F32)
    for k, tau in enumerate((t_in, t_out, jnp.full((S5_SUB, 1), float(S5_SUB), F32))):
        tab_ref[k] = jnp.concatenate(power(tau), axis=-1)


def _s5_params(a_re, a_im, log_dt, b_re, b_im, c_re, c_im):
    g, p, gch = S5_GROUPS, S5_STATE, S5_GROUP_CH
    ns, c = S5_NS, g * gch
    eye = jnp.eye(g, dtype=F32)
    row = lambda a: a.astype(F32).reshape(2, 1, ns)
    ldt = jnp.broadcast_to(log_dt.astype(F32)[:, :, None], (2, g, p))
    b_bd = lambda b: jnp.einsum("gpc,gh->gchp", b.astype(F32), eye).reshape(c, ns)
    c_bd = lambda cc: jnp.einsum("dgcp,gh->dgphc", cc.astype(F32), eye).reshape(2, ns, c)
    per_dir = lambda *shape: pl.BlockSpec((None,) + shape, lambda d: (d,) + (0,) * len(shape))
    shared = pl.BlockSpec((c, ns), lambda d: (0, 0))
    return pl.pallas_call(
        _s5_param_kernel,
        grid=(2,),
        in_specs=[per_dir(1, ns)] * 3 + [shared] * 2 + [per_dir(ns, c)] * 2,
        out_specs=[per_dir(c, 2 * ns), per_dir(S5_SUB, c, c), per_dir(2 * ns, c), per_dir(3, S5_SUB, 2 * ns)],
        out_shape=[jax.ShapeDtypeStruct((2, c, 2 * ns), BF16),
                   jax.ShapeDtypeStruct((2, S5_SUB, c, c), BF16),
                   jax.ShapeDtypeStruct((2, 2 * ns, c), BF16),
                   jax.ShapeDtypeStruct((2, 3, S5_SUB, 2 * ns), F32)],
        compiler_params=_cparams(("parallel",), 48),
        name="s5_params",
    )(row(a_re), row(a_im), row(ldt), b_bd(b_re), b_bd(b_im), c_bd(c_re), c_bd(c_im))


def _s5_direction(u_ref, bb_ref, kst_ref, cbd_ref, tab_ref, y_ref, x_ref, xin_ref, xpv_ref, reverse):
    ns, sub = S5_NS, S5_SUB
    nsc = TS // sub
    u = u_ref[...]
    z = _dot(u, bb_ref[...])
    z_re = z[:, :ns].reshape(nsc, sub, ns)
    z_im = z[:, ns:].reshape(nsc, sub, ns)
    w = tab_ref[0]
    w_re, w_im = w[:, :ns][None], w[:, ns:][None]
    group_sum = lambda a: jnp.broadcast_to(jnp.sum(a, axis=1, keepdims=True), a.shape)
    xin_ref[0] = group_sum(w_re * z_re - w_im * z_im)
    xin_ref[1] = group_sum(w_re * z_im + w_im * z_re)
    a = tab_ref[2]
    a_re, a_im = a[:, :ns], a[:, ns:]

    def step(k, carry):
        x_re, x_im = carry
        n = nsc - 1 - k if reverse else k
        xpv_ref[0, n] = x_re
        xpv_ref[1, n] = x_im
        return (a_re * x_re - a_im * x_im + xin_ref[0, n], a_re * x_im + a_im * x_re + xin_ref[1, n])

    x_re, x_im = lax.fori_loop(0, nsc, step, (x_ref[0], x_ref[1]))
    x_ref[0] = x_re
    x_ref[1] = x_im
    o = tab_ref[1]
    o_re, o_im = o[:, :ns][None], o[:, ns:][None]
    p_re = (xpv_ref[0] * o_re - xpv_ref[1] * o_im).reshape(TS, ns)
    p_im = (xpv_ref[0] * o_im + xpv_ref[1] * o_re).reshape(TS, ns)
    y = _dot(jnp.concatenate([p_re, p_im], axis=-1).astype(BF16), cbd_ref[...])
    uf = u.astype(F32)
    pos = lax.broadcasted_iota(jnp.int32, (TS, 1), 0) % sub
    for tau in range(sub):
        if tau == 0:
            shifted = u
        else:
            rolled = pltpu.roll(uf, TS - tau if reverse else tau, axis=0)
            inside = (pos + tau < sub) if reverse else (pos >= tau)
            shifted = jnp.where(inside, rolled, 0.0).astype(BF16)
        y = y + _dot(shifted, kst_ref[tau])
    y_ref[...] = y


def _s5_scan_kernel(uf_ref, ub_ref, bbf, bbb, kstf, kstb, cbdf, cbdb, tabf, tabb, yf_ref, yb_ref,
                    x_ref, xin_ref, xpv_ref):
    @pl.when(pl.program_id(1) == 0)
    def _():
        x_ref[...] = jnp.zeros(x_ref.shape, F32)

    _s5_direction(uf_ref, bbf, kstf, cbdf, tabf, yf_ref, x_ref.at[0], xin_ref.at[0], xpv_ref.at[0], False)
    _s5_direction(ub_ref, bbb, kstb, cbdb, tabb, yb_ref, x_ref.at[1], xin_ref.at[1], xpv_ref.at[1], True)


def _s5_scan(z, bb, kst, cbd, tab, nb, s):
    nt_rows = z.shape[0]
    tps = s // TS
    nlt = nb * tps
    fwd = lambda b, n: jnp.where(n == 0, nlt + b, b * tps + n - 1)
    bwd = lambda b, n: jnp.where(n == 0, nlt + b, b * tps + tps - n)
    ns, c, nsc = S5_NS, S5_GROUPS * S5_GROUP_CH, TS // S5_SUB

    def both(a):
        return [pl.BlockSpec((None,) + a.shape[1:], lambda b, n, d=d: (d,) + (0,) * (a.ndim - 1)) for d in range(2)]

    state = lambda *lead: pltpu.VMEM((2, 2) + lead + (S5_SUB, ns), F32)
    return pl.pallas_call(
        _s5_scan_kernel,
        grid=(nb, tps + 1),
        in_specs=[pl.BlockSpec((TS, c), lambda b, n: (fwd(b, n), C_U // 256)),
                  pl.BlockSpec((TS, c), lambda b, n: (bwd(b, n), C_U // 256))]
                 + both(bb) + both(kst) + both(cbd) + both(tab),
        out_specs=[pl.BlockSpec((TS, c), lambda b, n: (fwd(b, n), 0)),
                   pl.BlockSpec((TS, c), lambda b, n: (bwd(b, n), 0))],
        out_shape=[jax.ShapeDtypeStruct((nt_rows, c), F32)] * 2,
        scratch_shapes=[state(), state(nsc), state(nsc)],
        compiler_params=_cparams(("parallel", "arbitrary"), 56),
        name="s5_scan",
    )(z, z, bb, bb, kst, kst, cbd, cbd, tab, tab)


def _s5_finish_kernel(yf_ref, yb_ref, u_ref, d_ref, w_ref, b_ref, o_ref):
    y = yf_ref[...] + yb_ref[...] + d_ref[...] * u_ref[...].astype(F32)
    y = jax.nn.gelu(y)
    gate = _sigmoid(_dot(y.astype(BF16), w_ref[...]) + b_ref[...])
    o_ref[...] = (y * gate).astype(BF16)


def _s5_finish(yf, yb, z, d_row, glu_w, glu_b_row):
    nt = yf.shape[0] // TM
    full = lambda a: pl.BlockSpec(a.shape, lambda i: (0,) * a.ndim)
    return pl.pallas_call(
        _s5_finish_kernel,
        grid=(nt,),
        in_specs=[pl.BlockSpec((TM, 256), lambda i: (i, 0)),
                  pl.BlockSpec((TM, 256), lambda i: (i, 0)),
                  pl.BlockSpec((TM, 256), lambda i: (i, C_U // 256)),
                  full(d_row), full(glu_w), full(glu_b_row)],
        out_specs=pl.BlockSpec((TM, 256), lambda i: (i, 0)),
        out_shape=jax.ShapeDtypeStruct((yf.shape[0], 256), BF16),
        compiler_params=_cparams(("parallel",)),
        name="s5_finish",
    )(yf, yb, z, d_row, glu_w, glu_b_row)


def _merge_kernel(x_ref, y0_ref, y1_ref, y2_ref, y3_ref, gates_ref, wb_ref, wo_ref, g1_ref, gain_ref, o_ref):
    acc = jnp.zeros((TM, D_MODEL), F32)
    for bi, y_ref in enumerate((y0_ref, y1_ref, y2_ref, y3_ref)):
        proj = _dot(y_ref[...], wb_ref[bi])
        acc = acc + gates_ref[:, bi * D_MODEL:(bi + 1) * D_MODEL].astype(F32) * proj
    y = _dot(acc.astype(BF16), wo_ref[...])
    o_ref[...] = x_ref[...] + g1_ref[...] * _rms(y, gain_ref[...])


def _merge(x, ys, gates, wb, wo, g1, gain, layer, tps, nb):
    nt = x.shape[0] // TM
    row = lambda i: (jnp.minimum(i // tps, nb), 0, 0)
    tile = lambda w: pl.BlockSpec((TM, w), lambda i: (i, 0))
    return pl.pallas_call(
        _merge_kernel,
        grid=(nt,),
        in_specs=[tile(D_MODEL)] + [tile(BRANCH_W)] * 4 + [tile(4 * D_MODEL),
                  pl.BlockSpec((None, 4, BRANCH_W, D_MODEL), lambda i: (layer, 0, 0, 0)),
                  pl.BlockSpec((None, D_MODEL, D_MODEL), lambda i: (layer, 0, 0)),
                  pl.BlockSpec((None, 1, D_MODEL), row),
                  pl.BlockSpec((1, D_MODEL), lambda i: (0, 0))],
        out_specs=tile(D_MODEL),
        out_shape=jax.ShapeDtypeStruct(x.shape, F32),
        compiler_params=_cparams(("parallel",), 48),
        name="merge",
    )(x, *ys, gates, wb, wo, g1, gain)


def _mlp_kernel(x_ref, sc_ref, sh_ref, g2_ref, gin_ref, gout_ref, w1_ref, w2_ref, o_ref, h_ref, acc_ref):
    j = pl.program_id(1)

    @pl.when(j == 0)
    def _():
        h_ref[...] = _norm_mod(x_ref[...], gin_ref[...], sc_ref[...], sh_ref[...]).astype(BF16)
        acc_ref[...] = jnp.zeros(acc_ref.shape, F32)

    t = jnp.maximum(_dot(h_ref[...], w1_ref[...]), 0.0)
    acc_ref[...] += _dot((t * t).astype(BF16), w2_ref[...])

    @pl.when(j == pl.num_programs(1) - 1)
    def _():
        o_ref[...] = x_ref[...] + g2_ref[...] * _rms(acc_ref[...], gout_ref[...])


def _mlp(x, sc, sh, g2, gin, gout, w1, w2, layer, tps, nb, tf=1024):
    nt = x.shape[0] // TM
    row = lambda i, j: (jnp.minimum(i // tps, nb), 0, 0)
    vec = pl.BlockSpec((1, D_MODEL), lambda i, j: (0, 0))
    return pl.pallas_call(
        _mlp_kernel,
        grid=(nt, D_FF // tf),
        in_specs=[pl.BlockSpec((TM, D_MODEL), lambda i, j: (i, 0)),
                  pl.BlockSpec((None, 1, D_MODEL), row), pl.BlockSpec((None, 1, D_MODEL), row),
                  pl.BlockSpec((None, 1, D_MODEL), row), vec, vec,
                  pl.BlockSpec((None, D_MODEL, tf), lambda i, j: (layer, 0, j)),
                  pl.BlockSpec((None, tf, D_MODEL), lambda i, j: (layer, j, 0))],
        out_specs=pl.BlockSpec((TM, D_MODEL), lambda i, j: (i, 0)),
        out_shape=jax.ShapeDtypeStruct(x.shape, F32),
        scratch_shapes=[pltpu.VMEM((TM, D_MODEL), BF16), pltpu.VMEM((TM, D_MODEL), F32)],
        compiler_params=_cparams(("parallel", "arbitrary"), 48),
        name="mlp",
    )(x, sc, sh, g2, gin, gout, w1, w2)


def _pack_w_in(w_in):
    o_na, o_cq, o_ckv, o_gdn, o_z, o_a, o_b, o_u, o_gate = 0, 768, 1024, 1184, 1952, 2208, 2216, 2224, 2480
    idx = np.zeros(ZW, np.int32)
    keep = np.zeros(ZW, np.float32)

    def put(dst, src):
        idx[dst:dst + len(src)] = src
        keep[dst:dst + len(src)] = 1.0

    put(C_GDN, np.arange(o_gdn, o_gdn + 768))
    put(C_NA, np.arange(o_na, o_na + 768))
    put(C_CQ, np.arange(o_cq, o_cq + 256))
    put(C_CKV, np.arange(o_ckv, o_ckv + 160))
    put(C_CKV + L_A, np.arange(o_a, o_a + 8))
    put(C_CKV + L_B, np.arange(o_b, o_b + 8))
    put(C_CKV + L_KRS, o_ckv + MLA_KV_LORA + ROPE_SWAP)
    put(C_Z, np.arange(o_z, o_z + 256))
    put(C_U, np.arange(o_u, o_u + 256))
    small = (jnp.take(w_in, jnp.asarray(idx), axis=2) * jnp.asarray(keep)).astype(BF16)
    gates = w_in[:, :, o_gate:].astype(BF16)
    return small, gates


def _mla_weights(w_uq, w_ukv):
    depth = w_uq.shape[0]
    hq = MLA_NOPE + MLA_ROPE
    wq = w_uq.reshape(depth, -1, N_HEADS, hq)
    pad = lambda a, lo, hi: jnp.pad(a, ((0, 0), (0, 0), (0, 0), (lo, hi)))
    wq_ext = pad(wq, 0, 128 - hq).reshape(depth, -1, N_HEADS * 128)
    wq_sw = pad(wq[..., MLA_NOPE:][..., ROPE_SWAP], MLA_NOPE, 128 - hq).reshape(depth, -1, N_HEADS * 128)
    wkv = w_ukv.reshape(depth, -1, N_HEADS, MLA_NOPE + HEAD_W)
    wk = pad(wkv[..., :MLA_NOPE], 0, 128 - MLA_NOPE).reshape(depth, -1, N_HEADS * 128)
    wv = pad(wkv[..., MLA_NOPE:], 0, 128 - HEAD_W).reshape(depth, -1, N_HEADS * 128)
    return [a.astype(BF16) for a in (wq_ext, wq_sw, wk, wv, jnp.swapaxes(wv, 1, 2))]


def _rope_place_mats():
    p1 = np.zeros((256, N_HEADS * 128), np.float32)
    p2 = np.zeros((256, N_HEADS * 128), np.float32)
    for h in range(N_HEADS):
        for r in range(MLA_ROPE):
            p1[L_KR + r, h * 128 + MLA_NOPE + r] = 1.0
            p2[L_KRS + r, h * 128 + MLA_NOPE + r] = 1.0
    return jnp.asarray(p1, BF16), jnp.asarray(p2, BF16)


def _rope_tables(s):
    quarter = MLA_ROPE // 4
    inv_freq = ROPE_BASE ** (-jnp.arange(quarter, dtype=F32) / quarter)
    t = jnp.arange(s)
    ang_r = (t // GRID_W).astype(F32)[:, None] * inv_freq[None, :]
    ang_c = (t % GRID_W).astype(F32)[:, None] * inv_freq[None, :]
    cr, sr, cc, sn = jnp.cos(ang_r), jnp.sin(ang_r), jnp.cos(ang_c), jnp.sin(ang_c)
    cos = jnp.concatenate([jnp.ones((s, MLA_NOPE), F32), cr, cr, cc, cc, jnp.ones((s, 32), F32)], axis=1)
    sin = jnp.concatenate([jnp.zeros((s, MLA_NOPE), F32), -sr, sr, -sn, sn, jnp.zeros((s, 32), F32)], axis=1)
    cos = jnp.concatenate([cos, jnp.ones((TM, 128), F32)], axis=0)
    sin = jnp.concatenate([sin, jnp.zeros((TM, 128), F32)], axis=0)
    return cos, sin


def _head_block_ones():
    r = np.arange(BRANCH_W)
    return jnp.asarray((r[:, None] // HEAD_W == r[None, :] // HEAD_W).astype(np.float32))


def _gdn_expand_mats():
    e = np.zeros((2, 256, 2 * N_HEADS * 128), np.float32)
    for dh in range(2 * N_HEADS):
        e[0, L_A + dh, dh * 128:(dh + 1) * 128] = 1.0
        e[1, L_B + dh, dh * 128:(dh + 1) * 128] = 1.0
    return jnp.asarray(e)


def _lane_row(vals, offset, width=256):
    return jnp.zeros((1, width), F32).at[0, offset:offset + vals.shape[0]].set(vals.astype(F32))


def _gdn_mixer(z, conv_w, a_log, dt_bias, norm_w, consts, nb, s, ctx_len):
    bseg, ef = consts["bseg"], consts["ef"]
    alog_row = _lane_row(a_log.reshape(-1), L_A)
    dtb_row = _lane_row(dt_bias.reshape(-1), L_A)
    qk, kk, vk, gcx, btx, gt = _gdn_prep(z, conv_w, alog_row, dtb_row, bseg, ef, consts["sel"], s // TS, nb * s // TS)
    of, ob = _gdn_scan(qk, kk, vk, gcx, btx, gt, nb, s)
    return _gdn_finish(of, ob, z, norm_w.astype(F32)[None, :])


def _s5_mixer(z, a_re, a_im, log_dt, b_re, b_im, c_re, c_im, d_skip, glu_w, glu_b, nb, s, ctx_len):
    bb, kst, cbd, tab = _s5_params(a_re, a_im, log_dt, b_re, b_im, c_re, c_im)
    yf, yb = _s5_scan(z, bb, kst, cbd, tab, nb, s)
    return _s5_finish(yf, yb, z, d_skip.astype(F32)[None, :], glu_w.astype(BF16), glu_b.astype(F32)[None, :])


def kernel(x, c, ctx, c_ctx, ada_w, ada_b, norm_gains, w_in, na_rpb, mla_q_norm, mla_kv_norm, mla_w_uq, mla_w_ukv, gdn_conv, gdn_a_log, gdn_dt_bias, gdn_norm, s5_a_re, s5_a_im, s5_log_dt, s5_b_re, s5_b_im, s5_c_re, s5_c_im, s5_d, s5_glu_w, s5_glu_b, w_branch, w_out, mlp_w1, mlp_w2):
    nb, s, d = x.shape
    ctx_len = ctx.shape[1]
    depth = ada_w.shape[0]
    assert d == D_MODEL and ctx_len == TS and nb * ctx_len == TM and s % TM == 0 and s // GRID_W >= 16
    nl = nb * s
    tps = s // TM

    xs = jnp.concatenate([x.reshape(nl, d), ctx.reshape(nb * ctx_len, d)], axis=0)
    cvec = jnp.zeros((8, d), F32).at[:nb].set(c).at[nb].set(c_ctx)
    mod = _modulation(cvec, ada_w, ada_b)
    mod = mod[:, :nb + 1].reshape(depth, nb + 1, 6, 1, d)

    w_small, w_gates = _pack_w_in(w_in)
    wq_ext, wq_sw, wk_ext, wv, wvt = _mla_weights(mla_w_uq, mla_w_ukv)
    p1, p2 = _rope_place_mats()
    cos_t, sin_t = _rope_tables(s)
    sel = np.zeros((8, 256), np.float32)
    sel[np.arange(8), L_A + np.arange(8)] = 1.0
    consts = {"bseg": _head_block_ones(), "ef": _gdn_expand_mats(), "sel": jnp.asarray(sel)}
    wb = w_branch.astype(BF16)
    wo = w_out.astype(BF16)
    w1 = mlp_w1.astype(BF16)
    w2 = mlp_w2.astype(BF16)
    gains = norm_gains.astype(F32)

    for l in range(depth):
        sh1, sc1, g1, sh2, sc2, g2 = [mod[l, :, i] for i in range(6)]
        z = _inproj(xs, sc1, sh1, gains[l, 0][None], w_small, l, tps, nb, False)
        gates = _inproj(xs, sc1, sh1, gains[l, 0][None], w_gates, l, tps, nb, True)

        y_na = jnp.concatenate([
            _na_latent(z, na_rpb[l], nb, s),
            _ctx_attention(z, z, z, C_NA // 256, C_NA // 256 + 1, C_NA // 256 + 2, 256, 256, nl // TS, nb,
                           HEAD_W ** -0.5, False, "na_ctx")], axis=0)

        qm, km, vm, vt = _mla_prep(z, cos_t, sin_t, mla_q_norm[l].astype(F32)[None],
                                   mla_kv_norm[l].astype(F32)[None], wq_ext[l], wq_sw[l], wk_ext[l], wv[l], wvt[l],
                                   p1, p2, tps, nl // TM)
        y_mla = jnp.concatenate([
            _mla_latent(qm, km, vt, nb, s),
            _ctx_attention(qm, km, vm, 0, 0, 0, N_HEADS * 128, N_HEADS * 128, nl // TS, nb, 1.0, True,
                           "mla_ctx")], axis=0)

        y_gdn = _gdn_mixer(z, gdn_conv[l].astype(F32), gdn_a_log[l], gdn_dt_bias[l], gdn_norm[l], consts,
                           nb, s, ctx_len)
        y_s5 = _s5_mixer(z, s5_a_re[l], s5_a_im[l], s5_log_dt[l], s5_b_re[l], s5_b_im[l], s5_c_re[l], s5_c_im[l],
                         s5_d[l], s5_glu_w[l], s5_glu_b[l], nb, s, ctx_len)

        xs = _merge(xs, (y_na, y_mla, y_gdn, y_s5), gates, wb, wo, g1, gains[l, 1][None], l, tps, nb)
        xs = _mlp(xs, sc2, sh2, g2, gains[l, 2][None], gains[l, 3][None], w1, w2, l, tps, nb)
    return xs[:nl].reshape(nb, s, d)
```

```python
import functools
import math

import numpy as np
import jax
import jax.numpy as jnp
from jax import lax
from jax.experimental import pallas as pl
from jax.experimental.pallas import tpu as pltpu

F32 = jnp.float32
BF16 = jnp.bfloat16
HI = lax.Precision.HIGHEST
EPS = 1e-6

D_MODEL = 1024
GRID_W = 64
NA_WIN_H = 8
NA_WIN_W = 16
N_HEADS = 4
HEAD_W = 64
BRANCH_W = 256
MLA_NOPE = 64
MLA_ROPE = 32
MLA_KV_LORA = 128
ROPE_BASE = 10000.0
GDN_CONV = 4
CHUNK = 64
S5_GROUPS = 16
S5_GROUP_CH = 16
S5_STATE = 64
D_FF = 4 * D_MODEL

TM = 512
TN = 512
TS = 256
NEG = -1e30

C_GDN = 0
C_NA = 768
C_CQ = 1536
C_CKV = 1792
C_Z = 2048
C_U = 2304
ZW = 2560
L_KR = 128
L_A = 160
L_B = 168
L_KRS = 176
ROPE_SWAP = np.concatenate([np.arange(8, 16), np.arange(0, 8), np.arange(24, 32), np.arange(16, 24)])


def _cparams(sem, vmem_mb=None):
    kw = dict(dimension_semantics=sem)
    if vmem_mb is not None:
        kw["vmem_limit_bytes"] = vmem_mb * 1024 * 1024
    return pltpu.CompilerParams(**kw)


def _dot(a, b, **kw):
    return jnp.dot(a, b, preferred_element_type=F32, **kw)


def _dot_nt(a, b):
    return lax.dot_general(a, b, (((1,), (1,)), ((), ())), preferred_element_type=F32)


def _sigmoid(x):
    return 1.0 / (1.0 + jnp.exp(-x))


def _silu(x):
    return x * _sigmoid(x)


def _mod_kernel(c_ref, w_ref, b_ref, o_ref):
    c = c_ref[...]
    o_ref[...] = _dot(_silu(c), w_ref[...], precision=HI) + b_ref[...]


def _modulation(cvec, ada_w, ada_b):
    depth, d, n = ada_w.shape
    tn = 1536
    return pl.pallas_call(
        _mod_kernel,
        grid=(depth, n // tn),
        in_specs=[pl.BlockSpec((8, d), lambda l, j: (0, 0)),
                  pl.BlockSpec((None, d, tn), lambda l, j: (l, 0, j)),
                  pl.BlockSpec((None, 1, tn), lambda l, j: (l, 0, j))],
        out_specs=pl.BlockSpec((None, 8, tn), lambda l, j: (l, 0, j)),
        out_shape=jax.ShapeDtypeStruct((depth, 8, n), F32),
        compiler_params=_cparams(("parallel", "parallel"), 40),
        name="modulation",
    )(cvec, ada_w, ada_b.reshape(depth, 1, n))


def _norm_mod(x, gain, sc, sh):
    r = lax.rsqrt(jnp.mean(x * x, axis=-1, keepdims=True) + EPS)
    return (x * r * gain) * (1.0 + sc) + sh


def _inproj_kernel(x_ref, sc_ref, sh_ref, gain_ref, w_ref, o_ref, h_ref, *, gate):
    @pl.when(pl.program_id(1) == 0)
    def _():
        h_ref[...] = _norm_mod(x_ref[...], gain_ref[...], sc_ref[...], sh_ref[...]).astype(BF16)

    acc = _dot(h_ref[...], w_ref[...])
    if gate:
        acc = _sigmoid(acc)
    o_ref[...] = acc.astype(BF16)


def _inproj(x, sc, sh, gain, w, layer, tps, nb, gate):
    nt = x.shape[0] // TM
    width = w.shape[-1]
    tn = width // 2
    row = lambda i, j: (jnp.minimum(i // tps, nb), 0, 0)
    return pl.pallas_call(
        functools.partial(_inproj_kernel, gate=gate),
        grid=(nt, width // tn),
        in_specs=[pl.BlockSpec((TM, D_MODEL), lambda i, j: (i, 0)),
                  pl.BlockSpec((None, 1, D_MODEL), row),
                  pl.BlockSpec((None, 1, D_MODEL), row),
                  pl.BlockSpec((1, D_MODEL), lambda i, j: (0, 0)),
                  pl.BlockSpec((None, D_MODEL, tn), lambda i, j: (layer, 0, j))],
        out_specs=pl.BlockSpec((TM, tn), lambda i, j: (i, j)),
        out_shape=jax.ShapeDtypeStruct((x.shape[0], width), BF16),
        scratch_shapes=[pltpu.VMEM((TM, D_MODEL), BF16)],
        compiler_params=_cparams(("parallel", "arbitrary"), 40),
        name="inproj_gates" if gate else "inproj",
    )(x, sc, sh, gain, w)


def _head_lane_mask(width, head_w, h):
    lane = lax.broadcasted_iota(jnp.int32, (1, width), 1)
    return (lane >= h * head_w) & (lane < (h + 1) * head_w)


def _na_build_bias(rpb_ref, bias_ref, r0, kb0, rows_total):
    w = GRID_W
    qc = lax.broadcasted_iota(jnp.int32, (w, 2 * w), 0)
    lane = lax.broadcasted_iota(jnp.int32, (w, 2 * w), 1)
    kc = lane % w
    cs = jnp.clip(qc - NA_WIN_W // 2, 0, w - NA_WIN_W)
    col_ok = (kc >= cs) & (kc < cs + NA_WIN_W)
    left = lane < w
    neg = jnp.full((w, 2 * w), NEG, F32)
    for h in range(N_HEADS):
        t = rpb_ref[h]
        toep = []
        for a in range(2 * NA_WIN_H - 1):
            row = jnp.broadcast_to(t[a:a + 1, :], (w, 2 * w))
            ra = pltpu.roll(row, 2 * w - (NA_WIN_W - 1), axis=1, stride=1, stride_axis=0)
            rb = pltpu.roll(ra, w, axis=1)
            toep.append((jnp.where(col_ok, ra, NEG), jnp.where(col_ok, rb, NEG)))
        for qr in range(8):
            rs = min(max(r0 + qr - NA_WIN_H // 2, 0), rows_total - NA_WIN_H)
            for kp in range(8):
                halves = []
                for side in range(2):
                    kr = kb0 + 2 * kp + side
                    halves.append(toep[kr - (r0 + qr) + NA_WIN_H - 1][side] if rs <= kr < rs + NA_WIN_H else neg)
                bias_ref[h, qr * w:(qr + 1) * w, kp * 2 * w:(kp + 1) * 2 * w] = jnp.where(left, halves[0], halves[1])


def _na_kernel(q_ref, k_ref, v_ref, kc_ref, vc_ref, rpb_ref, o_ref, bias_ref, *, rows_total):
    i = pl.program_id(1)
    last = pl.num_programs(1) - 1

    @pl.when(i == 0)
    def _():
        _na_build_bias(rpb_ref, bias_ref, 0, 0, rows_total)

    @pl.when(i == 1)
    def _():
        _na_build_bias(rpb_ref, bias_ref, 8, 4, rows_total)

    @pl.when(i == last)
    def _():
        _na_build_bias(rpb_ref, bias_ref, rows_total - 8, rows_total - 16, rows_total)

    kb = jnp.clip(2 * i - 1, 0, rows_total // 4 - 4)
    start = pl.multiple_of(kb * (4 * GRID_W), 4 * GRID_W)
    nk = 2 * NA_WIN_H * GRID_W
    q = q_ref[...]
    kw = k_ref[pl.ds(start, nk), :]
    vw = v_ref[pl.ds(start, nk), :]
    kc = kc_ref[...]
    vc = vc_ref[...]
    scale = HEAD_W ** -0.5
    out = jnp.zeros(q.shape, F32)
    for h in range(N_HEADS):
        hm = _head_lane_mask(BRANCH_W, HEAD_W, h)
        qh = jnp.where(hm, q, jnp.zeros_like(q))
        sb = _dot_nt(qh, kw) * scale + bias_ref[h]
        sc = _dot_nt(qh, kc) * scale
        m = jnp.maximum(jnp.max(sb, axis=-1, keepdims=True), jnp.max(sc, axis=-1, keepdims=True))
        pb = jnp.exp(sb - m)
        pc = jnp.exp(sc - m)
        den = jnp.sum(pb, axis=-1, keepdims=True) + jnp.sum(pc, axis=-1, keepdims=True)
        o = _dot(pb.astype(BF16), vw) + _dot(pc.astype(BF16), vc)
        out = jnp.where(hm, o / den, out)
    o_ref[...] = out.astype(BF16)


def _na_latent(z, rpb, nb, s):
    rows_total = s // GRID_W
    qb = 8 * GRID_W
    nq = s // qb
    nl = nb * s
    rpb = jnp.pad(rpb.astype(F32), ((0, 0), (0, 1), (0, 2 * GRID_W - (2 * NA_WIN_W - 1))))
    return pl.pallas_call(
        functools.partial(_na_kernel, rows_total=rows_total),
        grid=(nb, nq),
        in_specs=[pl.BlockSpec((qb, BRANCH_W), lambda b, i: (b * nq + i, C_NA // 256)),
                  pl.BlockSpec((s, BRANCH_W), lambda b, i: (b, C_NA // 256 + 1)),
                  pl.BlockSpec((s, BRANCH_W), lambda b, i: (b, C_NA // 256 + 2)),
                  pl.BlockSpec((TS, BRANCH_W), lambda b, i: (nl // TS + b, C_NA // 256 + 1)),
                  pl.BlockSpec((TS, BRANCH_W), lambda b, i: (nl // TS + b, C_NA // 256 + 2)),
                  pl.BlockSpec(rpb.shape, lambda b, i: (0, 0, 0))],
        out_specs=pl.BlockSpec((qb, BRANCH_W), lambda b, i: (b * nq + i, 0)),
        out_shape=jax.ShapeDtypeStruct((nl, BRANCH_W), BF16),
        scratch_shapes=[pltpu.VMEM((N_HEADS, qb, 2 * qb), F32)],
        compiler_params=_cparams(("parallel", "arbitrary"), 56),
        name="na_latent",
    )(z, z, z, z, z, rpb)


def _ctx_attn_kernel(q_ref, k_ref, v_ref, o_ref, *, scale, base2):
    q = q_ref[...]
    k = k_ref[...]
    v = v_ref[...]
    qw = q.shape[-1]
    vw = v.shape[-1] // N_HEADS
    outs = []
    for h in range(N_HEADS):
        qh = jnp.where(_head_lane_mask(qw, qw // N_HEADS, h), q, jnp.zeros_like(q))
        s = _dot_nt(qh, k) * scale
        m = jnp.max(s, axis=-1, keepdims=True)
        p = jnp.exp2(s - m) if base2 else jnp.exp(s - m)
        den = jnp.sum(p, axis=-1, keepdims=True)
        o = _dot(p.astype(BF16), v)
        outs.append(o[:, h * vw:h * vw + HEAD_W] / den)
    o_ref[...] = jnp.concatenate(outs, axis=-1).astype(BF16)


def _ctx_attention(q, k, v, qcol, kcol, vcol, qw, vw, row0, nb, scale, base2, name):
    return pl.pallas_call(
        functools.partial(_ctx_attn_kernel, scale=scale, base2=base2),
        grid=(nb,),
        in_specs=[pl.BlockSpec((TS, qw), lambda b: (row0 + b, qcol)),
                  pl.BlockSpec((TS, qw), lambda b: (row0 + b, kcol)),
                  pl.BlockSpec((TS, vw), lambda b: (row0 + b, vcol))],
        out_specs=pl.BlockSpec((TS, BRANCH_W), lambda b: (b, 0)),
        out_shape=jax.ShapeDtypeStruct((nb * TS, BRANCH_W), BF16),
        compiler_params=_cparams(("parallel",)),
        name=name,
    )(q, k, v)


def _rms(x, gain):
    return x * lax.rsqrt(jnp.mean(x * x, axis=-1, keepdims=True) + EPS) * gain


def _mla_prep_kernel(cq_ref, ckv_ref, cos_ref, sin_ref, qn_ref, kvn_ref, wq_ref, wqs_ref, wk_ref, wv_ref, wvt_ref,
                     p1_ref, p2_ref, q_ref, k_ref, v_ref, vt_ref):
    cos = jnp.concatenate([cos_ref[...]] * N_HEADS, axis=-1)
    sin = jnp.concatenate([sin_ref[...]] * N_HEADS, axis=-1)
    cqn = _rms(cq_ref[...].astype(F32), qn_ref[...]).astype(BF16)
    scale = (MLA_NOPE + MLA_ROPE) ** -0.5 * math.log2(math.e)
    q = _dot(cqn, wq_ref[...]) * cos + _dot(cqn, wqs_ref[...]) * sin
    q_ref[...] = (q * scale).astype(BF16)
    ckv = ckv_ref[...]
    kvn = _rms(ckv[:, :MLA_KV_LORA].astype(F32), kvn_ref[...]).astype(BF16)
    k = (_dot(kvn, wk_ref[...]) + _dot(ckv, p1_ref[...])) * cos + _dot(ckv, p2_ref[...]) * sin
    k_ref[...] = k.astype(BF16)
    lane = lax.broadcasted_iota(jnp.int32, (1, N_HEADS * 128), 1)
    v_ref[...] = jnp.where(lane % 128 == HEAD_W, 1.0, _dot(kvn, wv_ref[...])).astype(BF16)
    row = lax.broadcasted_iota(jnp.int32, (N_HEADS * 128, 1), 0)
    vt_ref[...] = jnp.where(row % 128 == HEAD_W, 1.0, _dot_nt(wvt_ref[...], kvn)).astype(BF16)


def _mla_prep(z, cos_t, sin_t, qn, kvn, wq, wqs, wk, wv, wvt, p1, p2, tps, n_lat_tiles):
    nt_rows = z.shape[0]
    nt = nt_rows // TM
    full = lambda a: pl.BlockSpec(a.shape, lambda i: (0,) * a.ndim)
    tab = lambda i: (jnp.where(i < n_lat_tiles, i % tps, tps), 0)
    hw = N_HEADS * 128
    return pl.pallas_call(
        _mla_prep_kernel,
        grid=(nt,),
        in_specs=[pl.BlockSpec((TM, 256), lambda i: (i, C_CQ // 256)),
                  pl.BlockSpec((TM, 256), lambda i: (i, C_CKV // 256)),
                  pl.BlockSpec((TM, 128), tab), pl.BlockSpec((TM, 128), tab),
                  full(qn), full(kvn), full(wq), full(wqs), full(wk), full(wv), full(wvt), full(p1), full(p2)],
        out_specs=[pl.BlockSpec((TM, hw), lambda i: (i, 0)),
                   pl.BlockSpec((TM, hw), lambda i: (i, 0)),
                   pl.BlockSpec((TM, hw), lambda i: (i, 0)),
                   pl.BlockSpec((hw, TM), lambda i: (0, i))],
        out_shape=[jax.ShapeDtypeStruct((nt_rows, hw), BF16)] * 3 + [jax.ShapeDtypeStruct((hw, nt_rows), BF16)],
        compiler_params=_cparams(("parallel",)),
        name="mla_prep",
    )(z, z, cos_t, sin_t, qn, kvn, wq, wqs, wk, wv, wvt, p1, p2)


def _flash_kernel(q_ref, kl_ref, vl_ref, kc_ref, vc_ref, o_ref, st_ref, *, tk, n_lat):
    tq = q_ref.shape[0]
    s_len = n_lat * tk
    heads = (slice(0, 128), slice(128, 256))
    qs = [q_ref[:, hs] for hs in heads]
    group_max = lambda st: jnp.max(st.reshape(st.shape[0] // 8, 8, tq), axis=0)

    def score_tile(hh, k, rows, mx):
        st = _dot_nt(k, qs[hh])
        st_ref[hh, rows, :] = st
        return jnp.maximum(mx, group_max(st))

    def max_body(t, carry):
        rows = pl.ds(pl.multiple_of(t * tk, tk), tk)
        return tuple(score_tile(hh, kl_ref[rows, heads[hh]], rows, carry[hh]) for hh in range(2))

    mx = lax.fori_loop(0, n_lat, max_body, tuple(jnp.full((8, tq), NEG, F32) for _ in range(2)), unroll=4)
    ctx_rows = slice(s_len, s_len + kc_ref.shape[0])
    ms = [jnp.max(score_tile(hh, kc_ref[:, heads[hh]], ctx_rows, mx[hh]), axis=0, keepdims=True) for hh in range(2)]

    def weighted(hh, rows, vt):
        return _dot(vt, jnp.exp2(st_ref[hh, rows, :] - ms[hh]).astype(BF16))

    def acc_body(t, carry):
        r0 = pl.multiple_of(t * tk, tk)
        return tuple(carry[hh] + weighted(hh, pl.ds(r0, tk), vl_ref[heads[hh], pl.ds(r0, tk)]) for hh in range(2))

    accs = lax.fori_loop(0, n_lat, acc_body, tuple(jnp.zeros((128, tq), F32) for _ in range(2)), unroll=4)
    outs = []
    for hh in range(2):
        acc = (accs[hh] + weighted(hh, ctx_rows, vc_ref[heads[hh], :])).T
        outs.append(acc[:, :HEAD_W] / acc[:, HEAD_W:HEAD_W + 1])
    o_ref[...] = jnp.concatenate(outs, axis=-1).astype(BF16)


def _mla_latent(qm, km, vt, nb, s, tq=256, tk=512):
    nq = s // tq
    nl = nb * s
    return pl.pallas_call(
        functools.partial(_flash_kernel, tk=tk, n_lat=s // tk),
        grid=(nb, 2, nq),
        in_specs=[pl.BlockSpec((tq, 256), lambda b, hp, i: (b * nq + i, hp)),
                  pl.BlockSpec((s, 256), lambda b, hp, i: (b, hp)),
                  pl.BlockSpec((256, s), lambda b, hp, i: (hp, b)),
                  pl.BlockSpec((TS, 256), lambda b, hp, i: (nl // TS + b, hp)),
                  pl.BlockSpec((256, TS), lambda b, hp, i: (hp, nl // TS + b))],
        out_specs=pl.BlockSpec((tq, 128), lambda b, hp, i: (b * nq + i, hp)),
        out_shape=jax.ShapeDtypeStruct((nl, BRANCH_W), BF16),
        scratch_shapes=[pltpu.VMEM((2, s + TS, tq), F32)],
        compiler_params=_cparams(("parallel", "parallel", "arbitrary"), 56),
        name="mla_flash",
    )(qm, km, vt, km, vt)


def _gdn_prep_kernel(prev_ref, cur_ref, next_ref, ckv_ref, conv_ref, alog_ref, dtb_ref, bseg_ref, sel_ref,
                     qk_ref, kk_ref, vk_ref, gcx_ref, btx_ref, gt_ref, *, tps, n_lat_tiles):
    i = pl.program_id(0)
    is_ctx = i >= n_lat_tiles
    first = is_ctx | (i % tps == 0)
    last = is_ctx | (i % tps == tps - 1)
    prev = jnp.where(first, 0.0, prev_ref[...].astype(F32))
    nxt = jnp.where(last, 0.0, next_ref[...].astype(F32))
    ext = jnp.concatenate([prev, cur_ref[...].astype(F32), nxt], axis=0)
    n_ext = TS + 16
    acc = jnp.zeros((TS, 3 * BRANCH_W), F32)
    for j in range(GDN_CONV):
        shifted = pltpu.roll(ext, n_ext - (8 - GDN_CONV // 2 + j), axis=0)[:TS]
        acc = acc + shifted * conv_ref[j:j + 1, :]
    x = _silu(acc)
    bseg = bseg_ref[...]

    def l2n(a):
        sq = a * a
        hi = sq.astype(BF16)
        lo = (sq - hi.astype(F32)).astype(BF16)
        return a * lax.rsqrt(_dot(hi, bseg) + _dot(lo, bseg) + EPS)

    q = l2n(x[:, :BRANCH_W]) * (HEAD_W ** -0.5)
    k = l2n(x[:, BRANCH_W:2 * BRANCH_W])
    v = x[:, 2 * BRANCH_W:]
    for h in range(N_HEADS):
        hs = slice(h * HEAD_W, (h + 1) * HEAD_W)
        qk_ref[h] = jnp.concatenate([q[:, hs], k[:, hs]], axis=-1)
        kk_ref[h] = jnp.concatenate([k[:, hs], k[:, hs]], axis=-1)
        vk_ref[h] = jnp.concatenate([v[:, hs], k[:, hs]], axis=-1)

    ab = ckv_ref[:, 128:].astype(F32)
    la, lb = L_A - 128, L_B - 128
    sp_in = ab + dtb_ref[:, 128:]
    softplus = jnp.maximum(sp_in, 0.0) + jnp.log1p(jnp.exp(-jnp.abs(sp_in)))
    lane = lax.broadcasted_iota(jnp.int32, (1, 128), 1)
    g = jnp.where((lane >= la) & (lane < lb), -jnp.exp(alog_ref[:, 128:]) * softplus, 0.0)
    beta = _sigmoid(ab)
    pos = lax.broadcasted_iota(jnp.int32, (TS, 1), 0) % CHUNK
    pre, suf = g, g
    step = 1
    while step < CHUNK:
        pre = pre + jnp.where(pos >= step, pltpu.roll(pre, step, axis=0), 0.0)
        suf = suf + jnp.where(pos + step < CHUNK, pltpu.roll(suf, TS - step, axis=0), 0.0)
        step *= 2
    gc = jnp.where(lane < la + N_HEADS, pre, suf)
    gt_ref[...] = lax.dot_general(sel_ref[...], gc, (((1,), (1,)), ((), ())), precision=HI,
                                  preferred_element_type=F32)
    spread = lambda a, l0: jnp.concatenate(
        [jnp.broadcast_to(a[:, l0 + dh:l0 + dh + 1], (TS, 128)) for dh in range(2 * N_HEADS)], axis=-1)
    gcx_ref[...] = spread(gc, la)
    btx_ref[...] = spread(beta, lb)


def _gdn_prep(z, conv_w, alog_row, dtb_row, bseg, sel, tps_s, n_lat_tiles):
    nt_rows = z.shape[0]
    nt = nt_rows // TS
    hb = TS // 8
    full = lambda a: pl.BlockSpec(a.shape, lambda i: (0,) * a.ndim)
    xw = 2 * N_HEADS * 128
    hsp = pl.BlockSpec((N_HEADS, TS, 128), lambda i: (0, i, 0))
    return pl.pallas_call(
        functools.partial(_gdn_prep_kernel, tps=tps_s, n_lat_tiles=n_lat_tiles),
        grid=(nt,),
        in_specs=[pl.BlockSpec((8, 768), lambda i: (jnp.maximum(i * hb - 1, 0), 0)),
                  pl.BlockSpec((TS, 768), lambda i: (i, 0)),
                  pl.BlockSpec((8, 768), lambda i: (jnp.minimum((i + 1) * hb, nt * hb - 1), 0)),
                  pl.BlockSpec((TS, 256), lambda i: (i, C_CKV // 256)),
                  full(conv_w), full(alog_row), full(dtb_row), full(bseg), full(sel)],
        out_specs=[hsp] * 3
                  + [pl.BlockSpec((TS, xw), lambda i: (i, 0))] * 2
                  + [pl.BlockSpec((None, 8, TS), lambda i: (i, 0, 0))],
        out_shape=[jax.ShapeDtypeStruct((N_HEADS, nt_rows, 128), F32)] * 3
                  + [jax.ShapeDtypeStruct((nt_rows, xw), F32)] * 2
                  + [jax.ShapeDtypeStruct((nt, 8, TS), F32)],
        compiler_params=_cparams(("parallel",)),
        name="gdn_prep",
    )(z, z, z, z, conv_w, alog_row, dtb_row, bseg, sel)


def _tri_solve(n, x, reverse):
    h = n.shape[0]
    bs = 8
    nblk = CHUNK // bs
    r = lax.broadcasted_iota(jnp.int32, (CHUNK, CHUNK), 0)
    c = lax.broadcasted_iota(jnp.int32, (CHUNK, CHUNK), 1)
    off = jnp.where((r // bs != c // bs)[None], n, 0.0)
    diag = jnp.stack([n[:, bs * b:bs * (b + 1), bs * b:bs * (b + 1)] for b in range(nblk)], axis=1)

    ws = [a.reshape(h, nblk, bs, a.shape[-1]) for a in (x, off)]
    for j in (range(bs - 1, 0, -1) if reverse else range(bs - 1)):
        col = jnp.broadcast_to(diag[..., j:j + 1], (h, nblk, bs, 2 * HEAD_W))
        ws = [w - col[..., :w.shape[-1]] * w[:, :, j:j + 1, :] for w in ws]
    z, m = [w.reshape(h, CHUNK, w.shape[-1]) for w in ws]

    bmm = lambda a, b: jnp.einsum("hij,hjk->hik", a.astype(BF16), b.astype(BF16), preferred_element_type=F32)
    m2 = bmm(m, m)
    w1 = z - bmm(m, z)
    w2 = w1 + bmm(m2, w1)
    return w2 + bmm(bmm(m2, m2), w2)


def _gdn_chunk_terms(qk_ref, kk_ref, vk_ref, gc_ref, bt_ref, gt_ref, ci, d, reverse):
    c = CHUNK
    nh = N_HEADS
    r = lax.broadcasted_iota(jnp.int32, (c, c), 0)
    cc = lax.broadcasted_iota(jnp.int32, (c, c), 1)
    incl = ((r <= cc) if reverse else (r >= cc))[None]
    strict = ((r < cc) if reverse else (r > cc))[None]
    lo = lax.broadcasted_iota(jnp.int32, (1, 1, 2 * HEAD_W), 2) < HEAD_W
    bmm_nt = lambda a, b: jnp.einsum("hid,hjd->hij", a, b, preferred_element_type=F32)
    sl = slice(ci * c, (ci + 1) * c)
    qk = qk_ref[:, sl, :]
    kk = kk_ref[:, sl, :]
    vk = vk_ref[:, sl, :]
    gc = jnp.stack([gc_ref[sl, 128 * h:128 * (h + 1)] for h in range(nh)])
    bt = jnp.stack([bt_ref[sl, 128 * h:128 * (h + 1)] for h in range(nh)])
    grow = jnp.stack([gt_ref[d * nh + h:d * nh + h + 1, sl] for h in range(nh)])
    dec = jnp.where(incl, jnp.exp(jnp.minimum(gc[:, :, :c] - grow, 0.0)), 0.0)
    kkb = kk.astype(BF16)
    k_dot_k = 0.5 * bmm_nt(kkb, kkb)
    q_only = jnp.where(lo, qk, 0.0)
    q_dot_k = bmm_nt(q_only.astype(BF16), kkb) * dec
    n = jnp.where(strict, bt[:, :, :c] * k_dot_k * dec, 0.0)
    egc = jnp.exp(gc)
    x = _tri_solve(n, vk * jnp.where(lo, bt, bt * egc), reverse)
    edge = 0 if reverse else c - 1
    glast = gc[:, edge:edge + 1, :]
    ktail = kk * jnp.exp(glast - gc)
    return (jnp.where(lo, 0.0, x).astype(BF16), x[:, :, :HEAD_W], (q_only * egc).astype(BF16), q_dot_k.astype(BF16),
            ktail.astype(BF16), jnp.exp(glast)[:, :, :HEAD_W])


def _gdn_scan_kernel(qkf, qkb, kkf, kkb, vkf, vkb, gcf, gcb, btf, btb, gtf, gtb, of_ref, ob_ref, s_ref):
    @pl.when(pl.program_id(1) == 0)
    def _():
        s_ref[...] = jnp.zeros(s_ref.shape, F32)

    n_chunks = TS // CHUNK
    dirs = ((qkf, kkf, vkf, gcf, btf, gtf, of_ref, False), (qkb, kkb, vkb, gcb, btb, gtb, ob_ref, True))
    order = [range(n_chunks), range(n_chunks - 1, -1, -1)]
    terms = [[_gdn_chunk_terms(*dirs[d][:6], ci, d, dirs[d][7]) for ci in order[d]] for d in range(2)]
    bmm = lambda a, b: jnp.einsum("hij,hjk->hik", a, b, preferred_element_type=F32)
    bmm_tn = lambda a, b: jnp.einsum("hcd,hce->hde", a, b, preferred_element_type=F32)
    state = [s_ref[0], s_ref[1]]
    for step in range(n_chunks):
        for d in range(2):
            k_cum, u, q_dec, q_dot_k, ktail, total = terms[d][step]
            ci = order[d][step]
            s2b = state[d].astype(BF16)
            vb = (u - bmm(k_cum, s2b)).astype(BF16)
            dirs[d][6][:, ci * CHUNK:(ci + 1) * CHUNK, :] = bmm(q_dec, s2b) + bmm(q_dot_k, vb)
            state[d] = state[d] * total + bmm_tn(ktail, vb)
    s_ref[0] = state[0]
    s_ref[1] = state[1]


def _gdn_scan(qk, kk, vk, gcx, btx, gt, nb, s):
    h, nt_rows, _ = qk.shape
    tps = s // TS
    nlt = nb * tps
    fwd = lambda b, n: jnp.where(n == 0, nlt + b, b * tps + n - 1)
    bwd = lambda b, n: jnp.where(n == 0, nlt + b, b * tps + tps - n)
    hw = h * 128

    def views(shape, imap):
        return [pl.BlockSpec(shape, functools.partial(imap, t)) for t in (fwd, bwd)]

    heads = views((h, TS, 128), lambda t, b, n: (0, t(b, n), 0))
    lanes = [pl.BlockSpec((TS, hw), lambda b, n: (fwd(b, n), 0)), pl.BlockSpec((TS, hw), lambda b, n: (bwd(b, n), 1))]
    rows = views((None, 8, TS), lambda t, b, n: (t(b, n), 0, 0))
    outs = views((h, TS, HEAD_W), lambda t, b, n: (0, t(b, n), 0))
    return pl.pallas_call(
        _gdn_scan_kernel,
        grid=(nb, tps + 1),
        in_specs=heads * 3 + lanes * 2 + rows,
        out_specs=outs,
        out_shape=[jax.ShapeDtypeStruct((h, nt_rows, HEAD_W), F32)] * 2,
        scratch_shapes=[pltpu.VMEM((2, h, 2 * HEAD_W, HEAD_W), F32)],
        compiler_params=_cparams(("parallel", "arbitrary")),
        name="gdn_scan",
    )(qk, qk, kk, kk, vk, vk, gcx, gcx, btx, btx, gt, gt)


def _gdn_finish_kernel(of_ref, ob_ref, z_ref, nw_ref, y_ref):
    o = of_ref[...] + ob_ref[...]
    y = o * lax.rsqrt(jnp.mean(o * o, axis=-1, keepdims=True) + EPS) * nw_ref[...]
    y = jnp.concatenate([y[h] for h in range(N_HEADS)], axis=-1)
    y_ref[...] = (y * _silu(z_ref[...].astype(F32))).astype(BF16)


def _gdn_finish(of, ob, z, nw_row):
    nt = of.shape[1] // TM
    hsp = pl.BlockSpec((N_HEADS, TM, HEAD_W), lambda i: (0, i, 0))
    return pl.pallas_call(
        _gdn_finish_kernel,
        grid=(nt,),
        in_specs=[hsp, hsp, pl.BlockSpec((TM, 256), lambda i: (i, C_Z // 256)),
                  pl.BlockSpec(nw_row.shape, lambda i: (0, 0))],
        out_specs=pl.BlockSpec((TM, 256), lambda i: (i, 0)),
        out_shape=jax.ShapeDtypeStruct((of.shape[1], 256), BF16),
        compiler_params=_cparams(("parallel",)),
        name="gdn_finish",
    )(of, ob, z, nw_row)


S5_SUB = 8
S5_NS = S5_GROUPS * S5_STATE


def _s5_param_kernel(are_ref, aim_ref, ldt_ref, bre_ref, bim_ref, cre_ref, cim_ref,
                     bb_ref, kst_ref, cbd_ref, tab_ref):
    d = pl.program_id(0)
    lam_re = jnp.minimum(are_ref[...], -1e-4)
    lam_im = aim_ref[...]
    dt = jnp.exp(ldt_ref[...])

    def power(tau):
        mag = jnp.exp(lam_re * dt * tau)
        ang = lam_im * dt * tau
        return mag * jnp.cos(ang), mag * jnp.sin(ang)

    idx = lax.broadcasted_iota(jnp.int32, (S5_SUB, 1), 0)
    p_re, p_im = power(idx.astype(F32))
    lb_re, lb_im = p_re[1:2], p_im[1:2]
    den = lam_re * lam_re + lam_im * lam_im
    f_re = ((lb_re - 1.0) * lam_re + lb_im * lam_im) / den
    f_im = (lb_im * lam_re - (lb_re - 1.0) * lam_im) / den
    bb_re = f_re * bre_ref[...] - f_im * bim_ref[...]
    bb_im = f_re * bim_ref[...] + f_im * bre_ref[...]
    bb_ref[...] = jnp.concatenate([bb_re, bb_im], axis=-1).astype(BF16)
    c_re = cre_ref[...]
    c_im = cim_ref[...]
    cbd_ref[...] = jnp.concatenate([c_re, -c_im], axis=0).astype(BF16)
    for tau in range(S5_SUB):
        g_re = bb_re * p_re[tau:tau + 1] - bb_im * p_im[tau:tau + 1]
        g_im = bb_re * p_im[tau:tau + 1] + bb_im * p_re[tau:tau + 1]
        kst_ref[tau] = (_dot(g_re, c_re, precision=HI) - _dot(g_im, c_im, precision=HI)).astype(BF16)
    fwd = d == 0
    t_in = jnp.where(fwd, S5_SUB - 1 - idx, idx).astype(F32)
    t_out = jnp.where(fwd, idx + 1, S5_SUB - idx).astype(F32)
    for k, tau in enumerate((t_in, t_out, jnp.full((S5_SUB, 1), float(S5_SUB), F32))):
        tab_ref[k] = jnp.concatenate(power(tau), axis=-1)


def _s5_params(a_re, a_im, log_dt, b_re, b_im, c_re, c_im):
    g, p, gch = S5_GROUPS, S5_STATE, S5_GROUP_CH
    ns, c = S5_NS, g * gch
    eye = jnp.eye(g, dtype=F32)
    row = lambda a: a.astype(F32).reshape(2, 1, ns)
    ldt = jnp.broadcast_to(log_dt.astype(F32)[:, :, None], (2, g, p))
    b_bd = lambda b: jnp.einsum("gpc,gh->gchp", b.astype(F32), eye).reshape(c, ns)
    c_bd = lambda cc: jnp.einsum("dgcp,gh->dgphc", cc.astype(F32), eye).reshape(2, ns, c)
    per_dir = lambda *shape: pl.BlockSpec((None,) + shape, lambda d: (d,) + (0,) * len(shape))
    shared = pl.BlockSpec((c, ns), lambda d: (0, 0))
    return pl.pallas_call(
        _s5_param_kernel,
        grid=(2,),
        in_specs=[per_dir(1, ns)] * 3 + [shared] * 2 + [per_dir(ns, c)] * 2,
        out_specs=[per_dir(c, 2 * ns), per_dir(S5_SUB, c, c), per_dir(2 * ns, c), per_dir(3, S5_SUB, 2 * ns)],
        out_shape=[jax.ShapeDtypeStruct((2, c, 2 * ns), BF16),
                   jax.ShapeDtypeStruct((2, S5_SUB, c, c), BF16),
                   jax.ShapeDtypeStruct((2, 2 * ns, c), BF16),
                   jax.ShapeDtypeStruct((2, 3, S5_SUB, 2 * ns), F32)],
        compiler_params=_cparams(("parallel",), 48),
        name="s5_params",
    )(row(a_re), row(a_im), row(ldt), b_bd(b_re), b_bd(b_im), c_bd(c_re), c_bd(c_im))


def _s5_direction(u_ref, bb_ref, kst_ref, cbd_ref, tab_ref, y_ref, x_ref, xin_ref, xpv_ref, reverse):
    ns, sub = S5_NS, S5_SUB
    nsc = TS // sub
    u = u_ref[...]
    z = _dot(u, bb_ref[...])
    z_re = z[:, :ns].reshape(nsc, sub, ns)
    z_im = z[:, ns:].reshape(nsc, sub, ns)
    w = tab_ref[0]
    w_re, w_im = w[:, :ns][None], w[:, ns:][None]
    group_sum = lambda a: jnp.broadcast_to(jnp.sum(a, axis=1, keepdims=True), a.shape)
    xin_ref[0] = group_sum(w_re * z_re - w_im * z_im)
    xin_ref[1] = group_sum(w_re * z_im + w_im * z_re)
    a = tab_ref[2]
    a_re, a_im = a[:, :ns], a[:, ns:]

    def step(k, carry):
        x_re, x_im = carry
        n = nsc - 1 - k if reverse else k
        xpv_ref[0, n] = x_re
        xpv_ref[1, n] = x_im
        return (a_re * x_re - a_im * x_im + xin_ref[0, n], a_re * x_im + a_im * x_re + xin_ref[1, n])

    x_re, x_im = lax.fori_loop(0, nsc, step, (x_ref[0], x_ref[1]))
    x_ref[0] = x_re
    x_ref[1] = x_im
    o = tab_ref[1]
    o_re, o_im = o[:, :ns][None], o[:, ns:][None]
    p_re = (xpv_ref[0] * o_re - xpv_ref[1] * o_im).reshape(TS, ns)
    p_im = (xpv_ref[0] * o_im + xpv_ref[1] * o_re).reshape(TS, ns)
    y = _dot(jnp.concatenate([p_re, p_im], axis=-1).astype(BF16), cbd_ref[...])
    uf = u.astype(F32)
    pos = lax.broadcasted_iota(jnp.int32, (TS, 1), 0) % sub
    for tau in range(sub):
        if tau == 0:
            shifted = u
        else:
            rolled = pltpu.roll(uf, TS - tau if reverse else tau, axis=0)
            inside = (pos + tau < sub) if reverse else (pos >= tau)
            shifted = jnp.where(inside, rolled, 0.0).astype(BF16)
        y = y + _dot(shifted, kst_ref[tau])
    y_ref[...] = y


def _s5_scan_kernel(uf_ref, ub_ref, bbf, bbb, kstf, kstb, cbdf, cbdb, tabf, tabb, yf_ref, yb_ref,
                    x_ref, xin_ref, xpv_ref):
    @pl.when(pl.program_id(1) == 0)
    def _():
        x_ref[...] = jnp.zeros(x_ref.shape, F32)

    _s5_direction(uf_ref, bbf, kstf, cbdf, tabf, yf_ref, x_ref.at[0], xin_ref.at[0], xpv_ref.at[0], False)
    _s5_direction(ub_ref, bbb, kstb, cbdb, tabb, yb_ref, x_ref.at[1], xin_ref.at[1], xpv_ref.at[1], True)


def _s5_scan(z, bb, kst, cbd, tab, nb, s):
    nt_rows = z.shape[0]
    tps = s // TS
    nlt = nb * tps
    fwd = lambda b, n: jnp.where(n == 0, nlt + b, b * tps + n - 1)
    bwd = lambda b, n: jnp.where(n == 0, nlt + b, b * tps + tps - n)
    ns, c, nsc = S5_NS, S5_GROUPS * S5_GROUP_CH, TS // S5_SUB

    def both(a):
        return [pl.BlockSpec((None,) + a.shape[1:], lambda b, n, d=d: (d,) + (0,) * (a.ndim - 1)) for d in range(2)]

    state = lambda *lead: pltpu.VMEM((2, 2) + lead + (S5_SUB, ns), F32)
    return pl.pallas_call(
        _s5_scan_kernel,
        grid=(nb, tps + 1),
        in_specs=[pl.BlockSpec((TS, c), lambda b, n: (fwd(b, n), C_U // 256)),
                  pl.BlockSpec((TS, c), lambda b, n: (bwd(b, n), C_U // 256))]
                 + both(bb) + both(kst) + both(cbd) + both(tab),
        out_specs=[pl.BlockSpec((TS, c), lambda b, n: (fwd(b, n), 0)),
                   pl.BlockSpec((TS, c), lambda b, n: (bwd(b, n), 0))],
        out_shape=[jax.ShapeDtypeStruct((nt_rows, c), F32)] * 2,
        scratch_shapes=[state(), state(nsc), state(nsc)],
        compiler_params=_cparams(("parallel", "arbitrary"), 56),
        name="s5_scan",
    )(z, z, bb, bb, kst, kst, cbd, cbd, tab, tab)


def _s5_finish_kernel(yf_ref, yb_ref, u_ref, d_ref, w_ref, b_ref, o_ref):
    y = yf_ref[...] + yb_ref[...] + d_ref[...] * u_ref[...].astype(F32)
    y = jax.nn.gelu(y)
    gate = _sigmoid(_dot(y.astype(BF16), w_ref[...]) + b_ref[...])
    o_ref[...] = (y * gate).astype(BF16)


def _s5_finish(yf, yb, z, d_row, glu_w, glu_b_row):
    nt = yf.shape[0] // TM
    full = lambda a: pl.BlockSpec(a.shape, lambda i: (0,) * a.ndim)
    return pl.pallas_call(
        _s5_finish_kernel,
        grid=(nt,),
        in_specs=[pl.BlockSpec((TM, 256), lambda i: (i, 0)),
                  pl.BlockSpec((TM, 256), lambda i: (i, 0)),
                  pl.BlockSpec((TM, 256), lambda i: (i, C_U // 256)),
                  full(d_row), full(glu_w), full(glu_b_row)],
        out_specs=pl.BlockSpec((TM, 256), lambda i: (i, 0)),
        out_shape=jax.ShapeDtypeStruct((yf.shape[0], 256), BF16),
        compiler_params=_cparams(("parallel",)),
        name="s5_finish",
    )(yf, yb, z, d_row, glu_w, glu_b_row)


def _merge_kernel(x_ref, y0_ref, y1_ref, y2_ref, y3_ref, gates_ref, wb_ref, wo_ref, g1_ref, gain_ref, o_ref):
    acc = jnp.zeros((TM, D_MODEL), F32)
    for bi, y_ref in enumerate((y0_ref, y1_ref, y2_ref, y3_ref)):
        proj = _dot(y_ref[...], wb_ref[bi])
        acc = acc + gates_ref[:, bi * D_MODEL:(bi + 1) * D_MODEL].astype(F32) * proj
    y = _dot(acc.astype(BF16), wo_ref[...])
    o_ref[...] = x_ref[...] + g1_ref[...] * _rms(y, gain_ref[...])


def _merge(x, ys, gates, wb, wo, g1, gain, layer, tps, nb):
    nt = x.shape[0] // TM
    row = lambda i: (jnp.minimum(i // tps, nb), 0, 0)
    tile = lambda w: pl.BlockSpec((TM, w), lambda i: (i, 0))
    return pl.pallas_call(
        _merge_kernel,
        grid=(nt,),
        in_specs=[tile(D_MODEL)] + [tile(BRANCH_W)] * 4 + [tile(4 * D_MODEL),
                  pl.BlockSpec((None, 4, BRANCH_W, D_MODEL), lambda i: (layer, 0, 0, 0)),
                  pl.BlockSpec((None, D_MODEL, D_MODEL), lambda i: (layer, 0, 0)),
                  pl.BlockSpec((None, 1, D_MODEL), row),
                  pl.BlockSpec((1, D_MODEL), lambda i: (0, 0))],
        out_specs=tile(D_MODEL),
        out_shape=jax.ShapeDtypeStruct(x.shape, F32),
        compiler_params=_cparams(("parallel",), 48),
        name="merge",
    )(x, *ys, gates, wb, wo, g1, gain)


def _mlp_kernel(x_ref, sc_ref, sh_ref, g2_ref, gin_ref, gout_ref, w1_ref, w2_ref, o_ref, h_ref, acc_ref):
    j = pl.program_id(1)

    @pl.when(j == 0)
    def _():
        h_ref[...] = _norm_mod(x_ref[...], gin_ref[...], sc_ref[...], sh_ref[...]).astype(BF16)
        acc_ref[...] = jnp.zeros(acc_ref.shape, F32)

    t = jnp.maximum(_dot(h_ref[...], w1_ref[...]), 0.0)
    acc_ref[...] += _dot((t * t).astype(BF16), w2_ref[...])

    @pl.when(j == pl.num_programs(1) - 1)
    def _():
        o_ref[...] = x_ref[...] + g2_ref[...] * _rms(acc_ref[...], gout_ref[...])


def _mlp(x, sc, sh, g2, gin, gout, w1, w2, layer, tps, nb, tf=1024):
    nt = x.shape[0] // TM
    row = lambda i, j: (jnp.minimum(i // tps, nb), 0, 0)
    vec = pl.BlockSpec((1, D_MODEL), lambda i, j: (0, 0))
    return pl.pallas_call(
        _mlp_kernel,
        grid=(nt, D_FF // tf),
        in_specs=[pl.BlockSpec((TM, D_MODEL), lambda i, j: (i, 0)),
                  pl.BlockSpec((None, 1, D_MODEL), row), pl.BlockSpec((None, 1, D_MODEL), row),
                  pl.BlockSpec((None, 1, D_MODEL), row), vec, vec,
                  pl.BlockSpec((None, D_MODEL, tf), lambda i, j: (layer, 0, j)),
                  pl.BlockSpec((None, tf, D_MODEL), lambda i, j: (layer, j, 0))],
        out_specs=pl.BlockSpec((TM, D_MODEL), lambda i, j: (i, 0)),
        out_shape=jax.ShapeDtypeStruct(x.shape, F32),
        scratch_shapes=[pltpu.VMEM((TM, D_MODEL), BF16), pltpu.VMEM((TM, D_MODEL), F32)],
        compiler_params=_cparams(("parallel", "arbitrary"), 48),
        name="mlp",
    )(x, sc, sh, g2, gin, gout, w1, w2)


def _pack_w_in(w_in):
    o_na, o_cq, o_ckv, o_gdn, o_z, o_a, o_b, o_u, o_gate = 0, 768, 1024, 1184, 1952, 2208, 2216, 2224, 2480
    idx = np.zeros(ZW, np.int32)
    keep = np.zeros(ZW, np.float32)

    def put(dst, src):
        idx[dst:dst + len(src)] = src
        keep[dst:dst + len(src)] = 1.0

    put(C_GDN, np.arange(o_gdn, o_gdn + 768))
    put(C_NA, np.arange(o_na, o_na + 768))
    put(C_CQ, np.arange(o_cq, o_cq + 256))
    put(C_CKV, np.arange(o_ckv, o_ckv + 160))
    put(C_CKV + L_A, np.arange(o_a, o_a + 8))
    put(C_CKV + L_B, np.arange(o_b, o_b + 8))
    put(C_CKV + L_KRS, o_ckv + MLA_KV_LORA + ROPE_SWAP)
    put(C_Z, np.arange(o_z, o_z + 256))
    put(C_U, np.arange(o_u, o_u + 256))
    small = (jnp.take(w_in, jnp.asarray(idx), axis=2) * jnp.asarray(keep)).astype(BF16)
    gates = w_in[:, :, o_gate:].astype(BF16)
    return small, gates


def _mla_weights(w_uq, w_ukv):
    depth = w_uq.shape[0]
    hq = MLA_NOPE + MLA_ROPE
    wq = w_uq.reshape(depth, -1, N_HEADS, hq)
    pad = lambda a, lo, hi: jnp.pad(a, ((0, 0), (0, 0), (0, 0), (lo, hi)))
    wq_ext = pad(wq, 0, 128 - hq).reshape(depth, -1, N_HEADS * 128)
    wq_sw = pad(wq[..., MLA_NOPE:][..., ROPE_SWAP], MLA_NOPE, 128 - hq).reshape(depth, -1, N_HEADS * 128)
    wkv = w_ukv.reshape(depth, -1, N_HEADS, MLA_NOPE + HEAD_W)
    wk = pad(wkv[..., :MLA_NOPE], 0, 128 - MLA_NOPE).reshape(depth, -1, N_HEADS * 128)
    wv = pad(wkv[..., MLA_NOPE:], 0, 128 - HEAD_W).reshape(depth, -1, N_HEADS * 128)
    return [a.astype(BF16) for a in (wq_ext, wq_sw, wk, wv, jnp.swapaxes(wv, 1, 2))]


def _rope_place_mats():
    p1 = np.zeros((256, N_HEADS * 128), np.float32)
    p2 = np.zeros((256, N_HEADS * 128), np.float32)
    for h in range(N_HEADS):
        for r in range(MLA_ROPE):
            p1[L_KR + r, h * 128 + MLA_NOPE + r] = 1.0
            p2[L_KRS + r, h * 128 + MLA_NOPE + r] = 1.0
    return jnp.asarray(p1, BF16), jnp.asarray(p2, BF16)


def _rope_tables(s):
    quarter = MLA_ROPE // 4
    inv_freq = ROPE_BASE ** (-jnp.arange(quarter, dtype=F32) / quarter)
    t = jnp.arange(s)
    ang_r = (t // GRID_W).astype(F32)[:, None] * inv_freq[None, :]
    ang_c = (t % GRID_W).astype(F32)[:, None] * inv_freq[None, :]
    cr, sr, cc, sn = jnp.cos(ang_r), jnp.sin(ang_r), jnp.cos(ang_c), jnp.sin(ang_c)
    cos = jnp.concatenate([jnp.ones((s, MLA_NOPE), F32), cr, cr, cc, cc, jnp.ones((s, 32), F32)], axis=1)
    sin = jnp.concatenate([jnp.zeros((s, MLA_NOPE), F32), -sr, sr, -sn, sn, jnp.zeros((s, 32), F32)], axis=1)
    cos = jnp.concatenate([cos, jnp.ones((TM, 128), F32)], axis=0)
    sin = jnp.concatenate([sin, jnp.zeros((TM, 128), F32)], axis=0)
    return cos, sin


def _head_block_ones():
    r = np.arange(BRANCH_W)
    return jnp.asarray((r[:, None] // HEAD_W == r[None, :] // HEAD_W).astype(np.float32))


def _lane_row(vals, offset, width=256):
    return jnp.zeros((1, width), F32).at[0, offset:offset + vals.shape[0]].set(vals.astype(F32))


def _gdn_mixer(z, conv_w, a_log, dt_bias, norm_w, consts, nb, s, ctx_len):
    alog_row = _lane_row(a_log.reshape(-1), L_A)
    dtb_row = _lane_row(dt_bias.reshape(-1), L_A)
    qk, kk, vk, gcx, btx, gt = _gdn_prep(z, conv_w, alog_row, dtb_row, consts["bseg"], consts["sel"],
                                         s // TS, nb * s // TS)
    of, ob = _gdn_scan(qk, kk, vk, gcx, btx, gt, nb, s)
    return _gdn_finish(of, ob, z, norm_w.astype(F32)[None, :])


def _s5_mixer(z, a_re, a_im, log_dt, b_re, b_im, c_re, c_im, d_skip, glu_w, glu_b, nb, s, ctx_len):
    bb, kst, cbd, tab = _s5_params(a_re, a_im, log_dt, b_re, b_im, c_re, c_im)
    yf, yb = _s5_scan(z, bb, kst, cbd, tab, nb, s)
    return _s5_finish(yf, yb, z, d_skip.astype(F32)[None, :], glu_w.astype(BF16), glu_b.astype(F32)[None, :])


def kernel(x, c, ctx, c_ctx, ada_w, ada_b, norm_gains, w_in, na_rpb, mla_q_norm, mla_kv_norm, mla_w_uq, mla_w_ukv, gdn_conv, gdn_a_log, gdn_dt_bias, gdn_norm, s5_a_re, s5_a_im, s5_log_dt, s5_b_re, s5_b_im, s5_c_re, s5_c_im, s5_d, s5_glu_w, s5_glu_b, w_branch, w_out, mlp_w1, mlp_w2):
    nb, s, d = x.shape
    ctx_len = ctx.shape[1]
    depth = ada_w.shape[0]
    assert d == D_MODEL and ctx_len == TS and nb * ctx_len == TM and s % TM == 0 and s // GRID_W >= 16
    nl = nb * s
    tps = s // TM

    xs = jnp.concatenate([x.reshape(nl, d), ctx.reshape(nb * ctx_len, d)], axis=0)
    cvec = jnp.zeros((8, d), F32).at[:nb].set(c).at[nb].set(c_ctx)
    mod = _modulation(cvec, ada_w, ada_b)
    mod = mod[:, :nb + 1].reshape(depth, nb + 1, 6, 1, d)

    w_small, w_gates = _pack_w_in(w_in)
    wq_ext, wq_sw, wk_ext, wv, wvt = _mla_weights(mla_w_uq, mla_w_ukv)
    p1, p2 = _rope_place_mats()
    cos_t, sin_t = _rope_tables(s)
    sel = np.zeros((8, 128), np.float32)
    sel[np.arange(8), L_A - 128 + np.arange(8)] = 1.0
    consts = {"bseg": _head_block_ones().astype(BF16), "sel": jnp.asarray(sel)}
    wb = w_branch.astype(BF16)
    wo = w_out.astype(BF16)
    w1 = mlp_w1.astype(BF16)
    w2 = mlp_w2.astype(BF16)
    gains = norm_gains.astype(F32)

    for l in range(depth):
        sh1, sc1, g1, sh2, sc2, g2 = [mod[l, :, i] for i in range(6)]
        z = _inproj(xs, sc1, sh1, gains[l, 0][None], w_small, l, tps, nb, False)
        gates = _inproj(xs, sc1, sh1, gains[l, 0][None], w_gates, l, tps, nb, True)

        y_na = jnp.concatenate([
            _na_latent(z, na_rpb[l], nb, s),
            _ctx_attention(z, z, z, C_NA // 256, C_NA // 256 + 1, C_NA // 256 + 2, 256, 256, nl // TS, nb,
                           HEAD_W ** -0.5, False, "na_ctx")], axis=0)

        qm, km, vm, vt = _mla_prep(z, cos_t, sin_t, mla_q_norm[l].astype(F32)[None],
                                   mla_kv_norm[l].astype(F32)[None], wq_ext[l], wq_sw[l], wk_ext[l], wv[l], wvt[l],
                                   p1, p2, tps, nl // TM)
        y_mla = jnp.concatenate([
            _mla_latent(qm, km, vt, nb, s),
            _ctx_attention(qm, km, vm, 0, 0, 0, N_HEADS * 128, N_HEADS * 128, nl // TS, nb, 1.0, True,
                           "mla_ctx")], axis=0)

        y_gdn = _gdn_mixer(z, gdn_conv[l].astype(F32), gdn_a_log[l], gdn_dt_bias[l], gdn_norm[l], consts,
                           nb, s, ctx_len)
        y_s5 = _s5_mixer(z, s5_a_re[l], s5_a_im[l], s5_log_dt[l], s5_b_re[l], s5_b_im[l], s5_c_re[l], s5_c_im[l],
                         s5_d[l], s5_glu_w[l], s5_glu_b[l], nb, s, ctx_len)

        xs = _merge(xs, (y_na, y_mla, y_gdn, y_s5), gates, wb, wo, g1, gains[l, 1][None], l, tps, nb)
        xs = _mlp(xs, sc2, sh2, g2, gains[l, 2][None], gains[l, 3][None], w1, w2, l, tps, nb)
    return xs[:nl].reshape(nb, s, d)
```

```python
import functools
import math

import numpy as np
import jax
import jax.numpy as jnp
from jax import lax
from jax.experimental import pallas as pl
from jax.experimental.pallas import tpu as pltpu

F32 = jnp.float32
BF16 = jnp.bfloat16
HI = lax.Precision.HIGHEST
EPS = 1e-6

D_MODEL = 1024
GRID_W = 64
NA_WIN_H = 8
NA_WIN_W = 16
N_HEADS = 4
HEAD_W = 64
BRANCH_W = 256
MLA_NOPE = 64
MLA_ROPE = 32
MLA_KV_LORA = 128
ROPE_BASE = 10000.0
GDN_CONV = 4
CHUNK = 64
S5_GROUPS = 16
S5_GROUP_CH = 16
S5_STATE = 64
D_FF = 4 * D_MODEL

TM = 512
TN = 512
TS = 256
NEG = -1e30

C_GDN = 0
C_NA = 768
C_CQ = 1536
C_CKV = 1792
C_Z = 2048
C_U = 2304
ZW = 2560
L_KR = 128
L_A = 160
L_B = 168
L_KRS = 176
ROPE_SWAP = np.concatenate([np.arange(8, 16), np.arange(0, 8), np.arange(24, 32), np.arange(16, 24)])


def _cparams(sem, vmem_mb=None):
    kw = dict(dimension_semantics=sem)
    if vmem_mb is not None:
        kw["vmem_limit_bytes"] = vmem_mb * 1024 * 1024
    return pltpu.CompilerParams(**kw)


def _dot(a, b, **kw):
    return jnp.dot(a, b, preferred_element_type=F32, **kw)


def _dot_nt(a, b):
    return lax.dot_general(a, b, (((1,), (1,)), ((), ())), preferred_element_type=F32)


def _sigmoid(x):
    return 1.0 / (1.0 + jnp.exp(-x))


def _silu(x):
    return x * _sigmoid(x)


def _mod_kernel(c_ref, w_ref, b_ref, o_ref):
    c = c_ref[...]
    o_ref[...] = _dot(_silu(c), w_ref[...], precision=HI) + b_ref[...]


def _modulation(cvec, ada_w, ada_b):
    depth, d, n = ada_w.shape
    tn = 1536
    return pl.pallas_call(
        _mod_kernel,
        grid=(depth, n // tn),
        in_specs=[pl.BlockSpec((8, d), lambda l, j: (0, 0)),
                  pl.BlockSpec((None, d, tn), lambda l, j: (l, 0, j)),
                  pl.BlockSpec((None, 1, tn), lambda l, j: (l, 0, j))],
        out_specs=pl.BlockSpec((None, 8, tn), lambda l, j: (l, 0, j)),
        out_shape=jax.ShapeDtypeStruct((depth, 8, n), F32),
        compiler_params=_cparams(("parallel", "parallel"), 40),
        name="modulation",
    )(cvec, ada_w, ada_b.reshape(depth, 1, n))


def _norm_mod(x, gain, sc, sh):
    r = lax.rsqrt(jnp.mean(x * x, axis=-1, keepdims=True) + EPS)
    return (x * r * gain) * (1.0 + sc) + sh


def _inproj_kernel(x_ref, sc_ref, sh_ref, gain_ref, w_ref, o_ref, h_ref, *, gate):
    @pl.when(pl.program_id(1) == 0)
    def _():
        h_ref[...] = _norm_mod(x_ref[...], gain_ref[...], sc_ref[...], sh_ref[...]).astype(BF16)

    acc = _dot(h_ref[...], w_ref[...])
    if gate:
        acc = _sigmoid(acc)
    o_ref[...] = acc.astype(BF16)


def _inproj(x, sc, sh, gain, w, layer, tps, nb, gate):
    nt = x.shape[0] // TM
    width = w.shape[-1]
    tn = width // 2
    row = lambda i, j: (jnp.minimum(i // tps, nb), 0, 0)
    return pl.pallas_call(
        functools.partial(_inproj_kernel, gate=gate),
        grid=(nt, width // tn),
        in_specs=[pl.BlockSpec((TM, D_MODEL), lambda i, j: (i, 0)),
                  pl.BlockSpec((None, 1, D_MODEL), row),
                  pl.BlockSpec((None, 1, D_MODEL), row),
                  pl.BlockSpec((1, D_MODEL), lambda i, j: (0, 0)),
                  pl.BlockSpec((None, D_MODEL, tn), lambda i, j: (layer, 0, j))],
        out_specs=pl.BlockSpec((TM, tn), lambda i, j: (i, j)),
        out_shape=jax.ShapeDtypeStruct((x.shape[0], width), BF16),
        scratch_shapes=[pltpu.VMEM((TM, D_MODEL), BF16)],
        compiler_params=_cparams(("parallel", "arbitrary"), 40),
        name="inproj_gates" if gate else "inproj",
    )(x, sc, sh, gain, w)


def _head_lane_mask(width, head_w, h):
    lane = lax.broadcasted_iota(jnp.int32, (1, width), 1)
    return (lane >= h * head_w) & (lane < (h + 1) * head_w)


def _na_build_bias(rpb_ref, bias_ref, r0, kb0, rows_total):
    w = GRID_W
    qc = lax.broadcasted_iota(jnp.int32, (w, 2 * w), 0)
    lane = lax.broadcasted_iota(jnp.int32, (w, 2 * w), 1)
    kc = lane % w
    cs = jnp.clip(qc - NA_WIN_W // 2, 0, w - NA_WIN_W)
    col_ok = (kc >= cs) & (kc < cs + NA_WIN_W)
    left = lane < w
    neg = jnp.full((w, 2 * w), NEG, F32)
    for h in range(N_HEADS):
        t = rpb_ref[h]
        toep = []
        for a in range(2 * NA_WIN_H - 1):
            row = jnp.broadcast_to(t[a:a + 1, :], (w, 2 * w))
            ra = pltpu.roll(row, 2 * w - (NA_WIN_W - 1), axis=1, stride=1, stride_axis=0)
            rb = pltpu.roll(ra, w, axis=1)
            toep.append((jnp.where(col_ok, ra, NEG), jnp.where(col_ok, rb, NEG)))
        for qr in range(8):
            rs = min(max(r0 + qr - NA_WIN_H // 2, 0), rows_total - NA_WIN_H)
            for kp in range(8):
                halves = []
                for side in range(2):
                    kr = kb0 + 2 * kp + side
                    halves.append(toep[kr - (r0 + qr) + NA_WIN_H - 1][side] if rs <= kr < rs + NA_WIN_H else neg)
                bias_ref[h, qr * w:(qr + 1) * w, kp * 2 * w:(kp + 1) * 2 * w] = jnp.where(left, halves[0], halves[1])


def _na_kernel(q_ref, k_ref, v_ref, kc_ref, vc_ref, rpb_ref, o_ref, bias_ref, *, rows_total):
    i = pl.program_id(1)
    last = pl.num_programs(1) - 1

    @pl.when(i == 0)
    def _():
        _na_build_bias(rpb_ref, bias_ref, 0, 0, rows_total)

    @pl.when(i == 1)
    def _():
        _na_build_bias(rpb_ref, bias_ref, 8, 4, rows_total)

    @pl.when(i == last)
    def _():
        _na_build_bias(rpb_ref, bias_ref, rows_total - 8, rows_total - 16, rows_total)

    kb = jnp.clip(2 * i - 1, 0, rows_total // 4 - 4)
    start = pl.multiple_of(kb * (4 * GRID_W), 4 * GRID_W)
    nk = 2 * NA_WIN_H * GRID_W
    q = q_ref[...]
    kw = k_ref[pl.ds(start, nk), :]
    vw = v_ref[pl.ds(start, nk), :]
    kc = kc_ref[...]
    vc = vc_ref[...]
    scale = HEAD_W ** -0.5
    out = jnp.zeros(q.shape, F32)
    for h in range(N_HEADS):
        hm = _head_lane_mask(BRANCH_W, HEAD_W, h)
        qh = jnp.where(hm, q, jnp.zeros_like(q))
        sb = _dot_nt(qh, kw) * scale + bias_ref[h]
        sc = _dot_nt(qh, kc) * scale
        m = jnp.maximum(jnp.max(sb, axis=-1, keepdims=True), jnp.max(sc, axis=-1, keepdims=True))
        pb = jnp.exp(sb - m)
        pc = jnp.exp(sc - m)
        den = jnp.sum(pb, axis=-1, keepdims=True) + jnp.sum(pc, axis=-1, keepdims=True)
        o = _dot(pb.astype(BF16), vw) + _dot(pc.astype(BF16), vc)
        out = jnp.where(hm, o / den, out)
    o_ref[...] = out.astype(BF16)


def _na_latent(z, rpb, nb, s):
    rows_total = s // GRID_W
    qb = 8 * GRID_W
    nq = s // qb
    nl = nb * s
    rpb = jnp.pad(rpb.astype(F32), ((0, 0), (0, 1), (0, 2 * GRID_W - (2 * NA_WIN_W - 1))))
    return pl.pallas_call(
        functools.partial(_na_kernel, rows_total=rows_total),
        grid=(nb, nq),
        in_specs=[pl.BlockSpec((qb, BRANCH_W), lambda b, i: (b * nq + i, C_NA // 256)),
                  pl.BlockSpec((s, BRANCH_W), lambda b, i: (b, C_NA // 256 + 1)),
                  pl.BlockSpec((s, BRANCH_W), lambda b, i: (b, C_NA // 256 + 2)),
                  pl.BlockSpec((TS, BRANCH_W), lambda b, i: (nl // TS + b, C_NA // 256 + 1)),
                  pl.BlockSpec((TS, BRANCH_W), lambda b, i: (nl // TS + b, C_NA // 256 + 2)),
                  pl.BlockSpec(rpb.shape, lambda b, i: (0, 0, 0))],
        out_specs=pl.BlockSpec((qb, BRANCH_W), lambda b, i: (b * nq + i, 0)),
        out_shape=jax.ShapeDtypeStruct((nl, BRANCH_W), BF16),
        scratch_shapes=[pltpu.VMEM((N_HEADS, qb, 2 * qb), F32)],
        compiler_params=_cparams(("parallel", "arbitrary"), 56),
        name="na_latent",
    )(z, z, z, z, z, rpb)


def _ctx_attn_kernel(q_ref, k_ref, v_ref, o_ref, *, scale, base2):
    q = q_ref[...]
    k = k_ref[...]
    v = v_ref[...]
    qw = q.shape[-1]
    vw = v.shape[-1] // N_HEADS
    outs = []
    for h in range(N_HEADS):
        qh = jnp.where(_head_lane_mask(qw, qw // N_HEADS, h), q, jnp.zeros_like(q))
        s = _dot_nt(qh, k) * scale
        m = jnp.max(s, axis=-1, keepdims=True)
        p = jnp.exp2(s - m) if base2 else jnp.exp(s - m)
        den = jnp.sum(p, axis=-1, keepdims=True)
        o = _dot(p.astype(BF16), v)
        outs.append(o[:, h * vw:h * vw + HEAD_W] / den)
    o_ref[...] = jnp.concatenate(outs, axis=-1).astype(BF16)


def _ctx_attention(q, k, v, qcol, kcol, vcol, qw, vw, row0, nb, scale, base2, name):
    return pl.pallas_call(
        functools.partial(_ctx_attn_kernel, scale=scale, base2=base2),
        grid=(nb,),
        in_specs=[pl.BlockSpec((TS, qw), lambda b: (row0 + b, qcol)),
                  pl.BlockSpec((TS, qw), lambda b: (row0 + b, kcol)),
                  pl.BlockSpec((TS, vw), lambda b: (row0 + b, vcol))],
        out_specs=pl.BlockSpec((TS, BRANCH_W), lambda b: (b, 0)),
        out_shape=jax.ShapeDtypeStruct((nb * TS, BRANCH_W), BF16),
        compiler_params=_cparams(("parallel",)),
        name=name,
    )(q, k, v)


def _rms(x, gain):
    return x * lax.rsqrt(jnp.mean(x * x, axis=-1, keepdims=True) + EPS) * gain


def _mla_prep_kernel(cq_ref, ckv_ref, cos_ref, sin_ref, qn_ref, kvn_ref, wq_ref, wqs_ref, wk_ref, wv_ref, wvt_ref,
                     p1_ref, p2_ref, q_ref, k_ref, v_ref, vt_ref):
    cos = jnp.concatenate([cos_ref[...]] * N_HEADS, axis=-1)
    sin = jnp.concatenate([sin_ref[...]] * N_HEADS, axis=-1)
    cqn = _rms(cq_ref[...].astype(F32), qn_ref[...]).astype(BF16)
    scale = (MLA_NOPE + MLA_ROPE) ** -0.5 * math.log2(math.e)
    q = _dot(cqn, wq_ref[...]) * cos + _dot(cqn, wqs_ref[...]) * sin
    q_ref[...] = (q * scale).astype(BF16)
    ckv = ckv_ref[...]
    kvn = _rms(ckv[:, :MLA_KV_LORA].astype(F32), kvn_ref[...]).astype(BF16)
    k = (_dot(kvn, wk_ref[...]) + _dot(ckv, p1_ref[...])) * cos + _dot(ckv, p2_ref[...]) * sin
    k_ref[...] = k.astype(BF16)
    lane = lax.broadcasted_iota(jnp.int32, (1, N_HEADS * 128), 1)
    v_ref[...] = jnp.where(lane % 128 == HEAD_W, 1.0, _dot(kvn, wv_ref[...])).astype(BF16)
    row = lax.broadcasted_iota(jnp.int32, (N_HEADS * 128, 1), 0)
    vt_ref[...] = jnp.where(row % 128 == HEAD_W, 1.0, _dot_nt(wvt_ref[...], kvn)).astype(BF16)


def _mla_prep(z, cos_t, sin_t, qn, kvn, wq, wqs, wk, wv, wvt, p1, p2, tps, n_lat_tiles):
    nt_rows = z.shape[0]
    nt = nt_rows // TM
    full = lambda a: pl.BlockSpec(a.shape, lambda i: (0,) * a.ndim)
    tab = lambda i: (jnp.where(i < n_lat_tiles, i % tps, tps), 0)
    hw = N_HEADS * 128
    return pl.pallas_call(
        _mla_prep_kernel,
        grid=(nt,),
        in_specs=[pl.BlockSpec((TM, 256), lambda i: (i, C_CQ // 256)),
                  pl.BlockSpec((TM, 256), lambda i: (i, C_CKV // 256)),
                  pl.BlockSpec((TM, 128), tab), pl.BlockSpec((TM, 128), tab),
                  full(qn), full(kvn), full(wq), full(wqs), full(wk), full(wv), full(wvt), full(p1), full(p2)],
        out_specs=[pl.BlockSpec((TM, hw), lambda i: (i, 0)),
                   pl.BlockSpec((TM, hw), lambda i: (i, 0)),
                   pl.BlockSpec((TM, hw), lambda i: (i, 0)),
                   pl.BlockSpec((hw, TM), lambda i: (0, i))],
        out_shape=[jax.ShapeDtypeStruct((nt_rows, hw), BF16)] * 3 + [jax.ShapeDtypeStruct((hw, nt_rows), BF16)],
        compiler_params=_cparams(("parallel",)),
        name="mla_prep",
    )(z, z, cos_t, sin_t, qn, kvn, wq, wqs, wk, wv, wvt, p1, p2)


def _flash_kernel(q_ref, kl_ref, vl_ref, kc_ref, vc_ref, o_ref, st_ref, *, tk, n_lat):
    tq = q_ref.shape[0]
    s_len = n_lat * tk
    heads = (slice(0, 128), slice(128, 256))
    qs = [q_ref[:, hs] for hs in heads]
    group_max = lambda st: jnp.max(st.reshape(st.shape[0] // 8, 8, tq), axis=0)

    def score_tile(hh, k, rows, mx):
        st = _dot_nt(k, qs[hh])
        st_ref[hh, rows, :] = st
        return jnp.maximum(mx, group_max(st))

    def max_body(t, carry):
        rows = pl.ds(pl.multiple_of(t * tk, tk), tk)
        return tuple(score_tile(hh, kl_ref[rows, heads[hh]], rows, carry[hh]) for hh in range(2))

    mx = lax.fori_loop(0, n_lat, max_body, tuple(jnp.full((8, tq), NEG, F32) for _ in range(2)), unroll=4)
    ctx_rows = slice(s_len, s_len + kc_ref.shape[0])
    ms = [jnp.max(score_tile(hh, kc_ref[:, heads[hh]], ctx_rows, mx[hh]), axis=0, keepdims=True) for hh in range(2)]

    def weighted(hh, rows, vt):
        return _dot(vt, jnp.exp2(st_ref[hh, rows, :] - ms[hh]).astype(BF16))

    def acc_body(t, carry):
        r0 = pl.multiple_of(t * tk, tk)
        return tuple(carry[hh] + weighted(hh, pl.ds(r0, tk), vl_ref[heads[hh], pl.ds(r0, tk)]) for hh in range(2))

    accs = lax.fori_loop(0, n_lat, acc_body, tuple(jnp.zeros((128, tq), F32) for _ in range(2)), unroll=4)
    outs = []
    for hh in range(2):
        acc = (accs[hh] + weighted(hh, ctx_rows, vc_ref[heads[hh], :])).T
        outs.append(acc[:, :HEAD_W] / acc[:, HEAD_W:HEAD_W + 1])
    o_ref[...] = jnp.concatenate(outs, axis=-1).astype(BF16)


def _mla_latent(qm, km, vt, nb, s, tq=256, tk=512):
    nq = s // tq
    nl = nb * s
    return pl.pallas_call(
        functools.partial(_flash_kernel, tk=tk, n_lat=s // tk),
        grid=(nb, 2, nq),
        in_specs=[pl.BlockSpec((tq, 256), lambda b, hp, i: (b * nq + i, hp)),
                  pl.BlockSpec((s, 256), lambda b, hp, i: (b, hp)),
                  pl.BlockSpec((256, s), lambda b, hp, i: (hp, b)),
                  pl.BlockSpec((TS, 256), lambda b, hp, i: (nl // TS + b, hp)),
                  pl.BlockSpec((256, TS), lambda b, hp, i: (hp, nl // TS + b))],
        out_specs=pl.BlockSpec((tq, 128), lambda b, hp, i: (b * nq + i, hp)),
        out_shape=jax.ShapeDtypeStruct((nl, BRANCH_W), BF16),
        scratch_shapes=[pltpu.VMEM((2, s + TS, tq), F32)],
        compiler_params=_cparams(("parallel", "parallel", "arbitrary"), 56),
        name="mla_flash",
    )(qm, km, vt, km, vt)


def _gdn_prep_kernel(prev_ref, cur_ref, next_ref, ckv_ref, conv_ref, alog_ref, dtb_ref, bseg_ref, sel_ref,
                     qk_ref, kk_ref, vk_ref, gcx_ref, btx_ref, gt_ref, *, tps, n_lat_tiles):
    i = pl.program_id(0)
    is_ctx = i >= n_lat_tiles
    first = is_ctx | (i % tps == 0)
    last = is_ctx | (i % tps == tps - 1)
    prev = jnp.where(first, 0.0, prev_ref[...].astype(F32))
    nxt = jnp.where(last, 0.0, next_ref[...].astype(F32))
    ext = jnp.concatenate([prev, cur_ref[...].astype(F32), nxt], axis=0)
    n_ext = TS + 16
    acc = jnp.zeros((TS, 3 * BRANCH_W), F32)
    for j in range(GDN_CONV):
        shifted = pltpu.roll(ext, n_ext - (8 - GDN_CONV // 2 + j), axis=0)[:TS]
        acc = acc + shifted * conv_ref[j:j + 1, :]
    x = _silu(acc)
    bseg = bseg_ref[...]

    def l2n(a):
        sq = a * a
        hi = sq.astype(BF16)
        lo = (sq - hi.astype(F32)).astype(BF16)
        return a * lax.rsqrt(_dot(hi, bseg) + _dot(lo, bseg) + EPS)

    q = l2n(x[:, :BRANCH_W]) * (HEAD_W ** -0.5)
    k = l2n(x[:, BRANCH_W:2 * BRANCH_W])
    v = x[:, 2 * BRANCH_W:]
    for h in range(N_HEADS):
        hs = slice(h * HEAD_W, (h + 1) * HEAD_W)
        qk_ref[h] = jnp.concatenate([q[:, hs], k[:, hs]], axis=-1)
        kk_ref[h] = jnp.concatenate([k[:, hs], k[:, hs]], axis=-1)
        vk_ref[h] = jnp.concatenate([v[:, hs], k[:, hs]], axis=-1)

    ab = ckv_ref[:, 128:].astype(F32)
    la, lb = L_A - 128, L_B - 128
    sp_in = ab + dtb_ref[:, 128:]
    softplus = jnp.maximum(sp_in, 0.0) + jnp.log1p(jnp.exp(-jnp.abs(sp_in)))
    lane = lax.broadcasted_iota(jnp.int32, (1, 128), 1)
    g = jnp.where((lane >= la) & (lane < lb), -jnp.exp(alog_ref[:, 128:]) * softplus, 0.0)
    beta = _sigmoid(ab)
    pos = lax.broadcasted_iota(jnp.int32, (TS, 1), 0) % CHUNK
    pre, suf = g, g
    step = 1
    while step < CHUNK:
        pre = pre + jnp.where(pos >= step, pltpu.roll(pre, step, axis=0), 0.0)
        suf = suf + jnp.where(pos + step < CHUNK, pltpu.roll(suf, TS - step, axis=0), 0.0)
        step *= 2
    gc = jnp.where(lane < la + N_HEADS, pre, suf)
    gt_ref[...] = lax.dot_general(sel_ref[...], gc, (((1,), (1,)), ((), ())), precision=HI,
                                  preferred_element_type=F32)
    spread = lambda a, l0: jnp.concatenate(
        [jnp.broadcast_to(a[:, l0 + dh:l0 + dh + 1], (TS, 128)) for dh in range(2 * N_HEADS)], axis=-1)
    gcx_ref[...] = spread(gc, la)
    btx_ref[...] = spread(beta, lb)


def _gdn_prep(z, conv_w, alog_row, dtb_row, bseg, sel, tps_s, n_lat_tiles):
    nt_rows = z.shape[0]
    nt = nt_rows // TS
    hb = TS // 8
    full = lambda a: pl.BlockSpec(a.shape, lambda i: (0,) * a.ndim)
    xw = 2 * N_HEADS * 128
    hsp = pl.BlockSpec((N_HEADS, TS, 128), lambda i: (0, i, 0))
    return pl.pallas_call(
        functools.partial(_gdn_prep_kernel, tps=tps_s, n_lat_tiles=n_lat_tiles),
        grid=(nt,),
        in_specs=[pl.BlockSpec((8, 768), lambda i: (jnp.maximum(i * hb - 1, 0), 0)),
                  pl.BlockSpec((TS, 768), lambda i: (i, 0)),
                  pl.BlockSpec((8, 768), lambda i: (jnp.minimum((i + 1) * hb, nt * hb - 1), 0)),
                  pl.BlockSpec((TS, 256), lambda i: (i, C_CKV // 256)),
                  full(conv_w), full(alog_row), full(dtb_row), full(bseg), full(sel)],
        out_specs=[hsp] * 3
                  + [pl.BlockSpec((TS, xw), lambda i: (i, 0))] * 2
                  + [pl.BlockSpec((None, 8, TS), lambda i: (i, 0, 0))],
        out_shape=[jax.ShapeDtypeStruct((N_HEADS, nt_rows, 128), F32)] * 3
                  + [jax.ShapeDtypeStruct((nt_rows, xw), F32)] * 2
                  + [jax.ShapeDtypeStruct((nt, 8, TS), F32)],
        compiler_params=_cparams(("parallel",)),
        name="gdn_prep",
    )(z, z, z, z, conv_w, alog_row, dtb_row, bseg, sel)


def _tri_solve(n, x, reverse):
    h = n.shape[0]
    bs = 8
    nblk = CHUNK // bs
    r = lax.broadcasted_iota(jnp.int32, (CHUNK, CHUNK), 0)
    c = lax.broadcasted_iota(jnp.int32, (CHUNK, CHUNK), 1)
    off = jnp.where((r // bs != c // bs)[None], n, 0.0)
    diag = jnp.stack([n[:, bs * b:bs * (b + 1), bs * b:bs * (b + 1)] for b in range(nblk)], axis=1)

    ws = [a.reshape(h, nblk, bs, a.shape[-1]) for a in (x, off)]
    for j in (range(bs - 1, 0, -1) if reverse else range(bs - 1)):
        col = jnp.broadcast_to(diag[..., j:j + 1], (h, nblk, bs, 2 * HEAD_W))
        ws = [w - col[..., :w.shape[-1]] * w[:, :, j:j + 1, :] for w in ws]
    z, m = [w.reshape(h, CHUNK, w.shape[-1]) for w in ws]

    bmm = lambda a, b: jnp.einsum("hij,hjk->hik", a.astype(BF16), b.astype(BF16), preferred_element_type=F32)
    m2 = bmm(m, m)
    w1 = z - bmm(m, z)
    w2 = w1 + bmm(m2, w1)
    return w2 + bmm(bmm(m2, m2), w2)


def _gdn_chunk_terms(qk_ref, kk_ref, vk_ref, gc_ref, bt_ref, gt_ref, d, reverse):
    c = CHUNK
    nh = N_HEADS
    nc = TS // c
    r = lax.broadcasted_iota(jnp.int32, (c, c), 0)
    cc = lax.broadcasted_iota(jnp.int32, (c, c), 1)
    incl = ((r <= cc) if reverse else (r >= cc))[None]
    strict = ((r < cc) if reverse else (r > cc))[None]
    lo = lax.broadcasted_iota(jnp.int32, (1, 1, 2 * HEAD_W), 2) < HEAD_W
    bmm_nt = lambda a, b: jnp.einsum("hid,hjd->hij", a, b, preferred_element_type=F32)
    chunks = lambda a: a.reshape(nh * nc, c, a.shape[-1])
    qk = chunks(qk_ref[...])
    kk = chunks(kk_ref[...])
    vk = chunks(vk_ref[...])
    gc = chunks(jnp.stack([gc_ref[:, 128 * h:128 * (h + 1)] for h in range(nh)]))
    bt = chunks(jnp.stack([bt_ref[:, 128 * h:128 * (h + 1)] for h in range(nh)]))
    grow = jnp.stack([gt_ref[d * nh + h:d * nh + h + 1, ci * c:(ci + 1) * c]
                      for h in range(nh) for ci in range(nc)])
    dec = jnp.where(incl, jnp.exp(jnp.minimum(gc[:, :, :c] - grow, 0.0)), 0.0)
    kkb = kk.astype(BF16)
    k_dot_k = 0.5 * bmm_nt(kkb, kkb)
    q_only = jnp.where(lo, qk, 0.0)
    q_dot_k = bmm_nt(q_only.astype(BF16), kkb) * dec
    n = jnp.where(strict, bt[:, :, :c] * k_dot_k * dec, 0.0)
    egc = jnp.exp(gc)
    x = _tri_solve(n, vk * jnp.where(lo, bt, bt * egc), reverse)
    edge = 0 if reverse else c - 1
    glast = gc[:, edge:edge + 1, :]
    ktail = kk * jnp.exp(glast - gc)
    return (jnp.where(lo, 0.0, x).astype(BF16), x[:, :, :HEAD_W], (q_only * egc).astype(BF16), q_dot_k.astype(BF16),
            ktail.astype(BF16), jnp.exp(glast)[:, :, :HEAD_W])


def _gdn_scan_kernel(qkf, qkb, kkf, kkb, vkf, vkb, gcf, gcb, btf, btb, gtf, gtb, of_ref, ob_ref, s_ref):
    @pl.when(pl.program_id(1) == 0)
    def _():
        s_ref[...] = jnp.zeros(s_ref.shape, F32)

    n_chunks = TS // CHUNK
    dirs = ((qkf, kkf, vkf, gcf, btf, gtf, of_ref, False), (qkb, kkb, vkb, gcb, btb, gtb, ob_ref, True))
    order = [range(n_chunks), range(n_chunks - 1, -1, -1)]
    terms = [_gdn_chunk_terms(*dirs[d][:6], d, dirs[d][7]) for d in range(2)]
    pick = lambda a, ci: a.reshape((N_HEADS, n_chunks) + a.shape[1:])[:, ci]
    bmm = lambda a, b: jnp.einsum("hij,hjk->hik", a, b, preferred_element_type=F32)
    bmm_tn = lambda a, b: jnp.einsum("hcd,hce->hde", a, b, preferred_element_type=F32)
    state = [s_ref[0], s_ref[1]]
    for step in range(n_chunks):
        for d in range(2):
            ci = order[d][step]
            k_cum, u, q_dec, q_dot_k, ktail, total = [pick(a, ci) for a in terms[d]]
            s2b = state[d].astype(BF16)
            vb = (u - bmm(k_cum, s2b)).astype(BF16)
            dirs[d][6][:, ci * CHUNK:(ci + 1) * CHUNK, :] = bmm(q_dec, s2b) + bmm(q_dot_k, vb)
            state[d] = state[d] * total + bmm_tn(ktail, vb)
    s_ref[0] = state[0]
    s_ref[1] = state[1]


def _gdn_scan(qk, kk, vk, gcx, btx, gt, nb, s):
    h, nt_rows, _ = qk.shape
    tps = s // TS
    nlt = nb * tps
    fwd = lambda b, n: jnp.where(n == 0, nlt + b, b * tps + n - 1)
    bwd = lambda b, n: jnp.where(n == 0, nlt + b, b * tps + tps - n)
    hw = h * 128

    def views(shape, imap):
        return [pl.BlockSpec(shape, functools.partial(imap, t)) for t in (fwd, bwd)]

    heads = views((h, TS, 128), lambda t, b, n: (0, t(b, n), 0))
    lanes = [pl.BlockSpec((TS, hw), lambda b, n: (fwd(b, n), 0)), pl.BlockSpec((TS, hw), lambda b, n: (bwd(b, n), 1))]
    rows = views((None, 8, TS), lambda t, b, n: (t(b, n), 0, 0))
    outs = views((h, TS, HEAD_W), lambda t, b, n: (0, t(b, n), 0))
    return pl.pallas_call(
        _gdn_scan_kernel,
        grid=(nb, tps + 1),
        in_specs=heads * 3 + lanes * 2 + rows,
        out_specs=outs,
        out_shape=[jax.ShapeDtypeStruct((h, nt_rows, HEAD_W), F32)] * 2,
        scratch_shapes=[pltpu.VMEM((2, h, 2 * HEAD_W, HEAD_W), F32)],
        compiler_params=_cparams(("parallel", "arbitrary")),
        name="gdn_scan",
    )(qk, qk, kk, kk, vk, vk, gcx, gcx, btx, btx, gt, gt)


def _gdn_finish_kernel(of_ref, ob_ref, z_ref, nw_ref, y_ref):
    o = of_ref[...] + ob_ref[...]
    y = o * lax.rsqrt(jnp.mean(o * o, axis=-1, keepdims=True) + EPS) * nw_ref[...]
    y = jnp.concatenate([y[h] for h in range(N_HEADS)], axis=-1)
    y_ref[...] = (y * _silu(z_ref[...].astype(F32))).astype(BF16)


def _gdn_finish(of, ob, z, nw_row):
    nt = of.shape[1] // TM
    hsp = pl.BlockSpec((N_HEADS, TM, HEAD_W), lambda i: (0, i, 0))
    return pl.pallas_call(
        _gdn_finish_kernel,
        grid=(nt,),
        in_specs=[hsp, hsp, pl.BlockSpec((TM, 256), lambda i: (i, C_Z // 256)),
                  pl.BlockSpec(nw_row.shape, lambda i: (0, 0))],
        out_specs=pl.BlockSpec((TM, 256), lambda i: (i, 0)),
        out_shape=jax.ShapeDtypeStruct((of.shape[1], 256), BF16),
        compiler_params=_cparams(("parallel",)),
        name="gdn_finish",
    )(of, ob, z, nw_row)


S5_SUB = 8
S5_NS = S5_GROUPS * S5_STATE


def _s5_param_kernel(are_ref, aim_ref, ldt_ref, bre_ref, bim_ref, cre_ref, cim_ref,
                     bb_ref, kst_ref, cbd_ref, tab_ref):
    d = pl.program_id(0)
    lam_re = jnp.minimum(are_ref[...], -1e-4)
    lam_im = aim_ref[...]
    dt = jnp.exp(ldt_ref[...])

    def power(tau):
        mag = jnp.exp(lam_re * dt * tau)
        ang = lam_im * dt * tau
        return mag * jnp.cos(ang), mag * jnp.sin(ang)

    idx = lax.broadcasted_iota(jnp.int32, (S5_SUB, 1), 0)
    p_re, p_im = power(idx.astype(F32))
    lb_re, lb_im = p_re[1:2], p_im[1:2]
    den = lam_re * lam_re + lam_im * lam_im
    f_re = ((lb_re - 1.0) * lam_re + lb_im * lam_im) / den
    f_im = (lb_im * lam_re - (lb_re - 1.0) * lam_im) / den
    bb_re = f_re * bre_ref[...] - f_im * bim_ref[...]
    bb_im = f_re * bim_ref[...] + f_im * bre_ref[...]
    bb_ref[...] = jnp.concatenate([bb_re, bb_im], axis=-1).astype(BF16)
    c_re = cre_ref[...]
    c_im = cim_ref[...]
    cbd_ref[...] = jnp.concatenate([c_re, -c_im], axis=0).astype(BF16)
    for tau in range(S5_SUB):
        g_re = bb_re * p_re[tau:tau + 1] - bb_im * p_im[tau:tau + 1]
        g_im = bb_re * p_im[tau:tau + 1] + bb_im * p_re[tau:tau + 1]
        kst_ref[tau] = (_dot(g_re, c_re, precision=HI) - _dot(g_im, c_im, precision=HI)).astype(BF16)
    fwd = d == 0
    t_in = jnp.where(fwd, S5_SUB - 1 - idx, idx).astype(F32)
    t_out = jnp.where(fwd, idx + 1, S5_SUB - idx).astype(F32)
    for k, tau in enumerate((t_in, t_out, jnp.full((S5_SUB, 1), float(S5_SUB), F32))):
        tab_ref[k] = jnp.concatenate(power(tau), axis=-1)


def _s5_params(a_re, a_im, log_dt, b_re, b_im, c_re, c_im):
    g, p, gch = S5_GROUPS, S5_STATE, S5_GROUP_CH
    ns, c = S5_NS, g * gch
    eye = jnp.eye(g, dtype=F32)
    row = lambda a: a.astype(F32).reshape(2, 1, ns)
    ldt = jnp.broadcast_to(log_dt.astype(F32)[:, :, None], (2, g, p))
    b_bd = lambda b: jnp.einsum("gpc,gh->gchp", b.astype(F32), eye).reshape(c, ns)
    c_bd = lambda cc: jnp.einsum("dgcp,gh->dgphc", cc.astype(F32), eye).reshape(2, ns, c)
    per_dir = lambda *shape: pl.BlockSpec((None,) + shape, lambda d: (d,) + (0,) * len(shape))
    shared = pl.BlockSpec((c, ns), lambda d: (0, 0))
    return pl.pallas_call(
        _s5_param_kernel,
        grid=(2,),
        in_specs=[per_dir(1, ns)] * 3 + [shared] * 2 + [per_dir(ns, c)] * 2,
        out_specs=[per_dir(c, 2 * ns), per_dir(S5_SUB, c, c), per_dir(2 * ns, c), per_dir(3, S5_SUB, 2 * ns)],
        out_shape=[jax.ShapeDtypeStruct((2, c, 2 * ns), BF16),
                   jax.ShapeDtypeStruct((2, S5_SUB, c, c), BF16),
                   jax.ShapeDtypeStruct((2, 2 * ns, c), BF16),
                   jax.ShapeDtypeStruct((2, 3, S5_SUB, 2 * ns), F32)],
        compiler_params=_cparams(("parallel",), 48),
        name="s5_params",
    )(row(a_re), row(a_im), row(ldt), b_bd(b_re), b_bd(b_im), c_bd(c_re), c_bd(c_im))


def _s5_direction(u_ref, bb_ref, kst_ref, cbd_ref, tab_ref, y_ref, x_ref, xin_ref, xpv_ref, reverse):
    ns, sub = S5_NS, S5_SUB
    nsc = TS // sub
    u = u_ref[...]
    z = _dot(u, bb_ref[...])
    z_re = z[:, :ns].reshape(nsc, sub, ns)
    z_im = z[:, ns:].reshape(nsc, sub, ns)
    w = tab_ref[0]
    w_re, w_im = w[:, :ns][None], w[:, ns:][None]
    group_sum = lambda a: jnp.broadcast_to(jnp.sum(a, axis=1, keepdims=True), a.shape)
    xin_ref[0] = group_sum(w_re * z_re - w_im * z_im)
    xin_ref[1] = group_sum(w_re * z_im + w_im * z_re)
    a = tab_ref[2]
    a_re, a_im = a[:, :ns], a[:, ns:]

    def step(k, carry):
        x_re, x_im = carry
        n = nsc - 1 - k if reverse else k
        xpv_ref[0, n] = x_re
        xpv_ref[1, n] = x_im
        return (a_re * x_re - a_im * x_im + xin_ref[0, n], a_re * x_im + a_im * x_re + xin_ref[1, n])

    x_re, x_im = lax.fori_loop(0, nsc, step, (x_ref[0], x_ref[1]))
    x_ref[0] = x_re
    x_ref[1] = x_im
    o = tab_ref[1]
    o_re, o_im = o[:, :ns][None], o[:, ns:][None]
    p_re = (xpv_ref[0] * o_re - xpv_ref[1] * o_im).reshape(TS, ns)
    p_im = (xpv_ref[0] * o_im + xpv_ref[1] * o_re).reshape(TS, ns)
    y = _dot(jnp.concatenate([p_re, p_im], axis=-1).astype(BF16), cbd_ref[...])
    uf = u.astype(F32)
    pos = lax.broadcasted_iota(jnp.int32, (TS, 1), 0) % sub
    for tau in range(sub):
        if tau == 0:
            shifted = u
        else:
            rolled = pltpu.roll(uf, TS - tau if reverse else tau, axis=0)
            inside = (pos + tau < sub) if reverse else (pos >= tau)
            shifted = jnp.where(inside, rolled, 0.0).astype(BF16)
        y = y + _dot(shifted, kst_ref[tau])
    y_ref[...] = y


def _s5_scan_kernel(uf_ref, ub_ref, bbf, bbb, kstf, kstb, cbdf, cbdb, tabf, tabb, yf_ref, yb_ref,
                    x_ref, xin_ref, xpv_ref):
    @pl.when(pl.program_id(1) == 0)
    def _():
        x_ref[...] = jnp.zeros(x_ref.shape, F32)

    _s5_direction(uf_ref, bbf, kstf, cbdf, tabf, yf_ref, x_ref.at[0], xin_ref.at[0], xpv_ref.at[0], False)
    _s5_direction(ub_ref, bbb, kstb, cbdb, tabb, yb_ref, x_ref.at[1], xin_ref.at[1], xpv_ref.at[1], True)


def _s5_scan(z, bb, kst, cbd, tab, nb, s):
    nt_rows = z.shape[0]
    tps = s // TS
    nlt = nb * tps
    fwd = lambda b, n: jnp.where(n == 0, nlt + b, b * tps + n - 1)
    bwd = lambda b, n: jnp.where(n == 0, nlt + b, b * tps + tps - n)
    ns, c, nsc = S5_NS, S5_GROUPS * S5_GROUP_CH, TS // S5_SUB

    def both(a):
        return [pl.BlockSpec((None,) + a.shape[1:], lambda b, n, d=d: (d,) + (0,) * (a.ndim - 1)) for d in range(2)]

    state = lambda *lead: pltpu.VMEM((2, 2) + lead + (S5_SUB, ns), F32)
    return pl.pallas_call(
        _s5_scan_kernel,
        grid=(nb, tps + 1),
        in_specs=[pl.BlockSpec((TS, c), lambda b, n: (fwd(b, n), C_U // 256)),
                  pl.BlockSpec((TS, c), lambda b, n: (bwd(b, n), C_U // 256))]
                 + both(bb) + both(kst) + both(cbd) + both(tab),
        out_specs=[pl.BlockSpec((TS, c), lambda b, n: (fwd(b, n), 0)),
                   pl.BlockSpec((TS, c), lambda b, n: (bwd(b, n), 0))],
        out_shape=[jax.ShapeDtypeStruct((nt_rows, c), F32)] * 2,
        scratch_shapes=[state(), state(nsc), state(nsc)],
        compiler_params=_cparams(("parallel", "arbitrary"), 56),
        name="s5_scan",
    )(z, z, bb, bb, kst, kst, cbd, cbd, tab, tab)


def _s5_finish_kernel(yf_ref, yb_ref, u_ref, d_ref, w_ref, b_ref, o_ref):
    y = yf_ref[...] + yb_ref[...] + d_ref[...] * u_ref[...].astype(F32)
    y = jax.nn.gelu(y)
    gate = _sigmoid(_dot(y.astype(BF16), w_ref[...]) + b_ref[...])
    o_ref[...] = (y * gate).astype(BF16)


def _s5_finish(yf, yb, z, d_row, glu_w, glu_b_row):
    nt = yf.shape[0] // TM
    full = lambda a: pl.BlockSpec(a.shape, lambda i: (0,) * a.ndim)
    return pl.pallas_call(
        _s5_finish_kernel,
        grid=(nt,),
        in_specs=[pl.BlockSpec((TM, 256), lambda i: (i, 0)),
                  pl.BlockSpec((TM, 256), lambda i: (i, 0)),
                  pl.BlockSpec((TM, 256), lambda i: (i, C_U // 256)),
                  full(d_row), full(glu_w), full(glu_b_row)],
        out_specs=pl.BlockSpec((TM, 256), lambda i: (i, 0)),
        out_shape=jax.ShapeDtypeStruct((yf.shape[0], 256), BF16),
        compiler_params=_cparams(("parallel",)),
        name="s5_finish",
    )(yf, yb, z, d_row, glu_w, glu_b_row)


def _merge_kernel(x_ref, y0_ref, y1_ref, y2_ref, y3_ref, gates_ref, wb_ref, wo_ref, g1_ref, gain_ref, o_ref):
    acc = jnp.zeros((TM, D_MODEL), F32)
    for bi, y_ref in enumerate((y0_ref, y1_ref, y2_ref, y3_ref)):
        proj = _dot(y_ref[...], wb_ref[bi])
        acc = acc + gates_ref[:, bi * D_MODEL:(bi + 1) * D_MODEL].astype(F32) * proj
    y = _dot(acc.astype(BF16), wo_ref[...])
    o_ref[...] = x_ref[...] + g1_ref[...] * _rms(y, gain_ref[...])


def _merge(x, ys, gates, wb, wo, g1, gain, layer, tps, nb):
    nt = x.shape[0] // TM
    row = lambda i: (jnp.minimum(i // tps, nb), 0, 0)
    tile = lambda w: pl.BlockSpec((TM, w), lambda i: (i, 0))
    return pl.pallas_call(
        _merge_kernel,
        grid=(nt,),
        in_specs=[tile(D_MODEL)] + [tile(BRANCH_W)] * 4 + [tile(4 * D_MODEL),
                  pl.BlockSpec((None, 4, BRANCH_W, D_MODEL), lambda i: (layer, 0, 0, 0)),
                  pl.BlockSpec((None, D_MODEL, D_MODEL), lambda i: (layer, 0, 0)),
                  pl.BlockSpec((None, 1, D_MODEL), row),
                  pl.BlockSpec((1, D_MODEL), lambda i: (0, 0))],
        out_specs=tile(D_MODEL),
        out_shape=jax.ShapeDtypeStruct(x.shape, F32),
        compiler_params=_cparams(("parallel",), 48),
        name="merge",
    )(x, *ys, gates, wb, wo, g1, gain)


def _mlp_kernel(x_ref, sc_ref, sh_ref, g2_ref, gin_ref, gout_ref, w1_ref, w2_ref, o_ref, h_ref, acc_ref):
    j = pl.program_id(1)

    @pl.when(j == 0)
    def _():
        h_ref[...] = _norm_mod(x_ref[...], gin_ref[...], sc_ref[...], sh_ref[...]).astype(BF16)
        acc_ref[...] = jnp.zeros(acc_ref.shape, F32)

    t = jnp.maximum(_dot(h_ref[...], w1_ref[...]), 0.0)
    acc_ref[...] += _dot((t * t).astype(BF16), w2_ref[...])

    @pl.when(j == pl.num_programs(1) - 1)
    def _():
        o_ref[...] = x_ref[...] + g2_ref[...] * _rms(acc_ref[...], gout_ref[...])


def _mlp(x, sc, sh, g2, gin, gout, w1, w2, layer, tps, nb, tf=2048):
    nt = x.shape[0] // TM
    row = lambda i, j: (jnp.minimum(i // tps, nb), 0, 0)
    vec = pl.BlockSpec((1, D_MODEL), lambda i, j: (0, 0))
    return pl.pallas_call(
        _mlp_kernel,
        grid=(nt, D_FF // tf),
        in_specs=[pl.BlockSpec((TM, D_MODEL), lambda i, j: (i, 0)),
                  pl.BlockSpec((None, 1, D_MODEL), row), pl.BlockSpec((None, 1, D_MODEL), row),
                  pl.BlockSpec((None, 1, D_MODEL), row), vec, vec,
                  pl.BlockSpec((None, D_MODEL, tf), lambda i, j: (layer, 0, j)),
                  pl.BlockSpec((None, tf, D_MODEL), lambda i, j: (layer, j, 0))],
        out_specs=pl.BlockSpec((TM, D_MODEL), lambda i, j: (i, 0)),
        out_shape=jax.ShapeDtypeStruct(x.shape, F32),
        scratch_shapes=[pltpu.VMEM((TM, D_MODEL), BF16), pltpu.VMEM((TM, D_MODEL), F32)],
        compiler_params=_cparams(("parallel", "arbitrary"), 48),
        name="mlp",
    )(x, sc, sh, g2, gin, gout, w1, w2)


def _pack_w_in(w_in):
    o_na, o_cq, o_ckv, o_gdn, o_z, o_a, o_b, o_u, o_gate = 0, 768, 1024, 1184, 1952, 2208, 2216, 2224, 2480
    cols = lambda lo, n: w_in[:, :, lo:lo + n]
    o_kr = o_ckv + MLA_KV_LORA
    swapped = [cols(o_kr + 8, 8), cols(o_kr, 8), cols(o_kr + 24, 8), cols(o_kr + 16, 8)]
    small = jnp.concatenate(
        [cols(o_gdn, 768), cols(o_na, 768), cols(o_cq, 256),
         cols(o_ckv, 160), cols(o_a, 8), cols(o_b, 8)] + swapped
        + [jnp.zeros(w_in.shape[:2] + (256 - L_KRS - MLA_ROPE,), w_in.dtype), cols(o_z, 256), cols(o_u, 256)], axis=2)
    assert small.shape[2] == ZW
    return small.astype(BF16), w_in[:, :, o_gate:].astype(BF16)


def _mla_weights(w_uq, w_ukv):
    depth = w_uq.shape[0]
    hq = MLA_NOPE + MLA_ROPE
    wq = w_uq.reshape(depth, -1, N_HEADS, hq)
    pad = lambda a, lo, hi: jnp.pad(a, ((0, 0), (0, 0), (0, 0), (lo, hi)))
    wq_ext = pad(wq, 0, 128 - hq).reshape(depth, -1, N_HEADS * 128)
    wq_sw = pad(wq[..., MLA_NOPE:][..., ROPE_SWAP], MLA_NOPE, 128 - hq).reshape(depth, -1, N_HEADS * 128)
    wkv = w_ukv.reshape(depth, -1, N_HEADS, MLA_NOPE + HEAD_W)
    wk = pad(wkv[..., :MLA_NOPE], 0, 128 - MLA_NOPE).reshape(depth, -1, N_HEADS * 128)
    wv = pad(wkv[..., MLA_NOPE:], 0, 128 - HEAD_W).reshape(depth, -1, N_HEADS * 128)
    return [a.astype(BF16) for a in (wq_ext, wq_sw, wk, wv, jnp.swapaxes(wv, 1, 2))]


def _rope_place_mats():
    p1 = np.zeros((256, N_HEADS * 128), np.float32)
    p2 = np.zeros((256, N_HEADS * 128), np.float32)
    for h in range(N_HEADS):
        for r in range(MLA_ROPE):
            p1[L_KR + r, h * 128 + MLA_NOPE + r] = 1.0
            p2[L_KRS + r, h * 128 + MLA_NOPE + r] = 1.0
    return jnp.asarray(p1, BF16), jnp.asarray(p2, BF16)


def _rope_tables(s):
    quarter = MLA_ROPE // 4
    inv_freq = ROPE_BASE ** (-jnp.arange(quarter, dtype=F32) / quarter)
    t = jnp.arange(s)
    ang_r = (t // GRID_W).astype(F32)[:, None] * inv_freq[None, :]
    ang_c = (t % GRID_W).astype(F32)[:, None] * inv_freq[None, :]
    cr, sr, cc, sn = jnp.cos(ang_r), jnp.sin(ang_r), jnp.cos(ang_c), jnp.sin(ang_c)
    cos = jnp.concatenate([jnp.ones((s, MLA_NOPE), F32), cr, cr, cc, cc, jnp.ones((s, 32), F32)], axis=1)
    sin = jnp.concatenate([jnp.zeros((s, MLA_NOPE), F32), -sr, sr, -sn, sn, jnp.zeros((s, 32), F32)], axis=1)
    cos = jnp.concatenate([cos, jnp.ones((TM, 128), F32)], axis=0)
    sin = jnp.concatenate([sin, jnp.zeros((TM, 128), F32)], axis=0)
    return cos, sin


def _head_block_ones():
    r = np.arange(BRANCH_W)
    return jnp.asarray((r[:, None] // HEAD_W == r[None, :] // HEAD_W).astype(np.float32))


def _lane_row(vals, offset, width=256):
    return jnp.zeros((1, width), F32).at[0, offset:offset + vals.shape[0]].set(vals.astype(F32))


def _gdn_mixer(z, conv_w, a_log, dt_bias, norm_w, consts, nb, s, ctx_len):
    alog_row = _lane_row(a_log.reshape(-1), L_A)
    dtb_row = _lane_row(dt_bias.reshape(-1), L_A)
    qk, kk, vk, gcx, btx, gt = _gdn_prep(z, conv_w, alog_row, dtb_row, consts["bseg"], consts["sel"],
                                         s // TS, nb * s // TS)
    of, ob = _gdn_scan(qk, kk, vk, gcx, btx, gt, nb, s)
    return _gdn_finish(of, ob, z, norm_w.astype(F32)[None, :])


def _s5_mixer(z, a_re, a_im, log_dt, b_re, b_im, c_re, c_im, d_skip, glu_w, glu_b, nb, s, ctx_len):
    bb, kst, cbd, tab = _s5_params(a_re, a_im, log_dt, b_re, b_im, c_re, c_im)
    yf, yb = _s5_scan(z, bb, kst, cbd, tab, nb, s)
    return _s5_finish(yf, yb, z, d_skip.astype(F32)[None, :], glu_w.astype(BF16), glu_b.astype(F32)[None, :])


def kernel(x, c, ctx, c_ctx, ada_w, ada_b, norm_gains, w_in, na_rpb, mla_q_norm, mla_kv_norm, mla_w_uq, mla_w_ukv, gdn_conv, gdn_a_log, gdn_dt_bias, gdn_norm, s5_a_re, s5_a_im, s5_log_dt, s5_b_re, s5_b_im, s5_c_re, s5_c_im, s5_d, s5_glu_w, s5_glu_b, w_branch, w_out, mlp_w1, mlp_w2):
    nb, s, d = x.shape
    ctx_len = ctx.shape[1]
    depth = ada_w.shape[0]
    assert d == D_MODEL and ctx_len == TS and nb * ctx_len == TM and s % TM == 0 and s // GRID_W >= 16
    nl = nb * s
    tps = s // TM

    xs = jnp.concatenate([x.reshape(nl, d), ctx.reshape(nb * ctx_len, d)], axis=0)
    cvec = jnp.zeros((8, d), F32).at[:nb].set(c).at[nb].set(c_ctx)
    mod = _modulation(cvec, ada_w, ada_b)
    mod = mod[:, :nb + 1].reshape(depth, nb + 1, 6, 1, d)

    w_small, w_gates = _pack_w_in(w_in)
    wq_ext, wq_sw, wk_ext, wv, wvt = _mla_weights(mla_w_uq, mla_w_ukv)
    p1, p2 = _rope_place_mats()
    cos_t, sin_t = _rope_tables(s)
    sel = np.zeros((8, 128), np.float32)
    sel[np.arange(8), L_A - 128 + np.arange(8)] = 1.0
    consts = {"bseg": _head_block_ones().astype(BF16), "sel": jnp.asarray(sel)}
    wb = w_branch.astype(BF16)
    wo = w_out.astype(BF16)
    w1 = mlp_w1.astype(BF16)
    w2 = mlp_w2.astype(BF16)
    gains = norm_gains.astype(F32)

    for l in range(depth):
        sh1, sc1, g1, sh2, sc2, g2 = [mod[l, :, i] for i in range(6)]
        z = _inproj(xs, sc1, sh1, gains[l, 0][None], w_small, l, tps, nb, False)
        gates = _inproj(xs, sc1, sh1, gains[l, 0][None], w_gates, l, tps, nb, True)

        y_na = jnp.concatenate([
            _na_latent(z, na_rpb[l], nb, s),
            _ctx_attention(z, z, z, C_NA // 256, C_NA // 256 + 1, C_NA // 256 + 2, 256, 256, nl // TS, nb,
                           HEAD_W ** -0.5, False, "na_ctx")], axis=0)

        qm, km, vm, vt = _mla_prep(z, cos_t, sin_t, mla_q_norm[l].astype(F32)[None],
                                   mla_kv_norm[l].astype(F32)[None], wq_ext[l], wq_sw[l], wk_ext[l], wv[l], wvt[l],
                                   p1, p2, tps, nl // TM)
        y_mla = jnp.concatenate([
            _mla_latent(qm, km, vt, nb, s),
            _ctx_attention(qm, km, vm, 0, 0, 0, N_HEADS * 128, N_HEADS * 128, nl // TS, nb, 1.0, True,
                           "mla_ctx")], axis=0)

        y_gdn = _gdn_mixer(z, gdn_conv[l].astype(F32), gdn_a_log[l], gdn_dt_bias[l], gdn_norm[l], consts,
                           nb, s, ctx_len)
        y_s5 = _s5_mixer(z, s5_a_re[l], s5_a_im[l], s5_log_dt[l], s5_b_re[l], s5_b_im[l], s5_c_re[l], s5_c_im[l],
                         s5_d[l], s5_glu_w[l], s5_glu_b[l], nb, s, ctx_len)

        xs = _merge(xs, (y_na, y_mla, y_gdn, y_s5), gates, wb, wo, g1, gains[l, 1][None], l, tps, nb)
        xs = _mlp(xs, sc2, sh2, g2, gains[l, 2][None], gains[l, 3][None], w1, w2, l, tps, nb)
    return xs[:nl].reshape(nb, s, d)
```

```python
import functools
import math

import numpy as np
import jax
import jax.numpy as jnp
from jax import lax
from jax.experimental import pallas as pl
from jax.experimental.pallas import tpu as pltpu

F32 = jnp.float32
BF16 = jnp.bfloat16
HI = lax.Precision.HIGHEST
EPS = 1e-6

D_MODEL = 1024
GRID_W = 64
NA_WIN_H = 8
NA_WIN_W = 16
N_HEADS = 4
HEAD_W = 64
BRANCH_W = 256
MLA_NOPE = 64
MLA_ROPE = 32
MLA_KV_LORA = 128
ROPE_BASE = 10000.0
GDN_CONV = 4
CHUNK = 64
S5_GROUPS = 16
S5_GROUP_CH = 16
S5_STATE = 64
D_FF = 4 * D_MODEL

TM = 1024
TN = 512
TS = 256
NEG = -1e30

C_GDN = 0
C_NA = 768
C_CQ = 1536
C_CKV = 1792
C_Z = 2048
C_U = 2304
ZW = 2560
L_KR = 128
L_A = 160
L_B = 168
L_KRS = 176
ROPE_SWAP = np.concatenate([np.arange(8, 16), np.arange(0, 8), np.arange(24, 32), np.arange(16, 24)])


def _cparams(sem, vmem_mb=None):
    kw = dict(dimension_semantics=sem)
    if vmem_mb is not None:
        kw["vmem_limit_bytes"] = vmem_mb * 1024 * 1024
    return pltpu.CompilerParams(**kw)


def _dot(a, b, **kw):
    return jnp.dot(a, b, preferred_element_type=F32, **kw)


def _dot_nt(a, b):
    return lax.dot_general(a, b, (((1,), (1,)), ((), ())), preferred_element_type=F32)


def _sigmoid(x):
    return 1.0 / (1.0 + jnp.exp(-x))


def _silu(x):
    return x * _sigmoid(x)


def _mod_kernel(c_ref, w_ref, b_ref, o_ref):
    c = c_ref[...]
    o_ref[...] = _dot(_silu(c), w_ref[...], precision=HI) + b_ref[...]


def _modulation(cvec, ada_w, ada_b):
    depth, d, n = ada_w.shape
    tn = 1536
    return pl.pallas_call(
        _mod_kernel,
        grid=(depth, n // tn),
        in_specs=[pl.BlockSpec((8, d), lambda l, j: (0, 0)),
                  pl.BlockSpec((None, d, tn), lambda l, j: (l, 0, j)),
                  pl.BlockSpec((None, 1, tn), lambda l, j: (l, 0, j))],
        out_specs=pl.BlockSpec((None, 8, tn), lambda l, j: (l, 0, j)),
        out_shape=jax.ShapeDtypeStruct((depth, 8, n), F32),
        compiler_params=_cparams(("parallel", "parallel"), 40),
        name="modulation",
    )(cvec, ada_w, ada_b.reshape(depth, 1, n))


def _norm_mod(x, gain, sc, sh):
    r = lax.rsqrt(jnp.mean(x * x, axis=-1, keepdims=True) + EPS)
    return (x * r * gain) * (1.0 + sc) + sh


def _inproj_kernel(x_ref, sc_ref, sh_ref, gain_ref, w_ref, o_ref, h_ref, *, gate):
    @pl.when(pl.program_id(1) == 0)
    def _():
        h_ref[...] = _norm_mod(x_ref[...], gain_ref[...], sc_ref[...], sh_ref[...]).astype(BF16)

    acc = _dot(h_ref[...], w_ref[...])
    if gate:
        acc = _sigmoid(acc)
    o_ref[...] = acc.astype(BF16)


def _inproj(x, sc, sh, gain, w, layer, tps, nb, gate):
    nt = pl.cdiv(x.shape[0], TM)
    width = w.shape[-1]
    tn = width // 2
    row = lambda i, j: (jnp.minimum(i // tps, nb), 0, 0)
    return pl.pallas_call(
        functools.partial(_inproj_kernel, gate=gate),
        grid=(nt, width // tn),
        in_specs=[pl.BlockSpec((TM, D_MODEL), lambda i, j: (i, 0)),
                  pl.BlockSpec((None, 1, D_MODEL), row),
                  pl.BlockSpec((None, 1, D_MODEL), row),
                  pl.BlockSpec((1, D_MODEL), lambda i, j: (0, 0)),
                  pl.BlockSpec((None, D_MODEL, tn), lambda i, j: (layer, 0, j))],
        out_specs=pl.BlockSpec((TM, tn), lambda i, j: (i, j)),
        out_shape=jax.ShapeDtypeStruct((x.shape[0], width), BF16),
        scratch_shapes=[pltpu.VMEM((TM, D_MODEL), BF16)],
        compiler_params=_cparams(("parallel", "arbitrary"), 48),
        name="inproj_gates" if gate else "inproj",
    )(x, sc, sh, gain, w)


def _head_lane_mask(width, head_w, h):
    lane = lax.broadcasted_iota(jnp.int32, (1, width), 1)
    return (lane >= h * head_w) & (lane < (h + 1) * head_w)


def _na_build_bias(rpb_ref, bias_ref, r0, kb0, rows_total):
    w = GRID_W
    qc = lax.broadcasted_iota(jnp.int32, (w, 2 * w), 0)
    lane = lax.broadcasted_iota(jnp.int32, (w, 2 * w), 1)
    kc = lane % w
    cs = jnp.clip(qc - NA_WIN_W // 2, 0, w - NA_WIN_W)
    col_ok = (kc >= cs) & (kc < cs + NA_WIN_W)
    left = lane < w
    neg = jnp.full((w, 2 * w), NEG, F32)
    for h in range(N_HEADS):
        t = rpb_ref[h]
        toep = []
        for a in range(2 * NA_WIN_H - 1):
            row = jnp.broadcast_to(t[a:a + 1, :], (w, 2 * w))
            ra = pltpu.roll(row, 2 * w - (NA_WIN_W - 1), axis=1, stride=1, stride_axis=0)
            rb = pltpu.roll(ra, w, axis=1)
            toep.append((jnp.where(col_ok, ra, NEG), jnp.where(col_ok, rb, NEG)))
        for qr in range(8):
            rs = min(max(r0 + qr - NA_WIN_H // 2, 0), rows_total - NA_WIN_H)
            for kp in range(8):
                halves = []
                for side in range(2):
                    kr = kb0 + 2 * kp + side
                    halves.append(toep[kr - (r0 + qr) + NA_WIN_H - 1][side] if rs <= kr < rs + NA_WIN_H else neg)
                bias_ref[h, qr * w:(qr + 1) * w, kp * 2 * w:(kp + 1) * 2 * w] = jnp.where(left, halves[0], halves[1])


def _na_kernel(q_ref, k_ref, v_ref, kc_ref, vc_ref, rpb_ref, o_ref, bias_ref, *, rows_total):
    i = pl.program_id(1)
    last = pl.num_programs(1) - 1

    @pl.when(i == 0)
    def _():
        _na_build_bias(rpb_ref, bias_ref, 0, 0, rows_total)

    @pl.when(i == 1)
    def _():
        _na_build_bias(rpb_ref, bias_ref, 8, 4, rows_total)

    @pl.when(i == last)
    def _():
        _na_build_bias(rpb_ref, bias_ref, rows_total - 8, rows_total - 16, rows_total)

    kb = jnp.clip(2 * i - 1, 0, rows_total // 4 - 4)
    start = pl.multiple_of(kb * (4 * GRID_W), 4 * GRID_W)
    nk = 2 * NA_WIN_H * GRID_W
    q = q_ref[...]
    kw = k_ref[pl.ds(start, nk), :]
    vw = v_ref[pl.ds(start, nk), :]
    kc = kc_ref[...]
    vc = vc_ref[...]
    scale = HEAD_W ** -0.5
    out = jnp.zeros(q.shape, F32)
    for h in range(N_HEADS):
        hm = _head_lane_mask(BRANCH_W, HEAD_W, h)
        qh = jnp.where(hm, q, jnp.zeros_like(q))
        sb = _dot_nt(qh, kw) * scale + bias_ref[h]
        sc = _dot_nt(qh, kc) * scale
        m = jnp.maximum(jnp.max(sb, axis=-1, keepdims=True), jnp.max(sc, axis=-1, keepdims=True))
        pb = jnp.exp(sb - m)
        pc = jnp.exp(sc - m)
        den = jnp.sum(pb, axis=-1, keepdims=True) + jnp.sum(pc, axis=-1, keepdims=True)
        o = _dot(pb.astype(BF16), vw) + _dot(pc.astype(BF16), vc)
        out = jnp.where(hm, o / den, out)
    o_ref[...] = out.astype(BF16)


def _na_latent(z, rpb, nb, s):
    rows_total = s // GRID_W
    qb = 8 * GRID_W
    nq = s // qb
    nl = nb * s
    rpb = jnp.pad(rpb.astype(F32), ((0, 0), (0, 1), (0, 2 * GRID_W - (2 * NA_WIN_W - 1))))
    return pl.pallas_call(
        functools.partial(_na_kernel, rows_total=rows_total),
        grid=(nb, nq),
        in_specs=[pl.BlockSpec((qb, BRANCH_W), lambda b, i: (b * nq + i, C_NA // 256)),
                  pl.BlockSpec((s, BRANCH_W), lambda b, i: (b, C_NA // 256 + 1)),
                  pl.BlockSpec((s, BRANCH_W), lambda b, i: (b, C_NA // 256 + 2)),
                  pl.BlockSpec((TS, BRANCH_W), lambda b, i: (nl // TS + b, C_NA // 256 + 1)),
                  pl.BlockSpec((TS, BRANCH_W), lambda b, i: (nl // TS + b, C_NA // 256 + 2)),
                  pl.BlockSpec(rpb.shape, lambda b, i: (0, 0, 0))],
        out_specs=pl.BlockSpec((qb, BRANCH_W), lambda b, i: (b * nq + i, 0)),
        out_shape=jax.ShapeDtypeStruct((nl, BRANCH_W), BF16),
        scratch_shapes=[pltpu.VMEM((N_HEADS, qb, 2 * qb), F32)],
        compiler_params=_cparams(("parallel", "arbitrary"), 56),
        name="na_latent",
    )(z, z, z, z, z, rpb)


def _ctx_attn_kernel(q_ref, k_ref, v_ref, o_ref, *, scale, base2):
    q = q_ref[...]
    k = k_ref[...]
    v = v_ref[...]
    qw = q.shape[-1]
    vw = v.shape[-1] // N_HEADS
    outs = []
    for h in range(N_HEADS):
        qh = jnp.where(_head_lane_mask(qw, qw // N_HEADS, h), q, jnp.zeros_like(q))
        s = _dot_nt(qh, k) * scale
        m = jnp.max(s, axis=-1, keepdims=True)
        p = jnp.exp2(s - m) if base2 else jnp.exp(s - m)
        den = jnp.sum(p, axis=-1, keepdims=True)
        o = _dot(p.astype(BF16), v)
        outs.append(o[:, h * vw:h * vw + HEAD_W] / den)
    o_ref[...] = jnp.concatenate(outs, axis=-1).astype(BF16)


def _ctx_attention(q, k, v, qcol, kcol, vcol, qw, vw, row0, nb, scale, base2, name):
    return pl.pallas_call(
        functools.partial(_ctx_attn_kernel, scale=scale, base2=base2),
        grid=(nb,),
        in_specs=[pl.BlockSpec((TS, qw), lambda b: (row0 + b, qcol)),
                  pl.BlockSpec((TS, qw), lambda b: (row0 + b, kcol)),
                  pl.BlockSpec((TS, vw), lambda b: (row0 + b, vcol))],
        out_specs=pl.BlockSpec((TS, BRANCH_W), lambda b: (b, 0)),
        out_shape=jax.ShapeDtypeStruct((nb * TS, BRANCH_W), BF16),
        compiler_params=_cparams(("parallel",)),
        name=name,
    )(q, k, v)


def _rms(x, gain):
    return x * lax.rsqrt(jnp.mean(x * x, axis=-1, keepdims=True) + EPS) * gain


def _mla_prep_kernel(cq_ref, ckv_ref, cos_ref, sin_ref, qn_ref, kvn_ref, wq_ref, wqs_ref, wk_ref, wv_ref, wvt_ref,
                     p1_ref, p2_ref, q_ref, k_ref, v_ref, vt_ref):
    cos = jnp.concatenate([cos_ref[...]] * N_HEADS, axis=-1)
    sin = jnp.concatenate([sin_ref[...]] * N_HEADS, axis=-1)
    cqn = _rms(cq_ref[...].astype(F32), qn_ref[...]).astype(BF16)
    scale = (MLA_NOPE + MLA_ROPE) ** -0.5 * math.log2(math.e)
    q = _dot(cqn, wq_ref[...]) * cos + _dot(cqn, wqs_ref[...]) * sin
    q_ref[...] = (q * scale).astype(BF16)
    ckv = ckv_ref[...]
    kvn = _rms(ckv[:, :MLA_KV_LORA].astype(F32), kvn_ref[...]).astype(BF16)
    k = (_dot(kvn, wk_ref[...]) + _dot(ckv, p1_ref[...])) * cos + _dot(ckv, p2_ref[...]) * sin
    k_ref[...] = k.astype(BF16)
    lane = lax.broadcasted_iota(jnp.int32, (1, N_HEADS * 128), 1)
    v_ref[...] = jnp.where(lane % 128 == HEAD_W, 1.0, _dot(kvn, wv_ref[...])).astype(BF16)
    row = lax.broadcasted_iota(jnp.int32, (N_HEADS * 128, 1), 0)
    vt_ref[...] = jnp.where(row % 128 == HEAD_W, 1.0, _dot_nt(wvt_ref[...], kvn)).astype(BF16)


def _mla_prep(z, cos_t, sin_t, qn, kvn, wq, wqs, wk, wv, wvt, p1, p2, tps, n_lat_tiles):
    nt_rows = z.shape[0]
    nt = pl.cdiv(nt_rows, TM)
    full = lambda a: pl.BlockSpec(a.shape, lambda i: (0,) * a.ndim)
    tab = lambda i: (jnp.where(i < n_lat_tiles, i % tps, tps), 0)
    hw = N_HEADS * 128
    return pl.pallas_call(
        _mla_prep_kernel,
        grid=(nt,),
        in_specs=[pl.BlockSpec((TM, 256), lambda i: (i, C_CQ // 256)),
                  pl.BlockSpec((TM, 256), lambda i: (i, C_CKV // 256)),
                  pl.BlockSpec((TM, 128), tab), pl.BlockSpec((TM, 128), tab),
                  full(qn), full(kvn), full(wq), full(wqs), full(wk), full(wv), full(wvt), full(p1), full(p2)],
        out_specs=[pl.BlockSpec((TM, hw), lambda i: (i, 0)),
                   pl.BlockSpec((TM, hw), lambda i: (i, 0)),
                   pl.BlockSpec((TM, hw), lambda i: (i, 0)),
                   pl.BlockSpec((hw, TM), lambda i: (0, i))],
        out_shape=[jax.ShapeDtypeStruct((nt_rows, hw), BF16)] * 3 + [jax.ShapeDtypeStruct((hw, nt_rows), BF16)],
        compiler_params=_cparams(("parallel",)),
        name="mla_prep",
    )(z, z, cos_t, sin_t, qn, kvn, wq, wqs, wk, wv, wvt, p1, p2)


def _flash_kernel(q_ref, kl_ref, vl_ref, kc_ref, vc_ref, o_ref, st_ref, *, tk, n_lat):
    tq = q_ref.shape[0]
    s_len = n_lat * tk
    heads = (slice(0, 128), slice(128, 256))
    qs = [q_ref[:, hs] for hs in heads]
    group_max = lambda st: jnp.max(st.reshape(st.shape[0] // 8, 8, tq), axis=0)

    def score_tile(hh, k, rows, mx):
        st = _dot_nt(k, qs[hh])
        st_ref[hh, rows, :] = st
        return jnp.maximum(mx, group_max(st))

    def max_body(t, carry):
        rows = pl.ds(pl.multiple_of(t * tk, tk), tk)
        return tuple(score_tile(hh, kl_ref[rows, heads[hh]], rows, carry[hh]) for hh in range(2))

    mx = lax.fori_loop(0, n_lat, max_body, tuple(jnp.full((8, tq), NEG, F32) for _ in range(2)), unroll=4)
    ctx_rows = slice(s_len, s_len + kc_ref.shape[0])
    ms = [jnp.max(score_tile(hh, kc_ref[:, heads[hh]], ctx_rows, mx[hh]), axis=0, keepdims=True) for hh in range(2)]

    def weighted(hh, rows, vt):
        return _dot(vt, jnp.exp2(st_ref[hh, rows, :] - ms[hh]).astype(BF16))

    def acc_body(t, carry):
        r0 = pl.multiple_of(t * tk, tk)
        return tuple(carry[hh] + weighted(hh, pl.ds(r0, tk), vl_ref[heads[hh], pl.ds(r0, tk)]) for hh in range(2))

    accs = lax.fori_loop(0, n_lat, acc_body, tuple(jnp.zeros((128, tq), F32) for _ in range(2)), unroll=4)
    outs = []
    for hh in range(2):
        acc = (accs[hh] + weighted(hh, ctx_rows, vc_ref[heads[hh], :])).T
        outs.append(acc[:, :HEAD_W] / acc[:, HEAD_W:HEAD_W + 1])
    o_ref[...] = jnp.concatenate(outs, axis=-1).astype(BF16)


def _mla_latent(qm, km, vt, nb, s, tq=512, tk=512):
    nq = s // tq
    nl = nb * s
    once = dict(pipeline_mode=pl.Buffered(1))
    return pl.pallas_call(
        functools.partial(_flash_kernel, tk=tk, n_lat=s // tk),
        grid=(nb, 2, nq),
        in_specs=[pl.BlockSpec((tq, 256), lambda b, hp, i: (b * nq + i, hp)),
                  pl.BlockSpec((s, 256), lambda b, hp, i: (b, hp), **once),
                  pl.BlockSpec((256, s), lambda b, hp, i: (hp, b), **once),
                  pl.BlockSpec((TS, 256), lambda b, hp, i: (nl // TS + b, hp)),
                  pl.BlockSpec((256, TS), lambda b, hp, i: (hp, nl // TS + b))],
        out_specs=pl.BlockSpec((tq, 128), lambda b, hp, i: (b * nq + i, hp)),
        out_shape=jax.ShapeDtypeStruct((nl, BRANCH_W), BF16),
        scratch_shapes=[pltpu.VMEM((2, s + TS, tq), F32)],
        compiler_params=_cparams(("parallel", "parallel", "arbitrary"), 56),
        name="mla_flash",
    )(qm, km, vt, km, vt)


def _gdn_prep_kernel(prev_ref, cur_ref, next_ref, ckv_ref, conv_ref, alog_ref, dtb_ref, bseg_ref, sel_ref,
                     qk_ref, kk_ref, vk_ref, gcx_ref, btx_ref, gt_ref, *, tps, n_lat_tiles):
    i = pl.program_id(0)
    is_ctx = i >= n_lat_tiles
    first = is_ctx | (i % tps == 0)
    last = is_ctx | (i % tps == tps - 1)
    prev = jnp.where(first, 0.0, prev_ref[...].astype(F32))
    nxt = jnp.where(last, 0.0, next_ref[...].astype(F32))
    ext = jnp.concatenate([prev, cur_ref[...].astype(F32), nxt], axis=0)
    n_ext = TS + 16
    acc = jnp.zeros((TS, 3 * BRANCH_W), F32)
    for j in range(GDN_CONV):
        shifted = pltpu.roll(ext, n_ext - (8 - GDN_CONV // 2 + j), axis=0)[:TS]
        acc = acc + shifted * conv_ref[j:j + 1, :]
    x = _silu(acc)
    bseg = bseg_ref[...]

    def l2n(a):
        sq = a * a
        hi = sq.astype(BF16)
        lo = (sq - hi.astype(F32)).astype(BF16)
        return a * lax.rsqrt(_dot(hi, bseg) + _dot(lo, bseg) + EPS)

    q = l2n(x[:, :BRANCH_W]) * (HEAD_W ** -0.5)
    k = l2n(x[:, BRANCH_W:2 * BRANCH_W])
    v = x[:, 2 * BRANCH_W:]
    for h in range(N_HEADS):
        hs = slice(h * HEAD_W, (h + 1) * HEAD_W)
        qk_ref[h] = jnp.concatenate([q[:, hs], k[:, hs]], axis=-1)
        kk_ref[h] = jnp.concatenate([k[:, hs], k[:, hs]], axis=-1)
        vk_ref[h] = jnp.concatenate([v[:, hs], k[:, hs]], axis=-1)

    ab = ckv_ref[:, 128:].astype(F32)
    la, lb = L_A - 128, L_B - 128
    sp_in = ab + dtb_ref[:, 128:]
    softplus = jnp.maximum(sp_in, 0.0) + jnp.log1p(jnp.exp(-jnp.abs(sp_in)))
    lane = lax.broadcasted_iota(jnp.int32, (1, 128), 1)
    g = jnp.where((lane >= la) & (lane < lb), -jnp.exp(alog_ref[:, 128:]) * softplus, 0.0)
    beta = _sigmoid(ab)
    pos = lax.broadcasted_iota(jnp.int32, (TS, 1), 0) % CHUNK
    pre, suf = g, g
    step = 1
    while step < CHUNK:
        pre = pre + jnp.where(pos >= step, pltpu.roll(pre, step, axis=0), 0.0)
        suf = suf + jnp.where(pos + step < CHUNK, pltpu.roll(suf, TS - step, axis=0), 0.0)
        step *= 2
    gc = jnp.where(lane < la + N_HEADS, pre, suf)
    gt_ref[...] = lax.dot_general(sel_ref[...], gc, (((1,), (1,)), ((), ())), precision=HI,
                                  preferred_element_type=F32)
    spread = lambda a, l0: jnp.concatenate(
        [jnp.broadcast_to(a[:, l0 + dh:l0 + dh + 1], (TS, 128)) for dh in range(2 * N_HEADS)], axis=-1)
    gcx_ref[...] = spread(gc, la)
    btx_ref[...] = spread(beta, lb)


def _gdn_prep(z, conv_w, alog_row, dtb_row, bseg, sel, tps_s, n_lat_tiles):
    nt_rows = z.shape[0]
    nt = nt_rows // TS
    hb = TS // 8
    full = lambda a: pl.BlockSpec(a.shape, lambda i: (0,) * a.ndim)
    xw = 2 * N_HEADS * 128
    hsp = pl.BlockSpec((N_HEADS, TS, 128), lambda i: (0, i, 0))
    return pl.pallas_call(
        functools.partial(_gdn_prep_kernel, tps=tps_s, n_lat_tiles=n_lat_tiles),
        grid=(nt,),
        in_specs=[pl.BlockSpec((8, 768), lambda i: (jnp.maximum(i * hb - 1, 0), 0)),
                  pl.BlockSpec((TS, 768), lambda i: (i, 0)),
                  pl.BlockSpec((8, 768), lambda i: (jnp.minimum((i + 1) * hb, nt * hb - 1), 0)),
                  pl.BlockSpec((TS, 256), lambda i: (i, C_CKV // 256)),
                  full(conv_w), full(alog_row), full(dtb_row), full(bseg), full(sel)],
        out_specs=[hsp] * 3
                  + [pl.BlockSpec((TS, xw), lambda i: (i, 0))] * 2
                  + [pl.BlockSpec((None, 8, TS), lambda i: (i, 0, 0))],
        out_shape=[jax.ShapeDtypeStruct((N_HEADS, nt_rows, 128), F32)] * 3
                  + [jax.ShapeDtypeStruct((nt_rows, xw), F32)] * 2
                  + [jax.ShapeDtypeStruct((nt, 8, TS), F32)],
        compiler_params=_cparams(("parallel",)),
        name="gdn_prep",
    )(z, z, z, z, conv_w, alog_row, dtb_row, bseg, sel)


def _tri_solve(n, x, reverse):
    h = n.shape[0]
    bs = 8
    nblk = CHUNK // bs
    r = lax.broadcasted_iota(jnp.int32, (CHUNK, CHUNK), 0)
    c = lax.broadcasted_iota(jnp.int32, (CHUNK, CHUNK), 1)
    off = jnp.where((r // bs != c // bs)[None], n, 0.0)
    diag = jnp.stack([n[:, bs * b:bs * (b + 1), bs * b:bs * (b + 1)] for b in range(nblk)], axis=1)

    ws = [a.reshape(h, nblk, bs, a.shape[-1]) for a in (x, off)]
    for j in (range(bs - 1, 0, -1) if reverse else range(bs - 1)):
        col = jnp.broadcast_to(diag[..., j:j + 1], (h, nblk, bs, 2 * HEAD_W))
        ws = [w - col[..., :w.shape[-1]] * w[:, :, j:j + 1, :] for w in ws]
    z, m = [w.reshape(h, CHUNK, w.shape[-1]) for w in ws]

    bmm = lambda a, b: jnp.einsum("hij,hjk->hik", a.astype(BF16), b.astype(BF16), preferred_element_type=F32)
    m2 = bmm(m, m)
    w1 = z - bmm(m, z)
    w2 = w1 + bmm(m2, w1)
    return w2 + bmm(bmm(m2, m2), w2)


def _gdn_chunk_terms(qk_ref, kk_ref, vk_ref, gc_ref, bt_ref, gt_ref, d, reverse):
    c = CHUNK
    nh = N_HEADS
    nc = TS // c
    r = lax.broadcasted_iota(jnp.int32, (c, c), 0)
    cc = lax.broadcasted_iota(jnp.int32, (c, c), 1)
    incl = ((r <= cc) if reverse else (r >= cc))[None]
    strict = ((r < cc) if reverse else (r > cc))[None]
    lo = lax.broadcasted_iota(jnp.int32, (1, 1, 2 * HEAD_W), 2) < HEAD_W
    bmm_nt = lambda a, b: jnp.einsum("hid,hjd->hij", a, b, preferred_element_type=F32)
    chunks = lambda a: a.reshape(nh * nc, c, a.shape[-1])
    qk = chunks(qk_ref[...])
    kk = chunks(kk_ref[...])
    vk = chunks(vk_ref[...])
    gc = chunks(jnp.stack([gc_ref[:, 128 * h:128 * (h + 1)] for h in range(nh)]))
    bt = chunks(jnp.stack([bt_ref[:, 128 * h:128 * (h + 1)] for h in range(nh)]))
    grow = jnp.stack([gt_ref[d * nh + h:d * nh + h + 1, ci * c:(ci + 1) * c]
                      for h in range(nh) for ci in range(nc)])
    dec = jnp.where(incl, jnp.exp(jnp.minimum(gc[:, :, :c] - grow, 0.0)), 0.0)
    kkb = kk.astype(BF16)
    k_dot_k = 0.5 * bmm_nt(kkb, kkb)
    q_only = jnp.where(lo, qk, 0.0)
    q_dot_k = bmm_nt(q_only.astype(BF16), kkb) * dec
    n = jnp.where(strict, bt[:, :, :c] * k_dot_k * dec, 0.0)
    egc = jnp.exp(gc)
    x = _tri_solve(n, vk * jnp.where(lo, bt, bt * egc), reverse)
    edge = 0 if reverse else c - 1
    glast = gc[:, edge:edge + 1, :]
    ktail = kk * jnp.exp(glast - gc)
    return (jnp.where(lo, 0.0, x).astype(BF16), x[:, :, :HEAD_W], (q_only * egc).astype(BF16), q_dot_k.astype(BF16),
            ktail.astype(BF16), jnp.exp(glast)[:, :, :HEAD_W])


def _gdn_scan_kernel(qkf, qkb, kkf, kkb, vkf, vkb, gcf, gcb, btf, btb, gtf, gtb, of_ref, ob_ref, s_ref):
    @pl.when(pl.program_id(1) == 0)
    def _():
        s_ref[...] = jnp.zeros(s_ref.shape, F32)

    n_chunks = TS // CHUNK
    dirs = ((qkf, kkf, vkf, gcf, btf, gtf, of_ref, False), (qkb, kkb, vkb, gcb, btb, gtb, ob_ref, True))
    order = [range(n_chunks), range(n_chunks - 1, -1, -1)]
    terms = [_gdn_chunk_terms(*dirs[d][:6], d, dirs[d][7]) for d in range(2)]
    pick = lambda a, ci: a.reshape((N_HEADS, n_chunks) + a.shape[1:])[:, ci]
    bmm = lambda a, b: jnp.einsum("hij,hjk->hik", a, b, preferred_element_type=F32)
    bmm_tn = lambda a, b: jnp.einsum("hcd,hce->hde", a, b, preferred_element_type=F32)
    state = [s_ref[0], s_ref[1]]
    for step in range(n_chunks):
        for d in range(2):
            ci = order[d][step]
            k_cum, u, q_dec, q_dot_k, ktail, total = [pick(a, ci) for a in terms[d]]
            s2b = state[d].astype(BF16)
            vb = (u - bmm(k_cum, s2b)).astype(BF16)
            dirs[d][6][:, ci * CHUNK:(ci + 1) * CHUNK, :] = bmm(q_dec, s2b) + bmm(q_dot_k, vb)
            state[d] = state[d] * total + bmm_tn(ktail, vb)
    s_ref[0] = state[0]
    s_ref[1] = state[1]


def _gdn_scan(qk, kk, vk, gcx, btx, gt, nb, s):
    h, nt_rows, _ = qk.shape
    tps = s // TS
    nlt = nb * tps
    fwd = lambda b, n: jnp.where(n == 0, nlt + b, b * tps + n - 1)
    bwd = lambda b, n: jnp.where(n == 0, nlt + b, b * tps + tps - n)
    hw = h * 128

    def views(shape, imap):
        return [pl.BlockSpec(shape, functools.partial(imap, t)) for t in (fwd, bwd)]

    heads = views((h, TS, 128), lambda t, b, n: (0, t(b, n), 0))
    lanes = [pl.BlockSpec((TS, hw), lambda b, n: (fwd(b, n), 0)), pl.BlockSpec((TS, hw), lambda b, n: (bwd(b, n), 1))]
    rows = views((None, 8, TS), lambda t, b, n: (t(b, n), 0, 0))
    outs = views((h, TS, HEAD_W), lambda t, b, n: (0, t(b, n), 0))
    return pl.pallas_call(
        _gdn_scan_kernel,
        grid=(nb, tps + 1),
        in_specs=heads * 3 + lanes * 2 + rows,
        out_specs=outs,
        out_shape=[jax.ShapeDtypeStruct((h, nt_rows, HEAD_W), F32)] * 2,
        scratch_shapes=[pltpu.VMEM((2, h, 2 * HEAD_W, HEAD_W), F32)],
        compiler_params=_cparams(("parallel", "arbitrary")),
        name="gdn_scan",
    )(qk, qk, kk, kk, vk, vk, gcx, gcx, btx, btx, gt, gt)


def _gdn_finish_kernel(of_ref, ob_ref, z_ref, nw_ref, y_ref):
    o = of_ref[...] + ob_ref[...]
    y = o * lax.rsqrt(jnp.mean(o * o, axis=-1, keepdims=True) + EPS) * nw_ref[...]
    y = jnp.concatenate([y[h] for h in range(N_HEADS)], axis=-1)
    y_ref[...] = (y * _silu(z_ref[...].astype(F32))).astype(BF16)


def _gdn_finish(of, ob, z, nw_row):
    nt = pl.cdiv(of.shape[1], TM)
    hsp = pl.BlockSpec((N_HEADS, TM, HEAD_W), lambda i: (0, i, 0))
    return pl.pallas_call(
        _gdn_finish_kernel,
        grid=(nt,),
        in_specs=[hsp, hsp, pl.BlockSpec((TM, 256), lambda i: (i, C_Z // 256)),
                  pl.BlockSpec(nw_row.shape, lambda i: (0, 0))],
        out_specs=pl.BlockSpec((TM, 256), lambda i: (i, 0)),
        out_shape=jax.ShapeDtypeStruct((of.shape[1], 256), BF16),
        compiler_params=_cparams(("parallel",)),
        name="gdn_finish",
    )(of, ob, z, nw_row)


S5_SUB = 8
S5_NS = S5_GROUPS * S5_STATE


def _s5_param_kernel(are_ref, aim_ref, ldt_ref, bre_ref, bim_ref, cre_ref, cim_ref,
                     bb_ref, kst_ref, cbd_ref, tab_ref):
    d = pl.program_id(0)
    lam_re = jnp.minimum(are_ref[...], -1e-4)
    lam_im = aim_ref[...]
    dt = jnp.exp(ldt_ref[...])

    def power(tau):
        mag = jnp.exp(lam_re * dt * tau)
        ang = lam_im * dt * tau
        return mag * jnp.cos(ang), mag * jnp.sin(ang)

    idx = lax.broadcasted_iota(jnp.int32, (S5_SUB, 1), 0)
    p_re, p_im = power(idx.astype(F32))
    lb_re, lb_im = p_re[1:2], p_im[1:2]
    den = lam_re * lam_re + lam_im * lam_im
    f_re = ((lb_re - 1.0) * lam_re + lb_im * lam_im) / den
    f_im = (lb_im * lam_re - (lb_re - 1.0) * lam_im) / den
    bb_re = f_re * bre_ref[...] - f_im * bim_ref[...]
    bb_im = f_re * bim_ref[...] + f_im * bre_ref[...]
    bb_ref[...] = jnp.concatenate([bb_re, bb_im], axis=-1).astype(BF16)
    c_re = cre_ref[...]
    c_im = cim_ref[...]
    cbd_ref[...] = jnp.concatenate([c_re, -c_im], axis=0).astype(BF16)
    for tau in range(S5_SUB):
        g_re = bb_re * p_re[tau:tau + 1] - bb_im * p_im[tau:tau + 1]
        g_im = bb_re * p_im[tau:tau + 1] + bb_im * p_re[tau:tau + 1]
        kst_ref[tau] = (_dot(g_re, c_re, precision=HI) - _dot(g_im, c_im, precision=HI)).astype(BF16)
    fwd = d == 0
    t_in = jnp.where(fwd, S5_SUB - 1 - idx, idx).astype(F32)
    t_out = jnp.where(fwd, idx + 1, S5_SUB - idx).astype(F32)
    for k, tau in enumerate((t_in, t_out, jnp.full((S5_SUB, 1), float(S5_SUB), F32))):
        tab_ref[k] = jnp.concatenate(power(tau), axis=-1)


def _s5_params(a_re, a_im, log_dt, b_re, b_im, c_re, c_im):
    g, p, gch = S5_GROUPS, S5_STATE, S5_GROUP_CH
    ns, c = S5_NS, g * gch
    eye = jnp.eye(g, dtype=F32)
    row = lambda a: a.astype(F32).reshape(2, 1, ns)
    ldt = jnp.broadcast_to(log_dt.astype(F32)[:, :, None], (2, g, p))
    b_bd = lambda b: jnp.einsum("gpc,gh->gchp", b.astype(F32), eye).reshape(c, ns)
    c_bd = lambda cc: jnp.einsum("dgcp,gh->dgphc", cc.astype(F32), eye).reshape(2, ns, c)
    per_dir = lambda *shape: pl.BlockSpec((None,) + shape, lambda d: (d,) + (0,) * len(shape))
    shared = pl.BlockSpec((c, ns), lambda d: (0, 0))
    return pl.pallas_call(
        _s5_param_kernel,
        grid=(2,),
        in_specs=[per_dir(1, ns)] * 3 + [shared] * 2 + [per_dir(ns, c)] * 2,
        out_specs=[per_dir(c, 2 * ns), per_dir(S5_SUB, c, c), per_dir(2 * ns, c), per_dir(3, S5_SUB, 2 * ns)],
        out_shape=[jax.ShapeDtypeStruct((2, c, 2 * ns), BF16),
                   jax.ShapeDtypeStruct((2, S5_SUB, c, c), BF16),
                   jax.ShapeDtypeStruct((2, 2 * ns, c), BF16),
                   jax.ShapeDtypeStruct((2, 3, S5_SUB, 2 * ns), F32)],
        compiler_params=_cparams(("parallel",), 48),
        name="s5_params",
    )(row(a_re), row(a_im), row(ldt), b_bd(b_re), b_bd(b_im), c_bd(c_re), c_bd(c_im))


def _s5_direction(u_ref, bb_ref, kst_ref, cbd_ref, tab_ref, y_ref, x_ref, xin_ref, xpv_ref, reverse):
    ns, sub = S5_NS, S5_SUB
    nsc = TS // sub
    u = u_ref[...]
    z = _dot(u, bb_ref[...])
    z_re = z[:, :ns].reshape(nsc, sub, ns)
    z_im = z[:, ns:].reshape(nsc, sub, ns)
    w = tab_ref[0]
    w_re, w_im = w[:, :ns][None], w[:, ns:][None]
    group_sum = lambda a: jnp.broadcast_to(jnp.sum(a, axis=1, keepdims=True), a.shape)
    xin_ref[0] = group_sum(w_re * z_re - w_im * z_im)
    xin_ref[1] = group_sum(w_re * z_im + w_im * z_re)
    a = tab_ref[2]
    a_re, a_im = a[:, :ns], a[:, ns:]

    def step(k, carry):
        x_re, x_im = carry
        n = nsc - 1 - k if reverse else k
        xpv_ref[0, n] = x_re
        xpv_ref[1, n] = x_im
        return (a_re * x_re - a_im * x_im + xin_ref[0, n], a_re * x_im + a_im * x_re + xin_ref[1, n])

    x_re, x_im = lax.fori_loop(0, nsc, step, (x_ref[0], x_ref[1]))
    x_ref[0] = x_re
    x_ref[1] = x_im
    o = tab_ref[1]
    o_re, o_im = o[:, :ns][None], o[:, ns:][None]
    p_re = (xpv_ref[0] * o_re - xpv_ref[1] * o_im).reshape(TS, ns)
    p_im = (xpv_ref[0] * o_im + xpv_ref[1] * o_re).reshape(TS, ns)
    y = _dot(jnp.concatenate([p_re, p_im], axis=-1).astype(BF16), cbd_ref[...])
    uf = u.astype(F32)
    pos = lax.broadcasted_iota(jnp.int32, (TS, 1), 0) % sub
    for tau in range(sub):
        if tau == 0:
            shifted = u
        else:
            rolled = pltpu.roll(uf, TS - tau if reverse else tau, axis=0)
            inside = (pos + tau < sub) if reverse else (pos >= tau)
            shifted = jnp.where(inside, rolled, 0.0).astype(BF16)
        y = y + _dot(shifted, kst_ref[tau])
    y_ref[...] = y


def _s5_scan_kernel(uf_ref, ub_ref, bbf, bbb, kstf, kstb, cbdf, cbdb, tabf, tabb, yf_ref, yb_ref,
                    x_ref, xin_ref, xpv_ref):
    @pl.when(pl.program_id(1) == 0)
    def _():
        x_ref[...] = jnp.zeros(x_ref.shape, F32)

    _s5_direction(uf_ref, bbf, kstf, cbdf, tabf, yf_ref, x_ref.at[0], xin_ref.at[0], xpv_ref.at[0], False)
    _s5_direction(ub_ref, bbb, kstb, cbdb, tabb, yb_ref, x_ref.at[1], xin_ref.at[1], xpv_ref.at[1], True)


def _s5_scan(z, bb, kst, cbd, tab, nb, s):
    nt_rows = z.shape[0]
    tps = s // TS
    nlt = nb * tps
    fwd = lambda b, n: jnp.where(n == 0, nlt + b, b * tps + n - 1)
    bwd = lambda b, n: jnp.where(n == 0, nlt + b, b * tps + tps - n)
    ns, c, nsc = S5_NS, S5_GROUPS * S5_GROUP_CH, TS // S5_SUB

    def both(a):
        return [pl.BlockSpec((None,) + a.shape[1:], lambda b, n, d=d: (d,) + (0,) * (a.ndim - 1)) for d in range(2)]

    state = lambda *lead: pltpu.VMEM((2, 2) + lead + (S5_SUB, ns), F32)
    return pl.pallas_call(
        _s5_scan_kernel,
        grid=(nb, tps + 1),
        in_specs=[pl.BlockSpec((TS, c), lambda b, n: (fwd(b, n), C_U // 256)),
                  pl.BlockSpec((TS, c), lambda b, n: (bwd(b, n), C_U // 256))]
                 + both(bb) + both(kst) + both(cbd) + both(tab),
        out_specs=[pl.BlockSpec((TS, c), lambda b, n: (fwd(b, n), 0)),
                   pl.BlockSpec((TS, c), lambda b, n: (bwd(b, n), 0))],
        out_shape=[jax.ShapeDtypeStruct((nt_rows, c), F32)] * 2,
        scratch_shapes=[state(), state(nsc), state(nsc)],
        compiler_params=_cparams(("parallel", "arbitrary"), 56),
        name="s5_scan",
    )(z, z, bb, bb, kst, kst, cbd, cbd, tab, tab)


def _s5_finish_kernel(yf_ref, yb_ref, u_ref, d_ref, w_ref, b_ref, o_ref):
    y = yf_ref[...] + yb_ref[...] + d_ref[...] * u_ref[...].astype(F32)
    y = jax.nn.gelu(y)
    gate = _sigmoid(_dot(y.astype(BF16), w_ref[...]) + b_ref[...])
    o_ref[...] = (y * gate).astype(BF16)


def _s5_finish(yf, yb, z, d_row, glu_w, glu_b_row):
    nt = pl.cdiv(yf.shape[0], TM)
    full = lambda a: pl.BlockSpec(a.shape, lambda i: (0,) * a.ndim)
    return pl.pallas_call(
        _s5_finish_kernel,
        grid=(nt,),
        in_specs=[pl.BlockSpec((TM, 256), lambda i: (i, 0)),
                  pl.BlockSpec((TM, 256), lambda i: (i, 0)),
                  pl.BlockSpec((TM, 256), lambda i: (i, C_U // 256)),
                  full(d_row), full(glu_w), full(glu_b_row)],
        out_specs=pl.BlockSpec((TM, 256), lambda i: (i, 0)),
        out_shape=jax.ShapeDtypeStruct((yf.shape[0], 256), BF16),
        compiler_params=_cparams(("parallel",)),
        name="s5_finish",
    )(yf, yb, z, d_row, glu_w, glu_b_row)


def _merge_kernel(x_ref, y0_ref, y1_ref, y2_ref, y3_ref, gates_ref, wb_ref, wo_ref, g1_ref, gain_ref, o_ref):
    acc = jnp.zeros((TM, D_MODEL), F32)
    for bi, y_ref in enumerate((y0_ref, y1_ref, y2_ref, y3_ref)):
        proj = _dot(y_ref[...], wb_ref[bi])
        acc = acc + gates_ref[:, bi * D_MODEL:(bi + 1) * D_MODEL].astype(F32) * proj
    y = _dot(acc.astype(BF16), wo_ref[...])
    o_ref[...] = x_ref[...] + g1_ref[...] * _rms(y, gain_ref[...])


def _merge(x, ys, gates, wb, wo, g1, gain, layer, tps, nb):
    nt = pl.cdiv(x.shape[0], TM)
    row = lambda i: (jnp.minimum(i // tps, nb), 0, 0)
    tile = lambda w: pl.BlockSpec((TM, w), lambda i: (i, 0))
    return pl.pallas_call(
        _merge_kernel,
        grid=(nt,),
        in_specs=[tile(D_MODEL)] + [tile(BRANCH_W)] * 4 + [tile(4 * D_MODEL),
                  pl.BlockSpec((None, 4, BRANCH_W, D_MODEL), lambda i: (layer, 0, 0, 0)),
                  pl.BlockSpec((None, D_MODEL, D_MODEL), lambda i: (layer, 0, 0)),
                  pl.BlockSpec((None, 1, D_MODEL), row),
                  pl.BlockSpec((1, D_MODEL), lambda i: (0, 0))],
        out_specs=tile(D_MODEL),
        out_shape=jax.ShapeDtypeStruct(x.shape, F32),
        compiler_params=_cparams(("parallel",), 56),
        name="merge",
    )(x, *ys, gates, wb, wo, g1, gain)


def _mlp_kernel(x_ref, sc_ref, sh_ref, g2_ref, gin_ref, gout_ref, w1_ref, w2_ref, o_ref, h_ref, acc_ref):
    j = pl.program_id(1)

    @pl.when(j == 0)
    def _():
        h_ref[...] = _norm_mod(x_ref[...], gin_ref[...], sc_ref[...], sh_ref[...]).astype(BF16)
        acc_ref[...] = jnp.zeros(acc_ref.shape, F32)

    t = jnp.maximum(_dot(h_ref[...], w1_ref[...]), 0.0)
    acc_ref[...] += _dot((t * t).astype(BF16), w2_ref[...])

    @pl.when(j == pl.num_programs(1) - 1)
    def _():
        o_ref[...] = x_ref[...] + g2_ref[...] * _rms(acc_ref[...], gout_ref[...])


def _mlp(x, sc, sh, g2, gin, gout, w1, w2, layer, tps, nb, tf=2048):
    nt = pl.cdiv(x.shape[0], TM)
    row = lambda i, j: (jnp.minimum(i // tps, nb), 0, 0)
    vec = pl.BlockSpec((1, D_MODEL), lambda i, j: (0, 0))
    return pl.pallas_call(
        _mlp_kernel,
        grid=(nt, D_FF // tf),
        in_specs=[pl.BlockSpec((TM, D_MODEL), lambda i, j: (i, 0)),
                  pl.BlockSpec((None, 1, D_MODEL), row), pl.BlockSpec((None, 1, D_MODEL), row),
                  pl.BlockSpec((None, 1, D_MODEL), row), vec, vec,
                  pl.BlockSpec((None, D_MODEL, tf), lambda i, j: (layer, 0, j)),
                  pl.BlockSpec((None, tf, D_MODEL), lambda i, j: (layer, j, 0))],
        out_specs=pl.BlockSpec((TM, D_MODEL), lambda i, j: (i, 0)),
        out_shape=jax.ShapeDtypeStruct(x.shape, F32),
        scratch_shapes=[pltpu.VMEM((TM, D_MODEL), BF16), pltpu.VMEM((TM, D_MODEL), F32)],
        compiler_params=_cparams(("parallel", "arbitrary"), 58),
        name="mlp",
    )(x, sc, sh, g2, gin, gout, w1, w2)


def _pack_w_in(w_in):
    o_na, o_cq, o_ckv, o_gdn, o_z, o_a, o_b, o_u, o_gate = 0, 768, 1024, 1184, 1952, 2208, 2216, 2224, 2480
    cols = lambda lo, n: w_in[:, :, lo:lo + n]
    o_kr = o_ckv + MLA_KV_LORA
    swapped = [cols(o_kr + 8, 8), cols(o_kr, 8), cols(o_kr + 24, 8), cols(o_kr + 16, 8)]
    small = jnp.concatenate(
        [cols(o_gdn, 768), cols(o_na, 768), cols(o_cq, 256),
         cols(o_ckv, 160), cols(o_a, 8), cols(o_b, 8)] + swapped
        + [jnp.zeros(w_in.shape[:2] + (256 - L_KRS - MLA_ROPE,), w_in.dtype), cols(o_z, 256), cols(o_u, 256)], axis=2)
    assert small.shape[2] == ZW
    return small.astype(BF16), w_in[:, :, o_gate:].astype(BF16)


def _mla_weights(w_uq, w_ukv):
    depth = w_uq.shape[0]
    hq = MLA_NOPE + MLA_ROPE
    wq = w_uq.reshape(depth, -1, N_HEADS, hq)
    pad = lambda a, lo, hi: jnp.pad(a, ((0, 0), (0, 0), (0, 0), (lo, hi)))
    wq_ext = pad(wq, 0, 128 - hq).reshape(depth, -1, N_HEADS * 128)
    wq_sw = pad(wq[..., MLA_NOPE:][..., ROPE_SWAP], MLA_NOPE, 128 - hq).reshape(depth, -1, N_HEADS * 128)
    wkv = w_ukv.reshape(depth, -1, N_HEADS, MLA_NOPE + HEAD_W)
    wk = pad(wkv[..., :MLA_NOPE], 0, 128 - MLA_NOPE).reshape(depth, -1, N_HEADS * 128)
    wv = pad(wkv[..., MLA_NOPE:], 0, 128 - HEAD_W).reshape(depth, -1, N_HEADS * 128)
    return [a.astype(BF16) for a in (wq_ext, wq_sw, wk, wv, jnp.swapaxes(wv, 1, 2))]


def _rope_place_mats():
    p1 = np.zeros((256, N_HEADS * 128), np.float32)
    p2 = np.zeros((256, N_HEADS * 128), np.float32)
    for h in range(N_HEADS):
        for r in range(MLA_ROPE):
            p1[L_KR + r, h * 128 + MLA_NOPE + r] = 1.0
            p2[L_KRS + r, h * 128 + MLA_NOPE + r] = 1.0
    return jnp.asarray(p1, BF16), jnp.asarray(p2, BF16)


def _rope_tables(s):
    quarter = MLA_ROPE // 4
    inv_freq = ROPE_BASE ** (-jnp.arange(quarter, dtype=F32) / quarter)
    t = jnp.arange(s)
    ang_r = (t // GRID_W).astype(F32)[:, None] * inv_freq[None, :]
    ang_c = (t % GRID_W).astype(F32)[:, None] * inv_freq[None, :]
    cr, sr, cc, sn = jnp.cos(ang_r), jnp.sin(ang_r), jnp.cos(ang_c), jnp.sin(ang_c)
    cos = jnp.concatenate([jnp.ones((s, MLA_NOPE), F32), cr, cr, cc, cc, jnp.ones((s, 32), F32)], axis=1)
    sin = jnp.concatenate([jnp.zeros((s, MLA_NOPE), F32), -sr, sr, -sn, sn, jnp.zeros((s, 32), F32)], axis=1)
    cos = jnp.concatenate([cos, jnp.ones((TM, 128), F32)], axis=0)
    sin = jnp.concatenate([sin, jnp.zeros((TM, 128), F32)], axis=0)
    return cos, sin


def _head_block_ones():
    r = np.arange(BRANCH_W)
    return jnp.asarray((r[:, None] // HEAD_W == r[None, :] // HEAD_W).astype(np.float32))


def _lane_row(vals, offset, width=256):
    return jnp.zeros((1, width), F32).at[0, offset:offset + vals.shape[0]].set(vals.astype(F32))


def _gdn_mixer(z, conv_w, a_log, dt_bias, norm_w, consts, nb, s, ctx_len):
    alog_row = _lane_row(a_log.reshape(-1), L_A)
    dtb_row = _lane_row(dt_bias.reshape(-1), L_A)
    qk, kk, vk, gcx, btx, gt = _gdn_prep(z, conv_w, alog_row, dtb_row, consts["bseg"], consts["sel"],
                                         s // TS, nb * s // TS)
    of, ob = _gdn_scan(qk, kk, vk, gcx, btx, gt, nb, s)
    return _gdn_finish(of, ob, z, norm_w.astype(F32)[None, :])


def _s5_mixer(z, a_re, a_im, log_dt, b_re, b_im, c_re, c_im, d_skip, glu_w, glu_b, nb, s, ctx_len):
    bb, kst, cbd, tab = _s5_params(a_re, a_im, log_dt, b_re, b_im, c_re, c_im)
    yf, yb = _s5_scan(z, bb, kst, cbd, tab, nb, s)
    return _s5_finish(yf, yb, z, d_skip.astype(F32)[None, :], glu_w.astype(BF16), glu_b.astype(F32)[None, :])


def kernel(x, c, ctx, c_ctx, ada_w, ada_b, norm_gains, w_in, na_rpb, mla_q_norm, mla_kv_norm, mla_w_uq, mla_w_ukv, gdn_conv, gdn_a_log, gdn_dt_bias, gdn_norm, s5_a_re, s5_a_im, s5_log_dt, s5_b_re, s5_b_im, s5_c_re, s5_c_im, s5_d, s5_glu_w, s5_glu_b, w_branch, w_out, mlp_w1, mlp_w2):
    nb, s, d = x.shape
    ctx_len = ctx.shape[1]
    depth = ada_w.shape[0]
    assert d == D_MODEL and ctx_len == TS and nb * ctx_len <= TM and s % TM == 0 and s // GRID_W >= 16
    nl = nb * s
    tps = s // TM

    xs = jnp.concatenate([x.reshape(nl, d), ctx.reshape(nb * ctx_len, d)], axis=0)
    cvec = jnp.zeros((8, d), F32).at[:nb].set(c).at[nb].set(c_ctx)
    mod = _modulation(cvec, ada_w, ada_b)
    mod = mod[:, :nb + 1].reshape(depth, nb + 1, 6, 1, d)

    w_small, w_gates = _pack_w_in(w_in)
    wq_ext, wq_sw, wk_ext, wv, wvt = _mla_weights(mla_w_uq, mla_w_ukv)
    p1, p2 = _rope_place_mats()
    cos_t, sin_t = _rope_tables(s)
    sel = np.zeros((8, 128), np.float32)
    sel[np.arange(8), L_A - 128 + np.arange(8)] = 1.0
    consts = {"bseg": _head_block_ones().astype(BF16), "sel": jnp.asarray(sel)}
    wb = w_branch.astype(BF16)
    wo = w_out.astype(BF16)
    w1 = mlp_w1.astype(BF16)
    w2 = mlp_w2.astype(BF16)
    gains = norm_gains.astype(F32)

    for l in range(depth):
        sh1, sc1, g1, sh2, sc2, g2 = [mod[l, :, i] for i in range(6)]
        z = _inproj(xs, sc1, sh1, gains[l, 0][None], w_small, l, tps, nb, False)
        gates = _inproj(xs, sc1, sh1, gains[l, 0][None], w_gates, l, tps, nb, True)

        y_na = jnp.concatenate([
            _na_latent(z, na_rpb[l], nb, s),
            _ctx_attention(z, z, z, C_NA // 256, C_NA // 256 + 1, C_NA // 256 + 2, 256, 256, nl // TS, nb,
                           HEAD_W ** -0.5, False, "na_ctx")], axis=0)

        qm, km, vm, vt = _mla_prep(z, cos_t, sin_t, mla_q_norm[l].astype(F32)[None],
                                   mla_kv_norm[l].astype(F32)[None], wq_ext[l], wq_sw[l], wk_ext[l], wv[l], wvt[l],
                                   p1, p2, tps, nl // TM)
        y_mla = jnp.concatenate([
            _mla_latent(qm, km, vt, nb, s),
            _ctx_attention(qm, km, vm, 0, 0, 0, N_HEADS * 128, N_HEADS * 128, nl // TS, nb, 1.0, True,
                           "mla_ctx")], axis=0)

        y_gdn = _gdn_mixer(z, gdn_conv[l].astype(F32), gdn_a_log[l], gdn_dt_bias[l], gdn_norm[l], consts,
                           nb, s, ctx_len)
        y_s5 = _s5_mixer(z, s5_a_re[l], s5_a_im[l], s5_log_dt[l], s5_b_re[l], s5_b_im[l], s5_c_re[l], s5_c_im[l],
                         s5_d[l], s5_glu_w[l], s5_glu_b[l], nb, s, ctx_len)

        xs = _merge(xs, (y_na, y_mla, y_gdn, y_s5), gates, wb, wo, g1, gains[l, 1][None], l, tps, nb)
        xs = _mlp(xs, sc2, sh2, g2, gains[l, 2][None], gains[l, 3][None], w1, w2, l, tps, nb)
    return xs[:nl].reshape(nb, s, d)
```

```python
import functools
import math

import numpy as np
import jax
import jax.numpy as jnp
from jax import lax
from jax.experimental import pallas as pl
from jax.experimental.pallas import tpu as pltpu

F32 = jnp.float32
BF16 = jnp.bfloat16
HI = lax.Precision.HIGHEST
EPS = 1e-6

D_MODEL = 1024
GRID_W = 64
NA_WIN_H = 8
NA_WIN_W = 16
N_HEADS = 4
HEAD_W = 64
BRANCH_W = 256
MLA_NOPE = 64
MLA_ROPE = 32
MLA_KV_LORA = 128
ROPE_BASE = 10000.0
GDN_CONV = 4
CHUNK = 64
S5_GROUPS = 16
S5_GROUP_CH = 16
S5_STATE = 64
D_FF = 4 * D_MODEL

TM = 1024
TN = 512
TS = 256
NEG = -1e30

C_GDN = 0
C_NA = 768
C_CQ = 1536
C_CKV = 1792
C_Z = 2048
C_U = 2304
ZW = 2560
L_KR = 128
L_A = 160
L_B = 168
L_KRS = 176
ROPE_SWAP = np.concatenate([np.arange(8, 16), np.arange(0, 8), np.arange(24, 32), np.arange(16, 24)])


def _cparams(sem, vmem_mb=None):
    kw = dict(dimension_semantics=sem)
    if vmem_mb is not None:
        kw["vmem_limit_bytes"] = vmem_mb * 1024 * 1024
    return pltpu.CompilerParams(**kw)


def _dot(a, b, **kw):
    return jnp.dot(a, b, preferred_element_type=F32, **kw)


def _dot_nt(a, b):
    return lax.dot_general(a, b, (((1,), (1,)), ((), ())), preferred_element_type=F32)


def _sigmoid(x):
    return 0.5 * jnp.tanh(0.5 * x) + 0.5


def _silu(x):
    return x * _sigmoid(x)


def _mod_kernel(c_ref, w_ref, b_ref, o_ref):
    c = c_ref[...]
    o_ref[...] = _dot(_silu(c), w_ref[...], precision=HI) + b_ref[...]


def _modulation(cvec, ada_w, ada_b):
    depth, d, n = ada_w.shape
    tn = 1536
    return pl.pallas_call(
        _mod_kernel,
        grid=(depth, n // tn),
        in_specs=[pl.BlockSpec((8, d), lambda l, j: (0, 0)),
                  pl.BlockSpec((None, d, tn), lambda l, j: (l, 0, j)),
                  pl.BlockSpec((None, 1, tn), lambda l, j: (l, 0, j))],
        out_specs=pl.BlockSpec((None, 8, tn), lambda l, j: (l, 0, j)),
        out_shape=jax.ShapeDtypeStruct((depth, 8, n), F32),
        compiler_params=_cparams(("parallel", "parallel"), 40),
        name="modulation",
    )(cvec, ada_w, ada_b.reshape(depth, 1, n))


def _norm_mod(x, gain, sc, sh):
    r = lax.rsqrt(jnp.mean(x * x, axis=-1, keepdims=True) + EPS)
    return (x * r * gain) * (1.0 + sc) + sh


def _inproj_kernel(x_ref, sc_ref, sh_ref, gain_ref, w_ref, o_ref, h_ref, *, gate):
    @pl.when(pl.program_id(1) == 0)
    def _():
        h_ref[...] = _norm_mod(x_ref[...], gain_ref[...], sc_ref[...], sh_ref[...]).astype(BF16)

    acc = _dot(h_ref[...], w_ref[...])
    if gate:
        acc = _sigmoid(acc)
    o_ref[...] = acc.astype(BF16)


def _inproj(x, sc, sh, gain, w, layer, tps, nb, gate):
    nt = pl.cdiv(x.shape[0], TM)
    width = w.shape[-1]
    tn = width // 2
    row = lambda i, j: (jnp.minimum(i // tps, nb), 0, 0)
    return pl.pallas_call(
        functools.partial(_inproj_kernel, gate=gate),
        grid=(nt, width // tn),
        in_specs=[pl.BlockSpec((TM, D_MODEL), lambda i, j: (i, 0)),
                  pl.BlockSpec((None, 1, D_MODEL), row),
                  pl.BlockSpec((None, 1, D_MODEL), row),
                  pl.BlockSpec((1, D_MODEL), lambda i, j: (0, 0)),
                  pl.BlockSpec((None, D_MODEL, tn), lambda i, j: (layer, 0, j))],
        out_specs=pl.BlockSpec((TM, tn), lambda i, j: (i, j)),
        out_shape=jax.ShapeDtypeStruct((x.shape[0], width), BF16),
        scratch_shapes=[pltpu.VMEM((TM, D_MODEL), BF16)],
        compiler_params=_cparams(("parallel", "arbitrary"), 48),
        name="inproj_gates" if gate else "inproj",
    )(x, sc, sh, gain, w)


def _head_lane_mask(width, head_w, h):
    lane = lax.broadcasted_iota(jnp.int32, (1, width), 1)
    return (lane >= h * head_w) & (lane < (h + 1) * head_w)


def _na_build_bias(rpb_ref, bias_ref, r0, kb0, rows_total):
    w = GRID_W
    qc = lax.broadcasted_iota(jnp.int32, (w, 2 * w), 0)
    lane = lax.broadcasted_iota(jnp.int32, (w, 2 * w), 1)
    kc = lane % w
    cs = jnp.clip(qc - NA_WIN_W // 2, 0, w - NA_WIN_W)
    col_ok = (kc >= cs) & (kc < cs + NA_WIN_W)
    left = lane < w
    neg = jnp.full((w, 2 * w), NEG, F32)
    for h in range(N_HEADS):
        t = rpb_ref[h]
        toep = []
        for a in range(2 * NA_WIN_H - 1):
            row = jnp.broadcast_to(t[a:a + 1, :], (w, 2 * w))
            ra = pltpu.roll(row, 2 * w - (NA_WIN_W - 1), axis=1, stride=1, stride_axis=0)
            rb = pltpu.roll(ra, w, axis=1)
            toep.append((jnp.where(col_ok, ra, NEG), jnp.where(col_ok, rb, NEG)))
        for qr in range(8):
            rs = min(max(r0 + qr - NA_WIN_H // 2, 0), rows_total - NA_WIN_H)
            for kp in range(8):
                halves = []
                for side in range(2):
                    kr = kb0 + 2 * kp + side
                    halves.append(toep[kr - (r0 + qr) + NA_WIN_H - 1][side] if rs <= kr < rs + NA_WIN_H else neg)
                bias_ref[h, qr * w:(qr + 1) * w, kp * 2 * w:(kp + 1) * 2 * w] = jnp.where(left, halves[0], halves[1])


def _na_kernel(q_ref, k_ref, v_ref, kc_ref, vc_ref, rpb_ref, o_ref, bias_ref, *, rows_total):
    i = pl.program_id(1)
    last = pl.num_programs(1) - 1

    @pl.when(i == 0)
    def _():
        _na_build_bias(rpb_ref, bias_ref, 0, 0, rows_total)

    @pl.when(i == 1)
    def _():
        _na_build_bias(rpb_ref, bias_ref, 8, 4, rows_total)

    @pl.when(i == last)
    def _():
        _na_build_bias(rpb_ref, bias_ref, rows_total - 8, rows_total - 16, rows_total)

    kb = jnp.clip(2 * i - 1, 0, rows_total // 4 - 4)
    start = pl.multiple_of(kb * (4 * GRID_W), 4 * GRID_W)
    nk = 2 * NA_WIN_H * GRID_W
    q = q_ref[...]
    kw = k_ref[pl.ds(start, nk), :]
    vw = v_ref[pl.ds(start, nk), :]
    kc = kc_ref[...]
    vc = vc_ref[...]
    scale = HEAD_W ** -0.5
    out = jnp.zeros(q.shape, F32)
    for h in range(N_HEADS):
        hm = _head_lane_mask(BRANCH_W, HEAD_W, h)
        qh = jnp.where(hm, q, jnp.zeros_like(q))
        sb = _dot_nt(qh, kw) * scale + bias_ref[h]
        sc = _dot_nt(qh, kc) * scale
        m = jnp.maximum(jnp.max(sb, axis=-1, keepdims=True), jnp.max(sc, axis=-1, keepdims=True))
        pb = jnp.exp(sb - m)
        pc = jnp.exp(sc - m)
        den = jnp.sum(pb, axis=-1, keepdims=True) + jnp.sum(pc, axis=-1, keepdims=True)
        o = _dot(pb.astype(BF16), vw) + _dot(pc.astype(BF16), vc)
        out = jnp.where(hm, o / den, out)
    o_ref[...] = out.astype(BF16)


def _na_latent(z, rpb, nb, s):
    rows_total = s // GRID_W
    qb = 8 * GRID_W
    nq = s // qb
    nl = nb * s
    rpb = jnp.pad(rpb.astype(F32), ((0, 0), (0, 1), (0, 2 * GRID_W - (2 * NA_WIN_W - 1))))
    return pl.pallas_call(
        functools.partial(_na_kernel, rows_total=rows_total),
        grid=(nb, nq),
        in_specs=[pl.BlockSpec((qb, BRANCH_W), lambda b, i: (b * nq + i, C_NA // 256)),
                  pl.BlockSpec((s, BRANCH_W), lambda b, i: (b, C_NA // 256 + 1)),
                  pl.BlockSpec((s, BRANCH_W), lambda b, i: (b, C_NA // 256 + 2)),
                  pl.BlockSpec((TS, BRANCH_W), lambda b, i: (nl // TS + b, C_NA // 256 + 1)),
                  pl.BlockSpec((TS, BRANCH_W), lambda b, i: (nl // TS + b, C_NA // 256 + 2)),
                  pl.BlockSpec(rpb.shape, lambda b, i: (0, 0, 0))],
        out_specs=pl.BlockSpec((qb, BRANCH_W), lambda b, i: (b * nq + i, 0)),
        out_shape=jax.ShapeDtypeStruct((nl, BRANCH_W), BF16),
        scratch_shapes=[pltpu.VMEM((N_HEADS, qb, 2 * qb), F32)],
        compiler_params=_cparams(("parallel", "arbitrary"), 56),
        name="na_latent",
    )(z, z, z, z, z, rpb)


def _ctx_attn_kernel(q_ref, k_ref, v_ref, o_ref, *, scale, base2):
    q = q_ref[...]
    k = k_ref[...]
    v = v_ref[...]
    qw = q.shape[-1]
    vw = v.shape[-1] // N_HEADS
    outs = []
    for h in range(N_HEADS):
        qh = jnp.where(_head_lane_mask(qw, qw // N_HEADS, h), q, jnp.zeros_like(q))
        s = _dot_nt(qh, k) * scale
        m = jnp.max(s, axis=-1, keepdims=True)
        p = jnp.exp2(s - m) if base2 else jnp.exp(s - m)
        den = jnp.sum(p, axis=-1, keepdims=True)
        o = _dot(p.astype(BF16), v)
        outs.append(o[:, h * vw:h * vw + HEAD_W] / den)
    o_ref[...] = jnp.concatenate(outs, axis=-1).astype(BF16)


def _ctx_attention(q, k, v, qcol, kcol, vcol, qw, vw, row0, nb, scale, base2, name):
    return pl.pallas_call(
        functools.partial(_ctx_attn_kernel, scale=scale, base2=base2),
        grid=(nb,),
        in_specs=[pl.BlockSpec((TS, qw), lambda b: (row0 + b, qcol)),
                  pl.BlockSpec((TS, qw), lambda b: (row0 + b, kcol)),
                  pl.BlockSpec((TS, vw), lambda b: (row0 + b, vcol))],
        out_specs=pl.BlockSpec((TS, BRANCH_W), lambda b: (b, 0)),
        out_shape=jax.ShapeDtypeStruct((nb * TS, BRANCH_W), BF16),
        compiler_params=_cparams(("parallel",)),
        name=name,
    )(q, k, v)


def _rms(x, gain):
    return x * lax.rsqrt(jnp.mean(x * x, axis=-1, keepdims=True) + EPS) * gain


def _mla_prep_kernel(cq_ref, ckv_ref, cos_ref, sin_ref, qn_ref, kvn_ref, wq_ref, wqs_ref, wk_ref, wv_ref, wvt_ref,
                     p1_ref, p2_ref, q_ref, k_ref, v_ref, vt_ref):
    cos = jnp.concatenate([cos_ref[...]] * N_HEADS, axis=-1)
    sin = jnp.concatenate([sin_ref[...]] * N_HEADS, axis=-1)
    cqn = _rms(cq_ref[...].astype(F32), qn_ref[...]).astype(BF16)
    scale = (MLA_NOPE + MLA_ROPE) ** -0.5 * math.log2(math.e)
    q = _dot(cqn, wq_ref[...]) * cos + _dot(cqn, wqs_ref[...]) * sin
    q_ref[...] = (q * scale).astype(BF16)
    ckv = ckv_ref[...]
    kvn = _rms(ckv[:, :MLA_KV_LORA].astype(F32), kvn_ref[...]).astype(BF16)
    k = (_dot(kvn, wk_ref[...]) + _dot(ckv, p1_ref[...])) * cos + _dot(ckv, p2_ref[...]) * sin
    k_ref[...] = k.astype(BF16)
    lane = lax.broadcasted_iota(jnp.int32, (1, N_HEADS * 128), 1)
    v_ref[...] = jnp.where(lane % 128 == HEAD_W, 1.0, _dot(kvn, wv_ref[...])).astype(BF16)
    row = lax.broadcasted_iota(jnp.int32, (N_HEADS * 128, 1), 0)
    vt_ref[...] = jnp.where(row % 128 == HEAD_W, 1.0, _dot_nt(wvt_ref[...], kvn)).astype(BF16)


def _mla_prep(z, cos_t, sin_t, qn, kvn, wq, wqs, wk, wv, wvt, p1, p2, tps, n_lat_tiles):
    nt_rows = z.shape[0]
    nt = pl.cdiv(nt_rows, TM)
    full = lambda a: pl.BlockSpec(a.shape, lambda i: (0,) * a.ndim)
    tab = lambda i: (jnp.where(i < n_lat_tiles, i % tps, tps), 0)
    hw = N_HEADS * 128
    return pl.pallas_call(
        _mla_prep_kernel,
        grid=(nt,),
        in_specs=[pl.BlockSpec((TM, 256), lambda i: (i, C_CQ // 256)),
                  pl.BlockSpec((TM, 256), lambda i: (i, C_CKV // 256)),
                  pl.BlockSpec((TM, 128), tab), pl.BlockSpec((TM, 128), tab),
                  full(qn), full(kvn), full(wq), full(wqs), full(wk), full(wv), full(wvt), full(p1), full(p2)],
        out_specs=[pl.BlockSpec((TM, hw), lambda i: (i, 0)),
                   pl.BlockSpec((TM, hw), lambda i: (i, 0)),
                   pl.BlockSpec((TM, hw), lambda i: (i, 0)),
                   pl.BlockSpec((hw, TM), lambda i: (0, i))],
        out_shape=[jax.ShapeDtypeStruct((nt_rows, hw), BF16)] * 3 + [jax.ShapeDtypeStruct((hw, nt_rows), BF16)],
        compiler_params=_cparams(("parallel",)),
        name="mla_prep",
    )(z, z, cos_t, sin_t, qn, kvn, wq, wqs, wk, wv, wvt, p1, p2)


def _flash_kernel(q_ref, kl_ref, vl_ref, kc_ref, vc_ref, o_ref, st_ref, *, tk, n_lat):
    tq = q_ref.shape[0]
    s_len = n_lat * tk
    heads = (slice(0, 128), slice(128, 256))
    qs = [q_ref[:, hs] for hs in heads]
    group_max = lambda st: jnp.max(st.reshape(st.shape[0] // 8, 8, tq), axis=0)

    def score_tile(hh, k, rows, mx):
        st = _dot_nt(k, qs[hh])
        st_ref[hh, rows, :] = st
        return jnp.maximum(mx, group_max(st))

    def max_body(t, carry):
        rows = pl.ds(pl.multiple_of(t * tk, tk), tk)
        return tuple(score_tile(hh, kl_ref[rows, heads[hh]], rows, carry[hh]) for hh in range(2))

    mx = lax.fori_loop(0, n_lat, max_body, tuple(jnp.full((8, tq), NEG, F32) for _ in range(2)), unroll=4)
    ctx_rows = slice(s_len, s_len + kc_ref.shape[0])
    ms = [jnp.max(score_tile(hh, kc_ref[:, heads[hh]], ctx_rows, mx[hh]), axis=0, keepdims=True) for hh in range(2)]

    def weighted(hh, rows, vt):
        return _dot(vt, jnp.exp2(st_ref[hh, rows, :] - ms[hh]).astype(BF16))

    def acc_body(t, carry):
        r0 = pl.multiple_of(t * tk, tk)
        return tuple(carry[hh] + weighted(hh, pl.ds(r0, tk), vl_ref[heads[hh], pl.ds(r0, tk)]) for hh in range(2))

    accs = lax.fori_loop(0, n_lat, acc_body, tuple(jnp.zeros((128, tq), F32) for _ in range(2)), unroll=4)
    outs = []
    for hh in range(2):
        acc = (accs[hh] + weighted(hh, ctx_rows, vc_ref[heads[hh], :])).T
        outs.append(acc[:, :HEAD_W] / acc[:, HEAD_W:HEAD_W + 1])
    o_ref[...] = jnp.concatenate(outs, axis=-1).astype(BF16)


def _mla_latent(qm, km, vt, nb, s, tq=512, tk=1024):
    nq = s // tq
    nl = nb * s
    once = dict(pipeline_mode=pl.Buffered(1))
    return pl.pallas_call(
        functools.partial(_flash_kernel, tk=tk, n_lat=s // tk),
        grid=(nb, 2, nq),
        in_specs=[pl.BlockSpec((tq, 256), lambda b, hp, i: (b * nq + i, hp)),
                  pl.BlockSpec((s, 256), lambda b, hp, i: (b, hp), **once),
                  pl.BlockSpec((256, s), lambda b, hp, i: (hp, b), **once),
                  pl.BlockSpec((TS, 256), lambda b, hp, i: (nl // TS + b, hp)),
                  pl.BlockSpec((256, TS), lambda b, hp, i: (hp, nl // TS + b))],
        out_specs=pl.BlockSpec((tq, 128), lambda b, hp, i: (b * nq + i, hp)),
        out_shape=jax.ShapeDtypeStruct((nl, BRANCH_W), BF16),
        scratch_shapes=[pltpu.VMEM((2, s + TS, tq), F32)],
        compiler_params=_cparams(("parallel", "parallel", "arbitrary"), 56),
        name="mla_flash",
    )(qm, km, vt, km, vt)


def _gdn_prep_kernel(prev_ref, cur_ref, next_ref, ckv_ref, conv_ref, alog_ref, dtb_ref, bseg_ref, sel_ref,
                     qk_ref, kk_ref, vk_ref, gcx_ref, btx_ref, gt_ref, *, tps, n_lat_tiles):
    i = pl.program_id(0)
    is_ctx = i >= n_lat_tiles
    first = is_ctx | (i % tps == 0)
    last = is_ctx | (i % tps == tps - 1)
    prev = jnp.where(first, 0.0, prev_ref[...].astype(F32))
    nxt = jnp.where(last, 0.0, next_ref[...].astype(F32))
    ext = jnp.concatenate([prev, cur_ref[...].astype(F32), nxt], axis=0)
    n_ext = TS + 16
    acc = jnp.zeros((TS, 3 * BRANCH_W), F32)
    for j in range(GDN_CONV):
        shifted = pltpu.roll(ext, n_ext - (8 - GDN_CONV // 2 + j), axis=0)[:TS]
        acc = acc + shifted * conv_ref[j:j + 1, :]
    x = _silu(acc)
    bseg = bseg_ref[...]

    def l2n(a):
        sq = a * a
        hi = sq.astype(BF16)
        lo = (sq - hi.astype(F32)).astype(BF16)
        return a * lax.rsqrt(_dot(hi, bseg) + _dot(lo, bseg) + EPS)

    q = l2n(x[:, :BRANCH_W]) * (HEAD_W ** -0.5)
    k = l2n(x[:, BRANCH_W:2 * BRANCH_W])
    v = x[:, 2 * BRANCH_W:]
    for h in range(N_HEADS):
        hs = slice(h * HEAD_W, (h + 1) * HEAD_W)
        qk_ref[h] = jnp.concatenate([q[:, hs], k[:, hs]], axis=-1)
        kk_ref[h] = jnp.concatenate([k[:, hs], k[:, hs]], axis=-1)
        vk_ref[h] = jnp.concatenate([v[:, hs], k[:, hs]], axis=-1)

    ab = ckv_ref[:, 128:].astype(F32)
    la, lb = L_A - 128, L_B - 128
    sp_in = ab + dtb_ref[:, 128:]
    softplus = jnp.maximum(sp_in, 0.0) + jnp.log1p(jnp.exp(-jnp.abs(sp_in)))
    lane = lax.broadcasted_iota(jnp.int32, (1, 128), 1)
    g = jnp.where((lane >= la) & (lane < lb), -jnp.exp(alog_ref[:, 128:]) * softplus, 0.0)
    beta = _sigmoid(ab)
    pos = lax.broadcasted_iota(jnp.int32, (TS, 1), 0) % CHUNK
    pre, suf = g, g
    step = 1
    while step < CHUNK:
        pre = pre + jnp.where(pos >= step, pltpu.roll(pre, step, axis=0), 0.0)
        suf = suf + jnp.where(pos + step < CHUNK, pltpu.roll(suf, TS - step, axis=0), 0.0)
        step *= 2
    gc = jnp.where(lane < la + N_HEADS, pre, suf)
    gt_ref[...] = lax.dot_general(sel_ref[...], gc, (((1,), (1,)), ((), ())), precision=HI,
                                  preferred_element_type=F32)
    spread = lambda a, l0: jnp.concatenate(
        [jnp.broadcast_to(a[:, l0 + dh:l0 + dh + 1], (TS, 128)) for dh in range(2 * N_HEADS)], axis=-1)
    gcx_ref[...] = spread(gc, la)
    btx_ref[...] = spread(beta, lb)


def _gdn_prep(z, conv_w, alog_row, dtb_row, bseg, sel, tps_s, n_lat_tiles):
    nt_rows = z.shape[0]
    nt = nt_rows // TS
    hb = TS // 8
    full = lambda a: pl.BlockSpec(a.shape, lambda i: (0,) * a.ndim)
    xw = 2 * N_HEADS * 128
    hsp = pl.BlockSpec((N_HEADS, TS, 128), lambda i: (0, i, 0))
    return pl.pallas_call(
        functools.partial(_gdn_prep_kernel, tps=tps_s, n_lat_tiles=n_lat_tiles),
        grid=(nt,),
        in_specs=[pl.BlockSpec((8, 768), lambda i: (jnp.maximum(i * hb - 1, 0), 0)),
                  pl.BlockSpec((TS, 768), lambda i: (i, 0)),
                  pl.BlockSpec((8, 768), lambda i: (jnp.minimum((i + 1) * hb, nt * hb - 1), 0)),
                  pl.BlockSpec((TS, 256), lambda i: (i, C_CKV // 256)),
                  full(conv_w), full(alog_row), full(dtb_row), full(bseg), full(sel)],
        out_specs=[hsp] * 3
                  + [pl.BlockSpec((TS, xw), lambda i: (i, 0))] * 2
                  + [pl.BlockSpec((None, 8, TS), lambda i: (i, 0, 0))],
        out_shape=[jax.ShapeDtypeStruct((N_HEADS, nt_rows, 128), F32)] * 3
                  + [jax.ShapeDtypeStruct((nt_rows, xw), F32)] * 2
                  + [jax.ShapeDtypeStruct((nt, 8, TS), F32)],
        compiler_params=_cparams(("parallel",)),
        name="gdn_prep",
    )(z, z, z, z, conv_w, alog_row, dtb_row, bseg, sel)


def _tri_solve(n, x, reverse):
    h = n.shape[0]
    bs = 8
    nblk = CHUNK // bs
    r = lax.broadcasted_iota(jnp.int32, (CHUNK, CHUNK), 0)
    c = lax.broadcasted_iota(jnp.int32, (CHUNK, CHUNK), 1)
    off = jnp.where((r // bs != c // bs)[None], n, 0.0)
    diag = jnp.stack([n[:, bs * b:bs * (b + 1), bs * b:bs * (b + 1)] for b in range(nblk)], axis=1)

    ws = [a.reshape(h, nblk, bs, a.shape[-1]) for a in (x, off)]
    for j in (range(bs - 1, 0, -1) if reverse else range(bs - 1)):
        col = jnp.broadcast_to(diag[..., j:j + 1], (h, nblk, bs, 2 * HEAD_W))
        ws = [w - col[..., :w.shape[-1]] * w[:, :, j:j + 1, :] for w in ws]
    z, m = [w.reshape(h, CHUNK, w.shape[-1]) for w in ws]

    bmm = lambda a, b: jnp.einsum("hij,hjk->hik", a.astype(BF16), b.astype(BF16), preferred_element_type=F32)
    m2 = bmm(m, m)
    w1 = z - bmm(m, z)
    w2 = w1 + bmm(m2, w1)
    return w2 + bmm(bmm(m2, m2), w2)


def _gdn_chunk_terms(qk_ref, kk_ref, vk_ref, gc_ref, bt_ref, gt_ref, d, reverse):
    c = CHUNK
    nh = N_HEADS
    nc = TS // c
    r = lax.broadcasted_iota(jnp.int32, (c, c), 0)
    cc = lax.broadcasted_iota(jnp.int32, (c, c), 1)
    incl = ((r <= cc) if reverse else (r >= cc))[None]
    strict = ((r < cc) if reverse else (r > cc))[None]
    lo = lax.broadcasted_iota(jnp.int32, (1, 1, 2 * HEAD_W), 2) < HEAD_W
    bmm_nt = lambda a, b: jnp.einsum("hid,hjd->hij", a, b, preferred_element_type=F32)
    chunks = lambda a: a.reshape(nh * nc, c, a.shape[-1])
    qk = chunks(qk_ref[...])
    kk = chunks(kk_ref[...])
    vk = chunks(vk_ref[...])
    gc = chunks(jnp.stack([gc_ref[:, 128 * h:128 * (h + 1)] for h in range(nh)]))
    bt = chunks(jnp.stack([bt_ref[:, 128 * h:128 * (h + 1)] for h in range(nh)]))
    grow = jnp.stack([gt_ref[d * nh + h:d * nh + h + 1, ci * c:(ci + 1) * c]
                      for h in range(nh) for ci in range(nc)])
    dec = jnp.where(incl, jnp.exp(jnp.minimum(gc[:, :, :c] - grow, 0.0)), 0.0)
    kkb = kk.astype(BF16)
    k_dot_k = 0.5 * bmm_nt(kkb, kkb)
    q_only = jnp.where(lo, qk, 0.0)
    q_dot_k = bmm_nt(q_only.astype(BF16), kkb) * dec
    n = jnp.where(strict, bt[:, :, :c] * k_dot_k * dec, 0.0)
    egc = jnp.exp(gc)
    x = _tri_solve(n, vk * jnp.where(lo, bt, bt * egc), reverse)
    edge = 0 if reverse else c - 1
    glast = gc[:, edge:edge + 1, :]
    ktail = kk * jnp.exp(glast - gc)
    return (jnp.where(lo, 0.0, x).astype(BF16), x[:, :, :HEAD_W], (q_only * egc).astype(BF16), q_dot_k.astype(BF16),
            ktail.astype(BF16), jnp.exp(glast)[:, :, :HEAD_W])


def _gdn_scan_kernel(qkf, qkb, kkf, kkb, vkf, vkb, gcf, gcb, btf, btb, gtf, gtb, of_ref, ob_ref, s_ref):
    @pl.when(pl.program_id(1) == 0)
    def _():
        s_ref[...] = jnp.zeros(s_ref.shape, F32)

    n_chunks = TS // CHUNK
    dirs = ((qkf, kkf, vkf, gcf, btf, gtf, of_ref, False), (qkb, kkb, vkb, gcb, btb, gtb, ob_ref, True))
    order = [range(n_chunks), range(n_chunks - 1, -1, -1)]
    terms = [_gdn_chunk_terms(*dirs[d][:6], d, dirs[d][7]) for d in range(2)]
    pick = lambda a, ci: a.reshape((N_HEADS, n_chunks) + a.shape[1:])[:, ci]
    bmm = lambda a, b: jnp.einsum("hij,hjk->hik", a, b, preferred_element_type=F32)
    bmm_tn = lambda a, b: jnp.einsum("hcd,hce->hde", a, b, preferred_element_type=F32)
    state = [s_ref[0], s_ref[1]]
    for step in range(n_chunks):
        for d in range(2):
            ci = order[d][step]
            k_cum, u, q_dec, q_dot_k, ktail, total = [pick(a, ci) for a in terms[d]]
            s2b = state[d].astype(BF16)
            vb = (u - bmm(k_cum, s2b)).astype(BF16)
            dirs[d][6][:, ci * CHUNK:(ci + 1) * CHUNK, :] = bmm(q_dec, s2b) + bmm(q_dot_k, vb)
            state[d] = state[d] * total + bmm_tn(ktail, vb)
    s_ref[0] = state[0]
    s_ref[1] = state[1]


def _gdn_scan(qk, kk, vk, gcx, btx, gt, nb, s):
    h, nt_rows, _ = qk.shape
    tps = s // TS
    nlt = nb * tps
    fwd = lambda b, n: jnp.where(n == 0, nlt + b, b * tps + n - 1)
    bwd = lambda b, n: jnp.where(n == 0, nlt + b, b * tps + tps - n)
    hw = h * 128

    def views(shape, imap):
        return [pl.BlockSpec(shape, functools.partial(imap, t)) for t in (fwd, bwd)]

    heads = views((h, TS, 128), lambda t, b, n: (0, t(b, n), 0))
    lanes = [pl.BlockSpec((TS, hw), lambda b, n: (fwd(b, n), 0)), pl.BlockSpec((TS, hw), lambda b, n: (bwd(b, n), 1))]
    rows = views((None, 8, TS), lambda t, b, n: (t(b, n), 0, 0))
    outs = views((h, TS, HEAD_W), lambda t, b, n: (0, t(b, n), 0))
    return pl.pallas_call(
        _gdn_scan_kernel,
        grid=(nb, tps + 1),
        in_specs=heads * 3 + lanes * 2 + rows,
        out_specs=outs,
        out_shape=[jax.ShapeDtypeStruct((h, nt_rows, HEAD_W), F32)] * 2,
        scratch_shapes=[pltpu.VMEM((2, h, 2 * HEAD_W, HEAD_W), F32)],
        compiler_params=_cparams(("parallel", "arbitrary")),
        name="gdn_scan",
    )(qk, qk, kk, kk, vk, vk, gcx, gcx, btx, btx, gt, gt)


def _gdn_finish_kernel(of_ref, ob_ref, z_ref, nw_ref, y_ref):
    o = of_ref[...] + ob_ref[...]
    y = o * lax.rsqrt(jnp.mean(o * o, axis=-1, keepdims=True) + EPS) * nw_ref[...]
    y = jnp.concatenate([y[h] for h in range(N_HEADS)], axis=-1)
    y_ref[...] = (y * _silu(z_ref[...].astype(F32))).astype(BF16)


def _gdn_finish(of, ob, z, nw_row):
    nt = pl.cdiv(of.shape[1], TM)
    hsp = pl.BlockSpec((N_HEADS, TM, HEAD_W), lambda i: (0, i, 0))
    return pl.pallas_call(
        _gdn_finish_kernel,
        grid=(nt,),
        in_specs=[hsp, hsp, pl.BlockSpec((TM, 256), lambda i: (i, C_Z // 256)),
                  pl.BlockSpec(nw_row.shape, lambda i: (0, 0))],
        out_specs=pl.BlockSpec((TM, 256), lambda i: (i, 0)),
        out_shape=jax.ShapeDtypeStruct((of.shape[1], 256), BF16),
        compiler_params=_cparams(("parallel",)),
        name="gdn_finish",
    )(of, ob, z, nw_row)


S5_SUB = 8
S5_NS = S5_GROUPS * S5_STATE


def _s5_param_kernel(are_ref, aim_ref, ldt_ref, bre_ref, bim_ref, cre_ref, cim_ref,
                     bb_ref, kst_ref, cbd_ref, tab_ref):
    d = pl.program_id(0)
    lam_re = jnp.minimum(are_ref[...], -1e-4)
    lam_im = aim_ref[...]
    dt = jnp.exp(ldt_ref[...])

    def power(tau):
        mag = jnp.exp(lam_re * dt * tau)
        ang = lam_im * dt * tau
        return mag * jnp.cos(ang), mag * jnp.sin(ang)

    idx = lax.broadcasted_iota(jnp.int32, (S5_SUB, 1), 0)
    p_re, p_im = power(idx.astype(F32))
    lb_re, lb_im = p_re[1:2], p_im[1:2]
    den = lam_re * lam_re + lam_im * lam_im
    f_re = ((lb_re - 1.0) * lam_re + lb_im * lam_im) / den
    f_im = (lb_im * lam_re - (lb_re - 1.0) * lam_im) / den
    bb_re = f_re * bre_ref[...] - f_im * bim_ref[...]
    bb_im = f_re * bim_ref[...] + f_im * bre_ref[...]
    bb_ref[...] = jnp.concatenate([bb_re, bb_im], axis=-1).astype(BF16)
    c_re = cre_ref[...]
    c_im = cim_ref[...]
    cbd_ref[...] = jnp.concatenate([c_re, -c_im], axis=0).astype(BF16)
    for tau in range(S5_SUB):
        g_re = bb_re * p_re[tau:tau + 1] - bb_im * p_im[tau:tau + 1]
        g_im = bb_re * p_im[tau:tau + 1] + bb_im * p_re[tau:tau + 1]
        kst_ref[tau] = (_dot(g_re, c_re, precision=HI) - _dot(g_im, c_im, precision=HI)).astype(BF16)
    fwd = d == 0
    t_in = jnp.where(fwd, S5_SUB - 1 - idx, idx).astype(F32)
    t_out = jnp.where(fwd, idx + 1, S5_SUB - idx).astype(F32)
    for k, tau in enumerate((t_in, t_out, jnp.full((S5_SUB, 1), float(S5_SUB), F32))):
        tab_ref[k] = jnp.concatenate(power(tau), axis=-1)


def _s5_params(a_re, a_im, log_dt, b_re, b_im, c_re, c_im):
    g, p, gch = S5_GROUPS, S5_STATE, S5_GROUP_CH
    ns, c = S5_NS, g * gch
    eye = jnp.eye(g, dtype=F32)
    row = lambda a: a.astype(F32).reshape(2, 1, ns)
    ldt = jnp.broadcast_to(log_dt.astype(F32)[:, :, None], (2, g, p))
    b_bd = lambda b: jnp.einsum("gpc,gh->gchp", b.astype(F32), eye).reshape(c, ns)
    c_bd = lambda cc: jnp.einsum("dgcp,gh->dgphc", cc.astype(F32), eye).reshape(2, ns, c)
    per_dir = lambda *shape: pl.BlockSpec((None,) + shape, lambda d: (d,) + (0,) * len(shape))
    shared = pl.BlockSpec((c, ns), lambda d: (0, 0))
    return pl.pallas_call(
        _s5_param_kernel,
        grid=(2,),
        in_specs=[per_dir(1, ns)] * 3 + [shared] * 2 + [per_dir(ns, c)] * 2,
        out_specs=[per_dir(c, 2 * ns), per_dir(S5_SUB, c, c), per_dir(2 * ns, c), per_dir(3, S5_SUB, 2 * ns)],
        out_shape=[jax.ShapeDtypeStruct((2, c, 2 * ns), BF16),
                   jax.ShapeDtypeStruct((2, S5_SUB, c, c), BF16),
                   jax.ShapeDtypeStruct((2, 2 * ns, c), BF16),
                   jax.ShapeDtypeStruct((2, 3, S5_SUB, 2 * ns), F32)],
        compiler_params=_cparams(("parallel",), 48),
        name="s5_params",
    )(row(a_re), row(a_im), row(ldt), b_bd(b_re), b_bd(b_im), c_bd(c_re), c_bd(c_im))


def _s5_direction(u_ref, bb_ref, kst_ref, cbd_ref, tab_ref, y_ref, x_ref, xin_ref, xpv_ref, reverse):
    ns, sub = S5_NS, S5_SUB
    nsc = TS // sub
    u = u_ref[...]
    z = _dot(u, bb_ref[...])
    z_re = z[:, :ns].reshape(nsc, sub, ns)
    z_im = z[:, ns:].reshape(nsc, sub, ns)
    w = tab_ref[0]
    w_re, w_im = w[:, :ns][None], w[:, ns:][None]
    group_sum = lambda a: jnp.broadcast_to(jnp.sum(a, axis=1, keepdims=True), a.shape)
    xin_ref[0] = group_sum(w_re * z_re - w_im * z_im)
    xin_ref[1] = group_sum(w_re * z_im + w_im * z_re)
    a = tab_ref[2]
    a_re, a_im = a[:, :ns], a[:, ns:]

    def step(k, carry):
        x_re, x_im = carry
        n = nsc - 1 - k if reverse else k
        xpv_ref[0, n] = x_re
        xpv_ref[1, n] = x_im
        return (a_re * x_re - a_im * x_im + xin_ref[0, n], a_re * x_im + a_im * x_re + xin_ref[1, n])

    x_re, x_im = lax.fori_loop(0, nsc, step, (x_ref[0], x_ref[1]))
    x_ref[0] = x_re
    x_ref[1] = x_im
    o = tab_ref[1]
    o_re, o_im = o[:, :ns][None], o[:, ns:][None]
    p_re = (xpv_ref[0] * o_re - xpv_ref[1] * o_im).reshape(TS, ns)
    p_im = (xpv_ref[0] * o_im + xpv_ref[1] * o_re).reshape(TS, ns)
    y = _dot(jnp.concatenate([p_re, p_im], axis=-1).astype(BF16), cbd_ref[...])
    uf = u.astype(F32)
    pos = lax.broadcasted_iota(jnp.int32, (TS, 1), 0) % sub
    for tau in range(sub):
        if tau == 0:
            shifted = u
        else:
            rolled = pltpu.roll(uf, TS - tau if reverse else tau, axis=0)
            inside = (pos + tau < sub) if reverse else (pos >= tau)
            shifted = jnp.where(inside, rolled, 0.0).astype(BF16)
        y = y + _dot(shifted, kst_ref[tau])
    y_ref[...] = y


def _s5_scan_kernel(uf_ref, ub_ref, bbf, bbb, kstf, kstb, cbdf, cbdb, tabf, tabb, yf_ref, yb_ref,
                    x_ref, xin_ref, xpv_ref):
    @pl.when(pl.program_id(1) == 0)
    def _():
        x_ref[...] = jnp.zeros(x_ref.shape, F32)

    _s5_direction(uf_ref, bbf, kstf, cbdf, tabf, yf_ref, x_ref.at[0], xin_ref.at[0], xpv_ref.at[0], False)
    _s5_direction(ub_ref, bbb, kstb, cbdb, tabb, yb_ref, x_ref.at[1], xin_ref.at[1], xpv_ref.at[1], True)


def _s5_scan(z, bb, kst, cbd, tab, nb, s):
    nt_rows = z.shape[0]
    tps = s // TS
    nlt = nb * tps
    fwd = lambda b, n: jnp.where(n == 0, nlt + b, b * tps + n - 1)
    bwd = lambda b, n: jnp.where(n == 0, nlt + b, b * tps + tps - n)
    ns, c, nsc = S5_NS, S5_GROUPS * S5_GROUP_CH, TS // S5_SUB

    def both(a):
        return [pl.BlockSpec((None,) + a.shape[1:], lambda b, n, d=d: (d,) + (0,) * (a.ndim - 1)) for d in range(2)]

    state = lambda *lead: pltpu.VMEM((2, 2) + lead + (S5_SUB, ns), F32)
    return pl.pallas_call(
        _s5_scan_kernel,
        grid=(nb, tps + 1),
        in_specs=[pl.BlockSpec((TS, c), lambda b, n: (fwd(b, n), C_U // 256)),
                  pl.BlockSpec((TS, c), lambda b, n: (bwd(b, n), C_U // 256))]
                 + both(bb) + both(kst) + both(cbd) + both(tab),
        out_specs=[pl.BlockSpec((TS, c), lambda b, n: (fwd(b, n), 0)),
                   pl.BlockSpec((TS, c), lambda b, n: (bwd(b, n), 0))],
        out_shape=[jax.ShapeDtypeStruct((nt_rows, c), F32)] * 2,
        scratch_shapes=[state(), state(nsc), state(nsc)],
        compiler_params=_cparams(("parallel", "arbitrary"), 56),
        name="s5_scan",
    )(z, z, bb, bb, kst, kst, cbd, cbd, tab, tab)


def _s5_finish_kernel(yf_ref, yb_ref, u_ref, d_ref, w_ref, b_ref, o_ref):
    y = yf_ref[...] + yb_ref[...] + d_ref[...] * u_ref[...].astype(F32)
    y = jax.nn.gelu(y)
    gate = _sigmoid(_dot(y.astype(BF16), w_ref[...]) + b_ref[...])
    o_ref[...] = (y * gate).astype(BF16)


def _s5_finish(yf, yb, z, d_row, glu_w, glu_b_row):
    nt = pl.cdiv(yf.shape[0], TM)
    full = lambda a: pl.BlockSpec(a.shape, lambda i: (0,) * a.ndim)
    return pl.pallas_call(
        _s5_finish_kernel,
        grid=(nt,),
        in_specs=[pl.BlockSpec((TM, 256), lambda i: (i, 0)),
                  pl.BlockSpec((TM, 256), lambda i: (i, 0)),
                  pl.BlockSpec((TM, 256), lambda i: (i, C_U // 256)),
                  full(d_row), full(glu_w), full(glu_b_row)],
        out_specs=pl.BlockSpec((TM, 256), lambda i: (i, 0)),
        out_shape=jax.ShapeDtypeStruct((yf.shape[0], 256), BF16),
        compiler_params=_cparams(("parallel",)),
        name="s5_finish",
    )(yf, yb, z, d_row, glu_w, glu_b_row)


def _merge_kernel(x_ref, y0_ref, y1_ref, y2_ref, y3_ref, gates_ref, wb_ref, wo_ref, g1_ref, gain_ref, o_ref):
    acc = jnp.zeros((TM, D_MODEL), F32)
    for bi, y_ref in enumerate((y0_ref, y1_ref, y2_ref, y3_ref)):
        proj = _dot(y_ref[...], wb_ref[bi])
        acc = acc + gates_ref[:, bi * D_MODEL:(bi + 1) * D_MODEL].astype(F32) * proj
    y = _dot(acc.astype(BF16), wo_ref[...])
    o_ref[...] = x_ref[...] + g1_ref[...] * _rms(y, gain_ref[...])


def _merge(x, ys, gates, wb, wo, g1, gain, layer, tps, nb):
    nt = pl.cdiv(x.shape[0], TM)
    row = lambda i: (jnp.minimum(i // tps, nb), 0, 0)
    tile = lambda w: pl.BlockSpec((TM, w), lambda i: (i, 0))
    return pl.pallas_call(
        _merge_kernel,
        grid=(nt,),
        in_specs=[tile(D_MODEL)] + [tile(BRANCH_W)] * 4 + [tile(4 * D_MODEL),
                  pl.BlockSpec((None, 4, BRANCH_W, D_MODEL), lambda i: (layer, 0, 0, 0)),
                  pl.BlockSpec((None, D_MODEL, D_MODEL), lambda i: (layer, 0, 0)),
                  pl.BlockSpec((None, 1, D_MODEL), row),
                  pl.BlockSpec((1, D_MODEL), lambda i: (0, 0))],
        out_specs=tile(D_MODEL),
        out_shape=jax.ShapeDtypeStruct(x.shape, F32),
        compiler_params=_cparams(("parallel",), 56),
        name="merge",
    )(x, *ys, gates, wb, wo, g1, gain)


def _mlp_kernel(x_ref, sc_ref, sh_ref, g2_ref, gin_ref, gout_ref, w1_ref, w2_ref, o_ref, h_ref, acc_ref):
    j = pl.program_id(1)

    @pl.when(j == 0)
    def _():
        h_ref[...] = _norm_mod(x_ref[...], gin_ref[...], sc_ref[...], sh_ref[...]).astype(BF16)
        acc_ref[...] = jnp.zeros(acc_ref.shape, F32)

    t = jnp.maximum(_dot(h_ref[...], w1_ref[...]), 0.0)
    acc_ref[...] += _dot((t * t).astype(BF16), w2_ref[...])

    @pl.when(j == pl.num_programs(1) - 1)
    def _():
        o_ref[...] = x_ref[...] + g2_ref[...] * _rms(acc_ref[...], gout_ref[...])


def _mlp(x, sc, sh, g2, gin, gout, w1, w2, layer, tps, nb, tf=2048):
    nt = pl.cdiv(x.shape[0], TM)
    row = lambda i, j: (jnp.minimum(i // tps, nb), 0, 0)
    vec = pl.BlockSpec((1, D_MODEL), lambda i, j: (0, 0))
    return pl.pallas_call(
        _mlp_kernel,
        grid=(nt, D_FF // tf),
        in_specs=[pl.BlockSpec((TM, D_MODEL), lambda i, j: (i, 0)),
                  pl.BlockSpec((None, 1, D_MODEL), row), pl.BlockSpec((None, 1, D_MODEL), row),
                  pl.BlockSpec((None, 1, D_MODEL), row), vec, vec,
                  pl.BlockSpec((None, D_MODEL, tf), lambda i, j: (layer, 0, j)),
                  pl.BlockSpec((None, tf, D_MODEL), lambda i, j: (layer, j, 0))],
        out_specs=pl.BlockSpec((TM, D_MODEL), lambda i, j: (i, 0)),
        out_shape=jax.ShapeDtypeStruct(x.shape, F32),
        scratch_shapes=[pltpu.VMEM((TM, D_MODEL), BF16), pltpu.VMEM((TM, D_MODEL), F32)],
        compiler_params=_cparams(("parallel", "arbitrary"), 58),
        name="mlp",
    )(x, sc, sh, g2, gin, gout, w1, w2)


def _pack_w_in(w_in):
    o_na, o_cq, o_ckv, o_gdn, o_z, o_a, o_b, o_u, o_gate = 0, 768, 1024, 1184, 1952, 2208, 2216, 2224, 2480
    cols = lambda lo, n: w_in[:, :, lo:lo + n]
    o_kr = o_ckv + MLA_KV_LORA
    swapped = [cols(o_kr + 8, 8), cols(o_kr, 8), cols(o_kr + 24, 8), cols(o_kr + 16, 8)]
    small = jnp.concatenate(
        [cols(o_gdn, 768), cols(o_na, 768), cols(o_cq, 256),
         cols(o_ckv, 160), cols(o_a, 8), cols(o_b, 8)] + swapped
        + [jnp.zeros(w_in.shape[:2] + (256 - L_KRS - MLA_ROPE,), w_in.dtype), cols(o_z, 256), cols(o_u, 256)], axis=2)
    assert small.shape[2] == ZW
    return small.astype(BF16), w_in[:, :, o_gate:].astype(BF16)


def _mla_weights(w_uq, w_ukv):
    depth = w_uq.shape[0]
    hq = MLA_NOPE + MLA_ROPE
    wq = w_uq.reshape(depth, -1, N_HEADS, hq)
    pad = lambda a, lo, hi: jnp.pad(a, ((0, 0), (0, 0), (0, 0), (lo, hi)))
    wq_ext = pad(wq, 0, 128 - hq).reshape(depth, -1, N_HEADS * 128)
    wq_sw = pad(wq[..., MLA_NOPE:][..., ROPE_SWAP], MLA_NOPE, 128 - hq).reshape(depth, -1, N_HEADS * 128)
    wkv = w_ukv.reshape(depth, -1, N_HEADS, MLA_NOPE + HEAD_W)
    wk = pad(wkv[..., :MLA_NOPE], 0, 128 - MLA_NOPE).reshape(depth, -1, N_HEADS * 128)
    wv = pad(wkv[..., MLA_NOPE:], 0, 128 - HEAD_W).reshape(depth, -1, N_HEADS * 128)
    return [a.astype(BF16) for a in (wq_ext, wq_sw, wk, wv, jnp.swapaxes(wv, 1, 2))]


def _rope_place_mats():
    p1 = np.zeros((256, N_HEADS * 128), np.float32)
    p2 = np.zeros((256, N_HEADS * 128), np.float32)
    for h in range(N_HEADS):
        for r in range(MLA_ROPE):
            p1[L_KR + r, h * 128 + MLA_NOPE + r] = 1.0
            p2[L_KRS + r, h * 128 + MLA_NOPE + r] = 1.0
    return jnp.asarray(p1, BF16), jnp.asarray(p2, BF16)


def _rope_tables(s):
    quarter = MLA_ROPE // 4
    inv_freq = ROPE_BASE ** (-jnp.arange(quarter, dtype=F32) / quarter)
    t = jnp.arange(s)
    ang_r = (t // GRID_W).astype(F32)[:, None] * inv_freq[None, :]
    ang_c = (t % GRID_W).astype(F32)[:, None] * inv_freq[None, :]
    cr, sr, cc, sn = jnp.cos(ang_r), jnp.sin(ang_r), jnp.cos(ang_c), jnp.sin(ang_c)
    cos = jnp.concatenate([jnp.ones((s, MLA_NOPE), F32), cr, cr, cc, cc, jnp.ones((s, 32), F32)], axis=1)
    sin = jnp.concatenate([jnp.zeros((s, MLA_NOPE), F32), -sr, sr, -sn, sn, jnp.zeros((s, 32), F32)], axis=1)
    cos = jnp.concatenate([cos, jnp.ones((TM, 128), F32)], axis=0)
    sin = jnp.concatenate([sin, jnp.zeros((TM, 128), F32)], axis=0)
    return cos, sin


def _head_block_ones():
    r = np.arange(BRANCH_W)
    return jnp.asarray((r[:, None] // HEAD_W == r[None, :] // HEAD_W).astype(np.float32))


def _lane_row(vals, offset, width=256):
    return jnp.zeros((1, width), F32).at[0, offset:offset + vals.shape[0]].set(vals.astype(F32))


def _gdn_mixer(z, conv_w, a_log, dt_bias, norm_w, consts, nb, s, ctx_len):
    alog_row = _lane_row(a_log.reshape(-1), L_A)
    dtb_row = _lane_row(dt_bias.reshape(-1), L_A)
    qk, kk, vk, gcx, btx, gt = _gdn_prep(z, conv_w, alog_row, dtb_row, consts["bseg"], consts["sel"],
                                         s // TS, nb * s // TS)
    of, ob = _gdn_scan(qk, kk, vk, gcx, btx, gt, nb, s)
    return _gdn_finish(of, ob, z, norm_w.astype(F32)[None, :])


def _s5_mixer(z, a_re, a_im, log_dt, b_re, b_im, c_re, c_im, d_skip, glu_w, glu_b, nb, s, ctx_len):
    bb, kst, cbd, tab = _s5_params(a_re, a_im, log_dt, b_re, b_im, c_re, c_im)
    yf, yb = _s5_scan(z, bb, kst, cbd, tab, nb, s)
    return _s5_finish(yf, yb, z, d_skip.astype(F32)[None, :], glu_w.astype(BF16), glu_b.astype(F32)[None, :])


def kernel(x, c, ctx, c_ctx, ada_w, ada_b, norm_gains, w_in, na_rpb, mla_q_norm, mla_kv_norm, mla_w_uq, mla_w_ukv, gdn_conv, gdn_a_log, gdn_dt_bias, gdn_norm, s5_a_re, s5_a_im, s5_log_dt, s5_b_re, s5_b_im, s5_c_re, s5_c_im, s5_d, s5_glu_w, s5_glu_b, w_branch, w_out, mlp_w1, mlp_w2):
    nb, s, d = x.shape
    ctx_len = ctx.shape[1]
    depth = ada_w.shape[0]
    assert d == D_MODEL and ctx_len == TS and nb * ctx_len <= TM and s % TM == 0 and s // GRID_W >= 16
    nl = nb * s
    tps = s // TM

    xs = jnp.concatenate([x.reshape(nl, d), ctx.reshape(nb * ctx_len, d)], axis=0)
    cvec = jnp.zeros((8, d), F32).at[:nb].set(c).at[nb].set(c_ctx)
    mod = _modulation(cvec, ada_w, ada_b)
    mod = mod[:, :nb + 1].reshape(depth, nb + 1, 6, 1, d)

    w_small, w_gates = _pack_w_in(w_in)
    wq_ext, wq_sw, wk_ext, wv, wvt = _mla_weights(mla_w_uq, mla_w_ukv)
    p1, p2 = _rope_place_mats()
    cos_t, sin_t = _rope_tables(s)
    sel = np.zeros((8, 128), np.float32)
    sel[np.arange(8), L_A - 128 + np.arange(8)] = 1.0
    consts = {"bseg": _head_block_ones().astype(BF16), "sel": jnp.asarray(sel)}
    wb = w_branch.astype(BF16)
    wo = w_out.astype(BF16)
    w1 = mlp_w1.astype(BF16)
    w2 = mlp_w2.astype(BF16)
    gains = norm_gains.astype(F32)

    for l in range(depth):
        sh1, sc1, g1, sh2, sc2, g2 = [mod[l, :, i] for i in range(6)]
        z = _inproj(xs, sc1, sh1, gains[l, 0][None], w_small, l, tps, nb, False)
        gates = _inproj(xs, sc1, sh1, gains[l, 0][None], w_gates, l, tps, nb, True)

        y_na = jnp.concatenate([
            _na_latent(z, na_rpb[l], nb, s),
            _ctx_attention(z, z, z, C_NA // 256, C_NA // 256 + 1, C_NA // 256 + 2, 256, 256, nl // TS, nb,
                           HEAD_W ** -0.5, False, "na_ctx")], axis=0)

        qm, km, vm, vt = _mla_prep(z, cos_t, sin_t, mla_q_norm[l].astype(F32)[None],
                                   mla_kv_norm[l].astype(F32)[None], wq_ext[l], wq_sw[l], wk_ext[l], wv[l], wvt[l],
                                   p1, p2, tps, nl // TM)
        y_mla = jnp.concatenate([
            _mla_latent(qm, km, vt, nb, s),
            _ctx_attention(qm, km, vm, 0, 0, 0, N_HEADS * 128, N_HEADS * 128, nl // TS, nb, 1.0, True,
                           "mla_ctx")], axis=0)

        y_gdn = _gdn_mixer(z, gdn_conv[l].astype(F32), gdn_a_log[l], gdn_dt_bias[l], gdn_norm[l], consts,
                           nb, s, ctx_len)
        y_s5 = _s5_mixer(z, s5_a_re[l], s5_a_im[l], s5_log_dt[l], s5_b_re[l], s5_b_im[l], s5_c_re[l], s5_c_im[l],
                         s5_d[l], s5_glu_w[l], s5_glu_b[l], nb, s, ctx_len)

        xs = _merge(xs, (y_na, y_mla, y_gdn, y_s5), gates, wb, wo, g1, gains[l, 1][None], l, tps, nb)
        xs = _mlp(xs, sc2, sh2, g2, gains[l, 2][None], gains[l, 3][None], w1, w2, l, tps, nb)
    return xs[:nl].reshape(nb, s, d)
```

```python
import functools
import math

import numpy as np
import jax
import jax.numpy as jnp
from jax import lax
from jax.experimental import pallas as pl
from jax.experimental.pallas import tpu as pltpu

F32 = jnp.float32
BF16 = jnp.bfloat16
HI = lax.Precision.HIGHEST
EPS = 1e-6

D_MODEL = 1024
GRID_W = 64
NA_WIN_H = 8
NA_WIN_W = 16
N_HEADS = 4
HEAD_W = 64
BRANCH_W = 256
MLA_NOPE = 64
MLA_ROPE = 32
MLA_KV_LORA = 128
ROPE_BASE = 10000.0
GDN_CONV = 4
CHUNK = 64
S5_GROUPS = 16
S5_GROUP_CH = 16
S5_STATE = 64
D_FF = 4 * D_MODEL

TM = 1024
TN = 512
TS = 256
NEG = -1e30

C_GDN = 0
C_NA = 768
C_CQ = 1536
C_CKV = 1792
C_Z = 2048
C_U = 2304
ZW = 2560
L_KR = 128
L_A = 160
L_B = 168
L_KRS = 176
ROPE_SWAP = np.concatenate([np.arange(8, 16), np.arange(0, 8), np.arange(24, 32), np.arange(16, 24)])


def _cparams(sem, vmem_mb=None):
    kw = dict(dimension_semantics=sem)
    if vmem_mb is not None:
        kw["vmem_limit_bytes"] = vmem_mb * 1024 * 1024
    return pltpu.CompilerParams(**kw)


def _dot(a, b, **kw):
    return jnp.dot(a, b, preferred_element_type=F32, **kw)


def _dot_nt(a, b):
    return lax.dot_general(a, b, (((1,), (1,)), ((), ())), preferred_element_type=F32)


def _sigmoid(x):
    return 0.5 * jnp.tanh(0.5 * x) + 0.5


def _silu(x):
    return x * _sigmoid(x)


def _mod_kernel(c_ref, w_ref, b_ref, o_ref):
    c = c_ref[...]
    o_ref[...] = _dot(_silu(c), w_ref[...], precision=HI) + b_ref[...]


def _modulation(cvec, ada_w, ada_b):
    depth, d, n = ada_w.shape
    tn = 1536
    return pl.pallas_call(
        _mod_kernel,
        grid=(depth, n // tn),
        in_specs=[pl.BlockSpec((8, d), lambda l, j: (0, 0)),
                  pl.BlockSpec((None, d, tn), lambda l, j: (l, 0, j)),
                  pl.BlockSpec((None, 1, tn), lambda l, j: (l, 0, j))],
        out_specs=pl.BlockSpec((None, 8, tn), lambda l, j: (l, 0, j)),
        out_shape=jax.ShapeDtypeStruct((depth, 8, n), F32),
        compiler_params=_cparams(("parallel", "parallel"), 40),
        name="modulation",
    )(cvec, ada_w, ada_b.reshape(depth, 1, n))


def _norm_mod(x, gain, sc, sh):
    r = lax.rsqrt(jnp.mean(x * x, axis=-1, keepdims=True) + EPS)
    return (x * r * gain) * (1.0 + sc) + sh


def _inproj_kernel(x_ref, sc_ref, sh_ref, gain_ref, w_ref, o_ref, h_ref, *, gate):
    @pl.when(pl.program_id(1) == 0)
    def _():
        h_ref[...] = _norm_mod(x_ref[...], gain_ref[...], sc_ref[...], sh_ref[...]).astype(BF16)

    acc = _dot(h_ref[...], w_ref[...])
    if gate:
        acc = _sigmoid(acc)
    o_ref[...] = acc.astype(BF16)


def _inproj(x, sc, sh, gain, w, layer, tps, nb, gate):
    nt = pl.cdiv(x.shape[0], TM)
    width = w.shape[-1]
    tn = width // 2
    row = lambda i, j: (jnp.minimum(i // tps, nb), 0, 0)
    return pl.pallas_call(
        functools.partial(_inproj_kernel, gate=gate),
        grid=(nt, width // tn),
        in_specs=[pl.BlockSpec((TM, D_MODEL), lambda i, j: (i, 0)),
                  pl.BlockSpec((None, 1, D_MODEL), row),
                  pl.BlockSpec((None, 1, D_MODEL), row),
                  pl.BlockSpec((1, D_MODEL), lambda i, j: (0, 0)),
                  pl.BlockSpec((None, D_MODEL, tn), lambda i, j: (layer, 0, j))],
        out_specs=pl.BlockSpec((TM, tn), lambda i, j: (i, j)),
        out_shape=jax.ShapeDtypeStruct((x.shape[0], width), BF16),
        scratch_shapes=[pltpu.VMEM((TM, D_MODEL), BF16)],
        compiler_params=_cparams(("parallel", "arbitrary"), 48),
        name="inproj_gates" if gate else "inproj",
    )(x, sc, sh, gain, w)


def _head_lane_mask(width, head_w, h):
    lane = lax.broadcasted_iota(jnp.int32, (1, width), 1)
    return (lane >= h * head_w) & (lane < (h + 1) * head_w)


def _na_build_bias(rpb_ref, bias_ref, r0, kb0, rows_total):
    w = GRID_W
    qc = lax.broadcasted_iota(jnp.int32, (w, 2 * w), 0)
    lane = lax.broadcasted_iota(jnp.int32, (w, 2 * w), 1)
    kc = lane % w
    cs = jnp.clip(qc - NA_WIN_W // 2, 0, w - NA_WIN_W)
    col_ok = (kc >= cs) & (kc < cs + NA_WIN_W)
    left = lane < w
    neg = jnp.full((w, 2 * w), NEG, F32)
    for h in range(N_HEADS):
        t = rpb_ref[h]
        toep = []
        for a in range(2 * NA_WIN_H - 1):
            row = jnp.broadcast_to(t[a:a + 1, :], (w, 2 * w))
            ra = pltpu.roll(row, 2 * w - (NA_WIN_W - 1), axis=1, stride=1, stride_axis=0)
            rb = pltpu.roll(ra, w, axis=1)
            toep.append((jnp.where(col_ok, ra, NEG), jnp.where(col_ok, rb, NEG)))
        for qr in range(8):
            rs = min(max(r0 + qr - NA_WIN_H // 2, 0), rows_total - NA_WIN_H)
            for kp in range(8):
                halves = []
                for side in range(2):
                    kr = kb0 + 2 * kp + side
                    halves.append(toep[kr - (r0 + qr) + NA_WIN_H - 1][side] if rs <= kr < rs + NA_WIN_H else neg)
                bias_ref[h, qr * w:(qr + 1) * w, kp * 2 * w:(kp + 1) * 2 * w] = jnp.where(left, halves[0], halves[1])


def _na_kernel(q_ref, k_ref, v_ref, kc_ref, vc_ref, rpb_ref, o_ref, bias_ref, *, rows_total):
    i = pl.program_id(1)
    last = pl.num_programs(1) - 1

    @pl.when(i == 0)
    def _():
        _na_build_bias(rpb_ref, bias_ref, 0, 0, rows_total)

    @pl.when(i == 1)
    def _():
        _na_build_bias(rpb_ref, bias_ref, 8, 4, rows_total)

    @pl.when(i == last)
    def _():
        _na_build_bias(rpb_ref, bias_ref, rows_total - 8, rows_total - 16, rows_total)

    kb = jnp.clip(2 * i - 1, 0, rows_total // 4 - 4)
    start = pl.multiple_of(kb * (4 * GRID_W), 4 * GRID_W)
    nk = 2 * NA_WIN_H * GRID_W
    q = q_ref[...]
    kw = k_ref[pl.ds(start, nk), :]
    vw = v_ref[pl.ds(start, nk), :]
    kc = kc_ref[...]
    vc = vc_ref[...]
    scale = HEAD_W ** -0.5
    out = jnp.zeros(q.shape, F32)
    for h in range(N_HEADS):
        hm = _head_lane_mask(BRANCH_W, HEAD_W, h)
        qh = jnp.where(hm, q, jnp.zeros_like(q))
        sb = _dot_nt(qh, kw) * scale + bias_ref[h]
        sc = _dot_nt(qh, kc) * scale
        m = jnp.maximum(jnp.max(sb, axis=-1, keepdims=True), jnp.max(sc, axis=-1, keepdims=True))
        pb = jnp.exp(sb - m)
        pc = jnp.exp(sc - m)
        den = jnp.sum(pb, axis=-1, keepdims=True) + jnp.sum(pc, axis=-1, keepdims=True)
        o = _dot(pb.astype(BF16), vw) + _dot(pc.astype(BF16), vc)
        out = jnp.where(hm, o / den, out)
    o_ref[...] = out.astype(BF16)


def _na_latent(z, rpb, nb, s):
    rows_total = s // GRID_W
    qb = 8 * GRID_W
    nq = s // qb
    nl = nb * s
    rpb = jnp.pad(rpb.astype(F32), ((0, 0), (0, 1), (0, 2 * GRID_W - (2 * NA_WIN_W - 1))))
    return pl.pallas_call(
        functools.partial(_na_kernel, rows_total=rows_total),
        grid=(nb, nq),
        in_specs=[pl.BlockSpec((qb, BRANCH_W), lambda b, i: (b * nq + i, C_NA // 256)),
                  pl.BlockSpec((s, BRANCH_W), lambda b, i: (b, C_NA // 256 + 1)),
                  pl.BlockSpec((s, BRANCH_W), lambda b, i: (b, C_NA // 256 + 2)),
                  pl.BlockSpec((TS, BRANCH_W), lambda b, i: (nl // TS + b, C_NA // 256 + 1)),
                  pl.BlockSpec((TS, BRANCH_W), lambda b, i: (nl // TS + b, C_NA // 256 + 2)),
                  pl.BlockSpec(rpb.shape, lambda b, i: (0, 0, 0))],
        out_specs=pl.BlockSpec((qb, BRANCH_W), lambda b, i: (b * nq + i, 0)),
        out_shape=jax.ShapeDtypeStruct((nl, BRANCH_W), BF16),
        scratch_shapes=[pltpu.VMEM((N_HEADS, qb, 2 * qb), F32)],
        compiler_params=_cparams(("parallel", "arbitrary"), 56),
        name="na_latent",
    )(z, z, z, z, z, rpb)


def _ctx_attn_kernel(q_ref, k_ref, v_ref, o_ref, *, scale, base2):
    q = q_ref[...]
    k = k_ref[...]
    v = v_ref[...]
    qw = q.shape[-1]
    vw = v.shape[-1] // N_HEADS
    outs = []
    for h in range(N_HEADS):
        qh = jnp.where(_head_lane_mask(qw, qw // N_HEADS, h), q, jnp.zeros_like(q))
        s = _dot_nt(qh, k) * scale
        m = jnp.max(s, axis=-1, keepdims=True)
        p = jnp.exp2(s - m) if base2 else jnp.exp(s - m)
        den = jnp.sum(p, axis=-1, keepdims=True)
        o = _dot(p.astype(BF16), v)
        outs.append(o[:, h * vw:h * vw + HEAD_W] / den)
    o_ref[...] = jnp.concatenate(outs, axis=-1).astype(BF16)


def _ctx_attention(q, k, v, qcol, kcol, vcol, qw, vw, row0, nb, scale, base2, name):
    return pl.pallas_call(
        functools.partial(_ctx_attn_kernel, scale=scale, base2=base2),
        grid=(nb,),
        in_specs=[pl.BlockSpec((TS, qw), lambda b: (row0 + b, qcol)),
                  pl.BlockSpec((TS, qw), lambda b: (row0 + b, kcol)),
                  pl.BlockSpec((TS, vw), lambda b: (row0 + b, vcol))],
        out_specs=pl.BlockSpec((TS, BRANCH_W), lambda b: (b, 0)),
        out_shape=jax.ShapeDtypeStruct((nb * TS, BRANCH_W), BF16),
        compiler_params=_cparams(("parallel",)),
        name=name,
    )(q, k, v)


def _rms(x, gain):
    return x * lax.rsqrt(jnp.mean(x * x, axis=-1, keepdims=True) + EPS) * gain


def _mla_prep_kernel(cq_ref, ckv_ref, cos_ref, sin_ref, qn_ref, kvn_ref, wq_ref, wqs_ref, wk_ref, wv_ref, wvt_ref,
                     p1_ref, p2_ref, q_ref, k_ref, v_ref, vt_ref):
    cos = jnp.concatenate([cos_ref[...]] * N_HEADS, axis=-1)
    sin = jnp.concatenate([sin_ref[...]] * N_HEADS, axis=-1)
    cqn = _rms(cq_ref[...].astype(F32), qn_ref[...]).astype(BF16)
    scale = (MLA_NOPE + MLA_ROPE) ** -0.5 * math.log2(math.e)
    q = _dot(cqn, wq_ref[...]) * cos + _dot(cqn, wqs_ref[...]) * sin
    q_ref[...] = (q * scale).astype(BF16)
    ckv = ckv_ref[...]
    kvn = _rms(ckv[:, :MLA_KV_LORA].astype(F32), kvn_ref[...]).astype(BF16)
    k = (_dot(kvn, wk_ref[...]) + _dot(ckv, p1_ref[...])) * cos + _dot(ckv, p2_ref[...]) * sin
    k_ref[...] = k.astype(BF16)
    lane = lax.broadcasted_iota(jnp.int32, (1, N_HEADS * 128), 1)
    v_ref[...] = jnp.where(lane % 128 == HEAD_W, 1.0, _dot(kvn, wv_ref[...])).astype(BF16)
    row = lax.broadcasted_iota(jnp.int32, (N_HEADS * 128, 1), 0)
    vt_ref[...] = jnp.where(row % 128 == HEAD_W, 1.0, _dot_nt(wvt_ref[...], kvn)).astype(BF16)


def _mla_prep(z, cos_t, sin_t, qn, kvn, wq, wqs, wk, wv, wvt, p1, p2, tps, n_lat_tiles):
    nt_rows = z.shape[0]
    nt = pl.cdiv(nt_rows, TM)
    full = lambda a: pl.BlockSpec(a.shape, lambda i: (0,) * a.ndim)
    tab = lambda i: (jnp.where(i < n_lat_tiles, i % tps, tps), 0)
    hw = N_HEADS * 128
    return pl.pallas_call(
        _mla_prep_kernel,
        grid=(nt,),
        in_specs=[pl.BlockSpec((TM, 256), lambda i: (i, C_CQ // 256)),
                  pl.BlockSpec((TM, 256), lambda i: (i, C_CKV // 256)),
                  pl.BlockSpec((TM, 128), tab), pl.BlockSpec((TM, 128), tab),
                  full(qn), full(kvn), full(wq), full(wqs), full(wk), full(wv), full(wvt), full(p1), full(p2)],
        out_specs=[pl.BlockSpec((TM, hw), lambda i: (i, 0)),
                   pl.BlockSpec((TM, hw), lambda i: (i, 0)),
                   pl.BlockSpec((TM, hw), lambda i: (i, 0)),
                   pl.BlockSpec((hw, TM), lambda i: (0, i))],
        out_shape=[jax.ShapeDtypeStruct((nt_rows, hw), BF16)] * 3 + [jax.ShapeDtypeStruct((hw, nt_rows), BF16)],
        compiler_params=_cparams(("parallel",)),
        name="mla_prep",
    )(z, z, cos_t, sin_t, qn, kvn, wq, wqs, wk, wv, wvt, p1, p2)


def _flash_kernel(q_ref, kl_ref, vl_ref, kc_ref, vc_ref, o_ref, st_ref, *, tk, n_lat):
    tq = q_ref.shape[0]
    s_len = n_lat * tk
    heads = (slice(0, 128), slice(128, 256))
    qs = [q_ref[:, hs] for hs in heads]
    group_max = lambda st: jnp.max(st.reshape(st.shape[0] // 8, 8, tq), axis=0)

    def score_tile(hh, k, rows, mx):
        st = _dot_nt(k, qs[hh])
        st_ref[hh, rows, :] = st
        return jnp.maximum(mx, group_max(st))

    def max_body(t, carry):
        rows = pl.ds(pl.multiple_of(t * tk, tk), tk)
        return tuple(score_tile(hh, kl_ref[rows, heads[hh]], rows, carry[hh]) for hh in range(2))

    mx = lax.fori_loop(0, n_lat, max_body, tuple(jnp.full((8, tq), NEG, F32) for _ in range(2)), unroll=4)
    ctx_rows = slice(s_len, s_len + kc_ref.shape[0])
    ms = [jnp.max(score_tile(hh, kc_ref[:, heads[hh]], ctx_rows, mx[hh]), axis=0, keepdims=True) for hh in range(2)]

    def weighted(hh, rows, vt):
        return _dot(vt, jnp.exp2(st_ref[hh, rows, :] - ms[hh]).astype(BF16))

    def acc_body(t, carry):
        r0 = pl.multiple_of(t * tk, tk)
        return tuple(carry[hh] + weighted(hh, pl.ds(r0, tk), vl_ref[heads[hh], pl.ds(r0, tk)]) for hh in range(2))

    accs = lax.fori_loop(0, n_lat, acc_body, tuple(jnp.zeros((128, tq), F32) for _ in range(2)), unroll=4)
    outs = []
    for hh in range(2):
        acc = (accs[hh] + weighted(hh, ctx_rows, vc_ref[heads[hh], :])).T
        outs.append(acc[:, :HEAD_W] / acc[:, HEAD_W:HEAD_W + 1])
    o_ref[...] = jnp.concatenate(outs, axis=-1).astype(BF16)


def _mla_latent(qm, km, vt, nb, s, tq=512, tk=1024):
    nq = s // tq
    nl = nb * s
    once = dict(pipeline_mode=pl.Buffered(1))
    return pl.pallas_call(
        functools.partial(_flash_kernel, tk=tk, n_lat=s // tk),
        grid=(nb, 2, nq),
        in_specs=[pl.BlockSpec((tq, 256), lambda b, hp, i: (b * nq + i, hp)),
                  pl.BlockSpec((s, 256), lambda b, hp, i: (b, hp), **once),
                  pl.BlockSpec((256, s), lambda b, hp, i: (hp, b), **once),
                  pl.BlockSpec((TS, 256), lambda b, hp, i: (nl // TS + b, hp)),
                  pl.BlockSpec((256, TS), lambda b, hp, i: (hp, nl // TS + b))],
        out_specs=pl.BlockSpec((tq, 128), lambda b, hp, i: (b * nq + i, hp)),
        out_shape=jax.ShapeDtypeStruct((nl, BRANCH_W), BF16),
        scratch_shapes=[pltpu.VMEM((2, s + TS, tq), F32)],
        compiler_params=_cparams(("parallel", "parallel", "arbitrary"), 56),
        name="mla_flash",
    )(qm, km, vt, km, vt)


def _gdn_prep_kernel(prev_ref, cur_ref, next_ref, ckv_ref, conv_ref, alog_ref, dtb_ref, bseg_ref, sel_ref,
                     qk_ref, kk_ref, vk_ref, gb_ref, gt_ref, *, tps, n_lat_tiles):
    i = pl.program_id(0)
    is_ctx = i >= n_lat_tiles
    first = is_ctx | (i % tps == 0)
    last = is_ctx | (i % tps == tps - 1)
    prev = jnp.where(first, 0.0, prev_ref[...].astype(F32))
    nxt = jnp.where(last, 0.0, next_ref[...].astype(F32))
    ext = jnp.concatenate([prev, cur_ref[...].astype(F32), nxt], axis=0)
    n_ext = TS + 16
    acc = jnp.zeros((TS, 3 * BRANCH_W), F32)
    for j in range(GDN_CONV):
        shifted = pltpu.roll(ext, n_ext - (8 - GDN_CONV // 2 + j), axis=0)[:TS]
        acc = acc + shifted * conv_ref[j:j + 1, :]
    x = _silu(acc)
    bseg = bseg_ref[...]

    def l2n(a):
        sq = a * a
        hi = sq.astype(BF16)
        lo = (sq - hi.astype(F32)).astype(BF16)
        return a * lax.rsqrt(_dot(hi, bseg) + _dot(lo, bseg) + EPS)

    q = l2n(x[:, :BRANCH_W]) * (HEAD_W ** -0.5)
    k = l2n(x[:, BRANCH_W:2 * BRANCH_W])
    v = x[:, 2 * BRANCH_W:]
    for h in range(N_HEADS):
        hs = slice(h * HEAD_W, (h + 1) * HEAD_W)
        qk_ref[h] = jnp.concatenate([q[:, hs], k[:, hs]], axis=-1)
        kk_ref[h] = jnp.concatenate([k[:, hs], k[:, hs]], axis=-1)
        vk_ref[h] = jnp.concatenate([v[:, hs], k[:, hs]], axis=-1)

    ab = ckv_ref[:, 128:].astype(F32)
    la, lb = L_A - 128, L_B - 128
    sp_in = ab + dtb_ref[:, 128:]
    softplus = jnp.maximum(sp_in, 0.0) + jnp.log1p(jnp.exp(-jnp.abs(sp_in)))
    lane = lax.broadcasted_iota(jnp.int32, (1, 128), 1)
    g = jnp.where((lane >= la) & (lane < lb), -jnp.exp(alog_ref[:, 128:]) * softplus, 0.0)
    beta = _sigmoid(ab)
    pos = lax.broadcasted_iota(jnp.int32, (TS, 1), 0) % CHUNK
    pre, suf = g, g
    step = 1
    while step < CHUNK:
        pre = pre + jnp.where(pos >= step, pltpu.roll(pre, step, axis=0), 0.0)
        suf = suf + jnp.where(pos + step < CHUNK, pltpu.roll(suf, TS - step, axis=0), 0.0)
        step *= 2
    gc = jnp.where(lane < la + N_HEADS, pre, suf)
    gt_ref[...] = lax.dot_general(sel_ref[...], gc, (((1,), (1,)), ((), ())), precision=HI,
                                  preferred_element_type=F32)
    gb_ref[...] = jnp.where(lane < lb, gc, beta)


def _gdn_prep(z, conv_w, alog_row, dtb_row, bseg, sel, tps_s, n_lat_tiles):
    nt_rows = z.shape[0]
    nt = nt_rows // TS
    hb = TS // 8
    full = lambda a: pl.BlockSpec(a.shape, lambda i: (0,) * a.ndim)
    hsp = pl.BlockSpec((N_HEADS, TS, 128), lambda i: (0, i, 0))
    return pl.pallas_call(
        functools.partial(_gdn_prep_kernel, tps=tps_s, n_lat_tiles=n_lat_tiles),
        grid=(nt,),
        in_specs=[pl.BlockSpec((8, 768), lambda i: (jnp.maximum(i * hb - 1, 0), 0)),
                  pl.BlockSpec((TS, 768), lambda i: (i, 0)),
                  pl.BlockSpec((8, 768), lambda i: (jnp.minimum((i + 1) * hb, nt * hb - 1), 0)),
                  pl.BlockSpec((TS, 256), lambda i: (i, C_CKV // 256)),
                  full(conv_w), full(alog_row), full(dtb_row), full(bseg), full(sel)],
        out_specs=[hsp] * 3
                  + [pl.BlockSpec((TS, 128), lambda i: (i, 0)), pl.BlockSpec((None, 8, TS), lambda i: (i, 0, 0))],
        out_shape=[jax.ShapeDtypeStruct((N_HEADS, nt_rows, 128), F32)] * 3
                  + [jax.ShapeDtypeStruct((nt_rows, 128), F32), jax.ShapeDtypeStruct((nt, 8, TS), F32)],
        compiler_params=_cparams(("parallel",)),
        name="gdn_prep",
    )(z, z, z, z, conv_w, alog_row, dtb_row, bseg, sel)


def _tri_solve(n, x, reverse):
    h = n.shape[0]
    bs = 8
    nblk = CHUNK // bs
    r = lax.broadcasted_iota(jnp.int32, (CHUNK, CHUNK), 0)
    c = lax.broadcasted_iota(jnp.int32, (CHUNK, CHUNK), 1)
    off = jnp.where((r // bs != c // bs)[None], n, 0.0)
    diag = jnp.stack([n[:, bs * b:bs * (b + 1), bs * b:bs * (b + 1)] for b in range(nblk)], axis=1)

    ws = [a.reshape(h, nblk, bs, a.shape[-1]) for a in (x, off)]
    for j in (range(bs - 1, 0, -1) if reverse else range(bs - 1)):
        col = jnp.broadcast_to(diag[..., j:j + 1], (h, nblk, bs, 2 * HEAD_W))
        ws = [w - col[..., :w.shape[-1]] * w[:, :, j:j + 1, :] for w in ws]
    z, m = [w.reshape(h, CHUNK, w.shape[-1]) for w in ws]

    bmm = lambda a, b: jnp.einsum("hij,hjk->hik", a.astype(BF16), b.astype(BF16), preferred_element_type=F32)
    m2 = bmm(m, m)
    w1 = z - bmm(m, z)
    w2 = w1 + bmm(m2, w1)
    return w2 + bmm(bmm(m2, m2), w2)


def _gdn_chunk_terms(qk_ref, kk_ref, vk_ref, gb_ref, gt_ref, d, reverse):
    c = CHUNK
    nh = N_HEADS
    nc = TS // c
    r = lax.broadcasted_iota(jnp.int32, (c, c), 0)
    cc = lax.broadcasted_iota(jnp.int32, (c, c), 1)
    incl = ((r <= cc) if reverse else (r >= cc))[None]
    strict = ((r < cc) if reverse else (r > cc))[None]
    lo = lax.broadcasted_iota(jnp.int32, (1, 1, 2 * HEAD_W), 2) < HEAD_W
    bmm_nt = lambda a, b: jnp.einsum("hid,hjd->hij", a, b, preferred_element_type=F32)
    chunks = lambda a: a.reshape(nh * nc, c, a.shape[-1])
    qk = chunks(qk_ref[...])
    kk = chunks(kk_ref[...])
    vk = chunks(vk_ref[...])
    gb = gb_ref[...]
    spread = lambda l0: chunks(jnp.stack([jnp.broadcast_to(gb[:, l0 + h:l0 + h + 1], (TS, 2 * HEAD_W))
                                          for h in range(nh)]))
    gc = spread(L_A - 128 + d * nh)
    bt = spread(L_B - 128 + d * nh)
    grow = jnp.stack([gt_ref[d * nh + h:d * nh + h + 1, ci * c:(ci + 1) * c]
                      for h in range(nh) for ci in range(nc)])
    dec = jnp.where(incl, jnp.exp(jnp.minimum(gc[:, :, :c] - grow, 0.0)), 0.0)
    kkb = kk.astype(BF16)
    k_dot_k = 0.5 * bmm_nt(kkb, kkb)
    q_only = jnp.where(lo, qk, 0.0)
    q_dot_k = bmm_nt(q_only.astype(BF16), kkb) * dec
    n = jnp.where(strict, bt[:, :, :c] * k_dot_k * dec, 0.0)
    egc = jnp.exp(gc)
    x = _tri_solve(n, vk * jnp.where(lo, bt, bt * egc), reverse)
    edge = 0 if reverse else c - 1
    glast = gc[:, edge:edge + 1, :]
    ktail = kk * jnp.exp(glast - gc)
    return (jnp.where(lo, 0.0, x).astype(BF16), x[:, :, :HEAD_W], (q_only * egc).astype(BF16), q_dot_k.astype(BF16),
            ktail.astype(BF16), jnp.exp(glast)[:, :, :HEAD_W])


def _gdn_scan_kernel(qkf, qkb, kkf, kkb, vkf, vkb, gbf, gbb, gtf, gtb, of_ref, ob_ref, s_ref):
    @pl.when(pl.program_id(1) == 0)
    def _():
        s_ref[...] = jnp.zeros(s_ref.shape, F32)

    n_chunks = TS // CHUNK
    dirs = ((qkf, kkf, vkf, gbf, gtf, of_ref, False), (qkb, kkb, vkb, gbb, gtb, ob_ref, True))
    order = [range(n_chunks), range(n_chunks - 1, -1, -1)]
    terms = [_gdn_chunk_terms(*dirs[d][:5], d, dirs[d][6]) for d in range(2)]
    pick = lambda a, ci: a.reshape((N_HEADS, n_chunks) + a.shape[1:])[:, ci]
    bmm = lambda a, b: jnp.einsum("hij,hjk->hik", a, b, preferred_element_type=F32)
    bmm_tn = lambda a, b: jnp.einsum("hcd,hce->hde", a, b, preferred_element_type=F32)
    state = [s_ref[0], s_ref[1]]
    for step in range(n_chunks):
        for d in range(2):
            ci = order[d][step]
            k_cum, u, q_dec, q_dot_k, ktail, total = [pick(a, ci) for a in terms[d]]
            s2b = state[d].astype(BF16)
            vb = (u - bmm(k_cum, s2b)).astype(BF16)
            dirs[d][5][:, ci * CHUNK:(ci + 1) * CHUNK, :] = bmm(q_dec, s2b) + bmm(q_dot_k, vb)
            state[d] = state[d] * total + bmm_tn(ktail, vb)
    s_ref[0] = state[0]
    s_ref[1] = state[1]


def _gdn_scan(qk, kk, vk, gb, gt, nb, s):
    h, nt_rows, _ = qk.shape
    tps = s // TS
    nlt = nb * tps
    fwd = lambda b, n: jnp.where(n == 0, nlt + b, b * tps + n - 1)
    bwd = lambda b, n: jnp.where(n == 0, nlt + b, b * tps + tps - n)

    def views(shape, imap):
        return [pl.BlockSpec(shape, functools.partial(imap, t)) for t in (fwd, bwd)]

    heads = views((h, TS, 128), lambda t, b, n: (0, t(b, n), 0))
    lanes = views((TS, 128), lambda t, b, n: (t(b, n), 0))
    rows = views((None, 8, TS), lambda t, b, n: (t(b, n), 0, 0))
    outs = views((h, TS, HEAD_W), lambda t, b, n: (0, t(b, n), 0))
    return pl.pallas_call(
        _gdn_scan_kernel,
        grid=(nb, tps + 1),
        in_specs=heads * 3 + lanes + rows,
        out_specs=outs,
        out_shape=[jax.ShapeDtypeStruct((h, nt_rows, HEAD_W), F32)] * 2,
        scratch_shapes=[pltpu.VMEM((2, h, 2 * HEAD_W, HEAD_W), F32)],
        compiler_params=_cparams(("parallel", "arbitrary")),
        name="gdn_scan",
    )(qk, qk, kk, kk, vk, vk, gb, gb, gt, gt)


def _gdn_finish_kernel(of_ref, ob_ref, z_ref, nw_ref, y_ref):
    o = of_ref[...] + ob_ref[...]
    y = o * lax.rsqrt(jnp.mean(o * o, axis=-1, keepdims=True) + EPS) * nw_ref[...]
    y = jnp.concatenate([y[h] for h in range(N_HEADS)], axis=-1)
    y_ref[...] = (y * _silu(z_ref[...].astype(F32))).astype(BF16)


def _gdn_finish(of, ob, z, nw_row):
    nt = pl.cdiv(of.shape[1], TM)
    hsp = pl.BlockSpec((N_HEADS, TM, HEAD_W), lambda i: (0, i, 0))
    return pl.pallas_call(
        _gdn_finish_kernel,
        grid=(nt,),
        in_specs=[hsp, hsp, pl.BlockSpec((TM, 256), lambda i: (i, C_Z // 256)),
                  pl.BlockSpec(nw_row.shape, lambda i: (0, 0))],
        out_specs=pl.BlockSpec((TM, 256), lambda i: (i, 0)),
        out_shape=jax.ShapeDtypeStruct((of.shape[1], 256), BF16),
        compiler_params=_cparams(("parallel",)),
        name="gdn_finish",
    )(of, ob, z, nw_row)


S5_SUB = 8
S5_NS = S5_GROUPS * S5_STATE


def _s5_param_kernel(are_ref, aim_ref, ldt_ref, bre_ref, bim_ref, cre_ref, cim_ref,
                     bb_ref, kst_ref, cbd_ref, tab_ref):
    d = pl.program_id(0)
    lam_re = jnp.minimum(are_ref[...], -1e-4)
    lam_im = aim_ref[...]
    dt = jnp.exp(ldt_ref[...])

    def power(tau):
        mag = jnp.exp(lam_re * dt * tau)
        ang = lam_im * dt * tau
        return mag * jnp.cos(ang), mag * jnp.sin(ang)

    idx = lax.broadcasted_iota(jnp.int32, (S5_SUB, 1), 0)
    p_re, p_im = power(idx.astype(F32))
    lb_re, lb_im = p_re[1:2], p_im[1:2]
    den = lam_re * lam_re + lam_im * lam_im
    f_re = ((lb_re - 1.0) * lam_re + lb_im * lam_im) / den
    f_im = (lb_im * lam_re - (lb_re - 1.0) * lam_im) / den
    bb_re = f_re * bre_ref[...] - f_im * bim_ref[...]
    bb_im = f_re * bim_ref[...] + f_im * bre_ref[...]
    bb_ref[...] = jnp.concatenate([bb_re, bb_im], axis=-1).astype(BF16)
    c_re = cre_ref[...]
    c_im = cim_ref[...]
    cbd_ref[...] = jnp.concatenate([c_re, -c_im], axis=0).astype(BF16)
    for tau in range(S5_SUB):
        g_re = bb_re * p_re[tau:tau + 1] - bb_im * p_im[tau:tau + 1]
        g_im = bb_re * p_im[tau:tau + 1] + bb_im * p_re[tau:tau + 1]
        kst_ref[tau] = (_dot(g_re, c_re, precision=HI) - _dot(g_im, c_im, precision=HI)).astype(BF16)
    fwd = d == 0
    t_in = jnp.where(fwd, S5_SUB - 1 - idx, idx).astype(F32)
    t_out = jnp.where(fwd, idx + 1, S5_SUB - idx).astype(F32)
    for k, tau in enumerate((t_in, t_out, jnp.full((S5_SUB, 1), float(S5_SUB), F32))):
        tab_ref[k] = jnp.concatenate(power(tau), axis=-1)


def _s5_params(a_re, a_im, log_dt, b_re, b_im, c_re, c_im):
    g, p, gch = S5_GROUPS, S5_STATE, S5_GROUP_CH
    ns, c = S5_NS, g * gch
    eye = jnp.eye(g, dtype=F32)
    row = lambda a: a.astype(F32).reshape(2, 1, ns)
    ldt = jnp.broadcast_to(log_dt.astype(F32)[:, :, None], (2, g, p))
    b_bd = lambda b: jnp.einsum("gpc,gh->gchp", b.astype(F32), eye).reshape(c, ns)
    c_bd = lambda cc: jnp.einsum("dgcp,gh->dgphc", cc.astype(F32), eye).reshape(2, ns, c)
    per_dir = lambda *shape: pl.BlockSpec((None,) + shape, lambda d: (d,) + (0,) * len(shape))
    shared = pl.BlockSpec((c, ns), lambda d: (0, 0))
    return pl.pallas_call(
        _s5_param_kernel,
        grid=(2,),
        in_specs=[per_dir(1, ns)] * 3 + [shared] * 2 + [per_dir(ns, c)] * 2,
        out_specs=[per_dir(c, 2 * ns), per_dir(S5_SUB, c, c), per_dir(2 * ns, c), per_dir(3, S5_SUB, 2 * ns)],
        out_shape=[jax.ShapeDtypeStruct((2, c, 2 * ns), BF16),
                   jax.ShapeDtypeStruct((2, S5_SUB, c, c), BF16),
                   jax.ShapeDtypeStruct((2, 2 * ns, c), BF16),
                   jax.ShapeDtypeStruct((2, 3, S5_SUB, 2 * ns), F32)],
        compiler_params=_cparams(("parallel",), 48),
        name="s5_params",
    )(row(a_re), row(a_im), row(ldt), b_bd(b_re), b_bd(b_im), c_bd(c_re), c_bd(c_im))


def _s5_direction(u_ref, bb_ref, kst_ref, cbd_ref, tab_ref, y_ref, x_ref, xin_ref, xpv_ref, reverse):
    ns, sub = S5_NS, S5_SUB
    nsc = TS // sub
    u = u_ref[...]
    z = _dot(u, bb_ref[...])
    z_re = z[:, :ns].reshape(nsc, sub, ns)
    z_im = z[:, ns:].reshape(nsc, sub, ns)
    w = tab_ref[0]
    w_re, w_im = w[:, :ns][None], w[:, ns:][None]
    group_sum = lambda a: jnp.broadcast_to(jnp.sum(a, axis=1, keepdims=True), a.shape)
    xin_ref[0] = group_sum(w_re * z_re - w_im * z_im)
    xin_ref[1] = group_sum(w_re * z_im + w_im * z_re)
    a = tab_ref[2]
    a_re, a_im = a[:, :ns], a[:, ns:]

    def step(k, carry):
        x_re, x_im = carry
        n = nsc - 1 - k if reverse else k
        xpv_ref[0, n] = x_re
        xpv_ref[1, n] = x_im
        return (a_re * x_re - a_im * x_im + xin_ref[0, n], a_re * x_im + a_im * x_re + xin_ref[1, n])

    x_re, x_im = lax.fori_loop(0, nsc, step, (x_ref[0], x_ref[1]))
    x_ref[0] = x_re
    x_ref[1] = x_im
    o = tab_ref[1]
    o_re, o_im = o[:, :ns][None], o[:, ns:][None]
    p_re = (xpv_ref[0] * o_re - xpv_ref[1] * o_im).reshape(TS, ns)
    p_im = (xpv_ref[0] * o_im + xpv_ref[1] * o_re).reshape(TS, ns)
    y = _dot(jnp.concatenate([p_re, p_im], axis=-1).astype(BF16), cbd_ref[...])
    uf = u.astype(F32)
    pos = lax.broadcasted_iota(jnp.int32, (TS, 1), 0) % sub
    for tau in range(sub):
        if tau == 0:
            shifted = u
        else:
            rolled = pltpu.roll(uf, TS - tau if reverse else tau, axis=0)
            inside = (pos + tau < sub) if reverse else (pos >= tau)
            shifted = jnp.where(inside, rolled, 0.0).astype(BF16)
        y = y + _dot(shifted, kst_ref[tau])
    y_ref[...] = y


def _s5_scan_kernel(uf_ref, ub_ref, bbf, bbb, kstf, kstb, cbdf, cbdb, tabf, tabb, yf_ref, yb_ref,
                    x_ref, xin_ref, xpv_ref):
    @pl.when(pl.program_id(1) == 0)
    def _():
        x_ref[...] = jnp.zeros(x_ref.shape, F32)

    _s5_direction(uf_ref, bbf, kstf, cbdf, tabf, yf_ref, x_ref.at[0], xin_ref.at[0], xpv_ref.at[0], False)
    _s5_direction(ub_ref, bbb, kstb, cbdb, tabb, yb_ref, x_ref.at[1], xin_ref.at[1], xpv_ref.at[1], True)


def _s5_scan(z, bb, kst, cbd, tab, nb, s):
    nt_rows = z.shape[0]
    tps = s // TS
    nlt = nb * tps
    fwd = lambda b, n: jnp.where(n == 0, nlt + b, b * tps + n - 1)
    bwd = lambda b, n: jnp.where(n == 0, nlt + b, b * tps + tps - n)
    ns, c, nsc = S5_NS, S5_GROUPS * S5_GROUP_CH, TS // S5_SUB

    def both(a):
        return [pl.BlockSpec((None,) + a.shape[1:], lambda b, n, d=d: (d,) + (0,) * (a.ndim - 1)) for d in range(2)]

    state = lambda *lead: pltpu.VMEM((2, 2) + lead + (S5_SUB, ns), F32)
    return pl.pallas_call(
        _s5_scan_kernel,
        grid=(nb, tps + 1),
        in_specs=[pl.BlockSpec((TS, c), lambda b, n: (fwd(b, n), C_U // 256)),
                  pl.BlockSpec((TS, c), lambda b, n: (bwd(b, n), C_U // 256))]
                 + both(bb) + both(kst) + both(cbd) + both(tab),
        out_specs=[pl.BlockSpec((TS, c), lambda b, n: (fwd(b, n), 0)),
                   pl.BlockSpec((TS, c), lambda b, n: (bwd(b, n), 0))],
        out_shape=[jax.ShapeDtypeStruct((nt_rows, c), F32)] * 2,
        scratch_shapes=[state(), state(nsc), state(nsc)],
        compiler_params=_cparams(("parallel", "arbitrary"), 56),
        name="s5_scan",
    )(z, z, bb, bb, kst, kst, cbd, cbd, tab, tab)


def _s5_finish_kernel(yf_ref, yb_ref, u_ref, d_ref, w_ref, b_ref, o_ref):
    y = yf_ref[...] + yb_ref[...] + d_ref[...] * u_ref[...].astype(F32)
    y = jax.nn.gelu(y)
    gate = _sigmoid(_dot(y.astype(BF16), w_ref[...]) + b_ref[...])
    o_ref[...] = (y * gate).astype(BF16)


def _s5_finish(yf, yb, z, d_row, glu_w, glu_b_row):
    nt = pl.cdiv(yf.shape[0], TM)
    full = lambda a: pl.BlockSpec(a.shape, lambda i: (0,) * a.ndim)
    return pl.pallas_call(
        _s5_finish_kernel,
        grid=(nt,),
        in_specs=[pl.BlockSpec((TM, 256), lambda i: (i, 0)),
                  pl.BlockSpec((TM, 256), lambda i: (i, 0)),
                  pl.BlockSpec((TM, 256), lambda i: (i, C_U // 256)),
                  full(d_row), full(glu_w), full(glu_b_row)],
        out_specs=pl.BlockSpec((TM, 256), lambda i: (i, 0)),
        out_shape=jax.ShapeDtypeStruct((yf.shape[0], 256), BF16),
        compiler_params=_cparams(("parallel",)),
        name="s5_finish",
    )(yf, yb, z, d_row, glu_w, glu_b_row)


def _merge_kernel(x_ref, y0_ref, y1_ref, y2_ref, y3_ref, gates_ref, wb_ref, wo_ref, g1_ref, gain_ref, o_ref):
    acc = jnp.zeros((TM, D_MODEL), F32)
    for bi, y_ref in enumerate((y0_ref, y1_ref, y2_ref, y3_ref)):
        proj = _dot(y_ref[...], wb_ref[bi])
        acc = acc + gates_ref[:, bi * D_MODEL:(bi + 1) * D_MODEL].astype(F32) * proj
    y = _dot(acc.astype(BF16), wo_ref[...])
    o_ref[...] = x_ref[...] + g1_ref[...] * _rms(y, gain_ref[...])


def _merge(x, ys, gates, wb, wo, g1, gain, layer, tps, nb, n_rows):
    nt = pl.cdiv(n_rows, TM)
    row = lambda i: (jnp.minimum(i // tps, nb), 0, 0)
    tile = lambda w: pl.BlockSpec((TM, w), lambda i: (i, 0))
    return pl.pallas_call(
        _merge_kernel,
        grid=(nt,),
        in_specs=[tile(D_MODEL)] + [tile(BRANCH_W)] * 4 + [tile(4 * D_MODEL),
                  pl.BlockSpec((None, 4, BRANCH_W, D_MODEL), lambda i: (layer, 0, 0, 0)),
                  pl.BlockSpec((None, D_MODEL, D_MODEL), lambda i: (layer, 0, 0)),
                  pl.BlockSpec((None, 1, D_MODEL), row),
                  pl.BlockSpec((1, D_MODEL), lambda i: (0, 0))],
        out_specs=tile(D_MODEL),
        out_shape=jax.ShapeDtypeStruct((n_rows, D_MODEL), F32),
        compiler_params=_cparams(("parallel",), 56),
        name="merge",
    )(x, *ys, gates, wb, wo, g1, gain)


def _mlp_kernel(x_ref, sc_ref, sh_ref, g2_ref, gin_ref, gout_ref, w1_ref, w2_ref, o_ref, h_ref, acc_ref):
    j = pl.program_id(1)

    @pl.when(j == 0)
    def _():
        h_ref[...] = _norm_mod(x_ref[...], gin_ref[...], sc_ref[...], sh_ref[...]).astype(BF16)
        acc_ref[...] = jnp.zeros(acc_ref.shape, F32)

    t = jnp.maximum(_dot(h_ref[...], w1_ref[...]), 0.0)
    acc_ref[...] += _dot((t * t).astype(BF16), w2_ref[...])

    @pl.when(j == pl.num_programs(1) - 1)
    def _():
        o_ref[...] = x_ref[...] + g2_ref[...] * _rms(acc_ref[...], gout_ref[...])


def _mlp(x, sc, sh, g2, gin, gout, w1, w2, layer, tps, nb, tf=2048):
    nt = pl.cdiv(x.shape[0], TM)
    row = lambda i, j: (jnp.minimum(i // tps, nb), 0, 0)
    vec = pl.BlockSpec((1, D_MODEL), lambda i, j: (0, 0))
    return pl.pallas_call(
        _mlp_kernel,
        grid=(nt, D_FF // tf),
        in_specs=[pl.BlockSpec((TM, D_MODEL), lambda i, j: (i, 0)),
                  pl.BlockSpec((None, 1, D_MODEL), row), pl.BlockSpec((None, 1, D_MODEL), row),
                  pl.BlockSpec((None, 1, D_MODEL), row), vec, vec,
                  pl.BlockSpec((None, D_MODEL, tf), lambda i, j: (layer, 0, j)),
                  pl.BlockSpec((None, tf, D_MODEL), lambda i, j: (layer, j, 0))],
        out_specs=pl.BlockSpec((TM, D_MODEL), lambda i, j: (i, 0)),
        out_shape=jax.ShapeDtypeStruct(x.shape, F32),
        scratch_shapes=[pltpu.VMEM((TM, D_MODEL), BF16), pltpu.VMEM((TM, D_MODEL), F32)],
        compiler_params=_cparams(("parallel", "arbitrary"), 58),
        name="mlp",
    )(x, sc, sh, g2, gin, gout, w1, w2)


def _pack_w_in(w_in):
    o_na, o_cq, o_ckv, o_gdn, o_z, o_a, o_b, o_u, o_gate = 0, 768, 1024, 1184, 1952, 2208, 2216, 2224, 2480
    cols = lambda lo, n: w_in[:, :, lo:lo + n]
    o_kr = o_ckv + MLA_KV_LORA
    swapped = [cols(o_kr + 8, 8), cols(o_kr, 8), cols(o_kr + 24, 8), cols(o_kr + 16, 8)]
    small = jnp.concatenate(
        [cols(o_gdn, 768), cols(o_na, 768), cols(o_cq, 256),
         cols(o_ckv, 160), cols(o_a, 8), cols(o_b, 8)] + swapped
        + [jnp.zeros(w_in.shape[:2] + (256 - L_KRS - MLA_ROPE,), w_in.dtype), cols(o_z, 256), cols(o_u, 256)], axis=2)
    assert small.shape[2] == ZW
    return small.astype(BF16), w_in[:, :, o_gate:].astype(BF16)


def _mla_weights(w_uq, w_ukv):
    depth = w_uq.shape[0]
    hq = MLA_NOPE + MLA_ROPE
    wq = w_uq.reshape(depth, -1, N_HEADS, hq)
    pad = lambda a, lo, hi: jnp.pad(a, ((0, 0), (0, 0), (0, 0), (lo, hi)))
    wq_ext = pad(wq, 0, 128 - hq).reshape(depth, -1, N_HEADS * 128)
    wq_sw = pad(wq[..., MLA_NOPE:][..., ROPE_SWAP], MLA_NOPE, 128 - hq).reshape(depth, -1, N_HEADS * 128)
    wkv = w_ukv.reshape(depth, -1, N_HEADS, MLA_NOPE + HEAD_W)
    wk = pad(wkv[..., :MLA_NOPE], 0, 128 - MLA_NOPE).reshape(depth, -1, N_HEADS * 128)
    wv = pad(wkv[..., MLA_NOPE:], 0, 128 - HEAD_W).reshape(depth, -1, N_HEADS * 128)
    return [a.astype(BF16) for a in (wq_ext, wq_sw, wk, wv, jnp.swapaxes(wv, 1, 2))]


def _rope_place_mats():
    p1 = np.zeros((256, N_HEADS * 128), np.float32)
    p2 = np.zeros((256, N_HEADS * 128), np.float32)
    for h in range(N_HEADS):
        for r in range(MLA_ROPE):
            p1[L_KR + r, h * 128 + MLA_NOPE + r] = 1.0
            p2[L_KRS + r, h * 128 + MLA_NOPE + r] = 1.0
    return jnp.asarray(p1, BF16), jnp.asarray(p2, BF16)


def _rope_tables(s):
    quarter = MLA_ROPE // 4
    inv_freq = ROPE_BASE ** (-jnp.arange(quarter, dtype=F32) / quarter)
    t = jnp.arange(s)
    ang_r = (t // GRID_W).astype(F32)[:, None] * inv_freq[None, :]
    ang_c = (t % GRID_W).astype(F32)[:, None] * inv_freq[None, :]
    cr, sr, cc, sn = jnp.cos(ang_r), jnp.sin(ang_r), jnp.cos(ang_c), jnp.sin(ang_c)
    cos = jnp.concatenate([jnp.ones((s, MLA_NOPE), F32), cr, cr, cc, cc, jnp.ones((s, 32), F32)], axis=1)
    sin = jnp.concatenate([jnp.zeros((s, MLA_NOPE), F32), -sr, sr, -sn, sn, jnp.zeros((s, 32), F32)], axis=1)
    cos = jnp.concatenate([cos, jnp.ones((TM, 128), F32)], axis=0)
    sin = jnp.concatenate([sin, jnp.zeros((TM, 128), F32)], axis=0)
    return cos, sin


def _head_block_ones():
    r = np.arange(BRANCH_W)
    return jnp.asarray((r[:, None] // HEAD_W == r[None, :] // HEAD_W).astype(np.float32))


def _lane_row(vals, offset, width=256):
    return jnp.zeros((1, width), F32).at[0, offset:offset + vals.shape[0]].set(vals.astype(F32))


def _gdn_mixer(z, conv_w, a_log, dt_bias, norm_w, consts, nb, s, ctx_len):
    alog_row = _lane_row(a_log.reshape(-1), L_A)
    dtb_row = _lane_row(dt_bias.reshape(-1), L_A)
    qk, kk, vk, gb, gt = _gdn_prep(z, conv_w, alog_row, dtb_row, consts["bseg"], consts["sel"], s // TS, nb * s // TS)
    of, ob = _gdn_scan(qk, kk, vk, gb, gt, nb, s)
    return _gdn_finish(of, ob, z, norm_w.astype(F32)[None, :])


def _s5_mixer(z, a_re, a_im, log_dt, b_re, b_im, c_re, c_im, d_skip, glu_w, glu_b, nb, s, ctx_len):
    bb, kst, cbd, tab = _s5_params(a_re, a_im, log_dt, b_re, b_im, c_re, c_im)
    yf, yb = _s5_scan(z, bb, kst, cbd, tab, nb, s)
    return _s5_finish(yf, yb, z, d_skip.astype(F32)[None, :], glu_w.astype(BF16), glu_b.astype(F32)[None, :])


def kernel(x, c, ctx, c_ctx, ada_w, ada_b, norm_gains, w_in, na_rpb, mla_q_norm, mla_kv_norm, mla_w_uq, mla_w_ukv, gdn_conv, gdn_a_log, gdn_dt_bias, gdn_norm, s5_a_re, s5_a_im, s5_log_dt, s5_b_re, s5_b_im, s5_c_re, s5_c_im, s5_d, s5_glu_w, s5_glu_b, w_branch, w_out, mlp_w1, mlp_w2):
    nb, s, d = x.shape
    ctx_len = ctx.shape[1]
    depth = ada_w.shape[0]
    assert d == D_MODEL and ctx_len == TS and nb * ctx_len <= TM and s % TM == 0 and s // GRID_W >= 16
    nl = nb * s
    tps = s // TM

    xs = jnp.concatenate([x.reshape(nl, d), ctx.reshape(nb * ctx_len, d)], axis=0)
    cvec = jnp.zeros((8, d), F32).at[:nb].set(c).at[nb].set(c_ctx)
    mod = _modulation(cvec, ada_w, ada_b)
    mod = mod[:, :nb + 1].reshape(depth, nb + 1, 6, 1, d)

    w_small, w_gates = _pack_w_in(w_in)
    wq_ext, wq_sw, wk_ext, wv, wvt = _mla_weights(mla_w_uq, mla_w_ukv)
    p1, p2 = _rope_place_mats()
    cos_t, sin_t = _rope_tables(s)
    sel = np.zeros((8, 128), np.float32)
    sel[np.arange(8), L_A - 128 + np.arange(8)] = 1.0
    consts = {"bseg": _head_block_ones().astype(BF16), "sel": jnp.asarray(sel)}
    wb = w_branch.astype(BF16)
    wo = w_out.astype(BF16)
    w1 = mlp_w1.astype(BF16)
    w2 = mlp_w2.astype(BF16)
    gains = norm_gains.astype(F32)

    for l in range(depth):
        sh1, sc1, g1, sh2, sc2, g2 = [mod[l, :, i] for i in range(6)]
        z = _inproj(xs, sc1, sh1, gains[l, 0][None], w_small, l, tps, nb, False)
        gates = _inproj(xs, sc1, sh1, gains[l, 0][None], w_gates, l, tps, nb, True)

        need_ctx = l < depth - 1
        y_na = _na_latent(z, na_rpb[l], nb, s)
        qm, km, vm, vt = _mla_prep(z, cos_t, sin_t, mla_q_norm[l].astype(F32)[None],
                                   mla_kv_norm[l].astype(F32)[None], wq_ext[l], wq_sw[l], wk_ext[l], wv[l], wvt[l],
                                   p1, p2, tps, nl // TM)
        y_mla = _mla_latent(qm, km, vt, nb, s)
        if need_ctx:
            y_na = jnp.concatenate([y_na, _ctx_attention(
                z, z, z, C_NA // 256, C_NA // 256 + 1, C_NA // 256 + 2, 256, 256, nl // TS, nb,
                HEAD_W ** -0.5, False, "na_ctx")], axis=0)
            y_mla = jnp.concatenate([y_mla, _ctx_attention(
                qm, km, vm, 0, 0, 0, N_HEADS * 128, N_HEADS * 128, nl // TS, nb, 1.0, True, "mla_ctx")], axis=0)

        y_gdn = _gdn_mixer(z, gdn_conv[l].astype(F32), gdn_a_log[l], gdn_dt_bias[l], gdn_norm[l], consts,
                           nb, s, ctx_len)
        y_s5 = _s5_mixer(z, s5_a_re[l], s5_a_im[l], s5_log_dt[l], s5_b_re[l], s5_b_im[l], s5_c_re[l], s5_c_im[l],
                         s5_d[l], s5_glu_w[l], s5_glu_b[l], nb, s, ctx_len)

        xs = _merge(xs, (y_na, y_mla, y_gdn, y_s5), gates, wb, wo, g1, gains[l, 1][None], l, tps, nb,
                    xs.shape[0] if need_ctx else nl)
        xs = _mlp(xs, sc2, sh2, g2, gains[l, 2][None], gains[l, 3][None], w1, w2, l, tps, nb)
    return xs[:nl].reshape(nb, s, d)
```

```python
import functools
import math

import numpy as np
import jax
import jax.numpy as jnp
from jax import lax
from jax.experimental import pallas as pl
from jax.experimental.pallas import tpu as pltpu

F32 = jnp.float32
BF16 = jnp.bfloat16
HI = lax.Precision.HIGHEST
EPS = 1e-6

D_MODEL = 1024
GRID_W = 64
NA_WIN_H = 8
NA_WIN_W = 16
N_HEADS = 4
HEAD_W = 64
BRANCH_W = 256
MLA_NOPE = 64
MLA_ROPE = 32
MLA_KV_LORA = 128
ROPE_BASE = 10000.0
GDN_CONV = 4
CHUNK = 64
S5_GROUPS = 16
S5_GROUP_CH = 16
S5_STATE = 64
D_FF = 4 * D_MODEL

TM = 1024
TN = 512
TS = 256
NEG = -1e30

C_GDN = 0
C_NA = 768
C_CQ = 1536
C_CKV = 1792
C_Z = 2048
C_U = 2304
ZW = 2560
L_KR = 128
L_A = 160
L_B = 168
L_KRS = 176
ROPE_SWAP = np.concatenate([np.arange(8, 16), np.arange(0, 8), np.arange(24, 32), np.arange(16, 24)])


def _cparams(sem, vmem_mb=None):
    kw = dict(dimension_semantics=sem)
    if vmem_mb is not None:
        kw["vmem_limit_bytes"] = vmem_mb * 1024 * 1024
    return pltpu.CompilerParams(**kw)


def _dot(a, b, **kw):
    return jnp.dot(a, b, preferred_element_type=F32, **kw)


def _dot_nt(a, b):
    return lax.dot_general(a, b, (((1,), (1,)), ((), ())), preferred_element_type=F32)


def _sigmoid(x):
    return 0.5 * jnp.tanh(0.5 * x) + 0.5


def _silu(x):
    return x * _sigmoid(x)


def _mod_kernel(c_ref, w_ref, b_ref, o_ref):
    c = c_ref[...]
    o_ref[...] = _dot(_silu(c), w_ref[...], precision=HI) + b_ref[...]


def _modulation(cvec, ada_w, ada_b):
    depth, d, n = ada_w.shape
    tn = 1536
    return pl.pallas_call(
        _mod_kernel,
        grid=(depth, n // tn),
        in_specs=[pl.BlockSpec((8, d), lambda l, j: (0, 0)),
                  pl.BlockSpec((None, d, tn), lambda l, j: (l, 0, j)),
                  pl.BlockSpec((None, 1, tn), lambda l, j: (l, 0, j))],
        out_specs=pl.BlockSpec((None, 8, tn), lambda l, j: (l, 0, j)),
        out_shape=jax.ShapeDtypeStruct((depth, 8, n), F32),
        compiler_params=_cparams(("parallel", "parallel"), 40),
        name="modulation",
    )(cvec, ada_w, ada_b.reshape(depth, 1, n))


def _norm_mod(x, gain, sc, sh):
    r = lax.rsqrt(jnp.mean(x * x, axis=-1, keepdims=True) + EPS)
    return (x * r * gain) * (1.0 + sc) + sh


def _inproj_kernel(x_ref, sc_ref, sh_ref, gain_ref, w_ref, o_ref, h_ref, *, gate):
    @pl.when(pl.program_id(1) == 0)
    def _():
        h_ref[...] = _norm_mod(x_ref[...], gain_ref[...], sc_ref[...], sh_ref[...]).astype(BF16)

    acc = _dot(h_ref[...], w_ref[...])
    if gate:
        acc = _sigmoid(acc)
    o_ref[...] = acc.astype(BF16)


def _inproj(x, sc, sh, gain, w, layer, tps, nb, gate):
    nt = pl.cdiv(x.shape[0], TM)
    width = w.shape[-1]
    tn = width // 2
    row = lambda i, j: (jnp.minimum(i // tps, nb), 0, 0)
    return pl.pallas_call(
        functools.partial(_inproj_kernel, gate=gate),
        grid=(nt, width // tn),
        in_specs=[pl.BlockSpec((TM, D_MODEL), lambda i, j: (i, 0)),
                  pl.BlockSpec((None, 1, D_MODEL), row),
                  pl.BlockSpec((None, 1, D_MODEL), row),
                  pl.BlockSpec((1, D_MODEL), lambda i, j: (0, 0)),
                  pl.BlockSpec((None, D_MODEL, tn), lambda i, j: (layer, 0, j))],
        out_specs=pl.BlockSpec((TM, tn), lambda i, j: (i, j)),
        out_shape=jax.ShapeDtypeStruct((x.shape[0], width), BF16),
        scratch_shapes=[pltpu.VMEM((TM, D_MODEL), BF16)],
        compiler_params=_cparams(("parallel", "arbitrary"), 48),
        name="inproj_gates" if gate else "inproj",
    )(x, sc, sh, gain, w)


def _head_lane_mask(width, head_w, h):
    lane = lax.broadcasted_iota(jnp.int32, (1, width), 1)
    return (lane >= h * head_w) & (lane < (h + 1) * head_w)


def _na_build_bias(rpb_ref, bias_ref, r0, kb0, rows_total):
    w = GRID_W
    qc = lax.broadcasted_iota(jnp.int32, (w, 2 * w), 0)
    lane = lax.broadcasted_iota(jnp.int32, (w, 2 * w), 1)
    kc = lane % w
    cs = jnp.clip(qc - NA_WIN_W // 2, 0, w - NA_WIN_W)
    col_ok = (kc >= cs) & (kc < cs + NA_WIN_W)
    left = lane < w
    neg = jnp.full((w, 2 * w), NEG, F32)
    for h in range(N_HEADS):
        t = rpb_ref[h]
        toep = []
        for a in range(2 * NA_WIN_H - 1):
            row = jnp.broadcast_to(t[a:a + 1, :], (w, 2 * w))
            ra = pltpu.roll(row, 2 * w - (NA_WIN_W - 1), axis=1, stride=1, stride_axis=0)
            rb = pltpu.roll(ra, w, axis=1)
            toep.append((jnp.where(col_ok, ra, NEG), jnp.where(col_ok, rb, NEG)))
        for qr in range(8):
            rs = min(max(r0 + qr - NA_WIN_H // 2, 0), rows_total - NA_WIN_H)
            for kp in range(8):
                halves = []
                for side in range(2):
                    kr = kb0 + 2 * kp + side
                    halves.append(toep[kr - (r0 + qr) + NA_WIN_H - 1][side] if rs <= kr < rs + NA_WIN_H else neg)
                bias_ref[h, qr * w:(qr + 1) * w, kp * 2 * w:(kp + 1) * 2 * w] = jnp.where(left, halves[0], halves[1])


def _na_kernel(q_ref, k_ref, v_ref, kc_ref, vc_ref, rpb_ref, o_ref, bias_ref, *, rows_total):
    i = pl.program_id(1)
    last = pl.num_programs(1) - 1

    @pl.when(i == 0)
    def _():
        _na_build_bias(rpb_ref, bias_ref, 0, 0, rows_total)

    @pl.when(i == 1)
    def _():
        _na_build_bias(rpb_ref, bias_ref, 8, 4, rows_total)

    @pl.when(i == last)
    def _():
        _na_build_bias(rpb_ref, bias_ref, rows_total - 8, rows_total - 16, rows_total)

    kb = jnp.clip(2 * i - 1, 0, rows_total // 4 - 4)
    start = pl.multiple_of(kb * (4 * GRID_W), 4 * GRID_W)
    nk = 2 * NA_WIN_H * GRID_W
    q = q_ref[...]
    kw = k_ref[pl.ds(start, nk), :]
    vw = v_ref[pl.ds(start, nk), :]
    kc = kc_ref[...]
    vc = vc_ref[...]
    scale = HEAD_W ** -0.5
    out = jnp.zeros(q.shape, F32)
    for h in range(N_HEADS):
        hm = _head_lane_mask(BRANCH_W, HEAD_W, h)
        qh = jnp.where(hm, q, jnp.zeros_like(q))
        sb = _dot_nt(qh, kw) * scale + bias_ref[h]
        sc = _dot_nt(qh, kc) * scale
        m = jnp.maximum(jnp.max(sb, axis=-1, keepdims=True), jnp.max(sc, axis=-1, keepdims=True))
        pb = jnp.exp(sb - m)
        pc = jnp.exp(sc - m)
        den = jnp.sum(pb, axis=-1, keepdims=True) + jnp.sum(pc, axis=-1, keepdims=True)
        o = _dot(pb.astype(BF16), vw) + _dot(pc.astype(BF16), vc)
        out = jnp.where(hm, o / den, out)
    o_ref[...] = out.astype(BF16)


def _na_latent(z, rpb, nb, s, out_rows):
    rows_total = s // GRID_W
    qb = 8 * GRID_W
    nq = s // qb
    nl = nb * s
    rpb = jnp.pad(rpb.astype(F32), ((0, 0), (0, 1), (0, 2 * GRID_W - (2 * NA_WIN_W - 1))))
    return pl.pallas_call(
        functools.partial(_na_kernel, rows_total=rows_total),
        grid=(nb, nq),
        in_specs=[pl.BlockSpec((qb, BRANCH_W), lambda b, i: (b * nq + i, C_NA // 256)),
                  pl.BlockSpec((s, BRANCH_W), lambda b, i: (b, C_NA // 256 + 1)),
                  pl.BlockSpec((s, BRANCH_W), lambda b, i: (b, C_NA // 256 + 2)),
                  pl.BlockSpec((TS, BRANCH_W), lambda b, i: (nl // TS + b, C_NA // 256 + 1)),
                  pl.BlockSpec((TS, BRANCH_W), lambda b, i: (nl // TS + b, C_NA // 256 + 2)),
                  pl.BlockSpec(rpb.shape, lambda b, i: (0, 0, 0))],
        out_specs=pl.BlockSpec((qb, BRANCH_W), lambda b, i: (b * nq + i, 0)),
        out_shape=jax.ShapeDtypeStruct((out_rows, BRANCH_W), BF16),
        scratch_shapes=[pltpu.VMEM((N_HEADS, qb, 2 * qb), F32)],
        compiler_params=_cparams(("parallel", "arbitrary"), 56),
        name="na_latent",
    )(z, z, z, z, z, rpb)


def _ctx_attn_kernel(q_ref, k_ref, v_ref, y_hbm_ref, o_ref, *, scale, base2):
    del y_hbm_ref
    q = q_ref[...]
    k = k_ref[...]
    v = v_ref[...]
    qw = q.shape[-1]
    vw = v.shape[-1] // N_HEADS
    outs = []
    for h in range(N_HEADS):
        qh = jnp.where(_head_lane_mask(qw, qw // N_HEADS, h), q, jnp.zeros_like(q))
        s = _dot_nt(qh, k) * scale
        m = jnp.max(s, axis=-1, keepdims=True)
        p = jnp.exp2(s - m) if base2 else jnp.exp(s - m)
        den = jnp.sum(p, axis=-1, keepdims=True)
        o = _dot(p.astype(BF16), v)
        outs.append(o[:, h * vw:h * vw + HEAD_W] / den)
    o_ref[...] = jnp.concatenate(outs, axis=-1).astype(BF16)


def _ctx_attention(q, k, v, y, qcol, kcol, vcol, qw, vw, row0, nb, scale, base2, name):
    return pl.pallas_call(
        functools.partial(_ctx_attn_kernel, scale=scale, base2=base2),
        grid=(nb,),
        in_specs=[pl.BlockSpec((TS, qw), lambda b: (row0 + b, qcol)),
                  pl.BlockSpec((TS, qw), lambda b: (row0 + b, kcol)),
                  pl.BlockSpec((TS, vw), lambda b: (row0 + b, vcol)),
                  pl.BlockSpec(memory_space=pl.ANY)],
        out_specs=pl.BlockSpec((TS, BRANCH_W), lambda b: (row0 + b, 0)),
        out_shape=jax.ShapeDtypeStruct(y.shape, y.dtype),
        input_output_aliases={3: 0},
        compiler_params=_cparams(("parallel",)),
        name=name,
    )(q, k, v, y)


def _rms(x, gain):
    return x * lax.rsqrt(jnp.mean(x * x, axis=-1, keepdims=True) + EPS) * gain


def _mla_prep_kernel(cq_ref, ckv_ref, cos_ref, sin_ref, qn_ref, kvn_ref, wq_ref, wqs_ref, wk_ref, wv_ref, wvt_ref,
                     p1_ref, p2_ref, q_ref, k_ref, v_ref, vt_ref):
    cos = jnp.concatenate([cos_ref[...]] * N_HEADS, axis=-1)
    sin = jnp.concatenate([sin_ref[...]] * N_HEADS, axis=-1)
    cqn = _rms(cq_ref[...].astype(F32), qn_ref[...]).astype(BF16)
    scale = (MLA_NOPE + MLA_ROPE) ** -0.5 * math.log2(math.e)
    q = _dot(cqn, wq_ref[...]) * cos + _dot(cqn, wqs_ref[...]) * sin
    q_ref[...] = (q * scale).astype(BF16)
    ckv = ckv_ref[...]
    kvn = _rms(ckv[:, :MLA_KV_LORA].astype(F32), kvn_ref[...]).astype(BF16)
    k = (_dot(kvn, wk_ref[...]) + _dot(ckv, p1_ref[...])) * cos + _dot(ckv, p2_ref[...]) * sin
    k_ref[...] = k.astype(BF16)
    lane = lax.broadcasted_iota(jnp.int32, (1, N_HEADS * 128), 1)
    v_ref[...] = jnp.where(lane % 128 == HEAD_W, 1.0, _dot(kvn, wv_ref[...])).astype(BF16)
    row = lax.broadcasted_iota(jnp.int32, (N_HEADS * 128, 1), 0)
    vt_ref[...] = jnp.where(row % 128 == HEAD_W, 1.0, _dot_nt(wvt_ref[...], kvn)).astype(BF16)


def _mla_prep(z, cos_t, sin_t, qn, kvn, wq, wqs, wk, wv, wvt, p1, p2, tps, n_lat_tiles):
    nt_rows = z.shape[0]
    nt = pl.cdiv(nt_rows, TM)
    full = lambda a: pl.BlockSpec(a.shape, lambda i: (0,) * a.ndim)
    tab = lambda i: (jnp.where(i < n_lat_tiles, i % tps, tps), 0)
    hw = N_HEADS * 128
    return pl.pallas_call(
        _mla_prep_kernel,
        grid=(nt,),
        in_specs=[pl.BlockSpec((TM, 256), lambda i: (i, C_CQ // 256)),
                  pl.BlockSpec((TM, 256), lambda i: (i, C_CKV // 256)),
                  pl.BlockSpec((TM, 128), tab), pl.BlockSpec((TM, 128), tab),
                  full(qn), full(kvn), full(wq), full(wqs), full(wk), full(wv), full(wvt), full(p1), full(p2)],
        out_specs=[pl.BlockSpec((TM, hw), lambda i: (i, 0)),
                   pl.BlockSpec((TM, hw), lambda i: (i, 0)),
                   pl.BlockSpec((TM, hw), lambda i: (i, 0)),
                   pl.BlockSpec((hw, TM), lambda i: (0, i))],
        out_shape=[jax.ShapeDtypeStruct((nt_rows, hw), BF16)] * 3 + [jax.ShapeDtypeStruct((hw, nt_rows), BF16)],
        compiler_params=_cparams(("parallel",)),
        name="mla_prep",
    )(z, z, cos_t, sin_t, qn, kvn, wq, wqs, wk, wv, wvt, p1, p2)


def _flash_kernel(q_ref, kl_ref, vl_ref, kc_ref, vc_ref, o_ref, st_ref, *, tk, n_lat):
    tq = q_ref.shape[0]
    s_len = n_lat * tk
    heads = (slice(0, 128), slice(128, 256))
    qs = [q_ref[:, hs] for hs in heads]
    group_max = lambda st: jnp.max(st.reshape(st.shape[0] // 8, 8, tq), axis=0)

    def score_tile(hh, k, rows, mx):
        st = _dot_nt(k, qs[hh])
        st_ref[hh, rows, :] = st
        return jnp.maximum(mx, group_max(st))

    def max_body(t, carry):
        rows = pl.ds(pl.multiple_of(t * tk, tk), tk)
        return tuple(score_tile(hh, kl_ref[rows, heads[hh]], rows, carry[hh]) for hh in range(2))

    mx = lax.fori_loop(0, n_lat, max_body, tuple(jnp.full((8, tq), NEG, F32) for _ in range(2)), unroll=4)
    ctx_rows = slice(s_len, s_len + kc_ref.shape[0])
    ms = [jnp.max(score_tile(hh, kc_ref[:, heads[hh]], ctx_rows, mx[hh]), axis=0, keepdims=True) for hh in range(2)]

    def weighted(hh, rows, vt):
        return _dot(vt, jnp.exp2(st_ref[hh, rows, :] - ms[hh]).astype(BF16))

    def acc_body(t, carry):
        r0 = pl.multiple_of(t * tk, tk)
        return tuple(carry[hh] + weighted(hh, pl.ds(r0, tk), vl_ref[heads[hh], pl.ds(r0, tk)]) for hh in range(2))

    accs = lax.fori_loop(0, n_lat, acc_body, tuple(jnp.zeros((128, tq), F32) for _ in range(2)), unroll=4)
    outs = []
    for hh in range(2):
        acc = (accs[hh] + weighted(hh, ctx_rows, vc_ref[heads[hh], :])).T
        outs.append(acc[:, :HEAD_W] / acc[:, HEAD_W:HEAD_W + 1])
    o_ref[...] = jnp.concatenate(outs, axis=-1).astype(BF16)


def _mla_latent(qm, km, vt, nb, s, out_rows, tq=512, tk=1024):
    nq = s // tq
    nl = nb * s
    once = dict(pipeline_mode=pl.Buffered(1))
    return pl.pallas_call(
        functools.partial(_flash_kernel, tk=tk, n_lat=s // tk),
        grid=(nb, 2, nq),
        in_specs=[pl.BlockSpec((tq, 256), lambda b, hp, i: (b * nq + i, hp)),
                  pl.BlockSpec((s, 256), lambda b, hp, i: (b, hp), **once),
                  pl.BlockSpec((256, s), lambda b, hp, i: (hp, b), **once),
                  pl.BlockSpec((TS, 256), lambda b, hp, i: (nl // TS + b, hp)),
                  pl.BlockSpec((256, TS), lambda b, hp, i: (hp, nl // TS + b))],
        out_specs=pl.BlockSpec((tq, 128), lambda b, hp, i: (b * nq + i, hp)),
        out_shape=jax.ShapeDtypeStruct((out_rows, BRANCH_W), BF16),
        scratch_shapes=[pltpu.VMEM((2, s + TS, tq), F32)],
        compiler_params=_cparams(("parallel", "parallel", "arbitrary"), 56),
        name="mla_flash",
    )(qm, km, vt, km, vt)


def _gdn_prep_kernel(prev_ref, cur_ref, next_ref, ckv_ref, conv_ref, alog_ref, dtb_ref, bseg_ref, sel_ref,
                     qk_ref, kk_ref, vk_ref, gb_ref, gt_ref, *, tps, n_lat_tiles):
    i = pl.program_id(0)
    is_ctx = i >= n_lat_tiles
    first = is_ctx | (i % tps == 0)
    last = is_ctx | (i % tps == tps - 1)
    prev = jnp.where(first, 0.0, prev_ref[...].astype(F32))
    nxt = jnp.where(last, 0.0, next_ref[...].astype(F32))
    ext = jnp.concatenate([prev, cur_ref[...].astype(F32), nxt], axis=0)
    n_ext = TS + 16
    acc = jnp.zeros((TS, 3 * BRANCH_W), F32)
    for j in range(GDN_CONV):
        shifted = pltpu.roll(ext, n_ext - (8 - GDN_CONV // 2 + j), axis=0)[:TS]
        acc = acc + shifted * conv_ref[j:j + 1, :]
    x = _silu(acc)
    bseg = bseg_ref[...]

    def l2n(a):
        sq = a * a
        hi = sq.astype(BF16)
        lo = (sq - hi.astype(F32)).astype(BF16)
        return a * lax.rsqrt(_dot(hi, bseg) + _dot(lo, bseg) + EPS)

    q = l2n(x[:, :BRANCH_W]) * (HEAD_W ** -0.5)
    k = l2n(x[:, BRANCH_W:2 * BRANCH_W])
    v = x[:, 2 * BRANCH_W:]
    for h in range(N_HEADS):
        hs = slice(h * HEAD_W, (h + 1) * HEAD_W)
        qk_ref[h] = jnp.concatenate([q[:, hs], k[:, hs]], axis=-1)
        kk_ref[h] = jnp.concatenate([k[:, hs], k[:, hs]], axis=-1)
        vk_ref[h] = jnp.concatenate([v[:, hs], k[:, hs]], axis=-1)

    ab = ckv_ref[:, 128:].astype(F32)
    la, lb = L_A - 128, L_B - 128
    sp_in = ab + dtb_ref[:, 128:]
    softplus = jnp.maximum(sp_in, 0.0) + jnp.log1p(jnp.exp(-jnp.abs(sp_in)))
    lane = lax.broadcasted_iota(jnp.int32, (1, 128), 1)
    g = jnp.where((lane >= la) & (lane < lb), -jnp.exp(alog_ref[:, 128:]) * softplus, 0.0)
    beta = _sigmoid(ab)
    pos = lax.broadcasted_iota(jnp.int32, (TS, 1), 0) % CHUNK
    pre, suf = g, g
    step = 1
    while step < CHUNK:
        pre = pre + jnp.where(pos >= step, pltpu.roll(pre, step, axis=0), 0.0)
        suf = suf + jnp.where(pos + step < CHUNK, pltpu.roll(suf, TS - step, axis=0), 0.0)
        step *= 2
    gc = jnp.where(lane < la + N_HEADS, pre, suf)
    gt_ref[...] = lax.dot_general(sel_ref[...], gc, (((1,), (1,)), ((), ())), precision=HI,
                                  preferred_element_type=F32)
    gb_ref[...] = jnp.where(lane < lb, gc, beta)


def _gdn_prep(z, conv_w, alog_row, dtb_row, bseg, sel, tps_s, n_lat_tiles):
    nt_rows = z.shape[0]
    nt = nt_rows // TS
    hb = TS // 8
    full = lambda a: pl.BlockSpec(a.shape, lambda i: (0,) * a.ndim)
    hsp = pl.BlockSpec((N_HEADS, TS, 128), lambda i: (0, i, 0))
    return pl.pallas_call(
        functools.partial(_gdn_prep_kernel, tps=tps_s, n_lat_tiles=n_lat_tiles),
        grid=(nt,),
        in_specs=[pl.BlockSpec((8, 768), lambda i: (jnp.maximum(i * hb - 1, 0), 0)),
                  pl.BlockSpec((TS, 768), lambda i: (i, 0)),
                  pl.BlockSpec((8, 768), lambda i: (jnp.minimum((i + 1) * hb, nt * hb - 1), 0)),
                  pl.BlockSpec((TS, 256), lambda i: (i, C_CKV // 256)),
                  full(conv_w), full(alog_row), full(dtb_row), full(bseg), full(sel)],
        out_specs=[hsp] * 3
                  + [pl.BlockSpec((TS, 128), lambda i: (i, 0)), pl.BlockSpec((None, 8, TS), lambda i: (i, 0, 0))],
        out_shape=[jax.ShapeDtypeStruct((N_HEADS, nt_rows, 128), F32)] * 3
                  + [jax.ShapeDtypeStruct((nt_rows, 128), F32), jax.ShapeDtypeStruct((nt, 8, TS), F32)],
        compiler_params=_cparams(("parallel",)),
        name="gdn_prep",
    )(z, z, z, z, conv_w, alog_row, dtb_row, bseg, sel)


def _tri_solve(n, x, reverse):
    h = n.shape[0]
    bs = 8
    nblk = CHUNK // bs
    r = lax.broadcasted_iota(jnp.int32, (CHUNK, CHUNK), 0)
    c = lax.broadcasted_iota(jnp.int32, (CHUNK, CHUNK), 1)
    off = jnp.where((r // bs != c // bs)[None], n, 0.0)
    diag = jnp.stack([n[:, bs * b:bs * (b + 1), bs * b:bs * (b + 1)] for b in range(nblk)], axis=1)

    ws = [a.reshape(h, nblk, bs, a.shape[-1]) for a in (x, off)]
    for j in (range(bs - 1, 0, -1) if reverse else range(bs - 1)):
        col = jnp.broadcast_to(diag[..., j:j + 1], (h, nblk, bs, 2 * HEAD_W))
        ws = [w - col[..., :w.shape[-1]] * w[:, :, j:j + 1, :] for w in ws]
    z, m = [w.reshape(h, CHUNK, w.shape[-1]) for w in ws]

    bmm = lambda a, b: jnp.einsum("hij,hjk->hik", a.astype(BF16), b.astype(BF16), preferred_element_type=F32)
    m2 = bmm(m, m)
    w1 = z - bmm(m, z)
    w2 = w1 + bmm(m2, w1)
    return w2 + bmm(bmm(m2, m2), w2)


def _gdn_chunk_terms(qk_ref, kk_ref, vk_ref, gb_ref, gt_ref, d, reverse):
    c = CHUNK
    nh = N_HEADS
    nc = TS // c
    r = lax.broadcasted_iota(jnp.int32, (c, c), 0)
    cc = lax.broadcasted_iota(jnp.int32, (c, c), 1)
    incl = ((r <= cc) if reverse else (r >= cc))[None]
    strict = ((r < cc) if reverse else (r > cc))[None]
    lo = lax.broadcasted_iota(jnp.int32, (1, 1, 2 * HEAD_W), 2) < HEAD_W
    bmm_nt = lambda a, b: jnp.einsum("hid,hjd->hij", a, b, preferred_element_type=F32)
    chunks = lambda a: a.reshape(nh * nc, c, a.shape[-1])
    qk = chunks(qk_ref[...])
    kk = chunks(kk_ref[...])
    vk = chunks(vk_ref[...])
    gb = gb_ref[...]
    spread = lambda l0: chunks(jnp.stack([jnp.broadcast_to(gb[:, l0 + h:l0 + h + 1], (TS, 2 * HEAD_W))
                                          for h in range(nh)]))
    gc = spread(L_A - 128 + d * nh)
    bt = spread(L_B - 128 + d * nh)
    grow = jnp.stack([gt_ref[d * nh + h:d * nh + h + 1, ci * c:(ci + 1) * c]
                      for h in range(nh) for ci in range(nc)])
    dec = jnp.where(incl, jnp.exp(jnp.minimum(gc[:, :, :c] - grow, 0.0)), 0.0)
    kkb = kk.astype(BF16)
    k_dot_k = 0.5 * bmm_nt(kkb, kkb)
    q_only = jnp.where(lo, qk, 0.0)
    q_dot_k = bmm_nt(q_only.astype(BF16), kkb) * dec
    n = jnp.where(strict, bt[:, :, :c] * k_dot_k * dec, 0.0)
    egc = jnp.exp(gc)
    x = _tri_solve(n, vk * jnp.where(lo, bt, bt * egc), reverse)
    edge = 0 if reverse else c - 1
    glast = gc[:, edge:edge + 1, :]
    ktail = kk * jnp.exp(glast - gc)
    return (jnp.where(lo, 0.0, x).astype(BF16), x[:, :, :HEAD_W], (q_only * egc).astype(BF16), q_dot_k.astype(BF16),
            ktail.astype(BF16), jnp.exp(glast)[:, :, :HEAD_W])


def _gdn_scan_kernel(qkf, qkb, kkf, kkb, vkf, vkb, gbf, gbb, gtf, gtb, of_ref, ob_ref, s_ref):
    @pl.when(pl.program_id(1) == 0)
    def _():
        s_ref[...] = jnp.zeros(s_ref.shape, F32)

    n_chunks = TS // CHUNK
    dirs = ((qkf, kkf, vkf, gbf, gtf, of_ref, False), (qkb, kkb, vkb, gbb, gtb, ob_ref, True))
    order = [range(n_chunks), range(n_chunks - 1, -1, -1)]
    terms = [_gdn_chunk_terms(*dirs[d][:5], d, dirs[d][6]) for d in range(2)]
    pick = lambda a, ci: a.reshape((N_HEADS, n_chunks) + a.shape[1:])[:, ci]
    bmm = lambda a, b: jnp.einsum("hij,hjk->hik", a, b, preferred_element_type=F32)
    bmm_tn = lambda a, b: jnp.einsum("hcd,hce->hde", a, b, preferred_element_type=F32)
    state = [s_ref[0], s_ref[1]]
    for step in range(n_chunks):
        for d in range(2):
            ci = order[d][step]
            k_cum, u, q_dec, q_dot_k, ktail, total = [pick(a, ci) for a in terms[d]]
            s2b = state[d].astype(BF16)
            vb = (u - bmm(k_cum, s2b)).astype(BF16)
            dirs[d][5][:, ci * CHUNK:(ci + 1) * CHUNK, :] = bmm(q_dec, s2b) + bmm(q_dot_k, vb)
            state[d] = state[d] * total + bmm_tn(ktail, vb)
    s_ref[0] = state[0]
    s_ref[1] = state[1]


def _gdn_scan(qk, kk, vk, gb, gt, nb, s):
    h, nt_rows, _ = qk.shape
    tps = s // TS
    nlt = nb * tps
    fwd = lambda b, n: jnp.where(n == 0, nlt + b, b * tps + n - 1)
    bwd = lambda b, n: jnp.where(n == 0, nlt + b, b * tps + tps - n)

    def views(shape, imap):
        return [pl.BlockSpec(shape, functools.partial(imap, t)) for t in (fwd, bwd)]

    heads = views((h, TS, 128), lambda t, b, n: (0, t(b, n), 0))
    lanes = views((TS, 128), lambda t, b, n: (t(b, n), 0))
    rows = views((None, 8, TS), lambda t, b, n: (t(b, n), 0, 0))
    outs = views((h, TS, HEAD_W), lambda t, b, n: (0, t(b, n), 0))
    return pl.pallas_call(
        _gdn_scan_kernel,
        grid=(nb, tps + 1),
        in_specs=heads * 3 + lanes + rows,
        out_specs=outs,
        out_shape=[jax.ShapeDtypeStruct((h, nt_rows, HEAD_W), F32)] * 2,
        scratch_shapes=[pltpu.VMEM((2, h, 2 * HEAD_W, HEAD_W), F32)],
        compiler_params=_cparams(("parallel", "arbitrary")),
        name="gdn_scan",
    )(qk, qk, kk, kk, vk, vk, gb, gb, gt, gt)


def _gdn_finish_kernel(of_ref, ob_ref, z_ref, nw_ref, y_ref):
    o = of_ref[...] + ob_ref[...]
    y = o * lax.rsqrt(jnp.mean(o * o, axis=-1, keepdims=True) + EPS) * nw_ref[...]
    y = jnp.concatenate([y[h] for h in range(N_HEADS)], axis=-1)
    y_ref[...] = (y * _silu(z_ref[...].astype(F32))).astype(BF16)


def _gdn_finish(of, ob, z, nw_row):
    nt = pl.cdiv(of.shape[1], TM)
    hsp = pl.BlockSpec((N_HEADS, TM, HEAD_W), lambda i: (0, i, 0))
    return pl.pallas_call(
        _gdn_finish_kernel,
        grid=(nt,),
        in_specs=[hsp, hsp, pl.BlockSpec((TM, 256), lambda i: (i, C_Z // 256)),
                  pl.BlockSpec(nw_row.shape, lambda i: (0, 0))],
        out_specs=pl.BlockSpec((TM, 256), lambda i: (i, 0)),
        out_shape=jax.ShapeDtypeStruct((of.shape[1], 256), BF16),
        compiler_params=_cparams(("parallel",)),
        name="gdn_finish",
    )(of, ob, z, nw_row)


S5_SUB = 8
S5_NS = S5_GROUPS * S5_STATE


def _s5_param_kernel(are_ref, aim_ref, ldt_ref, bre_ref, bim_ref, cre_ref, cim_ref,
                     bb_ref, kst_ref, cbd_ref, tab_ref):
    d = pl.program_id(0)
    lam_re = jnp.minimum(are_ref[...], -1e-4)
    lam_im = aim_ref[...]
    dt = jnp.exp(ldt_ref[...])

    def power(tau):
        mag = jnp.exp(lam_re * dt * tau)
        ang = lam_im * dt * tau
        return mag * jnp.cos(ang), mag * jnp.sin(ang)

    idx = lax.broadcasted_iota(jnp.int32, (S5_SUB, 1), 0)
    p_re, p_im = power(idx.astype(F32))
    lb_re, lb_im = p_re[1:2], p_im[1:2]
    den = lam_re * lam_re + lam_im * lam_im
    f_re = ((lb_re - 1.0) * lam_re + lb_im * lam_im) / den
    f_im = (lb_im * lam_re - (lb_re - 1.0) * lam_im) / den
    bb_re = f_re * bre_ref[...] - f_im * bim_ref[...]
    bb_im = f_re * bim_ref[...] + f_im * bre_ref[...]
    bb_ref[...] = jnp.concatenate([bb_re, bb_im], axis=-1).astype(BF16)
    c_re = cre_ref[...]
    c_im = cim_ref[...]
    cbd_ref[...] = jnp.concatenate([c_re, -c_im], axis=0).astype(BF16)
    for tau in range(S5_SUB):
        g_re = bb_re * p_re[tau:tau + 1] - bb_im * p_im[tau:tau + 1]
        g_im = bb_re * p_im[tau:tau + 1] + bb_im * p_re[tau:tau + 1]
        kst_ref[tau] = (_dot(g_re, c_re, precision=HI) - _dot(g_im, c_im, precision=HI)).astype(BF16)
    fwd = d == 0
    t_in = jnp.where(fwd, S5_SUB - 1 - idx, idx).astype(F32)
    t_out = jnp.where(fwd, idx + 1, S5_SUB - idx).astype(F32)
    for k, tau in enumerate((t_in, t_out, jnp.full((S5_SUB, 1), float(S5_SUB), F32))):
        tab_ref[k] = jnp.concatenate(power(tau), axis=-1)


def _s5_params(a_re, a_im, log_dt, b_re, b_im, c_re, c_im):
    depth = a_re.shape[0]
    g, p, gch = S5_GROUPS, S5_STATE, S5_GROUP_CH
    ns, c = S5_NS, g * gch
    eye = jnp.eye(g, dtype=F32)
    row = lambda a: a.astype(F32).reshape(depth, 2, 1, ns)
    ldt = jnp.broadcast_to(log_dt.astype(F32)[..., None], (depth, 2, g, p))
    b_bd = lambda b: jnp.einsum("lgpc,gh->lgchp", b.astype(F32), eye).reshape(depth, c, ns)
    c_bd = lambda cc: jnp.einsum("ldgcp,gh->ldgphc", cc.astype(F32), eye).reshape(depth, 2, ns, c)
    per_dir = lambda *shape: pl.BlockSpec((None, None) + shape, lambda d, l: (l, d) + (0,) * len(shape))
    shared = pl.BlockSpec((None, c, ns), lambda d, l: (l, 0, 0))
    out = lambda *shape, dtype=BF16: jax.ShapeDtypeStruct((depth, 2) + shape, dtype)
    return pl.pallas_call(
        _s5_param_kernel,
        grid=(2, depth),
        in_specs=[per_dir(1, ns)] * 3 + [shared] * 2 + [per_dir(ns, c)] * 2,
        out_specs=[per_dir(c, 2 * ns), per_dir(S5_SUB, c, c), per_dir(2 * ns, c), per_dir(3, S5_SUB, 2 * ns)],
        out_shape=[out(c, 2 * ns), out(S5_SUB, c, c), out(2 * ns, c), out(3, S5_SUB, 2 * ns, dtype=F32)],
        compiler_params=_cparams(("parallel", "parallel"), 48),
        name="s5_params",
    )(row(a_re), row(a_im), row(ldt), b_bd(b_re), b_bd(b_im), c_bd(c_re), c_bd(c_im))


def _s5_direction(u_ref, bb_ref, kst_ref, cbd_ref, tab_ref, y_ref, x_ref, xin_ref, xpv_ref, reverse):
    ns, sub = S5_NS, S5_SUB
    nsc = TS // sub
    u = u_ref[...]
    z = _dot(u, bb_ref[...])
    z_re = z[:, :ns].reshape(nsc, sub, ns)
    z_im = z[:, ns:].reshape(nsc, sub, ns)
    w = tab_ref[0]
    w_re, w_im = w[:, :ns][None], w[:, ns:][None]
    group_sum = lambda a: jnp.broadcast_to(jnp.sum(a, axis=1, keepdims=True), a.shape)
    xin_ref[0] = group_sum(w_re * z_re - w_im * z_im)
    xin_ref[1] = group_sum(w_re * z_im + w_im * z_re)
    a = tab_ref[2]
    a_re, a_im = a[:, :ns], a[:, ns:]

    def step(k, carry):
        x_re, x_im = carry
        n = nsc - 1 - k if reverse else k
        xpv_ref[0, n] = x_re
        xpv_ref[1, n] = x_im
        return (a_re * x_re - a_im * x_im + xin_ref[0, n], a_re * x_im + a_im * x_re + xin_ref[1, n])

    x_re, x_im = lax.fori_loop(0, nsc, step, (x_ref[0], x_ref[1]))
    x_ref[0] = x_re
    x_ref[1] = x_im
    o = tab_ref[1]
    o_re, o_im = o[:, :ns][None], o[:, ns:][None]
    p_re = (xpv_ref[0] * o_re - xpv_ref[1] * o_im).reshape(TS, ns)
    p_im = (xpv_ref[0] * o_im + xpv_ref[1] * o_re).reshape(TS, ns)
    y = _dot(jnp.concatenate([p_re, p_im], axis=-1).astype(BF16), cbd_ref[...])
    uf = u.astype(F32)
    pos = lax.broadcasted_iota(jnp.int32, (TS, 1), 0) % sub
    for tau in range(sub):
        if tau == 0:
            shifted = u
        else:
            rolled = pltpu.roll(uf, TS - tau if reverse else tau, axis=0)
            inside = (pos + tau < sub) if reverse else (pos >= tau)
            shifted = jnp.where(inside, rolled, 0.0).astype(BF16)
        y = y + _dot(shifted, kst_ref[tau])
    y_ref[...] = y


def _s5_scan_kernel(uf_ref, ub_ref, bbf, bbb, kstf, kstb, cbdf, cbdb, tabf, tabb, yf_ref, yb_ref,
                    x_ref, xin_ref, xpv_ref):
    @pl.when(pl.program_id(1) == 0)
    def _():
        x_ref[...] = jnp.zeros(x_ref.shape, F32)

    _s5_direction(uf_ref, bbf, kstf, cbdf, tabf, yf_ref, x_ref.at[0], xin_ref.at[0], xpv_ref.at[0], False)
    _s5_direction(ub_ref, bbb, kstb, cbdb, tabb, yb_ref, x_ref.at[1], xin_ref.at[1], xpv_ref.at[1], True)


def _s5_scan(z, bb, kst, cbd, tab, layer, nb, s):
    nt_rows = z.shape[0]
    tps = s // TS
    nlt = nb * tps
    fwd = lambda b, n: jnp.where(n == 0, nlt + b, b * tps + n - 1)
    bwd = lambda b, n: jnp.where(n == 0, nlt + b, b * tps + tps - n)
    ns, c, nsc = S5_NS, S5_GROUPS * S5_GROUP_CH, TS // S5_SUB

    def both(a):
        return [pl.BlockSpec((None, None) + a.shape[2:], lambda b, n, d=d: (layer, d) + (0,) * (a.ndim - 2))
                for d in range(2)]

    state = lambda *lead: pltpu.VMEM((2, 2) + lead + (S5_SUB, ns), F32)
    return pl.pallas_call(
        _s5_scan_kernel,
        grid=(nb, tps + 1),
        in_specs=[pl.BlockSpec((TS, c), lambda b, n: (fwd(b, n), C_U // 256)),
                  pl.BlockSpec((TS, c), lambda b, n: (bwd(b, n), C_U // 256))]
                 + both(bb) + both(kst) + both(cbd) + both(tab),
        out_specs=[pl.BlockSpec((TS, c), lambda b, n: (fwd(b, n), 0)),
                   pl.BlockSpec((TS, c), lambda b, n: (bwd(b, n), 0))],
        out_shape=[jax.ShapeDtypeStruct((nt_rows, c), F32)] * 2,
        scratch_shapes=[state(), state(nsc), state(nsc)],
        compiler_params=_cparams(("parallel", "arbitrary"), 56),
        name="s5_scan",
    )(z, z, bb, bb, kst, kst, cbd, cbd, tab, tab)


def _s5_finish_kernel(yf_ref, yb_ref, u_ref, d_ref, w_ref, b_ref, o_ref):
    y = yf_ref[...] + yb_ref[...] + d_ref[...] * u_ref[...].astype(F32)
    y = jax.nn.gelu(y)
    gate = _sigmoid(_dot(y.astype(BF16), w_ref[...]) + b_ref[...])
    o_ref[...] = (y * gate).astype(BF16)


def _s5_finish(yf, yb, z, d_row, glu_w, glu_b_row):
    nt = pl.cdiv(yf.shape[0], TM)
    full = lambda a: pl.BlockSpec(a.shape, lambda i: (0,) * a.ndim)
    return pl.pallas_call(
        _s5_finish_kernel,
        grid=(nt,),
        in_specs=[pl.BlockSpec((TM, 256), lambda i: (i, 0)),
                  pl.BlockSpec((TM, 256), lambda i: (i, 0)),
                  pl.BlockSpec((TM, 256), lambda i: (i, C_U // 256)),
                  full(d_row), full(glu_w), full(glu_b_row)],
        out_specs=pl.BlockSpec((TM, 256), lambda i: (i, 0)),
        out_shape=jax.ShapeDtypeStruct((yf.shape[0], 256), BF16),
        compiler_params=_cparams(("parallel",)),
        name="s5_finish",
    )(yf, yb, z, d_row, glu_w, glu_b_row)


def _merge_kernel(x_ref, y0_ref, y1_ref, y2_ref, y3_ref, gates_ref, wb_ref, wo_ref, g1_ref, gain_ref, o_ref):
    acc = jnp.zeros((TM, D_MODEL), F32)
    for bi, y_ref in enumerate((y0_ref, y1_ref, y2_ref, y3_ref)):
        proj = _dot(y_ref[...], wb_ref[bi])
        acc = acc + gates_ref[:, bi * D_MODEL:(bi + 1) * D_MODEL].astype(F32) * proj
    y = _dot(acc.astype(BF16), wo_ref[...])
    o_ref[...] = x_ref[...] + g1_ref[...] * _rms(y, gain_ref[...])


def _merge(x, ys, gates, wb, wo, g1, gain, layer, tps, nb, n_rows):
    nt = pl.cdiv(n_rows, TM)
    row = lambda i: (jnp.minimum(i // tps, nb), 0, 0)
    tile = lambda w: pl.BlockSpec((TM, w), lambda i: (i, 0))
    return pl.pallas_call(
        _merge_kernel,
        grid=(nt,),
        in_specs=[tile(D_MODEL)] + [tile(BRANCH_W)] * 4 + [tile(4 * D_MODEL),
                  pl.BlockSpec((None, 4, BRANCH_W, D_MODEL), lambda i: (layer, 0, 0, 0)),
                  pl.BlockSpec((None, D_MODEL, D_MODEL), lambda i: (layer, 0, 0)),
                  pl.BlockSpec((None, 1, D_MODEL), row),
                  pl.BlockSpec((1, D_MODEL), lambda i: (0, 0))],
        out_specs=tile(D_MODEL),
        out_shape=jax.ShapeDtypeStruct((n_rows, D_MODEL), F32),
        compiler_params=_cparams(("parallel",), 56),
        name="merge",
    )(x, *ys, gates, wb, wo, g1, gain)


def _mlp_kernel(x_ref, sc_ref, sh_ref, g2_ref, gin_ref, gout_ref, w1_ref, w2_ref, o_ref, h_ref, acc_ref):
    j = pl.program_id(1)

    @pl.when(j == 0)
    def _():
        h_ref[...] = _norm_mod(x_ref[...], gin_ref[...], sc_ref[...], sh_ref[...]).astype(BF16)
        acc_ref[...] = jnp.zeros(acc_ref.shape, F32)

    t = jnp.maximum(_dot(h_ref[...], w1_ref[...]), 0.0)
    acc_ref[...] += _dot((t * t).astype(BF16), w2_ref[...])

    @pl.when(j == pl.num_programs(1) - 1)
    def _():
        o_ref[...] = x_ref[...] + g2_ref[...] * _rms(acc_ref[...], gout_ref[...])


def _mlp(x, sc, sh, g2, gin, gout, w1, w2, layer, tps, nb, tf=2048):
    nt = pl.cdiv(x.shape[0], TM)
    row = lambda i, j: (jnp.minimum(i // tps, nb), 0, 0)
    vec = pl.BlockSpec((1, D_MODEL), lambda i, j: (0, 0))
    return pl.pallas_call(
        _mlp_kernel,
        grid=(nt, D_FF // tf),
        in_specs=[pl.BlockSpec((TM, D_MODEL), lambda i, j: (i, 0)),
                  pl.BlockSpec((None, 1, D_MODEL), row), pl.BlockSpec((None, 1, D_MODEL), row),
                  pl.BlockSpec((None, 1, D_MODEL), row), vec, vec,
                  pl.BlockSpec((None, D_MODEL, tf), lambda i, j: (layer, 0, j)),
                  pl.BlockSpec((None, tf, D_MODEL), lambda i, j: (layer, j, 0))],
        out_specs=pl.BlockSpec((TM, D_MODEL), lambda i, j: (i, 0)),
        out_shape=jax.ShapeDtypeStruct(x.shape, F32),
        scratch_shapes=[pltpu.VMEM((TM, D_MODEL), BF16), pltpu.VMEM((TM, D_MODEL), F32)],
        compiler_params=_cparams(("parallel", "arbitrary"), 58),
        name="mlp",
    )(x, sc, sh, g2, gin, gout, w1, w2)


def _pack_w_in(w_in):
    o_na, o_cq, o_ckv, o_gdn, o_z, o_a, o_b, o_u, o_gate = 0, 768, 1024, 1184, 1952, 2208, 2216, 2224, 2480
    cols = lambda lo, n: w_in[:, :, lo:lo + n]
    o_kr = o_ckv + MLA_KV_LORA
    swapped = [cols(o_kr + 8, 8), cols(o_kr, 8), cols(o_kr + 24, 8), cols(o_kr + 16, 8)]
    small = jnp.concatenate(
        [cols(o_gdn, 768), cols(o_na, 768), cols(o_cq, 256),
         cols(o_ckv, 160), cols(o_a, 8), cols(o_b, 8)] + swapped
        + [jnp.zeros(w_in.shape[:2] + (256 - L_KRS - MLA_ROPE,), w_in.dtype), cols(o_z, 256), cols(o_u, 256)], axis=2)
    assert small.shape[2] == ZW
    return small.astype(BF16), w_in[:, :, o_gate:].astype(BF16)


def _mla_weights(w_uq, w_ukv):
    depth = w_uq.shape[0]
    hq = MLA_NOPE + MLA_ROPE
    wq = w_uq.reshape(depth, -1, N_HEADS, hq)
    pad = lambda a, lo, hi: jnp.pad(a, ((0, 0), (0, 0), (0, 0), (lo, hi)))
    wq_ext = pad(wq, 0, 128 - hq).reshape(depth, -1, N_HEADS * 128)
    wq_sw = pad(wq[..., MLA_NOPE:][..., ROPE_SWAP], MLA_NOPE, 128 - hq).reshape(depth, -1, N_HEADS * 128)
    wkv = w_ukv.reshape(depth, -1, N_HEADS, MLA_NOPE + HEAD_W)
    wk = pad(wkv[..., :MLA_NOPE], 0, 128 - MLA_NOPE).reshape(depth, -1, N_HEADS * 128)
    wv = pad(wkv[..., MLA_NOPE:], 0, 128 - HEAD_W).reshape(depth, -1, N_HEADS * 128)
    return [a.astype(BF16) for a in (wq_ext, wq_sw, wk, wv, jnp.swapaxes(wv, 1, 2))]


def _rope_place_mats():
    p1 = np.zeros((256, N_HEADS * 128), np.float32)
    p2 = np.zeros((256, N_HEADS * 128), np.float32)
    for h in range(N_HEADS):
        for r in range(MLA_ROPE):
            p1[L_KR + r, h * 128 + MLA_NOPE + r] = 1.0
            p2[L_KRS + r, h * 128 + MLA_NOPE + r] = 1.0
    return jnp.asarray(p1, BF16), jnp.asarray(p2, BF16)


def _rope_tables(s):
    quarter = MLA_ROPE // 4
    inv_freq = ROPE_BASE ** (-jnp.arange(quarter, dtype=F32) / quarter)
    t = jnp.arange(s)
    ang_r = (t // GRID_W).astype(F32)[:, None] * inv_freq[None, :]
    ang_c = (t % GRID_W).astype(F32)[:, None] * inv_freq[None, :]
    cr, sr, cc, sn = jnp.cos(ang_r), jnp.sin(ang_r), jnp.cos(ang_c), jnp.sin(ang_c)
    cos = jnp.concatenate([jnp.ones((s, MLA_NOPE), F32), cr, cr, cc, cc, jnp.ones((s, 32), F32)], axis=1)
    sin = jnp.concatenate([jnp.zeros((s, MLA_NOPE), F32), -sr, sr, -sn, sn, jnp.zeros((s, 32), F32)], axis=1)
    cos = jnp.concatenate([cos, jnp.ones((TM, 128), F32)], axis=0)
    sin = jnp.concatenate([sin, jnp.zeros((TM, 128), F32)], axis=0)
    return cos, sin


def _head_block_ones():
    r = np.arange(BRANCH_W)
    return jnp.asarray((r[:, None] // HEAD_W == r[None, :] // HEAD_W).astype(np.float32))


def _lane_row(vals, offset, width=256):
    return jnp.zeros((1, width), F32).at[0, offset:offset + vals.shape[0]].set(vals.astype(F32))


def _gdn_mixer(z, conv_w, a_log, dt_bias, norm_w, consts, nb, s, ctx_len):
    alog_row = _lane_row(a_log.reshape(-1), L_A)
    dtb_row = _lane_row(dt_bias.reshape(-1), L_A)
    qk, kk, vk, gb, gt = _gdn_prep(z, conv_w, alog_row, dtb_row, consts["bseg"], consts["sel"], s // TS, nb * s // TS)
    of, ob = _gdn_scan(qk, kk, vk, gb, gt, nb, s)
    return _gdn_finish(of, ob, z, norm_w.astype(F32)[None, :])


def _s5_mixer(z, operators, layer, d_skip, glu_w, glu_b, nb, s):
    yf, yb = _s5_scan(z, *operators, layer, nb, s)
    return _s5_finish(yf, yb, z, d_skip.astype(F32)[None, :], glu_w.astype(BF16), glu_b.astype(F32)[None, :])


def kernel(x, c, ctx, c_ctx, ada_w, ada_b, norm_gains, w_in, na_rpb, mla_q_norm, mla_kv_norm, mla_w_uq, mla_w_ukv, gdn_conv, gdn_a_log, gdn_dt_bias, gdn_norm, s5_a_re, s5_a_im, s5_log_dt, s5_b_re, s5_b_im, s5_c_re, s5_c_im, s5_d, s5_glu_w, s5_glu_b, w_branch, w_out, mlp_w1, mlp_w2):
    nb, s, d = x.shape
    ctx_len = ctx.shape[1]
    depth = ada_w.shape[0]
    assert d == D_MODEL and ctx_len == TS and nb * ctx_len <= TM and s % TM == 0 and s // GRID_W >= 16
    nl = nb * s
    tps = s // TM

    xs = jnp.concatenate([x.reshape(nl, d), ctx.reshape(nb * ctx_len, d)], axis=0)
    cvec = jnp.zeros((8, d), F32).at[:nb].set(c).at[nb].set(c_ctx)
    mod = _modulation(cvec, ada_w, ada_b)
    mod = mod[:, :nb + 1].reshape(depth, nb + 1, 6, 1, d)

    w_small, w_gates = _pack_w_in(w_in)
    wq_ext, wq_sw, wk_ext, wv, wvt = _mla_weights(mla_w_uq, mla_w_ukv)
    p1, p2 = _rope_place_mats()
    cos_t, sin_t = _rope_tables(s)
    sel = np.zeros((8, 128), np.float32)
    sel[np.arange(8), L_A - 128 + np.arange(8)] = 1.0
    consts = {"bseg": _head_block_ones().astype(BF16), "sel": jnp.asarray(sel)}
    wb = w_branch.astype(BF16)
    wo = w_out.astype(BF16)
    w1 = mlp_w1.astype(BF16)
    w2 = mlp_w2.astype(BF16)
    gains = norm_gains.astype(F32)
    s5_ops = _s5_params(s5_a_re, s5_a_im, s5_log_dt, s5_b_re, s5_b_im, s5_c_re, s5_c_im)

    for l in range(depth):
        sh1, sc1, g1, sh2, sc2, g2 = [mod[l, :, i] for i in range(6)]
        z = _inproj(xs, sc1, sh1, gains[l, 0][None], w_small, l, tps, nb, False)
        gates = _inproj(xs, sc1, sh1, gains[l, 0][None], w_gates, l, tps, nb, True)

        need_ctx = l < depth - 1
        y_rows = xs.shape[0] if need_ctx else nl
        y_na = _na_latent(z, na_rpb[l], nb, s, y_rows)
        qm, km, vm, vt = _mla_prep(z, cos_t, sin_t, mla_q_norm[l].astype(F32)[None],
                                   mla_kv_norm[l].astype(F32)[None], wq_ext[l], wq_sw[l], wk_ext[l], wv[l], wvt[l],
                                   p1, p2, tps, nl // TM)
        y_mla = _mla_latent(qm, km, vt, nb, s, y_rows)
        if need_ctx:
            y_na = _ctx_attention(z, z, z, y_na, C_NA // 256, C_NA // 256 + 1, C_NA // 256 + 2, 256, 256, nl // TS, nb,
                                  HEAD_W ** -0.5, False, "na_ctx")
            y_mla = _ctx_attention(qm, km, vm, y_mla, 0, 0, 0, N_HEADS * 128, N_HEADS * 128, nl // TS, nb, 1.0, True,
                                   "mla_ctx")

        y_gdn = _gdn_mixer(z, gdn_conv[l].astype(F32), gdn_a_log[l], gdn_dt_bias[l], gdn_norm[l], consts,
                           nb, s, ctx_len)
        y_s5 = _s5_mixer(z, s5_ops, l, s5_d[l], s5_glu_w[l], s5_glu_b[l], nb, s)

        xs = _merge(xs, (y_na, y_mla, y_gdn, y_s5), gates, wb, wo, g1, gains[l, 1][None], l, tps, nb, y_rows)
        xs = _mlp(xs, sc2, sh2, g2, gains[l, 2][None], gains[l, 3][None], w1, w2, l, tps, nb)
    return xs[:nl].reshape(nb, s, d)
```

```python
import functools
import math

import numpy as np
import jax
import jax.numpy as jnp
from jax import lax
from jax.experimental import pallas as pl
from jax.experimental.pallas import tpu as pltpu

F32 = jnp.float32
BF16 = jnp.bfloat16
HI = lax.Precision.HIGHEST
EPS = 1e-6

D_MODEL = 1024
GRID_W = 64
NA_WIN_H = 8
NA_WIN_W = 16
N_HEADS = 4
HEAD_W = 64
BRANCH_W = 256
MLA_NOPE = 64
MLA_ROPE = 32
MLA_KV_LORA = 128
ROPE_BASE = 10000.0
GDN_CONV = 4
CHUNK = 64
S5_GROUPS = 16
S5_GROUP_CH = 16
S5_STATE = 64
D_FF = 4 * D_MODEL

TM = 1024
TN = 512
TS = 256
NEG = -1e30

C_GDN = 0
C_NA = 768
C_CQ = 1536
C_CKV = 1792
C_Z = 2048
C_U = 2304
ZW = 2560
L_KR = 128
L_A = 160
L_B = 168
L_KRS = 176
ROPE_SWAP = np.concatenate([np.arange(8, 16), np.arange(0, 8), np.arange(24, 32), np.arange(16, 24)])


def _cparams(sem, vmem_mb=None):
    kw = dict(dimension_semantics=sem)
    if vmem_mb is not None:
        kw["vmem_limit_bytes"] = vmem_mb * 1024 * 1024
    return pltpu.CompilerParams(**kw)


def _dot(a, b, **kw):
    return jnp.dot(a, b, preferred_element_type=F32, **kw)


def _dot_nt(a, b):
    return lax.dot_general(a, b, (((1,), (1,)), ((), ())), preferred_element_type=F32)


def _sigmoid(x):
    return 0.5 * jnp.tanh(0.5 * x) + 0.5


def _silu(x):
    return x * _sigmoid(x)


def _mod_kernel(c_ref, w_ref, b_ref, o_ref):
    c = c_ref[...]
    o_ref[...] = _dot(_silu(c), w_ref[...], precision=HI) + b_ref[...]


def _modulation(cvec, ada_w, ada_b):
    depth, d, n = ada_w.shape
    tn = 1536
    return pl.pallas_call(
        _mod_kernel,
        grid=(depth, n // tn),
        in_specs=[pl.BlockSpec((8, d), lambda l, j: (0, 0)),
                  pl.BlockSpec((None, d, tn), lambda l, j: (l, 0, j)),
                  pl.BlockSpec((None, 1, tn), lambda l, j: (l, 0, j))],
        out_specs=pl.BlockSpec((None, 8, tn), lambda l, j: (l, 0, j)),
        out_shape=jax.ShapeDtypeStruct((depth, 8, n), F32),
        compiler_params=_cparams(("parallel", "parallel"), 40),
        name="modulation",
    )(cvec, ada_w, ada_b.reshape(depth, 1, n))


def _norm_mod(x, gain, sc, sh):
    r = lax.rsqrt(jnp.mean(x * x, axis=-1, keepdims=True) + EPS)
    return (x * r * gain) * (1.0 + sc) + sh


def _inproj_kernel(x_ref, sc_ref, sh_ref, gain_ref, w_ref, o_ref, h_ref, *, gate):
    @pl.when(pl.program_id(1) == 0)
    def _():
        h_ref[...] = _norm_mod(x_ref[...], gain_ref[...], sc_ref[...], sh_ref[...]).astype(BF16)

    acc = _dot(h_ref[...], w_ref[...])
    if gate:
        acc = _sigmoid(acc)
    o_ref[...] = acc.astype(BF16)


def _inproj(x, sc, sh, gain, w, layer, tps, nb, gate):
    nt = pl.cdiv(x.shape[0], TM)
    width = w.shape[-1]
    tn = width // 2
    row = lambda i, j: (jnp.minimum(i // tps, nb), 0, 0)
    return pl.pallas_call(
        functools.partial(_inproj_kernel, gate=gate),
        grid=(nt, width // tn),
        in_specs=[pl.BlockSpec((TM, D_MODEL), lambda i, j: (i, 0)),
                  pl.BlockSpec((None, 1, D_MODEL), row),
                  pl.BlockSpec((None, 1, D_MODEL), row),
                  pl.BlockSpec((1, D_MODEL), lambda i, j: (0, 0)),
                  pl.BlockSpec((None, D_MODEL, tn), lambda i, j: (layer, 0, j))],
        out_specs=pl.BlockSpec((TM, tn), lambda i, j: (i, j)),
        out_shape=jax.ShapeDtypeStruct((x.shape[0], width), BF16),
        scratch_shapes=[pltpu.VMEM((TM, D_MODEL), BF16)],
        compiler_params=_cparams(("parallel", "arbitrary"), 48),
        name="inproj_gates" if gate else "inproj",
    )(x, sc, sh, gain, w)


def _head_lane_mask(width, head_w, h):
    lane = lax.broadcasted_iota(jnp.int32, (1, width), 1)
    return (lane >= h * head_w) & (lane < (h + 1) * head_w)


def _na_build_bias(rpb_ref, bias_ref, r0, kb0, rows_total):
    w = GRID_W
    qc = lax.broadcasted_iota(jnp.int32, (w, 2 * w), 0)
    lane = lax.broadcasted_iota(jnp.int32, (w, 2 * w), 1)
    kc = lane % w
    cs = jnp.clip(qc - NA_WIN_W // 2, 0, w - NA_WIN_W)
    col_ok = (kc >= cs) & (kc < cs + NA_WIN_W)
    left = lane < w
    neg = jnp.full((w, 2 * w), NEG, F32)
    for h in range(N_HEADS):
        t = rpb_ref[h]
        toep = []
        for a in range(2 * NA_WIN_H - 1):
            row = jnp.broadcast_to(t[a:a + 1, :], (w, 2 * w))
            ra = pltpu.roll(row, 2 * w - (NA_WIN_W - 1), axis=1, stride=1, stride_axis=0)
            rb = pltpu.roll(ra, w, axis=1)
            toep.append((jnp.where(col_ok, ra, NEG), jnp.where(col_ok, rb, NEG)))
        for qr in range(8):
            rs = min(max(r0 + qr - NA_WIN_H // 2, 0), rows_total - NA_WIN_H)
            for kp in range(8):
                halves = []
                for side in range(2):
                    kr = kb0 + 2 * kp + side
                    halves.append(toep[kr - (r0 + qr) + NA_WIN_H - 1][side] if rs <= kr < rs + NA_WIN_H else neg)
                bias_ref[h, qr * w:(qr + 1) * w, kp * 2 * w:(kp + 1) * 2 * w] = jnp.where(left, halves[0], halves[1])


def _na_kernel(q_ref, k_ref, v_ref, kc_ref, vc_ref, rpb_ref, o_ref, bias_ref, *, rows_total):
    i = pl.program_id(1)
    last = pl.num_programs(1) - 1

    @pl.when(i == 0)
    def _():
        _na_build_bias(rpb_ref, bias_ref, 0, 0, rows_total)

    @pl.when(i == 1)
    def _():
        _na_build_bias(rpb_ref, bias_ref, 8, 4, rows_total)

    @pl.when(i == last)
    def _():
        _na_build_bias(rpb_ref, bias_ref, rows_total - 8, rows_total - 16, rows_total)

    kb = jnp.clip(2 * i - 1, 0, rows_total // 4 - 4)
    start = pl.multiple_of(kb * (4 * GRID_W), 4 * GRID_W)
    nk = 2 * NA_WIN_H * GRID_W
    q = q_ref[...]
    kw = k_ref[pl.ds(start, nk), :]
    vw = v_ref[pl.ds(start, nk), :]
    kc = kc_ref[...]
    vc = vc_ref[...]
    scale = HEAD_W ** -0.5
    out = jnp.zeros(q.shape, F32)
    for h in range(N_HEADS):
        hm = _head_lane_mask(BRANCH_W, HEAD_W, h)
        qh = jnp.where(hm, q, jnp.zeros_like(q))
        sb = _dot_nt(qh, kw) * scale + bias_ref[h]
        sc = _dot_nt(qh, kc) * scale
        m = jnp.maximum(jnp.max(sb, axis=-1, keepdims=True), jnp.max(sc, axis=-1, keepdims=True))
        pb = jnp.exp(sb - m)
        pc = jnp.exp(sc - m)
        den = jnp.sum(pb, axis=-1, keepdims=True) + jnp.sum(pc, axis=-1, keepdims=True)
        o = _dot(pb.astype(BF16), vw) + _dot(pc.astype(BF16), vc)
        out = jnp.where(hm, o / den, out)
    o_ref[...] = out.astype(BF16)


def _na_latent(z, rpb, nb, s):
    rows_total = s // GRID_W
    qb = 8 * GRID_W
    nq = s // qb
    nl = nb * s
    rpb = jnp.pad(rpb.astype(F32), ((0, 0), (0, 1), (0, 2 * GRID_W - (2 * NA_WIN_W - 1))))
    return pl.pallas_call(
        functools.partial(_na_kernel, rows_total=rows_total),
        grid=(nb, nq),
        in_specs=[pl.BlockSpec((qb, BRANCH_W), lambda b, i: (b * nq + i, C_NA // 256)),
                  pl.BlockSpec((s, BRANCH_W), lambda b, i: (b, C_NA // 256 + 1)),
                  pl.BlockSpec((s, BRANCH_W), lambda b, i: (b, C_NA // 256 + 2)),
                  pl.BlockSpec((TS, BRANCH_W), lambda b, i: (nl // TS + b, C_NA // 256 + 1)),
                  pl.BlockSpec((TS, BRANCH_W), lambda b, i: (nl // TS + b, C_NA // 256 + 2)),
                  pl.BlockSpec(rpb.shape, lambda b, i: (0, 0, 0))],
        out_specs=pl.BlockSpec((qb, BRANCH_W), lambda b, i: (b * nq + i, 0)),
        out_shape=jax.ShapeDtypeStruct((nl, BRANCH_W), BF16),
        scratch_shapes=[pltpu.VMEM((N_HEADS, qb, 2 * qb), F32)],
        compiler_params=_cparams(("parallel", "arbitrary"), 56),
        name="na_latent",
    )(z, z, z, z, z, rpb)


def _ctx_attn_kernel(q_ref, k_ref, v_ref, o_ref, *, scale, base2):
    q = q_ref[...]
    k = k_ref[...]
    v = v_ref[...]
    qw = q.shape[-1]
    vw = v.shape[-1] // N_HEADS
    outs = []
    for h in range(N_HEADS):
        qh = jnp.where(_head_lane_mask(qw, qw // N_HEADS, h), q, jnp.zeros_like(q))
        s = _dot_nt(qh, k) * scale
        m = jnp.max(s, axis=-1, keepdims=True)
        p = jnp.exp2(s - m) if base2 else jnp.exp(s - m)
        den = jnp.sum(p, axis=-1, keepdims=True)
        o = _dot(p.astype(BF16), v)
        outs.append(o[:, h * vw:h * vw + HEAD_W] / den)
    o_ref[...] = jnp.concatenate(outs, axis=-1).astype(BF16)


def _ctx_attention(q, k, v, qcol, kcol, vcol, qw, vw, row0, nb, scale, base2, name):
    return pl.pallas_call(
        functools.partial(_ctx_attn_kernel, scale=scale, base2=base2),
        grid=(nb,),
        in_specs=[pl.BlockSpec((TS, qw), lambda b: (row0 + b, qcol)),
                  pl.BlockSpec((TS, qw), lambda b: (row0 + b, kcol)),
                  pl.BlockSpec((TS, vw), lambda b: (row0 + b, vcol))],
        out_specs=pl.BlockSpec((TS, BRANCH_W), lambda b: (b, 0)),
        out_shape=jax.ShapeDtypeStruct((nb * TS, BRANCH_W), BF16),
        compiler_params=_cparams(("parallel",)),
        name=name,
    )(q, k, v)


def _rms(x, gain):
    return x * lax.rsqrt(jnp.mean(x * x, axis=-1, keepdims=True) + EPS) * gain


def _mla_prep_kernel(cq_ref, ckv_ref, cos_ref, sin_ref, qn_ref, kvn_ref, wq_ref, wqs_ref, wk_ref, wv_ref, wvt_ref,
                     p1_ref, p2_ref, q_ref, k_ref, v_ref, vt_ref):
    cos = jnp.concatenate([cos_ref[...]] * N_HEADS, axis=-1)
    sin = jnp.concatenate([sin_ref[...]] * N_HEADS, axis=-1)
    cqn = _rms(cq_ref[...].astype(F32), qn_ref[...]).astype(BF16)
    scale = (MLA_NOPE + MLA_ROPE) ** -0.5 * math.log2(math.e)
    q = _dot(cqn, wq_ref[...]) * cos + _dot(cqn, wqs_ref[...]) * sin
    q_ref[...] = (q * scale).astype(BF16)
    ckv = ckv_ref[...]
    kvn = _rms(ckv[:, :MLA_KV_LORA].astype(F32), kvn_ref[...]).astype(BF16)
    k = (_dot(kvn, wk_ref[...]) + _dot(ckv, p1_ref[...])) * cos + _dot(ckv, p2_ref[...]) * sin
    k_ref[...] = k.astype(BF16)
    lane = lax.broadcasted_iota(jnp.int32, (1, N_HEADS * 128), 1)
    v_ref[...] = jnp.where(lane % 128 == HEAD_W, 1.0, _dot(kvn, wv_ref[...])).astype(BF16)
    row = lax.broadcasted_iota(jnp.int32, (N_HEADS * 128, 1), 0)
    vt_ref[...] = jnp.where(row % 128 == HEAD_W, 1.0, _dot_nt(wvt_ref[...], kvn)).astype(BF16)


def _mla_prep(z, cos_t, sin_t, qn, kvn, wq, wqs, wk, wv, wvt, p1, p2, tps, n_lat_tiles):
    nt_rows = z.shape[0]
    nt = pl.cdiv(nt_rows, TM)
    full = lambda a: pl.BlockSpec(a.shape, lambda i: (0,) * a.ndim)
    tab = lambda i: (jnp.where(i < n_lat_tiles, i % tps, tps), 0)
    hw = N_HEADS * 128
    return pl.pallas_call(
        _mla_prep_kernel,
        grid=(nt,),
        in_specs=[pl.BlockSpec((TM, 256), lambda i: (i, C_CQ // 256)),
                  pl.BlockSpec((TM, 256), lambda i: (i, C_CKV // 256)),
                  pl.BlockSpec((TM, 128), tab), pl.BlockSpec((TM, 128), tab),
                  full(qn), full(kvn), full(wq), full(wqs), full(wk), full(wv), full(wvt), full(p1), full(p2)],
        out_specs=[pl.BlockSpec((TM, hw), lambda i: (i, 0)),
                   pl.BlockSpec((TM, hw), lambda i: (i, 0)),
                   pl.BlockSpec((TM, hw), lambda i: (i, 0)),
                   pl.BlockSpec((hw, TM), lambda i: (0, i))],
        out_shape=[jax.ShapeDtypeStruct((nt_rows, hw), BF16)] * 3 + [jax.ShapeDtypeStruct((hw, nt_rows), BF16)],
        compiler_params=_cparams(("parallel",)),
        name="mla_prep",
    )(z, z, cos_t, sin_t, qn, kvn, wq, wqs, wk, wv, wvt, p1, p2)


def _flash_kernel(q_ref, kl_ref, vl_ref, kc_ref, vc_ref, o_ref, st_ref, *, tk, n_lat):
    tq = q_ref.shape[0]
    s_len = n_lat * tk
    heads = (slice(0, 128), slice(128, 256))
    qs = [q_ref[:, hs] for hs in heads]
    group_max = lambda st: jnp.max(st.reshape(st.shape[0] // 8, 8, tq), axis=0)

    def score_tile(hh, k, rows, mx):
        st = _dot_nt(k, qs[hh])
        st_ref[hh, rows, :] = st
        return jnp.maximum(mx, group_max(st))

    def max_body(t, carry):
        rows = pl.ds(pl.multiple_of(t * tk, tk), tk)
        return tuple(score_tile(hh, kl_ref[rows, heads[hh]], rows, carry[hh]) for hh in range(2))

    mx = lax.fori_loop(0, n_lat, max_body, tuple(jnp.full((8, tq), NEG, F32) for _ in range(2)), unroll=4)
    ctx_rows = slice(s_len, s_len + kc_ref.shape[0])
    ms = [jnp.max(score_tile(hh, kc_ref[:, heads[hh]], ctx_rows, mx[hh]), axis=0, keepdims=True) for hh in range(2)]

    def weighted(hh, rows, vt):
        return _dot(vt, jnp.exp2(st_ref[hh, rows, :] - ms[hh]).astype(BF16))

    def acc_body(t, carry):
        r0 = pl.multiple_of(t * tk, tk)
        return tuple(carry[hh] + weighted(hh, pl.ds(r0, tk), vl_ref[heads[hh], pl.ds(r0, tk)]) for hh in range(2))

    accs = lax.fori_loop(0, n_lat, acc_body, tuple(jnp.zeros((128, tq), F32) for _ in range(2)), unroll=4)
    outs = []
    for hh in range(2):
        acc = (accs[hh] + weighted(hh, ctx_rows, vc_ref[heads[hh], :])).T
        outs.append(acc[:, :HEAD_W] / acc[:, HEAD_W:HEAD_W + 1])
    o_ref[...] = jnp.concatenate(outs, axis=-1).astype(BF16)


def _mla_latent(qm, km, vt, nb, s, tq=512, tk=1024):
    nq = s // tq
    nl = nb * s
    once = dict(pipeline_mode=pl.Buffered(1))
    return pl.pallas_call(
        functools.partial(_flash_kernel, tk=tk, n_lat=s // tk),
        grid=(nb, 2, nq),
        in_specs=[pl.BlockSpec((tq, 256), lambda b, hp, i: (b * nq + i, hp)),
                  pl.BlockSpec((s, 256), lambda b, hp, i: (b, hp), **once),
                  pl.BlockSpec((256, s), lambda b, hp, i: (hp, b), **once),
                  pl.BlockSpec((TS, 256), lambda b, hp, i: (nl // TS + b, hp)),
                  pl.BlockSpec((256, TS), lambda b, hp, i: (hp, nl // TS + b))],
        out_specs=pl.BlockSpec((tq, 128), lambda b, hp, i: (b * nq + i, hp)),
        out_shape=jax.ShapeDtypeStruct((nl, BRANCH_W), BF16),
        scratch_shapes=[pltpu.VMEM((2, s + TS, tq), F32)],
        compiler_params=_cparams(("parallel", "parallel", "arbitrary"), 56),
        name="mla_flash",
    )(qm, km, vt, km, vt)


def _gdn_prep_kernel(prev_ref, cur_ref, next_ref, ckv_ref, conv_ref, alog_ref, dtb_ref, bseg_ref, sel_ref,
                     qk_ref, kk_ref, vk_ref, gb_ref, gt_ref, *, tps, n_lat_tiles):
    i = pl.program_id(0)
    is_ctx = i >= n_lat_tiles
    first = is_ctx | (i % tps == 0)
    last = is_ctx | (i % tps == tps - 1)
    prev = jnp.where(first, 0.0, prev_ref[...].astype(F32))
    nxt = jnp.where(last, 0.0, next_ref[...].astype(F32))
    ext = jnp.concatenate([prev, cur_ref[...].astype(F32), nxt], axis=0)
    n_ext = TS + 16
    acc = jnp.zeros((TS, 3 * BRANCH_W), F32)
    for j in range(GDN_CONV):
        shifted = pltpu.roll(ext, n_ext - (8 - GDN_CONV // 2 + j), axis=0)[:TS]
        acc = acc + shifted * conv_ref[j:j + 1, :]
    x = _silu(acc)
    bseg = bseg_ref[...]

    def l2n(a):
        sq = a * a
        hi = sq.astype(BF16)
        lo = (sq - hi.astype(F32)).astype(BF16)
        return a * lax.rsqrt(_dot(hi, bseg) + _dot(lo, bseg) + EPS)

    q = l2n(x[:, :BRANCH_W]) * (HEAD_W ** -0.5)
    k = l2n(x[:, BRANCH_W:2 * BRANCH_W])
    v = x[:, 2 * BRANCH_W:]
    for h in range(N_HEADS):
        hs = slice(h * HEAD_W, (h + 1) * HEAD_W)
        qk_ref[h] = jnp.concatenate([q[:, hs], k[:, hs]], axis=-1)
        kk_ref[h] = jnp.concatenate([k[:, hs], k[:, hs]], axis=-1)
        vk_ref[h] = jnp.concatenate([v[:, hs], k[:, hs]], axis=-1)

    ab = ckv_ref[:, 128:].astype(F32)
    la, lb = L_A - 128, L_B - 128
    sp_in = ab + dtb_ref[:, 128:]
    softplus = jnp.maximum(sp_in, 0.0) + jnp.log1p(jnp.exp(-jnp.abs(sp_in)))
    lane = lax.broadcasted_iota(jnp.int32, (1, 128), 1)
    g = jnp.where((lane >= la) & (lane < lb), -jnp.exp(alog_ref[:, 128:]) * softplus, 0.0)
    beta = _sigmoid(ab)
    pos = lax.broadcasted_iota(jnp.int32, (TS, 1), 0) % CHUNK
    pre, suf = g, g
    step = 1
    while step < CHUNK:
        pre = pre + jnp.where(pos >= step, pltpu.roll(pre, step, axis=0), 0.0)
        suf = suf + jnp.where(pos + step < CHUNK, pltpu.roll(suf, TS - step, axis=0), 0.0)
        step *= 2
    gc = jnp.where(lane < la + N_HEADS, pre, suf)
    gt_ref[...] = lax.dot_general(sel_ref[...], gc, (((1,), (1,)), ((), ())), precision=HI,
                                  preferred_element_type=F32)
    gb_ref[...] = jnp.where(lane < lb, gc, beta)


def _gdn_prep(z, conv_w, alog_row, dtb_row, bseg, sel, tps_s, n_lat_tiles):
    nt_rows = z.shape[0]
    nt = nt_rows // TS
    hb = TS // 8
    full = lambda a: pl.BlockSpec(a.shape, lambda i: (0,) * a.ndim)
    hsp = pl.BlockSpec((N_HEADS, TS, 128), lambda i: (0, i, 0))
    return pl.pallas_call(
        functools.partial(_gdn_prep_kernel, tps=tps_s, n_lat_tiles=n_lat_tiles),
        grid=(nt,),
        in_specs=[pl.BlockSpec((8, 768), lambda i: (jnp.maximum(i * hb - 1, 0), 0)),
                  pl.BlockSpec((TS, 768), lambda i: (i, 0)),
                  pl.BlockSpec((8, 768), lambda i: (jnp.minimum((i + 1) * hb, nt * hb - 1), 0)),
                  pl.BlockSpec((TS, 256), lambda i: (i, C_CKV // 256)),
                  full(conv_w), full(alog_row), full(dtb_row), full(bseg), full(sel)],
        out_specs=[hsp] * 3
                  + [pl.BlockSpec((TS, 128), lambda i: (i, 0)), pl.BlockSpec((None, 8, TS), lambda i: (i, 0, 0))],
        out_shape=[jax.ShapeDtypeStruct((N_HEADS, nt_rows, 128), F32)] * 3
                  + [jax.ShapeDtypeStruct((nt_rows, 128), F32), jax.ShapeDtypeStruct((nt, 8, TS), F32)],
        compiler_params=_cparams(("parallel",)),
        name="gdn_prep",
    )(z, z, z, z, conv_w, alog_row, dtb_row, bseg, sel)


def _tri_solve(n, x, reverse):
    h = n.shape[0]
    bs = 8
    nblk = CHUNK // bs
    r = lax.broadcasted_iota(jnp.int32, (CHUNK, CHUNK), 0)
    c = lax.broadcasted_iota(jnp.int32, (CHUNK, CHUNK), 1)
    off = jnp.where((r // bs != c // bs)[None], n, 0.0)
    diag = jnp.stack([n[:, bs * b:bs * (b + 1), bs * b:bs * (b + 1)] for b in range(nblk)], axis=1)

    ws = [a.reshape(h, nblk, bs, a.shape[-1]) for a in (x, off)]
    for j in (range(bs - 1, 0, -1) if reverse else range(bs - 1)):
        col = jnp.broadcast_to(diag[..., j:j + 1], (h, nblk, bs, 2 * HEAD_W))
        ws = [w - col[..., :w.shape[-1]] * w[:, :, j:j + 1, :] for w in ws]
    z, m = [w.reshape(h, CHUNK, w.shape[-1]) for w in ws]

    bmm = lambda a, b: jnp.einsum("hij,hjk->hik", a.astype(BF16), b.astype(BF16), preferred_element_type=F32)
    m2 = bmm(m, m)
    w1 = z - bmm(m, z)
    w2 = w1 + bmm(m2, w1)
    return w2 + bmm(bmm(m2, m2), w2)


def _gdn_chunk_terms(qk_ref, kk_ref, vk_ref, gb_ref, gt_ref, d, reverse):
    c = CHUNK
    nh = N_HEADS
    nc = TS // c
    r = lax.broadcasted_iota(jnp.int32, (c, c), 0)
    cc = lax.broadcasted_iota(jnp.int32, (c, c), 1)
    incl = ((r <= cc) if reverse else (r >= cc))[None]
    strict = ((r < cc) if reverse else (r > cc))[None]
    lo = lax.broadcasted_iota(jnp.int32, (1, 1, 2 * HEAD_W), 2) < HEAD_W
    bmm_nt = lambda a, b: jnp.einsum("hid,hjd->hij", a, b, preferred_element_type=F32)
    chunks = lambda a: a.reshape(nh * nc, c, a.shape[-1])
    qk = chunks(qk_ref[...])
    kk = chunks(kk_ref[...])
    vk = chunks(vk_ref[...])
    gb = gb_ref[...]
    spread = lambda l0: chunks(jnp.stack([jnp.broadcast_to(gb[:, l0 + h:l0 + h + 1], (TS, 2 * HEAD_W))
                                          for h in range(nh)]))
    gc = spread(L_A - 128 + d * nh)
    bt = spread(L_B - 128 + d * nh)
    grow = jnp.stack([gt_ref[d * nh + h:d * nh + h + 1, ci * c:(ci + 1) * c]
                      for h in range(nh) for ci in range(nc)])
    dec = jnp.where(incl, jnp.exp(jnp.minimum(gc[:, :, :c] - grow, 0.0)), 0.0)
    kkb = kk.astype(BF16)
    k_dot_k = 0.5 * bmm_nt(kkb, kkb)
    q_only = jnp.where(lo, qk, 0.0)
    q_dot_k = bmm_nt(q_only.astype(BF16), kkb) * dec
    n = jnp.where(strict, bt[:, :, :c] * k_dot_k * dec, 0.0)
    egc = jnp.exp(gc)
    x = _tri_solve(n, vk * jnp.where(lo, bt, bt * egc), reverse)
    edge = 0 if reverse else c - 1
    glast = gc[:, edge:edge + 1, :]
    ktail = kk * jnp.exp(glast - gc)
    return (jnp.where(lo, 0.0, x).astype(BF16), x[:, :, :HEAD_W], (q_only * egc).astype(BF16), q_dot_k.astype(BF16),
            ktail.astype(BF16), jnp.exp(glast)[:, :, :HEAD_W])


def _gdn_scan_kernel(qkf, qkb, kkf, kkb, vkf, vkb, gbf, gbb, gtf, gtb, of_ref, ob_ref, s_ref):
    @pl.when(pl.program_id(1) == 0)
    def _():
        s_ref[...] = jnp.zeros(s_ref.shape, F32)

    n_chunks = TS // CHUNK
    dirs = ((qkf, kkf, vkf, gbf, gtf, of_ref, False), (qkb, kkb, vkb, gbb, gtb, ob_ref, True))
    order = [range(n_chunks), range(n_chunks - 1, -1, -1)]
    terms = [_gdn_chunk_terms(*dirs[d][:5], d, dirs[d][6]) for d in range(2)]
    pick = lambda a, ci: a.reshape((N_HEADS, n_chunks) + a.shape[1:])[:, ci]
    bmm = lambda a, b: jnp.einsum("hij,hjk->hik", a, b, preferred_element_type=F32)
    bmm_tn = lambda a, b: jnp.einsum("hcd,hce->hde", a, b, preferred_element_type=F32)
    state = [s_ref[0], s_ref[1]]
    for step in range(n_chunks):
        for d in range(2):
            ci = order[d][step]
            k_cum, u, q_dec, q_dot_k, ktail, total = [pick(a, ci) for a in terms[d]]
            s2b = state[d].astype(BF16)
            vb = (u - bmm(k_cum, s2b)).astype(BF16)
            dirs[d][5][:, ci * CHUNK:(ci + 1) * CHUNK, :] = bmm(q_dec, s2b) + bmm(q_dot_k, vb)
            state[d] = state[d] * total + bmm_tn(ktail, vb)
    s_ref[0] = state[0]
    s_ref[1] = state[1]


def _gdn_scan(qk, kk, vk, gb, gt, nb, s):
    h, nt_rows, _ = qk.shape
    tps = s // TS
    nlt = nb * tps
    fwd = lambda b, n: jnp.where(n == 0, nlt + b, b * tps + n - 1)
    bwd = lambda b, n: jnp.where(n == 0, nlt + b, b * tps + tps - n)

    def views(shape, imap):
        return [pl.BlockSpec(shape, functools.partial(imap, t)) for t in (fwd, bwd)]

    heads = views((h, TS, 128), lambda t, b, n: (0, t(b, n), 0))
    lanes = views((TS, 128), lambda t, b, n: (t(b, n), 0))
    rows = views((None, 8, TS), lambda t, b, n: (t(b, n), 0, 0))
    outs = views((h, TS, HEAD_W), lambda t, b, n: (0, t(b, n), 0))
    return pl.pallas_call(
        _gdn_scan_kernel,
        grid=(nb, tps + 1),
        in_specs=heads * 3 + lanes + rows,
        out_specs=outs,
        out_shape=[jax.ShapeDtypeStruct((h, nt_rows, HEAD_W), F32)] * 2,
        scratch_shapes=[pltpu.VMEM((2, h, 2 * HEAD_W, HEAD_W), F32)],
        compiler_params=_cparams(("parallel", "arbitrary")),
        name="gdn_scan",
    )(qk, qk, kk, kk, vk, vk, gb, gb, gt, gt)


def _gdn_finish_kernel(of_ref, ob_ref, z_ref, nw_ref, y_ref):
    o = of_ref[...] + ob_ref[...]
    y = o * lax.rsqrt(jnp.mean(o * o, axis=-1, keepdims=True) + EPS) * nw_ref[...]
    y = jnp.concatenate([y[h] for h in range(N_HEADS)], axis=-1)
    y_ref[...] = (y * _silu(z_ref[...].astype(F32))).astype(BF16)


def _gdn_finish(of, ob, z, nw_row):
    nt = pl.cdiv(of.shape[1], TM)
    hsp = pl.BlockSpec((N_HEADS, TM, HEAD_W), lambda i: (0, i, 0))
    return pl.pallas_call(
        _gdn_finish_kernel,
        grid=(nt,),
        in_specs=[hsp, hsp, pl.BlockSpec((TM, 256), lambda i: (i, C_Z // 256)),
                  pl.BlockSpec(nw_row.shape, lambda i: (0, 0))],
        out_specs=pl.BlockSpec((TM, 256), lambda i: (i, 0)),
        out_shape=jax.ShapeDtypeStruct((of.shape[1], 256), BF16),
        compiler_params=_cparams(("parallel",)),
        name="gdn_finish",
    )(of, ob, z, nw_row)


S5_SUB = 8
S5_NS = S5_GROUPS * S5_STATE


def _s5_param_kernel(are_ref, aim_ref, ldt_ref, bre_ref, bim_ref, cre_ref, cim_ref,
                     bb_ref, kst_ref, cbd_ref, tab_ref):
    d = pl.program_id(0)
    lam_re = jnp.minimum(are_ref[...], -1e-4)
    lam_im = aim_ref[...]
    dt = jnp.exp(ldt_ref[...])

    def power(tau):
        mag = jnp.exp(lam_re * dt * tau)
        ang = lam_im * dt * tau
        return mag * jnp.cos(ang), mag * jnp.sin(ang)

    idx = lax.broadcasted_iota(jnp.int32, (S5_SUB, 1), 0)
    p_re, p_im = power(idx.astype(F32))
    lb_re, lb_im = p_re[1:2], p_im[1:2]
    den = lam_re * lam_re + lam_im * lam_im
    f_re = ((lb_re - 1.0) * lam_re + lb_im * lam_im) / den
    f_im = (lb_im * lam_re - (lb_re - 1.0) * lam_im) / den
    bb_re = f_re * bre_ref[...] - f_im * bim_ref[...]
    bb_im = f_re * bim_ref[...] + f_im * bre_ref[...]
    bb_ref[...] = jnp.concatenate([bb_re, bb_im], axis=-1).astype(BF16)
    c_re = cre_ref[...]
    c_im = cim_ref[...]
    cbd_ref[...] = jnp.concatenate([c_re, -c_im], axis=0).astype(BF16)
    for tau in range(S5_SUB):
        g_re = bb_re * p_re[tau:tau + 1] - bb_im * p_im[tau:tau + 1]
        g_im = bb_re * p_im[tau:tau + 1] + bb_im * p_re[tau:tau + 1]
        kst_ref[tau] = (_dot(g_re, c_re, precision=HI) - _dot(g_im, c_im, precision=HI)).astype(BF16)
    fwd = d == 0
    t_in = jnp.where(fwd, S5_SUB - 1 - idx, idx).astype(F32)
    t_out = jnp.where(fwd, idx + 1, S5_SUB - idx).astype(F32)
    for k, tau in enumerate((t_in, t_out, jnp.full((S5_SUB, 1), float(S5_SUB), F32))):
        tab_ref[k] = jnp.concatenate(power(tau), axis=-1)


def _s5_params(a_re, a_im, log_dt, b_re, b_im, c_re, c_im):
    depth = a_re.shape[0]
    g, p, gch = S5_GROUPS, S5_STATE, S5_GROUP_CH
    ns, c = S5_NS, g * gch
    eye = jnp.eye(g, dtype=F32)
    row = lambda a: a.astype(F32).reshape(depth, 2, 1, ns)
    ldt = jnp.broadcast_to(log_dt.astype(F32)[..., None], (depth, 2, g, p))
    b_bd = lambda b: jnp.einsum("lgpc,gh->lgchp", b.astype(F32), eye).reshape(depth, c, ns)
    c_bd = lambda cc: jnp.einsum("ldgcp,gh->ldgphc", cc.astype(F32), eye).reshape(depth, 2, ns, c)
    per_dir = lambda *shape: pl.BlockSpec((None, None) + shape, lambda d, l: (l, d) + (0,) * len(shape))
    shared = pl.BlockSpec((None, c, ns), lambda d, l: (l, 0, 0))
    out = lambda *shape, dtype=BF16: jax.ShapeDtypeStruct((depth, 2) + shape, dtype)
    return pl.pallas_call(
        _s5_param_kernel,
        grid=(2, depth),
        in_specs=[per_dir(1, ns)] * 3 + [shared] * 2 + [per_dir(ns, c)] * 2,
        out_specs=[per_dir(c, 2 * ns), per_dir(S5_SUB, c, c), per_dir(2 * ns, c), per_dir(3, S5_SUB, 2 * ns)],
        out_shape=[out(c, 2 * ns), out(S5_SUB, c, c), out(2 * ns, c), out(3, S5_SUB, 2 * ns, dtype=F32)],
        compiler_params=_cparams(("parallel", "parallel"), 48),
        name="s5_params",
    )(row(a_re), row(a_im), row(ldt), b_bd(b_re), b_bd(b_im), c_bd(c_re), c_bd(c_im))


def _s5_direction(u_ref, bb_ref, kst_ref, cbd_ref, tab_ref, y_ref, x_ref, xin_ref, xpv_ref, reverse):
    ns, sub = S5_NS, S5_SUB
    nsc = TS // sub
    u = u_ref[...]
    z = _dot(u, bb_ref[...])
    z_re = z[:, :ns].reshape(nsc, sub, ns)
    z_im = z[:, ns:].reshape(nsc, sub, ns)
    w = tab_ref[0]
    w_re, w_im = w[:, :ns][None], w[:, ns:][None]
    group_sum = lambda a: jnp.broadcast_to(jnp.sum(a, axis=1, keepdims=True), a.shape)
    xin_ref[0] = group_sum(w_re * z_re - w_im * z_im)
    xin_ref[1] = group_sum(w_re * z_im + w_im * z_re)
    a = tab_ref[2]
    a_re, a_im = a[:, :ns], a[:, ns:]

    def step(k, carry):
        x_re, x_im = carry
        n = nsc - 1 - k if reverse else k
        xpv_ref[0, n] = x_re
        xpv_ref[1, n] = x_im
        return (a_re * x_re - a_im * x_im + xin_ref[0, n], a_re * x_im + a_im * x_re + xin_ref[1, n])

    x_re, x_im = lax.fori_loop(0, nsc, step, (x_ref[0], x_ref[1]))
    x_ref[0] = x_re
    x_ref[1] = x_im
    o = tab_ref[1]
    o_re, o_im = o[:, :ns][None], o[:, ns:][None]
    p_re = (xpv_ref[0] * o_re - xpv_ref[1] * o_im).reshape(TS, ns)
    p_im = (xpv_ref[0] * o_im + xpv_ref[1] * o_re).reshape(TS, ns)
    y = _dot(jnp.concatenate([p_re, p_im], axis=-1).astype(BF16), cbd_ref[...])
    uf = u.astype(F32)
    pos = lax.broadcasted_iota(jnp.int32, (TS, 1), 0) % sub
    for tau in range(sub):
        if tau == 0:
            shifted = u
        else:
            rolled = pltpu.roll(uf, TS - tau if reverse else tau, axis=0)
            inside = (pos + tau < sub) if reverse else (pos >= tau)
            shifted = jnp.where(inside, rolled, 0.0).astype(BF16)
        y = y + _dot(shifted, kst_ref[tau])
    y_ref[...] = y


def _s5_scan_kernel(uf_ref, ub_ref, bbf, bbb, kstf, kstb, cbdf, cbdb, tabf, tabb, yf_ref, yb_ref,
                    x_ref, xin_ref, xpv_ref):
    @pl.when(pl.program_id(1) == 0)
    def _():
        x_ref[...] = jnp.zeros(x_ref.shape, F32)

    _s5_direction(uf_ref, bbf, kstf, cbdf, tabf, yf_ref, x_ref.at[0], xin_ref.at[0], xpv_ref.at[0], False)
    _s5_direction(ub_ref, bbb, kstb, cbdb, tabb, yb_ref, x_ref.at[1], xin_ref.at[1], xpv_ref.at[1], True)


def _s5_scan(z, bb, kst, cbd, tab, layer, nb, s):
    nt_rows = z.shape[0]
    tps = s // TS
    nlt = nb * tps
    fwd = lambda b, n: jnp.where(n == 0, nlt + b, b * tps + n - 1)
    bwd = lambda b, n: jnp.where(n == 0, nlt + b, b * tps + tps - n)
    ns, c, nsc = S5_NS, S5_GROUPS * S5_GROUP_CH, TS // S5_SUB

    def both(a):
        return [pl.BlockSpec((None, None) + a.shape[2:], lambda b, n, d=d: (layer, d) + (0,) * (a.ndim - 2))
                for d in range(2)]

    state = lambda *lead: pltpu.VMEM((2, 2) + lead + (S5_SUB, ns), F32)
    return pl.pallas_call(
        _s5_scan_kernel,
        grid=(nb, tps + 1),
        in_specs=[pl.BlockSpec((TS, c), lambda b, n: (fwd(b, n), C_U // 256)),
                  pl.BlockSpec((TS, c), lambda b, n: (bwd(b, n), C_U // 256))]
                 + both(bb) + both(kst) + both(cbd) + both(tab),
        out_specs=[pl.BlockSpec((TS, c), lambda b, n: (fwd(b, n), 0)),
                   pl.BlockSpec((TS, c), lambda b, n: (bwd(b, n), 0))],
        out_shape=[jax.ShapeDtypeStruct((nt_rows, c), F32)] * 2,
        scratch_shapes=[state(), state(nsc), state(nsc)],
        compiler_params=_cparams(("parallel", "arbitrary"), 56),
        name="s5_scan",
    )(z, z, bb, bb, kst, kst, cbd, cbd, tab, tab)


def _s5_finish_kernel(yf_ref, yb_ref, u_ref, d_ref, w_ref, b_ref, o_ref):
    y = yf_ref[...] + yb_ref[...] + d_ref[...] * u_ref[...].astype(F32)
    y = jax.nn.gelu(y)
    gate = _sigmoid(_dot(y.astype(BF16), w_ref[...]) + b_ref[...])
    o_ref[...] = (y * gate).astype(BF16)


def _s5_finish(yf, yb, z, d_row, glu_w, glu_b_row):
    nt = pl.cdiv(yf.shape[0], TM)
    full = lambda a: pl.BlockSpec(a.shape, lambda i: (0,) * a.ndim)
    return pl.pallas_call(
        _s5_finish_kernel,
        grid=(nt,),
        in_specs=[pl.BlockSpec((TM, 256), lambda i: (i, 0)),
                  pl.BlockSpec((TM, 256), lambda i: (i, 0)),
                  pl.BlockSpec((TM, 256), lambda i: (i, C_U // 256)),
                  full(d_row), full(glu_w), full(glu_b_row)],
        out_specs=pl.BlockSpec((TM, 256), lambda i: (i, 0)),
        out_shape=jax.ShapeDtypeStruct((yf.shape[0], 256), BF16),
        compiler_params=_cparams(("parallel",)),
        name="s5_finish",
    )(yf, yb, z, d_row, glu_w, glu_b_row)


def _merge_kernel(x_ref, y0_ref, y1_ref, y2_ref, y3_ref, gates_ref, wb_ref, wo_ref, g1_ref, gain_ref, o_ref):
    acc = jnp.zeros((TM, D_MODEL), F32)
    for bi, y_ref in enumerate((y0_ref, y1_ref, y2_ref, y3_ref)):
        proj = _dot(y_ref[...], wb_ref[bi])
        acc = acc + gates_ref[:, bi * D_MODEL:(bi + 1) * D_MODEL].astype(F32) * proj
    y = _dot(acc.astype(BF16), wo_ref[...])
    o_ref[...] = x_ref[...] + g1_ref[...] * _rms(y, gain_ref[...])


def _merge(x, ys, gates, wb, wo, g1, gain, layer, tps, nb, n_rows):
    nt = pl.cdiv(n_rows, TM)
    row = lambda i: (jnp.minimum(i // tps, nb), 0, 0)
    tile = lambda w: pl.BlockSpec((TM, w), lambda i: (i, 0))
    return pl.pallas_call(
        _merge_kernel,
        grid=(nt,),
        in_specs=[tile(D_MODEL)] + [tile(BRANCH_W)] * 4 + [tile(4 * D_MODEL),
                  pl.BlockSpec((None, 4, BRANCH_W, D_MODEL), lambda i: (layer, 0, 0, 0)),
                  pl.BlockSpec((None, D_MODEL, D_MODEL), lambda i: (layer, 0, 0)),
                  pl.BlockSpec((None, 1, D_MODEL), row),
                  pl.BlockSpec((1, D_MODEL), lambda i: (0, 0))],
        out_specs=tile(D_MODEL),
        out_shape=jax.ShapeDtypeStruct((n_rows, D_MODEL), F32),
        compiler_params=_cparams(("parallel",), 56),
        name="merge",
    )(x, *ys, gates, wb, wo, g1, gain)


def _mlp_kernel(x_ref, sc_ref, sh_ref, g2_ref, gin_ref, gout_ref, w1_ref, w2_ref, o_ref, h_ref, acc_ref):
    j = pl.program_id(1)

    @pl.when(j == 0)
    def _():
        h_ref[...] = _norm_mod(x_ref[...], gin_ref[...], sc_ref[...], sh_ref[...]).astype(BF16)
        acc_ref[...] = jnp.zeros(acc_ref.shape, F32)

    t = jnp.maximum(_dot(h_ref[...], w1_ref[...]), 0.0)
    acc_ref[...] += _dot((t * t).astype(BF16), w2_ref[...])

    @pl.when(j == pl.num_programs(1) - 1)
    def _():
        o_ref[...] = x_ref[...] + g2_ref[...] * _rms(acc_ref[...], gout_ref[...])


def _mlp(x, sc, sh, g2, gin, gout, w1, w2, layer, tps, nb, tf=2048):
    nt = pl.cdiv(x.shape[0], TM)
    row = lambda i, j: (jnp.minimum(i // tps, nb), 0, 0)
    vec = pl.BlockSpec((1, D_MODEL), lambda i, j: (0, 0))
    return pl.pallas_call(
        _mlp_kernel,
        grid=(nt, D_FF // tf),
        in_specs=[pl.BlockSpec((TM, D_MODEL), lambda i, j: (i, 0)),
                  pl.BlockSpec((None, 1, D_MODEL), row), pl.BlockSpec((None, 1, D_MODEL), row),
                  pl.BlockSpec((None, 1, D_MODEL), row), vec, vec,
                  pl.BlockSpec((None, D_MODEL, tf), lambda i, j: (layer, 0, j)),
                  pl.BlockSpec((None, tf, D_MODEL), lambda i, j: (layer, j, 0))],
        out_specs=pl.BlockSpec((TM, D_MODEL), lambda i, j: (i, 0)),
        out_shape=jax.ShapeDtypeStruct(x.shape, F32),
        scratch_shapes=[pltpu.VMEM((TM, D_MODEL), BF16), pltpu.VMEM((TM, D_MODEL), F32)],
        compiler_params=_cparams(("parallel", "arbitrary"), 58),
        name="mlp",
    )(x, sc, sh, g2, gin, gout, w1, w2)


def _pack_w_in(w_in):
    o_na, o_cq, o_ckv, o_gdn, o_z, o_a, o_b, o_u, o_gate = 0, 768, 1024, 1184, 1952, 2208, 2216, 2224, 2480
    cols = lambda lo, n: w_in[:, :, lo:lo + n]
    o_kr = o_ckv + MLA_KV_LORA
    swapped = [cols(o_kr + 8, 8), cols(o_kr, 8), cols(o_kr + 24, 8), cols(o_kr + 16, 8)]
    small = jnp.concatenate(
        [cols(o_gdn, 768), cols(o_na, 768), cols(o_cq, 256),
         cols(o_ckv, 160), cols(o_a, 8), cols(o_b, 8)] + swapped
        + [jnp.zeros(w_in.shape[:2] + (256 - L_KRS - MLA_ROPE,), w_in.dtype), cols(o_z, 256), cols(o_u, 256)], axis=2)
    assert small.shape[2] == ZW
    return small.astype(BF16), w_in[:, :, o_gate:].astype(BF16)


def _mla_weights(w_uq, w_ukv):
    depth = w_uq.shape[0]
    hq = MLA_NOPE + MLA_ROPE
    wq = w_uq.reshape(depth, -1, N_HEADS, hq)
    pad = lambda a, lo, hi: jnp.pad(a, ((0, 0), (0, 0), (0, 0), (lo, hi)))
    wq_ext = pad(wq, 0, 128 - hq).reshape(depth, -1, N_HEADS * 128)
    wq_sw = pad(wq[..., MLA_NOPE:][..., ROPE_SWAP], MLA_NOPE, 128 - hq).reshape(depth, -1, N_HEADS * 128)
    wkv = w_ukv.reshape(depth, -1, N_HEADS, MLA_NOPE + HEAD_W)
    wk = pad(wkv[..., :MLA_NOPE], 0, 128 - MLA_NOPE).reshape(depth, -1, N_HEADS * 128)
    wv = pad(wkv[..., MLA_NOPE:], 0, 128 - HEAD_W).reshape(depth, -1, N_HEADS * 128)
    return [a.astype(BF16) for a in (wq_ext, wq_sw, wk, wv, jnp.swapaxes(wv, 1, 2))]


def _rope_place_mats():
    p1 = np.zeros((256, N_HEADS * 128), np.float32)
    p2 = np.zeros((256, N_HEADS * 128), np.float32)
    for h in range(N_HEADS):
        for r in range(MLA_ROPE):
            p1[L_KR + r, h * 128 + MLA_NOPE + r] = 1.0
            p2[L_KRS + r, h * 128 + MLA_NOPE + r] = 1.0
    return jnp.asarray(p1, BF16), jnp.asarray(p2, BF16)


def _rope_tables(s):
    quarter = MLA_ROPE // 4
    inv_freq = ROPE_BASE ** (-jnp.arange(quarter, dtype=F32) / quarter)
    t = jnp.arange(s)
    ang_r = (t // GRID_W).astype(F32)[:, None] * inv_freq[None, :]
    ang_c = (t % GRID_W).astype(F32)[:, None] * inv_freq[None, :]
    cr, sr, cc, sn = jnp.cos(ang_r), jnp.sin(ang_r), jnp.cos(ang_c), jnp.sin(ang_c)
    cos = jnp.concatenate([jnp.ones((s, MLA_NOPE), F32), cr, cr, cc, cc, jnp.ones((s, 32), F32)], axis=1)
    sin = jnp.concatenate([jnp.zeros((s, MLA_NOPE), F32), -sr, sr, -sn, sn, jnp.zeros((s, 32), F32)], axis=1)
    cos = jnp.concatenate([cos, jnp.ones((TM, 128), F32)], axis=0)
    sin = jnp.concatenate([sin, jnp.zeros((TM, 128), F32)], axis=0)
    return cos, sin


def _head_block_ones():
    r = np.arange(BRANCH_W)
    return jnp.asarray((r[:, None] // HEAD_W == r[None, :] // HEAD_W).astype(np.float32))


def _lane_row(vals, offset, width=256):
    return jnp.zeros((1, width), F32).at[0, offset:offset + vals.shape[0]].set(vals.astype(F32))


def _gdn_mixer(z, conv_w, a_log, dt_bias, norm_w, consts, nb, s, ctx_len):
    alog_row = _lane_row(a_log.reshape(-1), L_A)
    dtb_row = _lane_row(dt_bias.reshape(-1), L_A)
    qk, kk, vk, gb, gt = _gdn_prep(z, conv_w, alog_row, dtb_row, consts["bseg"], consts["sel"], s // TS, nb * s // TS)
    of, ob = _gdn_scan(qk, kk, vk, gb, gt, nb, s)
    return _gdn_finish(of, ob, z, norm_w.astype(F32)[None, :])


def _s5_mixer(z, operators, layer, d_skip, glu_w, glu_b, nb, s):
    yf, yb = _s5_scan(z, *operators, layer, nb, s)
    return _s5_finish(yf, yb, z, d_skip.astype(F32)[None, :], glu_w.astype(BF16), glu_b.astype(F32)[None, :])


def kernel(x, c, ctx, c_ctx, ada_w, ada_b, norm_gains, w_in, na_rpb, mla_q_norm, mla_kv_norm, mla_w_uq, mla_w_ukv, gdn_conv, gdn_a_log, gdn_dt_bias, gdn_norm, s5_a_re, s5_a_im, s5_log_dt, s5_b_re, s5_b_im, s5_c_re, s5_c_im, s5_d, s5_glu_w, s5_glu_b, w_branch, w_out, mlp_w1, mlp_w2):
    nb, s, d = x.shape
    ctx_len = ctx.shape[1]
    depth = ada_w.shape[0]
    assert d == D_MODEL and ctx_len == TS and nb * ctx_len <= TM and s % TM == 0 and s // GRID_W >= 16
    nl = nb * s
    tps = s // TM

    xs = jnp.concatenate([x.reshape(nl, d), ctx.reshape(nb * ctx_len, d)], axis=0)
    cvec = jnp.zeros((8, d), F32).at[:nb].set(c).at[nb].set(c_ctx)
    mod = _modulation(cvec, ada_w, ada_b)
    mod = mod[:, :nb + 1].reshape(depth, nb + 1, 6, 1, d)

    w_small, w_gates = _pack_w_in(w_in)
    wq_ext, wq_sw, wk_ext, wv, wvt = _mla_weights(mla_w_uq, mla_w_ukv)
    p1, p2 = _rope_place_mats()
    cos_t, sin_t = _rope_tables(s)
    sel = np.zeros((8, 128), np.float32)
    sel[np.arange(8), L_A - 128 + np.arange(8)] = 1.0
    consts = {"bseg": _head_block_ones().astype(BF16), "sel": jnp.asarray(sel)}
    wb = w_branch.astype(BF16)
    wo = w_out.astype(BF16)
    w1 = mlp_w1.astype(BF16)
    w2 = mlp_w2.astype(BF16)
    gains = norm_gains.astype(F32)
    s5_ops = _s5_params(s5_a_re, s5_a_im, s5_log_dt, s5_b_re, s5_b_im, s5_c_re, s5_c_im)

    for l in range(depth):
        sh1, sc1, g1, sh2, sc2, g2 = [mod[l, :, i] for i in range(6)]
        z = _inproj(xs, sc1, sh1, gains[l, 0][None], w_small, l, tps, nb, False)
        gates = _inproj(xs, sc1, sh1, gains[l, 0][None], w_gates, l, tps, nb, True)

        need_ctx = l < depth - 1
        y_rows = xs.shape[0] if need_ctx else nl
        y_na = _na_latent(z, na_rpb[l], nb, s)
        qm, km, vm, vt = _mla_prep(z, cos_t, sin_t, mla_q_norm[l].astype(F32)[None],
                                   mla_kv_norm[l].astype(F32)[None], wq_ext[l], wq_sw[l], wk_ext[l], wv[l], wvt[l],
                                   p1, p2, tps, nl // TM)
        y_mla = _mla_latent(qm, km, vt, nb, s)
        if need_ctx:
            y_na = jnp.concatenate([y_na, _ctx_attention(
                z, z, z, C_NA // 256, C_NA // 256 + 1, C_NA // 256 + 2, 256, 256, nl // TS, nb,
                HEAD_W ** -0.5, False, "na_ctx")], axis=0)
            y_mla = jnp.concatenate([y_mla, _ctx_attention(
                qm, km, vm, 0, 0, 0, N_HEADS * 128, N_HEADS * 128, nl // TS, nb, 1.0, True, "mla_ctx")], axis=0)

        y_gdn = _gdn_mixer(z, gdn_conv[l].astype(F32), gdn_a_log[l], gdn_dt_bias[l], gdn_norm[l], consts,
                           nb, s, ctx_len)
        y_s5 = _s5_mixer(z, s5_ops, l, s5_d[l], s5_glu_w[l], s5_glu_b[l], nb, s)

        xs = _merge(xs, (y_na, y_mla, y_gdn, y_s5), gates, wb, wo, g1, gains[l, 1][None], l, tps, nb, y_rows)
        xs = _mlp(xs, sc2, sh2, g2, gains[l, 2][None], gains[l, 3][None], w1, w2, l, tps, nb)
    return xs[:nl].reshape(nb, s, d)
```
